```python
import jax
import jax.numpy as jnp
from jax import lax
import numpy as np

D_MODEL = 1024
BATCH = 2
SEQ = 16384
DEPTH = 4

CTX_LEN = 256
GRID_W = 64
HEAD_DIM = 64
GROUP_HEADS = 4
GROUP_W = GROUP_HEADS * HEAD_DIM
N_MIXERS = 4
MIX_W = N_MIXERS * GROUP_W
MLA_Q_RANK = 256
MLA_KV_RANK = 128
MLA_NOPE = 64
MLA_ROPE = 32
MLA_V = 64
MLA_BLOCK = 128
RET_CHUNK = 128
NA_KR = 8
NA_KC = 16
SWA_KV_HEADS = 2
SWA_WINDOW = 128
SWA_BLOCK = 128
FFN_HIDDEN = ((8 * D_MODEL + 3 * 256 - 1) // (3 * 256)) * 256
ROPE_THETA = 10000.0
EPS = 1e-6
NEG_INF = -1e30

IN_SIZES = (MLA_Q_RANK, MLA_KV_RANK, MLA_ROPE,
            GROUP_W, GROUP_W, GROUP_W, GROUP_W, GROUP_W,
            GROUP_W, GROUP_W, GROUP_W,
            GROUP_W, SWA_KV_HEADS * HEAD_DIM, SWA_KV_HEADS * HEAD_DIM)
IN_W = sum(IN_SIZES)
IN_SPLIT = tuple(sum(IN_SIZES[:i + 1]) for i in range(len(IN_SIZES) - 1))

kernel_name = 'hybrid_parallel_head_flow_block'


def rms_norm(x, g):
    xf = x.astype(jnp.float32)
    y = xf * lax.rsqrt(jnp.mean(xf * xf, axis=-1, keepdims=True) + EPS)
    return (y * g.astype(jnp.float32)).astype(x.dtype)


def head_rms(t):
    return t * lax.rsqrt(jnp.mean(t * t, axis=-1, keepdims=True) + EPS)


def heads(t, h):
    b, n, _ = t.shape
    return t.reshape(b, n, h, -1).transpose(0, 2, 1, 3)


def merge_heads(t):
    b, h, n, d = t.shape
    return t.transpose(0, 2, 1, 3).reshape(b, n, h * d)


def rope_1d(x, pos):
    d = x.shape[-1]
    inv = ROPE_THETA ** (-jnp.arange(0, d, 2, dtype=jnp.float32) / d)
    ang = pos.astype(jnp.float32)[:, None] * inv[None, :]
    cos, sin = jnp.cos(ang), jnp.sin(ang)
    xf = x.astype(jnp.float32)
    x1, x2 = xf[..., : d // 2], xf[..., d // 2:]
    return jnp.concatenate([x1 * cos - x2 * sin, x2 * cos + x1 * sin], axis=-1).astype(x.dtype)


def rope_2d(x, row, col):
    h = x.shape[-1] // 2
    return jnp.concatenate([rope_1d(x[..., :h], row), rope_1d(x[..., h:], col)], axis=-1)


def softmax_attend(q, k, v, scale, sink=None):
    s = jnp.einsum('bhqd,bhkd->bhqk', q, k, preferred_element_type=jnp.float32) * scale
    if sink is not None:
        s = jnp.concatenate([s, jnp.broadcast_to(sink.astype(jnp.float32)[None, :, None, None], s.shape[:-1] + (1,))], axis=-1)
    p = jax.nn.softmax(s, axis=-1)
    if sink is not None:
        p = p[..., :-1]
    return jnp.einsum('bhqk,bhkd->bhqd', p.astype(v.dtype), v)


def mla_mixer(xcq, xckv, xkr, ycq, yckv, ykr, row, col, q_norm, w_uq, kv_norm, w_ukv):
    H = GROUP_HEADS
    scale = (MLA_NOPE + MLA_ROPE) ** -0.5

    def qkv(cq, ckv, kr, rotate):
        q = heads(rms_norm(cq, q_norm) @ w_uq, H)
        kv = heads(rms_norm(ckv, kv_norm) @ w_ukv, H)
        q_nope, q_rope = q[..., :MLA_NOPE], q[..., MLA_NOPE:]
        k_nope, v = kv[..., :MLA_NOPE], kv[..., MLA_NOPE:]
        k_rope = kr[:, None]
        if rotate:
            q_rope = rope_2d(q_rope, row, col)
            k_rope = rope_2d(k_rope, row, col)
        k_rope = jnp.broadcast_to(k_rope, k_nope.shape[:-1] + (MLA_ROPE,))
        return (jnp.concatenate([q_nope, q_rope], axis=-1),
                jnp.concatenate([k_nope, k_rope], axis=-1), v)

    qx, kx, vx = qkv(xcq, xckv, xkr, True)
    qy, ky, vy = qkv(ycq, yckv, ykr, False)
    k_all = jnp.concatenate([kx, ky], axis=2)
    v_all = jnp.concatenate([vx, vy], axis=2)
    B, _, S, dq = qx.shape
    nb = S // MLA_BLOCK
    qb = jnp.moveaxis(qx.reshape(B, H, nb, MLA_BLOCK, dq), 2, 0)
    ob = lax.map(lambda qblk: softmax_attend(qblk, k_all, v_all, scale), qb)
    ox = jnp.moveaxis(ob, 0, 2).reshape(B, H, S, MLA_V)
    oy = softmax_attend(qy, ky, vy, scale)
    return merge_heads(ox), merge_heads(oy)


def retention_scan(q, k, v, log_g, state0):
    B, H, T, _ = q.shape
    C = RET_CHUNK
    n = T // C
    i = jnp.arange(C, dtype=jnp.float32)
    diff = i[:, None] - i[None, :]
    lg = log_g[:, None, None]
    inner_decay = jnp.where(diff >= 0, jnp.exp(lg * jnp.maximum(diff, 0.0)), 0.0)
    q_decay = jnp.exp(lg * (i + 1.0)[:, None])
    k_decay = jnp.exp(lg * (C - 1.0 - i)[:, None])
    chunk_decay = jnp.exp(lg * C)

    def chunks(t):
        return jnp.moveaxis(t.reshape(B, H, n, C, t.shape[-1]), 2, 0)

    def step(state, inp):
        qc, kc, vc = inp
        att = jnp.einsum('bhid,bhjd->bhij', qc, kc) * inner_decay
        out = (jnp.einsum('bhij,bhjd->bhid', att, vc)
               + jnp.einsum('bhid,bhde->bhie', qc, state) * q_decay)
        state = state * chunk_decay + jnp.einsum('bhjd,bhje->bhde', kc * k_decay, vc)
        return state, out

    state, out = lax.scan(step, state0, (chunks(q), chunks(k), chunks(v)))
    return jnp.moveaxis(out, 0, 2).reshape(B, H, T, v.shape[-1]), state


def retention_mixer(xq, xk, xv, xgf, xgb, yq, yk, yv, ygf, ygb, decay, row, col):
    H = GROUP_HEADS
    f32 = jnp.float32
    kscale = HEAD_DIM ** -0.5
    q = rope_2d(heads(xq, H), row, col).astype(f32)
    k = rope_2d(heads(xk, H), row, col).astype(f32) * kscale
    v = heads(xv, H).astype(f32)
    qy = heads(yq, H).astype(f32)
    ky = heads(yk, H).astype(f32) * kscale
    vy = heads(yv, H).astype(f32)
    log_g = jax.nn.log_sigmoid(decay.astype(f32))
    zero = jnp.zeros((q.shape[0], H, HEAD_DIM, HEAD_DIM), f32)
    flip = lambda t: jnp.flip(t, axis=2)
    yf, sf = retention_scan(qy, ky, vy, log_g[0], zero)
    of, _ = retention_scan(q, k, v, log_g[0], sf)
    yb, sb = retention_scan(flip(qy), flip(ky), flip(vy), log_g[1], zero)
    ob, _ = retention_scan(flip(q), flip(k), flip(v), log_g[1], sb)

    def gate_merge(o_f, o_b, gf, gb):
        out = (head_rms(o_f) * jax.nn.silu(heads(gf, H).astype(f32))
               + head_rms(flip(o_b)) * jax.nn.silu(heads(gb, H).astype(f32)))
        return merge_heads(out).astype(gf.dtype)

    return gate_merge(of, ob, xgf, xgb), gate_merge(yf, yb, ygf, ygb)


def na_mixer(xq, xk, xv, yq, yk, yv, rpb):
    B, S, _ = xq.shape
    H, d = GROUP_HEADS, HEAD_DIM
    rows = S // GRID_W
    kr = min(NA_KR, rows)
    scale = d ** -0.5
    q = xq.reshape(B, rows, GRID_W, H, d)
    k = xk.reshape(B, rows, GRID_W, H, d)
    v = xv.reshape(B, rows, GRID_W, H, d)
    r = jnp.arange(rows)
    r0 = jnp.clip(r - kr // 2, 0, rows - kr)
    row_idx = r0[:, None] + jnp.arange(kr)[None, :]
    kg = k[:, row_idx]
    vg = v[:, row_idx]
    c = jnp.arange(GRID_W)
    c0 = jnp.clip(c - NA_KC // 2, 0, GRID_W - NA_KC)
    col_in = (c[None, :] >= c0[:, None]) & (c[None, :] < c0[:, None] + NA_KC)
    dr = row_idx - r[:, None] + NA_KR - 1
    dc = jnp.clip(c[None, :] - c[:, None], -(NA_KC - 1), NA_KC - 1) + NA_KC - 1
    bias = rpb[:, dr[:, None, :, None], dc[None, :, None, :]].astype(jnp.float32)
    s_loc = jnp.einsum('brchd,brkwhd->bhrckw', q, kg, preferred_element_type=jnp.float32) * scale + bias
    s_loc = jnp.where(col_in[:, None, :], s_loc, NEG_INF)
    kyh = yk.reshape(B, -1, H, d)
    vyh = yv.reshape(B, -1, H, d)
    s_ctx = jnp.einsum('brchd,blhd->bhrcl', q, kyh, preferred_element_type=jnp.float32) * scale
    n_loc = kr * GRID_W
    p = jax.nn.softmax(jnp.concatenate([s_loc.reshape(B, H, rows, GRID_W, n_loc), s_ctx], axis=-1), axis=-1)
    p_loc = p[..., :n_loc].reshape(B, H, rows, GRID_W, kr, GRID_W).astype(xv.dtype)
    p_ctx = p[..., n_loc:].astype(xv.dtype)
    ox = (jnp.einsum('bhrckw,brkwhd->brchd', p_loc, vg)
          + jnp.einsum('bhrcl,blhd->brchd', p_ctx, vyh)).reshape(B, S, H * d)
    oy = softmax_attend(heads(yq, H), heads(yk, H), heads(yv, H), scale)
    return ox, merge_heads(oy)


def swa_mixer(xq, xk, xv, yq, yk, yv, sink, row, col):
    B, S, _ = xq.shape
    Hq, Hk = GROUP_HEADS, SWA_KV_HEADS
    G = Hq // Hk
    d = HEAD_DIM
    Bl = SWA_BLOCK
    nb = S // Bl
    scale = d ** -0.5
    q = rope_2d(heads(xq, Hq), row, col)
    k = rope_2d(heads(xk, Hk), row, col)
    v = heads(xv, Hk)
    qb = q.reshape(B, Hk, G, nb, Bl, d)

    def window(t):
        tp = jnp.pad(t, ((0, 0), (0, 0), (Bl, Bl), (0, 0))).reshape(B, Hk, nb + 2, Bl, d)
        return jnp.concatenate([tp[:, :, :-2], tp[:, :, 1:-1], tp[:, :, 2:]], axis=3)

    kw, vw = window(k), window(v)
    qi = jnp.arange(Bl)
    j = jnp.arange(3 * Bl)
    blk = jnp.arange(nb)
    key_pos = (blk[:, None] - 1) * Bl + j[None, :]
    delta = j[None, :] - Bl - qi[:, None]
    valid = (jnp.abs(delta) <= SWA_WINDOW)[None] & ((key_pos >= 0) & (key_pos < S))[:, None, :]
    s_win = jnp.einsum('bkgnqd,bknjd->bkgnqj', qb, kw, preferred_element_type=jnp.float32) * scale
    s_win = jnp.where(valid, s_win, NEG_INF)
    ky = heads(yk, Hk)
    vy = heads(yv, Hk)
    L = ky.shape[2]
    s_ctx = jnp.einsum('bkgnqd,bkld->bkgnql', qb, ky, preferred_element_type=jnp.float32) * scale
    s_sink = jnp.broadcast_to(sink.astype(jnp.float32).reshape(Hk, G, 1, 1, 1), (B, Hk, G, nb, Bl, 1))
    p = jax.nn.softmax(jnp.concatenate([s_win, s_ctx, s_sink], axis=-1), axis=-1)
    ox = (jnp.einsum('bkgnqj,bknjd->bkgnqd', p[..., :3 * Bl].astype(xv.dtype), vw)
          + jnp.einsum('bkgnql,bkld->bkgnqd', p[..., 3 * Bl:3 * Bl + L].astype(xv.dtype), vy))
    ox = ox.reshape(B, Hq, S, d)
    oy = softmax_attend(heads(yq, Hq), jnp.repeat(ky, G, axis=1), jnp.repeat(vy, G, axis=1), scale, sink)
    return merge_heads(ox), merge_heads(oy)


def token_mixers(px, py, row, col, mla_q_norm, mla_w_uq, mla_kv_norm, mla_w_ukv, ret_decay, na_rpb, swa_sink):
    xs = jnp.split(px, IN_SPLIT, axis=-1)
    ys = jnp.split(py, IN_SPLIT, axis=-1)
    mla_x, mla_y = mla_mixer(xs[0], xs[1], xs[2], ys[0], ys[1], ys[2], row, col,
                             mla_q_norm, mla_w_uq, mla_kv_norm, mla_w_ukv)
    ret_x, ret_y = retention_mixer(xs[3], xs[4], xs[5], xs[6], xs[7],
                                   ys[3], ys[4], ys[5], ys[6], ys[7], ret_decay, row, col)
    na_x, na_y = na_mixer(xs[8], xs[9], xs[10], ys[8], ys[9], ys[10], na_rpb)
    swa_x, swa_y = swa_mixer(xs[11], xs[12], xs[13], ys[11], ys[12], ys[13], swa_sink, row, col)
    return (jnp.concatenate([mla_x, ret_x, na_x, swa_x], axis=-1),
            jnp.concatenate([mla_y, ret_y, na_y, swa_y], axis=-1))


def swiglu(h, w1, w3, w2):
    return (jax.nn.silu(h @ w1) * (h @ w3)) @ w2


def setup_inputs(seed: int = 0) -> dict:
    key = jax.random.key(seed)
    ks = jax.random.split(key, 24)
    f32 = jnp.float32
    nrm = lambda k, shape, s: jax.random.normal(k, shape, f32) * s
    H = GROUP_HEADS
    L = DEPTH
    base_decay = jnp.log(2.0 ** (5.0 + jnp.arange(H, dtype=f32)) - 1.0)
    return {
        'x': nrm(ks[0], (BATCH, SEQ, D_MODEL), 1.0),
        'c': nrm(ks[1], (BATCH, D_MODEL), 1.0),
        'ctx': nrm(ks[2], (BATCH, CTX_LEN, D_MODEL), 1.0),
        'c_ctx': nrm(ks[3], (D_MODEL,), 1.0),
        'ada_w': nrm(ks[4], (L, D_MODEL, 6 * D_MODEL), 0.5 * D_MODEL ** -0.5),
        'ada_b': nrm(ks[5], (L, 6 * D_MODEL), 0.02),
        'norm1_g': 1.0 + nrm(ks[6], (L, D_MODEL), 0.02),
        'w_in': nrm(ks[7], (L, D_MODEL, IN_W), D_MODEL ** -0.5),
        'mla_q_norm': 1.0 + nrm(ks[8], (L, MLA_Q_RANK), 0.02),
        'mla_w_uq': nrm(ks[9], (L, MLA_Q_RANK, H * (MLA_NOPE + MLA_ROPE)), MLA_Q_RANK ** -0.5),
        'mla_kv_norm': 1.0 + nrm(ks[10], (L, MLA_KV_RANK), 0.02),
        'mla_w_ukv': nrm(ks[11], (L, MLA_KV_RANK, H * (MLA_NOPE + MLA_V)), MLA_KV_RANK ** -0.5),
        'ret_decay': base_decay[None, None, :] + nrm(ks[12], (L, 2, H), 0.1),
        'na_rpb': nrm(ks[13], (L, H, 2 * NA_KR - 1, 2 * NA_KC - 1), 0.1),
        'swa_sink': nrm(ks[14], (L, H), 0.5),
        'w_out': nrm(ks[15], (L, MIX_W, D_MODEL), MIX_W ** -0.5),
        'norm2_g': 1.0 + nrm(ks[16], (L, D_MODEL), 0.02),
        'ffn_w1': nrm(ks[17], (L, D_MODEL, FFN_HIDDEN), D_MODEL ** -0.5),
        'ffn_w3': nrm(ks[18], (L, D_MODEL, FFN_HIDDEN), D_MODEL ** -0.5),
        'ffn_w2': nrm(ks[19], (L, FFN_HIDDEN, D_MODEL), FFN_HIDDEN ** -0.5),
        'final_norm_g': 1.0 + nrm(ks[20], (D_MODEL,), 0.02),
    }


def reference(x, c, ctx, c_ctx, ada_w, ada_b, norm1_g, w_in, mla_q_norm, mla_w_uq, mla_kv_norm,
              mla_w_ukv, ret_decay, na_rpb, swa_sink, w_out, norm2_g, ffn_w1, ffn_w3, ffn_w2,
              final_norm_g):
    S = x.shape[1]
    t = jnp.arange(S)
    row = t // GRID_W
    col = t % GRID_W
    y = ctx
    sc = jax.nn.silu(c)
    scc = jax.nn.silu(c_ctx)
    for l in range(DEPTH):
        mod_x = (sc @ ada_w[l] + ada_b[l])[:, None, :]
        mod_y = scc @ ada_w[l] + ada_b[l]
        shx1, scx1, gx1, shx2, scx2, gx2 = jnp.split(mod_x, 6, axis=-1)
        shy1, scy1, gy1, shy2, scy2, gy2 = jnp.split(mod_y, 6, axis=-1)
        hx = rms_norm(x, norm1_g[l]) * (1.0 + scx1) + shx1
        hy = rms_norm(y, norm1_g[l]) * (1.0 + scy1) + shy1
        mx, my = token_mixers(hx @ w_in[l], hy @ w_in[l], row, col, mla_q_norm[l], mla_w_uq[l],
                              mla_kv_norm[l], mla_w_ukv[l], ret_decay[l], na_rpb[l], swa_sink[l])
        x = x + gx1 * (mx @ w_out[l])
        hx = rms_norm(x, norm2_g[l]) * (1.0 + scx2) + shx2
        x = x + gx2 * swiglu(hx, ffn_w1[l], ffn_w3[l], ffn_w2[l])
        if l < DEPTH - 1:
            y = y + gy1 * (my @ w_out[l])
            hy = rms_norm(y, norm2_g[l]) * (1.0 + scy2) + shy2
            y = y + gy2 * swiglu(hy, ffn_w1[l], ffn_w3[l], ffn_w2[l])
    return rms_norm(x, final_norm_g)
```

```python
import functools

import jax
import jax.numpy as jnp
from jax import lax
from jax.experimental import pallas as pl
from jax.experimental.pallas import tpu as pltpu

F32 = jnp.float32
BF16 = jnp.bfloat16

GRID_W = 64
HEAD_DIM = 64
N_HEADS = 4
GROUP_W = N_HEADS * HEAD_DIM
MLA_Q_RANK = 256
MLA_KV_RANK = 128
MLA_NOPE = 64
MLA_ROPE = 32
MLA_V = 64
MLA_HEAD_PAD = 128
NA_KR = 8
NA_KC = 16
SWA_KV_HEADS = 2
SWA_WINDOW = 128
SWA_BLOCK = 128
ROPE_THETA = 10000.0
EPS = 1e-6
NEG_INF = -1e30
LANES = 128
VMEM_LIMIT = 56 * 1024 * 1024

IN_SIZES = (MLA_Q_RANK, MLA_KV_RANK, MLA_ROPE,
            GROUP_W, GROUP_W, GROUP_W, GROUP_W, GROUP_W,
            GROUP_W, GROUP_W, GROUP_W,
            GROUP_W, SWA_KV_HEADS * HEAD_DIM, SWA_KV_HEADS * HEAD_DIM)

_O_CQ = 0
_O_CKV = _O_CQ + MLA_Q_RANK
_O_KREP = _O_CKV + MLA_KV_RANK
_O_RQK = _O_KREP + N_HEADS * MLA_HEAD_PAD
_O_RV = _O_RQK + 2 * GROUP_W
_O_RG = _O_RV + GROUP_W
_O_NA = _O_RG + 2 * GROUP_W
_O_SQK = _O_NA + 3 * GROUP_W
_O_SV = _O_SQK + GROUP_W + SWA_KV_HEADS * HEAD_DIM
_IN_COLS = _O_SV + SWA_KV_HEADS * HEAD_DIM


def _cparams(sem):
    return pltpu.CompilerParams(dimension_semantics=sem, vmem_limit_bytes=VMEM_LIMIT)


def _dot(a, b):
    return jnp.dot(a, b, preferred_element_type=F32)


def _dot_nt(a, b):
    return lax.dot_general(a, b, (((1,), (1,)), ((), ())), preferred_element_type=F32)


def _dot_tn(a, b):
    return lax.dot_general(a, b, (((0,), (0,)), ((), ())), preferred_element_type=F32)


def _rms(x):
    return x * lax.rsqrt(jnp.mean(x * x, axis=-1, keepdims=True) + EPS)


def _silu(x):
    return x * jax.nn.sigmoid(x)


def _mod_kernel(c_ref, w_ref, b_ref, o_ref):
    o_ref[0] = _dot(_silu(c_ref[...]), w_ref[0]) + b_ref[0]


def _modulation(cond, ada_w, ada_b):
    depth, d, d6 = ada_w.shape
    n = d6 // d
    return pl.pallas_call(
        _mod_kernel,
        grid=(depth, n),
        in_specs=[pl.BlockSpec((8, d), lambda l, j: (0, 0)),
                  pl.BlockSpec((1, d, d), lambda l, j: (l, 0, j)),
                  pl.BlockSpec((1, 1, d), lambda l, j: (l, 0, j))],
        out_specs=pl.BlockSpec((1, 8, d), lambda l, j: (l, 0, j)),
        out_shape=jax.ShapeDtypeStruct((depth, 8, d6), F32),
        compiler_params=_cparams(("parallel", "parallel")),
        name="ada_modulation",
    )(cond, ada_w, ada_b.reshape(depth, 1, d6))


def _rope_tables(seq):
    t = jnp.arange(seq)
    row = (t // GRID_W).astype(F32)[:, None]
    col = (t % GRID_W).astype(F32)[:, None]

    def parts(d):
        inv = ROPE_THETA ** (-jnp.arange(0, d, 2, dtype=F32) / d)
        ar, ac = row * inv[None, :], col * inv[None, :]
        z = jnp.zeros_like(ar)
        cos = jnp.concatenate([jnp.cos(ar), jnp.cos(ar), jnp.cos(ac), jnp.cos(ac)], axis=-1)
        s_next = jnp.concatenate([-jnp.sin(ar), z, -jnp.sin(ac), z], axis=-1)
        s_prev = jnp.concatenate([z, jnp.sin(ar), z, jnp.sin(ac)], axis=-1)
        return cos, s_prev, s_next

    c, sp, sn = parts(HEAD_DIM // 2)
    t64 = tuple(jnp.concatenate([a, a], axis=-1) for a in (c, sp, sn))
    c, sp, sn = parts(MLA_ROPE // 2)
    ones = jnp.ones((seq, MLA_NOPE), F32)
    zn = jnp.zeros((seq, MLA_NOPE), F32)
    pad1 = jnp.ones((seq, MLA_HEAD_PAD - MLA_NOPE - MLA_ROPE), F32)
    pad0 = jnp.zeros_like(pad1)
    tm = (jnp.concatenate([ones, c, pad1], axis=-1),
          jnp.concatenate([zn, sp, pad0], axis=-1),
          jnp.concatenate([zn, sn, pad0], axis=-1))
    return t64 + tm


def _rope(x, cos, s_prev, s_next, d):
    out = []
    for j in range(x.shape[-1] // LANES):
        xc = x[:, j * LANES:(j + 1) * LANES]
        out.append(xc * cos + pltpu.roll(xc, d, 1) * s_prev + pltpu.roll(xc, LANES - d, 1) * s_next)
    return out[0] if len(out) == 1 else jnp.concatenate(out, axis=-1)


def _inproj_kernel(*refs, rotate, mla_scale):
    if rotate:
        (x_ref, g_ref, sc_ref, sh_ref, w_ref, qn_ref, kvn_ref, wuq_ref, wuk_ref, wuv_ref,
         c64_ref, p64_ref, n64_ref, cm_ref, pm_ref, nm_ref,
         mq_ref, mk_ref, mv_ref, rq_ref, rk_ref, rv_ref, rg_ref,
         nq_ref, nk_ref, nv_ref, sq_ref, sk_ref, sv_ref) = refs
    else:
        (x_ref, g_ref, sc_ref, sh_ref, w_ref, qn_ref, kvn_ref, wuq_ref, wuk_ref, wuv_ref,
         mq_ref, mk_ref, mv_ref, rq_ref, rk_ref, rv_ref, rg_ref,
         nq_ref, nk_ref, nv_ref, sq_ref, sk_ref, sv_ref) = refs

    x = x_ref[0]
    h = (_rms(x) * g_ref[...]) * (1.0 + sc_ref[0]) + sh_ref[0]
    hb = h.astype(BF16)

    def proj(lo, hi):
        return _dot(hb, w_ref[:, lo:hi])

    def rope64(v):
        if not rotate:
            return v
        return _rope(v, c64_ref[...], p64_ref[...], n64_ref[...], HEAD_DIM // 4)

    def rope_mla(v):
        if not rotate:
            return v
        return _rope(v, cm_ref[...], pm_ref[...], nm_ref[...], MLA_ROPE // 4)

    cq = (_rms(proj(_O_CQ, _O_CKV)) * qn_ref[...]).astype(BF16)
    q = rope_mla(_dot(cq, wuq_ref[...])) * mla_scale
    mq_ref[0] = q.astype(BF16)
    ckv = (_rms(proj(_O_CKV, _O_KREP)) * kvn_ref[...]).astype(BF16)
    k = rope_mla(_dot(ckv, wuk_ref[...]) + proj(_O_KREP, _O_RQK))
    mk_ref[0] = k.astype(BF16)
    v = _dot(ckv, wuv_ref[...])
    lane = lax.broadcasted_iota(jnp.int32, v.shape, 1)
    mv_ref[0] = jnp.where(lane % MLA_HEAD_PAD >= MLA_V, 1.0, v).astype(BF16)

    rqk = proj(_O_RQK, _O_RV)
    rq_ref[0] = rope64(rqk[:, :GROUP_W]).astype(BF16)
    rk_ref[0] = rope64(rqk[:, GROUP_W:]).astype(BF16)
    rv_ref[0] = proj(_O_RV, _O_RG).astype(BF16)
    rg_ref[0] = proj(_O_RG, _O_NA)

    na = proj(_O_NA, _O_SQK)
    nq_ref[0] = na[:, :GROUP_W].astype(BF16)
    nk_ref[0] = na[:, GROUP_W:2 * GROUP_W].astype(BF16)
    nv_ref[0] = na[:, 2 * GROUP_W:].astype(BF16)

    sqk = proj(_O_SQK, _O_SV)
    sq_ref[0] = rope64(sqk[:, :GROUP_W]).astype(BF16)
    sk = sqk[:, GROUP_W:]
    if rotate:
        sk = sk * c64_ref[...] + (pltpu.roll(sk, HEAD_DIM // 4, 1) * p64_ref[...]
                                  + pltpu.roll(sk, LANES - HEAD_DIM // 4, 1) * n64_ref[...])
    sk_ref[0] = sk.astype(BF16)
    sv_ref[0] = proj(_O_SV, _IN_COLS).astype(BF16)


def _const_spec(shape):
    nd = len(shape)
    return pl.BlockSpec(shape, lambda *_: (0,) * nd, pipeline_mode=pl.Buffered(1))


def _inproj(x, gain, scale, shift, w, qn, kvn, wuq, wuk, wuv, tables, tm):
    b, t, d = x.shape
    rotate = tables is not None
    kv_w = SWA_KV_HEADS * HEAD_DIM
    mla_w = N_HEADS * MLA_HEAD_PAD
    tok = lambda wd: pl.BlockSpec((1, tm, wd), lambda bi, i: (bi, i, 0))
    vec = pl.BlockSpec((1, 1, d), lambda bi, i: (bi, 0, 0))
    in_specs = [tok(d), _const_spec((1, d)), vec, vec, _const_spec(w.shape),
                _const_spec(qn.shape), _const_spec(kvn.shape), _const_spec(wuq.shape),
                _const_spec(wuk.shape), _const_spec(wuv.shape)]
    args = [x, gain, scale, shift, w, qn, kvn, wuq, wuk, wuv]
    if rotate:
        in_specs += [pl.BlockSpec((tm, LANES), lambda bi, i: (i, 0))] * 6
        args += list(tables)
    widths = [mla_w, mla_w, mla_w, GROUP_W, GROUP_W, GROUP_W, 2 * GROUP_W,
              GROUP_W, GROUP_W, GROUP_W, GROUP_W, kv_w, kv_w]
    dtypes = [BF16] * 6 + [F32] + [BF16] * 6
    return pl.pallas_call(
        functools.partial(_inproj_kernel, rotate=rotate, mla_scale=(MLA_NOPE + MLA_ROPE) ** -0.5),
        grid=(b, t // tm),
        in_specs=in_specs,
        out_specs=[tok(wd) for wd in widths],
        out_shape=[jax.ShapeDtypeStruct((b, t, wd), dt) for wd, dt in zip(widths, dtypes)],
        compiler_params=_cparams(("parallel", "parallel")),
        name="in_proj_rot" if rotate else "in_proj_ctx",
    )(*args)


def _mla_kernel(q_ref, kx_ref, vx_ref, ky_ref, vy_ref, o_ref, *, tk, heads):
    s_len = kx_ref.shape[1]
    tq = q_ref.shape[1]
    outs = []
    for h in range(heads):
        cols = slice(h * MLA_HEAD_PAD, (h + 1) * MLA_HEAD_PAD)
        q = q_ref[0, :, cols]

        def step(k, v, carry):
            m, acc = carry
            s = _dot_nt(q, k)
            m_new = jnp.maximum(m, jnp.max(s, axis=-1, keepdims=True))
            p = jnp.exp(s - m_new).astype(BF16)
            return m_new, acc * jnp.exp(m - m_new) + _dot(p, v)

        def body(j, carry):
            off = pl.multiple_of(j * tk, tk)
            return step(kx_ref[0, pl.ds(off, tk), cols], vx_ref[0, pl.ds(off, tk), cols], carry)

        init = (jnp.full((tq, 1), NEG_INF, F32), jnp.zeros((tq, MLA_HEAD_PAD), F32))
        carry = lax.fori_loop(0, s_len // tk, body, init)
        _, acc = step(ky_ref[0, :, cols], vy_ref[0, :, cols], carry)
        outs.append(acc[:, :MLA_V] / acc[:, MLA_V:MLA_V + 1])
    o_ref[0] = jnp.concatenate(outs, axis=-1).astype(o_ref.dtype)


def _mla_attention(q, kx, vx, ky, vy, tq, tk):
    b, s, _ = q.shape
    l = ky.shape[1]
    hp = 2
    wd = hp * MLA_HEAD_PAD
    return pl.pallas_call(
        functools.partial(_mla_kernel, tk=tk, heads=hp),
        grid=(b, N_HEADS // hp, s // tq),
        in_specs=[pl.BlockSpec((1, tq, wd), lambda bi, hi, i: (bi, i, hi)),
                  pl.BlockSpec((1, s, wd), lambda bi, hi, i: (bi, 0, hi)),
                  pl.BlockSpec((1, s, wd), lambda bi, hi, i: (bi, 0, hi)),
                  pl.BlockSpec((1, l, wd), lambda bi, hi, i: (bi, 0, hi)),
                  pl.BlockSpec((1, l, wd), lambda bi, hi, i: (bi, 0, hi))],
        out_specs=pl.BlockSpec((1, tq, hp * MLA_V), lambda bi, hi, i: (bi, i, hi)),
        out_shape=jax.ShapeDtypeStruct((b, s, N_HEADS * MLA_V), BF16),
        compiler_params=_cparams(("parallel", "parallel", "arbitrary")),
        name="mla_attention",
    )(q, kx, vx, ky, vy)


def _ctx_attn_kernel(sink_ref, q_ref, k_ref, v_ref, o_ref, *, heads):
    outs = []
    for (q0, q1, k0, k1, v0, v1, sink_idx) in heads:
        q = q_ref[0, :, q0:q1]
        s = _dot_nt(q, k_ref[0, :, k0:k1])
        m = jnp.max(s, axis=-1, keepdims=True)
        if sink_idx is not None:
            sink = jnp.full((1, 1), sink_ref[sink_idx], F32)
            m = jnp.maximum(m, sink)
        p = jnp.exp(s - m)
        l = jnp.sum(p, axis=-1, keepdims=True)
        if sink_idx is not None:
            l = l + jnp.exp(sink - m)
        outs.append(_dot(p.astype(BF16), v_ref[0, :, v0:v1]) / l)
    o_ref[0] = jnp.concatenate(outs, axis=-1).astype(o_ref.dtype)


def _ctx_attention(q, k, v, heads, sink, name):
    b, l, _ = q.shape
    full = lambda a: pl.BlockSpec((1, l, a.shape[-1]), lambda bi: (bi, 0, 0))
    return pl.pallas_call(
        functools.partial(_ctx_attn_kernel, heads=heads),
        grid=(b,),
        in_specs=[pl.BlockSpec(memory_space=pltpu.SMEM), full(q), full(k), full(v)],
        out_specs=pl.BlockSpec((1, l, GROUP_W), lambda bi: (bi, 0, 0)),
        out_shape=jax.ShapeDtypeStruct((b, l, GROUP_W), BF16),
        compiler_params=_cparams(("parallel",)),
        name=name,
    )(sink, q, k, v)


def _ret_kernel(*refs, chunk, direction, has_prev):
    if has_prev:
        (dec_ref, q_ref, k_ref, v_ref, g_ref, s0_ref, prev_ref, o_ref, sn_ref,
         st_ref, dm_ref, qd_ref, kd_ref, cd_ref) = refs
    else:
        (dec_ref, q_ref, k_ref, v_ref, g_ref, s0_ref, o_ref, sn_ref,
         st_ref, dm_ref, qd_ref, kd_ref, cd_ref) = refs
        prev_ref = None
    i = pl.program_id(1)
    c = chunk
    fwd = direction == 0

    @pl.when(i == 0)
    def _init():
        st_ref[...] = s0_ref[0]
        ii = lax.broadcasted_iota(jnp.int32, (c, c), 0).astype(F32)
        jj = lax.broadcasted_iota(jnp.int32, (c, c), 1).astype(F32)
        diff = (ii - jj) if fwd else (jj - ii)
        pos = lax.broadcasted_iota(jnp.int32, (c, HEAD_DIM), 0).astype(F32)
        for h in range(N_HEADS):
            dec = dec_ref[direction * N_HEADS + h]
            lg = jax.nn.log_sigmoid(jnp.full((c, c), dec, F32))
            dm_ref[h] = jnp.where(diff >= 0, jnp.exp(lg * jnp.maximum(diff, 0.0)), 0.0)
            lg64 = jax.nn.log_sigmoid(jnp.full((c, HEAD_DIM), dec, F32))
            if fwd:
                qd_ref[h] = jnp.exp(lg64 * (pos + 1.0))
                kd_ref[h] = jnp.exp(lg64 * (c - 1.0 - pos))
            else:
                qd_ref[h] = jnp.exp(lg64 * (c - pos))
                kd_ref[h] = jnp.exp(lg64 * pos)
            cd_ref[h] = jnp.exp(jax.nn.log_sigmoid(jnp.full((HEAD_DIM, HEAD_DIM), dec, F32)) * c)

    outs = []
    for h in range(N_HEADS):
        cols = slice(h * HEAD_DIM, (h + 1) * HEAD_DIM)
        q = q_ref[0, :, cols]
        k = k_ref[0, :, cols]
        v = v_ref[0, :, cols]
        state = st_ref[h]
        att = _dot_nt(q, k) * dm_ref[h]
        o = _dot(att.astype(BF16), v) + _dot(q, state.astype(BF16)) * qd_ref[h]
        kk = (k.astype(F32) * kd_ref[h]).astype(BF16)
        st_ref[h] = state * cd_ref[h] + _dot_tn(kk, v)
        outs.append(_rms(o) * _silu(g_ref[0, :, cols]))
    res = jnp.concatenate(outs, axis=-1)
    if has_prev:
        res = res + prev_ref[0]
    o_ref[0] = res.astype(o_ref.dtype)

    @pl.when(i == pl.num_programs(1) - 1)
    def _fin():
        sn_ref[0] = st_ref[...]


def _retention_pass(dec, q, k, v, gates, state0, prev, direction, chunk, out_dtype):
    b, t, _ = q.shape
    n = t // chunk
    if direction == 0:
        blk = lambda bi, i: (bi, i, 0)
        gblk = lambda bi, i: (bi, i, 0)
    else:
        blk = lambda bi, i: (bi, n - 1 - i, 0)
        gblk = lambda bi, i: (bi, n - 1 - i, 1)
    tok = pl.BlockSpec((1, chunk, GROUP_W), blk)
    st_spec = pl.BlockSpec((1, N_HEADS, HEAD_DIM, HEAD_DIM), lambda bi, i: (bi, 0, 0, 0))
    in_specs = [pl.BlockSpec(memory_space=pltpu.SMEM), tok, tok, tok,
                pl.BlockSpec((1, chunk, GROUP_W), gblk), st_spec]
    args = [dec, q, k, v, gates, state0]
    if prev is not None:
        in_specs.append(tok)
        args.append(prev)
    return pl.pallas_call(
        functools.partial(_ret_kernel, chunk=chunk, direction=direction, has_prev=prev is not None),
        grid=(b, n),
        in_specs=in_specs,
        out_specs=[tok, st_spec],
        out_shape=[jax.ShapeDtypeStruct((b, t, GROUP_W), out_dtype),
                   jax.ShapeDtypeStruct((b, N_HEADS, HEAD_DIM, HEAD_DIM), F32)],
        scratch_shapes=[pltpu.VMEM((N_HEADS, HEAD_DIM, HEAD_DIM), F32),
                        pltpu.VMEM((N_HEADS, chunk, chunk), F32),
                        pltpu.VMEM((N_HEADS, chunk, HEAD_DIM), F32),
                        pltpu.VMEM((N_HEADS, chunk, HEAD_DIM), F32),
                        pltpu.VMEM((N_HEADS, HEAD_DIM, HEAD_DIM), F32)],
        compiler_params=_cparams(("parallel", "arbitrary")),
        name="retention_fwd" if direction == 0 else "retention_bwd",
    )(*args)


def _retention(dec, xq, xk, xv, xg, yq, yk, yv, yg, chunk):
    b = xq.shape[0]
    zero = jnp.zeros((b, N_HEADS, HEAD_DIM, HEAD_DIM), F32)
    ychunk = min(chunk, yq.shape[1])
    yb, sb = _retention_pass(dec, yq, yk, yv, yg, zero, None, 1, ychunk, F32)
    y, sf = _retention_pass(dec, yq, yk, yv, yg, zero, yb, 0, ychunk, BF16)
    xb, _ = _retention_pass(dec, xq, xk, xv, xg, sb, None, 1, chunk, F32)
    x, _ = _retention_pass(dec, xq, xk, xv, xg, sf, xb, 0, chunk, BF16)
    return x, y


def _na_bias_kernel(rpb_ref, o_ref):
    h = pl.program_id(0)
    dr0 = pl.program_id(1)
    c = lax.broadcasted_iota(jnp.int32, (GRID_W, GRID_W), 0)
    kc = lax.broadcasted_iota(jnp.int32, (GRID_W, GRID_W), 1)
    c0 = jnp.clip(c - NA_KC // 2, 0, GRID_W - NA_KC)
    col_in = (kc >= c0) & (kc < c0 + NA_KC)
    dc = jnp.clip(kc - c, -(NA_KC - 1), NA_KC - 1) + NA_KC - 1
    n_dc = 2 * NA_KC - 1
    for j in range(NA_KR):
        base = (h * (2 * NA_KR - 1) + dr0 + j) * n_dc
        acc = jnp.zeros((GRID_W, GRID_W), F32)
        for d in range(n_dc):
            acc = jnp.where(dc == d, rpb_ref[base + d], acc)
        o_ref[0, 0, :, j * GRID_W:(j + 1) * GRID_W] = jnp.where(col_in, acc, NEG_INF)


def _na_bias_table(rpb):
    return pl.pallas_call(
        _na_bias_kernel,
        grid=(N_HEADS, NA_KR),
        in_specs=[pl.BlockSpec(memory_space=pltpu.SMEM)],
        out_specs=pl.BlockSpec((1, 1, GRID_W, NA_KR * GRID_W), lambda h, r: (h, r, 0, 0)),
        out_shape=jax.ShapeDtypeStruct((N_HEADS, NA_KR, GRID_W, NA_KR * GRID_W), F32),
        compiler_params=_cparams(("parallel", "parallel")),
        name="na_bias_table",
    )(rpb.reshape(-1))


def _na_kernel(q_ref, k_ref, v_ref, ky_ref, vy_ref, tb_ref, o_ref, *, rows_per_step, n_rows):
    r_base = pl.program_id(1) * rows_per_step
    win = NA_KR * GRID_W

    def row_body(i, carry):
        r = r_base + i
        r0 = jnp.clip(r - NA_KR // 2, 0, n_rows - NA_KR)
        dr0 = r0 - r + NA_KR - 1
        koff = pl.multiple_of(r0 * GRID_W, GRID_W)
        qoff = pl.multiple_of(i * GRID_W, GRID_W)
        outs = []
        for h in range(N_HEADS):
            cols = slice(h * HEAD_DIM, (h + 1) * HEAD_DIM)
            q = q_ref[0, pl.ds(qoff, GRID_W), cols]
            s = _dot_nt(q, k_ref[0, pl.ds(koff, win), cols]) + tb_ref[h, dr0]
            sc = _dot_nt(q, ky_ref[0, :, cols])
            m = jnp.maximum(jnp.max(s, axis=-1, keepdims=True), jnp.max(sc, axis=-1, keepdims=True))
            p = jnp.exp(s - m)
            pc = jnp.exp(sc - m)
            l = jnp.sum(p, axis=-1, keepdims=True) + jnp.sum(pc, axis=-1, keepdims=True)
            o = _dot(p.astype(BF16), v_ref[0, pl.ds(koff, win), cols]) + _dot(pc.astype(BF16), vy_ref[0, :, cols])
            outs.append(o / l)
        o_ref[0, pl.ds(qoff, GRID_W), :] = jnp.concatenate(outs, axis=-1).astype(o_ref.dtype)
        return carry

    lax.fori_loop(0, rows_per_step, row_body, 0)


def _na_attention(q, k, v, ky, vy, table, rows_per_step):
    b, s, _ = q.shape
    l = ky.shape[1]
    n_rows = s // GRID_W
    tq = rows_per_step * GRID_W
    seq = lambda n: pl.BlockSpec((1, n, GROUP_W), lambda bi, i: (bi, 0, 0))
    return pl.pallas_call(
        functools.partial(_na_kernel, rows_per_step=rows_per_step, n_rows=n_rows),
        grid=(b, n_rows // rows_per_step),
        in_specs=[pl.BlockSpec((1, tq, GROUP_W), lambda bi, i: (bi, i, 0)),
                  seq(s), seq(s), seq(l), seq(l), _const_spec(table.shape)],
        out_specs=pl.BlockSpec((1, tq, GROUP_W), lambda bi, i: (bi, i, 0)),
        out_shape=jax.ShapeDtypeStruct((b, s, GROUP_W), BF16),
        compiler_params=_cparams(("parallel", "arbitrary")),
        name="na_attention",
    )(q, k, v, ky, vy, table)


def _swa_kernel(sink_ref, q_ref, kp_ref, kc_ref, kn_ref, vp_ref, vc_ref, vn_ref, ky_ref, vy_ref, o_ref):
    n = pl.program_id(1)
    nb = pl.num_programs(1)
    bl = SWA_BLOCK
    g = N_HEADS // SWA_KV_HEADS
    qi = lax.broadcasted_iota(jnp.int32, (g * bl, 3 * bl), 0) % bl
    j = lax.broadcasted_iota(jnp.int32, (g * bl, 3 * bl), 1)
    delta = j - bl - qi
    blk = n - 1 + j // bl
    valid = (jnp.abs(delta) <= SWA_WINDOW) & (blk >= 0) & (blk < nb)
    half = lax.broadcasted_iota(jnp.int32, (g * bl, 1), 0) // bl
    outs = [None] * N_HEADS
    for kh in range(SWA_KV_HEADS):
        kcols = slice(kh * HEAD_DIM, (kh + 1) * HEAD_DIM)
        q = jnp.concatenate([q_ref[0, :, (kh * g + gi) * HEAD_DIM:(kh * g + gi + 1) * HEAD_DIM]
                             for gi in range(g)], axis=0)
        kw = jnp.concatenate([kp_ref[0, :, kcols], kc_ref[0, :, kcols], kn_ref[0, :, kcols]], axis=0)
        vw = jnp.concatenate([vp_ref[0, :, kcols], vc_ref[0, :, kcols], vn_ref[0, :, kcols]], axis=0)
        s = jnp.where(valid, _dot_nt(q, kw), NEG_INF)
        sc = _dot_nt(q, ky_ref[0, :, kcols])
        sink = jnp.full((g * bl, 1), sink_ref[kh * g], F32)
        for gi in range(1, g):
            sink = jnp.where(half == gi, sink_ref[kh * g + gi], sink)
        m = jnp.maximum(jnp.maximum(jnp.max(s, axis=-1, keepdims=True),
                                    jnp.max(sc, axis=-1, keepdims=True)), sink)
        p = jnp.exp(s - m)
        pc = jnp.exp(sc - m)
        l = (jnp.sum(p, axis=-1, keepdims=True) + jnp.sum(pc, axis=-1, keepdims=True)
             + jnp.exp(sink - m))
        o = (_dot(p.astype(BF16), vw) + _dot(pc.astype(BF16), vy_ref[0, :, kcols])) / l
        for gi in range(g):
            outs[kh * g + gi] = o[gi * bl:(gi + 1) * bl]
    o_ref[0] = jnp.concatenate(outs, axis=-1).astype(o_ref.dtype)


def _swa_attention(sink, q, k, v, ky, vy):
    b, s, _ = q.shape
    l = ky.shape[1]
    nb = s // SWA_BLOCK
    kvw = SWA_KV_HEADS * HEAD_DIM
    prev = pl.BlockSpec((1, SWA_BLOCK, kvw), lambda bi, i: (bi, jnp.maximum(i - 1, 0), 0))
    cur = pl.BlockSpec((1, SWA_BLOCK, kvw), lambda bi, i: (bi, i, 0))
    nxt = pl.BlockSpec((1, SWA_BLOCK, kvw), lambda bi, i: (bi, jnp.minimum(i + 1, nb - 1), 0))
    ctx = pl.BlockSpec((1, l, kvw), lambda bi, i: (bi, 0, 0))
    return pl.pallas_call(
        _swa_kernel,
        grid=(b, nb),
        in_specs=[pl.BlockSpec(memory_space=pltpu.SMEM),
                  pl.BlockSpec((1, SWA_BLOCK, GROUP_W), lambda bi, i: (bi, i, 0)),
                  prev, cur, nxt, prev, cur, nxt, ctx, ctx],
        out_specs=pl.BlockSpec((1, SWA_BLOCK, GROUP_W), lambda bi, i: (bi, i, 0)),
        out_shape=jax.ShapeDtypeStruct((b, s, GROUP_W), BF16),
        compiler_params=_cparams(("parallel", "arbitrary")),
        name="swa_attention",
    )(sink, q, k, k, k, v, v, v, ky, vy)


def _outffn_kernel(*refs, hidden_chunk, final):
    if final:
        (x_ref, m0_ref, m1_ref, m2_ref, m3_ref, wo_ref, g1_ref, n2_ref, sc_ref, sh_ref, g2_ref,
         w1_ref, w3_ref, w2_ref, fg_ref, o_ref) = refs
    else:
        (x_ref, m0_ref, m1_ref, m2_ref, m3_ref, wo_ref, g1_ref, n2_ref, sc_ref, sh_ref, g2_ref,
         w1_ref, w3_ref, w2_ref, o_ref) = refs
    mix = None
    for gi, m_ref in enumerate((m0_ref, m1_ref, m2_ref, m3_ref)):
        part = _dot(m_ref[0], wo_ref[gi * GROUP_W:(gi + 1) * GROUP_W, :])
        mix = part if mix is None else mix + part
    x1 = x_ref[0] + g1_ref[0] * mix
    hb = ((_rms(x1) * n2_ref[...]) * (1.0 + sc_ref[0]) + sh_ref[0]).astype(BF16)
    hidden = w1_ref.shape[1]
    acc = None
    for c0 in range(0, hidden, hidden_chunk):
        a = _dot(hb, w1_ref[:, c0:c0 + hidden_chunk])
        bgate = _dot(hb, w3_ref[:, c0:c0 + hidden_chunk])
        u = (_silu(a) * bgate).astype(BF16)
        part = _dot(u, w2_ref[c0:c0 + hidden_chunk, :])
        acc = part if acc is None else acc + part
    x2 = x1 + g2_ref[0] * acc
    if final:
        x2 = _rms(x2) * fg_ref[...]
    o_ref[0] = x2


def _outffn(x, mixes, wo, g1, n2, sc2, sh2, g2, w1, w3, w2, final_g, tm):
    b, t, d = x.shape
    tok = lambda wd: pl.BlockSpec((1, tm, wd), lambda bi, i: (bi, i, 0))
    vec = pl.BlockSpec((1, 1, d), lambda bi, i: (bi, 0, 0))
    in_specs = ([tok(d)] + [tok(GROUP_W)] * 4
                + [_const_spec(wo.shape), vec, _const_spec((1, d)), vec, vec, vec,
                   _const_spec(w1.shape), _const_spec(w3.shape), _const_spec(w2.shape)])
    args = [x, *mixes, wo, g1, n2, sc2, sh2, g2, w1, w3, w2]
    final = final_g is not None
    if final:
        in_specs.append(_const_spec((1, d)))
        args.append(final_g)
    return pl.pallas_call(
        functools.partial(_outffn_kernel, hidden_chunk=256, final=final),
        grid=(b, t // tm),
        in_specs=in_specs,
        out_specs=tok(d),
        out_shape=jax.ShapeDtypeStruct((b, t, d), F32),
        compiler_params=_cparams(("parallel", "parallel")),
        name="out_proj_ffn_final" if final else "out_proj_ffn",
    )(*args)


def _prep_weights(w_in, mla_w_uq, mla_w_ukv):
    depth, d, _ = w_in.shape
    offs = [0]
    for sz in IN_SIZES:
        offs.append(offs[-1] + sz)
    cols = [w_in[:, :, offs[i]:offs[i + 1]] for i in range(len(IN_SIZES))]
    cq, ckv, kr, rq, rk, rv, rgf, rgb, nq, nk, nv, sq, sk, sv = cols
    scale = HEAD_DIM ** -0.5
    zpad = jnp.zeros((depth, d, MLA_HEAD_PAD - MLA_NOPE - MLA_ROPE), F32)
    znope = jnp.zeros((depth, d, MLA_NOPE), F32)
    krep = jnp.concatenate([znope, kr, zpad] * N_HEADS, axis=-1)
    w = jnp.concatenate([cq, ckv, krep, rq, rk * scale, rv, rgf, rgb, nq * scale, nk, nv,
                         sq * scale, sk, sv], axis=-1).astype(BF16)

    qr = mla_w_uq.shape[1]
    uq = mla_w_uq.reshape(depth, qr, N_HEADS, MLA_NOPE + MLA_ROPE)
    wuq = jnp.concatenate([uq, jnp.zeros((depth, qr, N_HEADS, MLA_HEAD_PAD - MLA_NOPE - MLA_ROPE), F32)],
                          axis=-1).reshape(depth, qr, N_HEADS * MLA_HEAD_PAD).astype(BF16)
    kvr = mla_w_ukv.shape[1]
    ukv = mla_w_ukv.reshape(depth, kvr, N_HEADS, MLA_NOPE + MLA_V)
    zk = jnp.zeros((depth, kvr, N_HEADS, MLA_HEAD_PAD - MLA_NOPE), F32)
    wuk = jnp.concatenate([ukv[..., :MLA_NOPE], zk], axis=-1).reshape(depth, kvr, -1).astype(BF16)
    zv = jnp.zeros((depth, kvr, N_HEADS, MLA_HEAD_PAD - MLA_V), F32)
    wuv = jnp.concatenate([ukv[..., MLA_NOPE:], zv], axis=-1).reshape(depth, kvr, -1).astype(BF16)
    return w, wuq, wuk, wuv


def _ctx_head_specs():
    mla = tuple((h * MLA_HEAD_PAD, (h + 1) * MLA_HEAD_PAD, h * MLA_HEAD_PAD, (h + 1) * MLA_HEAD_PAD,
                 h * MLA_HEAD_PAD, h * MLA_HEAD_PAD + MLA_V, None) for h in range(N_HEADS))
    na = tuple((h * HEAD_DIM, (h + 1) * HEAD_DIM) * 3 + (None,) for h in range(N_HEADS))
    g = N_HEADS // SWA_KV_HEADS
    swa = tuple((h * HEAD_DIM, (h + 1) * HEAD_DIM, (h // g) * HEAD_DIM, (h // g + 1) * HEAD_DIM,
                 (h // g) * HEAD_DIM, (h // g + 1) * HEAD_DIM, h) for h in range(N_HEADS))
    return mla, na, swa


def kernel(x, c, ctx, c_ctx, ada_w, ada_b, norm1_g, w_in, mla_q_norm, mla_w_uq, mla_kv_norm, mla_w_ukv,
           ret_decay, na_rpb, swa_sink, w_out, norm2_g, ffn_w1, ffn_w3, ffn_w2, final_norm_g):
    b, s, d = x.shape
    l_ctx = ctx.shape[1]
    depth = ada_w.shape[0]
    assert b + 1 <= 8 and s % 512 == 0 and (s // GRID_W) % 8 == 0 and l_ctx % 128 == 0

    cond = jnp.concatenate([c, c_ctx[None, :], jnp.zeros((8 - b - 1, d), F32)], axis=0)
    mod = _modulation(cond, ada_w, ada_b)
    tables = _rope_tables(s)
    w_all, wuq_all, wuk_all, wuv_all = _prep_weights(w_in, mla_w_uq, mla_w_ukv)
    wo_all = w_out.astype(BF16)
    w1_all, w3_all, w2_all = ffn_w1.astype(BF16), ffn_w3.astype(BF16), ffn_w2.astype(BF16)
    mla_heads, na_heads, swa_heads = _ctx_head_specs()
    no_sink = jnp.zeros((N_HEADS,), F32)

    tm_x = 512
    tm_y = min(256, l_ctx)
    y = ctx
    for l in range(depth):
        mx = [mod[l, :b, j * d:(j + 1) * d][:, None, :] for j in range(6)]
        my = [jnp.broadcast_to(mod[l, b, j * d:(j + 1) * d][None, None, :], (b, 1, d)) for j in range(6)]
        n1 = norm1_g[l][None, :]
        n2 = norm2_g[l][None, :]
        qn = mla_q_norm[l][None, :]
        kvn = mla_kv_norm[l][None, :]
        lw = (w_all[l], qn, kvn, wuq_all[l], wuk_all[l], wuv_all[l])

        px = _inproj(x, n1, mx[1], mx[0], *lw, tables, tm_x)
        py = _inproj(y, n1, my[1], my[0], *lw, None, tm_y)
        (xmq, xmk, xmv, xrq, xrk, xrv, xrg, xnq, xnk, xnv, xsq, xsk, xsv) = px
        (ymq, ymk, ymv, yrq, yrk, yrv, yrg, ynq, ynk, ynv, ysq, ysk, ysv) = py

        mla_x = _mla_attention(xmq, xmk, xmv, ymk, ymv, tq=256, tk=512)
        dec = ret_decay[l].reshape(-1)
        ret_x, ret_y = _retention(dec, xrq, xrk, xrv, xrg, yrq, yrk, yrv, yrg, chunk=128)
        table = _na_bias_table(na_rpb[l])
        na_x = _na_attention(xnq, xnk, xnv, ynk, ynv, table, rows_per_step=8)
        swa_x = _swa_attention(swa_sink[l], xsq, xsk, xsv, ysk, ysv)

        last = l == depth - 1
        x = _outffn(x, (mla_x, ret_x, na_x, swa_x), wo_all[l], mx[2], n2, mx[4], mx[3], mx[5],
                    w1_all[l], w3_all[l], w2_all[l], final_norm_g[None, :] if last else None, tm_x)
        if not last:
            mla_y = _ctx_attention(ymq, ymk, ymv, mla_heads, no_sink, "mla_ctx_attention")
            na_y = _ctx_attention(ynq, ynk, ynv, na_heads, no_sink, "na_ctx_attention")
            swa_y = _ctx_attention(ysq, ysk, ysv, swa_heads, swa_sink[l], "swa_ctx_attention")
            y = _outffn(y, (mla_y, ret_y, na_y, swa_y), wo_all[l], my[2], n2, my[4], my[3], my[5],
                        w1_all[l], w3_all[l], w2_all[l], None, tm_y)
    return x
```

```python
import functools

import jax
import jax.numpy as jnp
from jax import lax
from jax.experimental import pallas as pl
from jax.experimental.pallas import tpu as pltpu

F32 = jnp.float32
BF16 = jnp.bfloat16

GRID_W = 64
HEAD_DIM = 64
N_HEADS = 4
GROUP_W = N_HEADS * HEAD_DIM
MLA_Q_RANK = 256
MLA_KV_RANK = 128
MLA_NOPE = 64
MLA_ROPE = 32
MLA_V = 64
MLA_HEAD_PAD = 128
NA_KR = 8
NA_KC = 16
SWA_KV_HEADS = 2
SWA_WINDOW = 128
SWA_BLOCK = 128
ROPE_THETA = 10000.0
EPS = 1e-6
NEG_INF = -1e30
LOG2_E = 1.4426950408889634
LANES = 128
VMEM_LIMIT = 56 * 1024 * 1024

IN_SIZES = (MLA_Q_RANK, MLA_KV_RANK, MLA_ROPE,
            GROUP_W, GROUP_W, GROUP_W, GROUP_W, GROUP_W,
            GROUP_W, GROUP_W, GROUP_W,
            GROUP_W, SWA_KV_HEADS * HEAD_DIM, SWA_KV_HEADS * HEAD_DIM)

_O_CQ = 0
_O_CKV = _O_CQ + MLA_Q_RANK
_O_KREP = _O_CKV + MLA_KV_RANK
_O_RQK = _O_KREP + N_HEADS * MLA_HEAD_PAD
_O_RV = _O_RQK + 2 * GROUP_W
_O_RG = _O_RV + GROUP_W
_O_NA = _O_RG + 2 * GROUP_W
_O_SQK = _O_NA + 3 * GROUP_W
_O_SV = _O_SQK + GROUP_W + SWA_KV_HEADS * HEAD_DIM
_IN_COLS = _O_SV + SWA_KV_HEADS * HEAD_DIM


def _cparams(sem):
    return pltpu.CompilerParams(dimension_semantics=sem, vmem_limit_bytes=VMEM_LIMIT)


def _dot(a, b):
    return jnp.dot(a, b, preferred_element_type=F32)


def _dot_nt(a, b):
    return lax.dot_general(a, b, (((1,), (1,)), ((), ())), preferred_element_type=F32)


def _dot_tn(a, b):
    return lax.dot_general(a, b, (((0,), (0,)), ((), ())), preferred_element_type=F32)


def _rms(x):
    return x * lax.rsqrt(jnp.mean(x * x, axis=-1, keepdims=True) + EPS)


def _silu(x):
    return x * jax.nn.sigmoid(x)


def _mod_kernel(c_ref, w_ref, b_ref, o_ref):
    o_ref[0] = _dot(_silu(c_ref[...]), w_ref[0]) + b_ref[0]


def _modulation(cond, ada_w, ada_b):
    depth, d, d6 = ada_w.shape
    n = d6 // d
    return pl.pallas_call(
        _mod_kernel,
        grid=(depth, n),
        in_specs=[pl.BlockSpec((8, d), lambda l, j: (0, 0)),
                  pl.BlockSpec((1, d, d), lambda l, j: (l, 0, j)),
                  pl.BlockSpec((1, 1, d), lambda l, j: (l, 0, j))],
        out_specs=pl.BlockSpec((1, 8, d), lambda l, j: (l, 0, j)),
        out_shape=jax.ShapeDtypeStruct((depth, 8, d6), F32),
        compiler_params=_cparams(("parallel", "parallel")),
        name="ada_modulation",
    )(cond, ada_w, ada_b.reshape(depth, 1, d6))


def _rope_tables(seq):
    t = jnp.arange(seq)
    row = (t // GRID_W).astype(F32)[:, None]
    col = (t % GRID_W).astype(F32)[:, None]

    def parts(d):
        inv = ROPE_THETA ** (-jnp.arange(0, d, 2, dtype=F32) / d)
        ar, ac = row * inv[None, :], col * inv[None, :]
        z = jnp.zeros_like(ar)
        cos = jnp.concatenate([jnp.cos(ar), jnp.cos(ar), jnp.cos(ac), jnp.cos(ac)], axis=-1)
        s_next = jnp.concatenate([-jnp.sin(ar), z, -jnp.sin(ac), z], axis=-1)
        s_prev = jnp.concatenate([z, jnp.sin(ar), z, jnp.sin(ac)], axis=-1)
        return cos, s_prev, s_next

    c, sp, sn = parts(HEAD_DIM // 2)
    t64 = tuple(jnp.concatenate([a, a], axis=-1) for a in (c, sp, sn))
    c, sp, sn = parts(MLA_ROPE // 2)
    ones = jnp.ones((seq, MLA_NOPE), F32)
    zn = jnp.zeros((seq, MLA_NOPE), F32)
    pad1 = jnp.ones((seq, MLA_HEAD_PAD - MLA_NOPE - MLA_ROPE), F32)
    pad0 = jnp.zeros_like(pad1)
    tm = (jnp.concatenate([ones, c, pad1], axis=-1),
          jnp.concatenate([zn, sp, pad0], axis=-1),
          jnp.concatenate([zn, sn, pad0], axis=-1))
    return t64 + tm


def _rope(x, cos, s_prev, s_next, d):
    out = []
    for j in range(x.shape[-1] // LANES):
        xc = x[:, j * LANES:(j + 1) * LANES]
        out.append(xc * cos + pltpu.roll(xc, d, 1) * s_prev + pltpu.roll(xc, LANES - d, 1) * s_next)
    return out[0] if len(out) == 1 else jnp.concatenate(out, axis=-1)


def _inproj_kernel(*refs, rotate, mla_scale):
    if rotate:
        (x_ref, g_ref, sc_ref, sh_ref, w_ref, qn_ref, kvn_ref, wuq_ref, wuk_ref, wuv_ref,
         c64_ref, p64_ref, n64_ref, cm_ref, pm_ref, nm_ref,
         mq_ref, mk_ref, mv_ref, rq_ref, rk_ref, rv_ref, rg_ref,
         nq_ref, nk_ref, nv_ref, sq_ref, sk_ref, sv_ref) = refs
    else:
        (x_ref, g_ref, sc_ref, sh_ref, w_ref, qn_ref, kvn_ref, wuq_ref, wuk_ref, wuv_ref,
         mq_ref, mk_ref, mv_ref, rq_ref, rk_ref, rv_ref, rg_ref,
         nq_ref, nk_ref, nv_ref, sq_ref, sk_ref, sv_ref) = refs

    x = x_ref[0]
    h = (_rms(x) * g_ref[...]) * (1.0 + sc_ref[0]) + sh_ref[0]
    hb = h.astype(BF16)

    def proj(lo, hi):
        return _dot(hb, w_ref[:, lo:hi])

    def rope64(v):
        if not rotate:
            return v
        return _rope(v, c64_ref[...], p64_ref[...], n64_ref[...], HEAD_DIM // 4)

    def rope_mla(v):
        if not rotate:
            return v
        return _rope(v, cm_ref[...], pm_ref[...], nm_ref[...], MLA_ROPE // 4)

    cq = (_rms(proj(_O_CQ, _O_CKV)) * qn_ref[...]).astype(BF16)
    q = rope_mla(_dot(cq, wuq_ref[...])) * mla_scale
    mq_ref[0] = q.astype(BF16)
    ckv = (_rms(proj(_O_CKV, _O_KREP)) * kvn_ref[...]).astype(BF16)
    k = rope_mla(_dot(ckv, wuk_ref[...]) + proj(_O_KREP, _O_RQK))
    mk_ref[0] = k.astype(BF16)
    v = _dot(ckv, wuv_ref[...])
    lane = lax.broadcasted_iota(jnp.int32, v.shape, 1)
    mv_ref[0] = jnp.where(lane % MLA_HEAD_PAD >= MLA_V, 1.0, v).astype(BF16)

    rqk = proj(_O_RQK, _O_RV)
    rq_ref[0] = rope64(rqk[:, :GROUP_W]).astype(BF16)
    rk_ref[0] = rope64(rqk[:, GROUP_W:]).astype(BF16)
    rv_ref[0] = proj(_O_RV, _O_RG).astype(BF16)
    rg_ref[0] = proj(_O_RG, _O_NA)

    na = proj(_O_NA, _O_SQK)
    nq_ref[0] = na[:, :GROUP_W].astype(BF16)
    nk_ref[0] = na[:, GROUP_W:2 * GROUP_W].astype(BF16)
    nv_ref[0] = na[:, 2 * GROUP_W:].astype(BF16)

    sqk = proj(_O_SQK, _O_SV)
    sq_ref[0] = rope64(sqk[:, :GROUP_W]).astype(BF16)
    sk = sqk[:, GROUP_W:]
    if rotate:
        sk = sk * c64_ref[...] + (pltpu.roll(sk, HEAD_DIM // 4, 1) * p64_ref[...]
                                  + pltpu.roll(sk, LANES - HEAD_DIM // 4, 1) * n64_ref[...])
    sk_ref[0] = sk.astype(BF16)
    sv_ref[0] = proj(_O_SV, _IN_COLS).astype(BF16)


def _const_spec(shape):
    nd = len(shape)
    return pl.BlockSpec(shape, lambda *_: (0,) * nd, pipeline_mode=pl.Buffered(1))


def _inproj(x, gain, scale, shift, w, qn, kvn, wuq, wuk, wuv, tables, tm):
    b, t, d = x.shape
    rotate = tables is not None
    kv_w = SWA_KV_HEADS * HEAD_DIM
    mla_w = N_HEADS * MLA_HEAD_PAD
    tok = lambda wd: pl.BlockSpec((1, tm, wd), lambda bi, i: (bi, i, 0))
    vec = pl.BlockSpec((1, 1, d), lambda bi, i: (bi, 0, 0))
    in_specs = [tok(d), _const_spec((1, d)), vec, vec, _const_spec(w.shape),
                _const_spec(qn.shape), _const_spec(kvn.shape), _const_spec(wuq.shape),
                _const_spec(wuk.shape), _const_spec(wuv.shape)]
    args = [x, gain, scale, shift, w, qn, kvn, wuq, wuk, wuv]
    if rotate:
        in_specs += [pl.BlockSpec((tm, LANES), lambda bi, i: (i, 0))] * 6
        args += list(tables)
    widths = [mla_w, mla_w, mla_w, GROUP_W, GROUP_W, GROUP_W, 2 * GROUP_W,
              GROUP_W, GROUP_W, GROUP_W, GROUP_W, kv_w, kv_w]
    dtypes = [BF16] * 6 + [F32] + [BF16] * 6
    return pl.pallas_call(
        functools.partial(_inproj_kernel, rotate=rotate, mla_scale=(MLA_NOPE + MLA_ROPE) ** -0.5 * LOG2_E),
        grid=(b, t // tm),
        in_specs=in_specs,
        out_specs=[tok(wd) for wd in widths],
        out_shape=[jax.ShapeDtypeStruct((b, t, wd), dt) for wd, dt in zip(widths, dtypes)],
        compiler_params=_cparams(("parallel", "parallel")),
        name="in_proj_rot" if rotate else "in_proj_ctx",
    )(*args)


def _mla_kernel(q_ref, kx_ref, vx_ref, ky_ref, vy_ref, o_ref, m_ref, acc_ref, sa_ref, sb_ref, sc_ref,
                *, tk, heads, n_sub):
    n_chunks = kx_ref.shape[1] // tk
    tq = q_ref.shape[1]
    ts = tq // n_sub
    chains = [(h, u) for h in range(heads) for u in range(n_sub)]
    for c in range(len(chains)):
        m_ref[c] = jnp.full(m_ref.shape[1:], NEG_INF, F32)
        acc_ref[c] = jnp.zeros(acc_ref.shape[1:], F32)

    def rows(h):
        return slice(h * MLA_HEAD_PAD, (h + 1) * MLA_HEAD_PAD)

    def scores(c, k, dst_ref):
        h, u = chains[c]
        dst_ref[c] = _dot_nt(k, q_ref[0, u * ts:(u + 1) * ts, rows(h)])

    def absorb(c, src_ref, vt):
        st = src_ref[c]
        m_old = m_ref[c]
        m_new = jnp.maximum(m_old, jnp.max(st, axis=0, keepdims=True))
        pt = jnp.exp2(st - m_new).astype(BF16)
        acc_ref[c] = acc_ref[c] * jnp.exp2(m_old - m_new) + _dot(vt, pt)
        m_ref[c] = m_new

    def kx(j, h):
        return kx_ref[0, pl.ds(pl.multiple_of(j * tk, tk), tk), rows(h)]

    def stage(j_next, next_ref, j_cur, cur_ref):
        for c, (h, _) in enumerate(chains):
            scores(c, kx(j_next, h), next_ref)
            absorb(c, cur_ref, vx_ref[0, j_cur, rows(h), :])

    for c, (h, _) in enumerate(chains):
        scores(c, kx(0, h), sa_ref)

    def body(jj, carry):
        stage(2 * jj + 1, sb_ref, 2 * jj, sa_ref)
        stage(2 * jj + 2, sa_ref, 2 * jj + 1, sb_ref)
        return carry

    lax.fori_loop(0, n_chunks // 2 - 1, body, 0)
    stage(n_chunks - 1, sb_ref, n_chunks - 2, sa_ref)
    for c, (h, _) in enumerate(chains):
        scores(c, ky_ref[0, :, rows(h)], sc_ref)
        absorb(c, sb_ref, vx_ref[0, n_chunks - 1, rows(h), :])
    outs = [[None] * n_sub for _ in range(heads)]
    for c, (h, u) in enumerate(chains):
        absorb(c, sc_ref, vy_ref[0, 0, rows(h), :])
        acc = acc_ref[c].T
        outs[h][u] = acc[:, :MLA_V] / acc[:, MLA_V:MLA_V + 1]
    o_ref[0] = jnp.concatenate([jnp.concatenate(outs[h], axis=0) for h in range(heads)],
                               axis=-1).astype(o_ref.dtype)


def _mla_attention(q, kx, vx, ky, vy, tq, tk, n_sub):
    b, s, _ = q.shape
    l = ky.shape[1]
    hp = 2
    wd = hp * MLA_HEAD_PAD
    vxt = jnp.swapaxes(vx.reshape(b, s // tk, tk, -1), 2, 3)
    vyt = jnp.swapaxes(vy.reshape(b, 1, l, -1), 2, 3)
    ts = tq // n_sub
    return pl.pallas_call(
        functools.partial(_mla_kernel, tk=tk, heads=hp, n_sub=n_sub),
        grid=(b, N_HEADS // hp, s // tq),
        in_specs=[pl.BlockSpec((1, tq, wd), lambda bi, hi, i: (bi, i, hi)),
                  pl.BlockSpec((1, s, wd), lambda bi, hi, i: (bi, 0, hi)),
                  pl.BlockSpec((1, s // tk, wd, tk), lambda bi, hi, i: (bi, 0, hi, 0)),
                  pl.BlockSpec((1, l, wd), lambda bi, hi, i: (bi, 0, hi)),
                  pl.BlockSpec((1, 1, wd, l), lambda bi, hi, i: (bi, 0, hi, 0))],
        out_specs=pl.BlockSpec((1, tq, hp * MLA_V), lambda bi, hi, i: (bi, i, hi)),
        out_shape=jax.ShapeDtypeStruct((b, s, N_HEADS * MLA_V), BF16),
        scratch_shapes=[pltpu.VMEM((hp * n_sub, 1, ts), F32),
                        pltpu.VMEM((hp * n_sub, MLA_HEAD_PAD, ts), F32),
                        pltpu.VMEM((hp * n_sub, tk, ts), F32),
                        pltpu.VMEM((hp * n_sub, tk, ts), F32),
                        pltpu.VMEM((hp * n_sub, l, ts), F32)],
        compiler_params=_cparams(("parallel", "parallel", "arbitrary")),
        name="mla_attention",
    )(q, kx, vxt, ky, vyt)


def _ctx_attn_kernel(sink_ref, q_ref, k_ref, v_ref, o_ref, *, heads, base2):
    exp = jnp.exp2 if base2 else jnp.exp
    outs = []
    for (q0, q1, k0, k1, v0, v1, sink_idx) in heads:
        q = q_ref[0, :, q0:q1]
        s = _dot_nt(q, k_ref[0, :, k0:k1])
        m = jnp.max(s, axis=-1, keepdims=True)
        if sink_idx is not None:
            sink = jnp.full((1, 1), sink_ref[sink_idx], F32)
            m = jnp.maximum(m, sink)
        p = exp(s - m)
        l = jnp.sum(p, axis=-1, keepdims=True)
        if sink_idx is not None:
            l = l + exp(sink - m)
        outs.append(_dot(p.astype(BF16), v_ref[0, :, v0:v1]) / l)
    o_ref[0] = jnp.concatenate(outs, axis=-1).astype(o_ref.dtype)


def _ctx_attention(q, k, v, heads, sink, name, base2=False):
    b, l, _ = q.shape
    full = lambda a: pl.BlockSpec((1, l, a.shape[-1]), lambda bi: (bi, 0, 0))
    return pl.pallas_call(
        functools.partial(_ctx_attn_kernel, heads=heads, base2=base2),
        grid=(b,),
        in_specs=[pl.BlockSpec(memory_space=pltpu.SMEM), full(q), full(k), full(v)],
        out_specs=pl.BlockSpec((1, l, GROUP_W), lambda bi: (bi, 0, 0)),
        out_shape=jax.ShapeDtypeStruct((b, l, GROUP_W), BF16),
        compiler_params=_cparams(("parallel",)),
        name=name,
    )(sink, q, k, v)


def _ret_kernel(*refs, chunk, direction, has_prev):
    if has_prev:
        (dec_ref, q_ref, k_ref, v_ref, g_ref, s0_ref, prev_ref, o_ref, sn_ref,
         st_ref, dm_ref, qd_ref, kd_ref, cd_ref) = refs
    else:
        (dec_ref, q_ref, k_ref, v_ref, g_ref, s0_ref, o_ref, sn_ref,
         st_ref, dm_ref, qd_ref, kd_ref, cd_ref) = refs
        prev_ref = None
    i = pl.program_id(1)
    c = chunk
    fwd = direction == 0

    @pl.when(i == 0)
    def _init():
        st_ref[...] = s0_ref[0]
        ii = lax.broadcasted_iota(jnp.int32, (c, c), 0).astype(F32)
        jj = lax.broadcasted_iota(jnp.int32, (c, c), 1).astype(F32)
        diff = (ii - jj) if fwd else (jj - ii)
        pos = lax.broadcasted_iota(jnp.int32, (c, HEAD_DIM), 0).astype(F32)
        for h in range(N_HEADS):
            dec = dec_ref[direction * N_HEADS + h]
            lg = jax.nn.log_sigmoid(jnp.full((c, c), dec, F32))
            dm_ref[h] = jnp.where(diff >= 0, jnp.exp(lg * jnp.maximum(diff, 0.0)), 0.0)
            lg64 = jax.nn.log_sigmoid(jnp.full((c, HEAD_DIM), dec, F32))
            if fwd:
                qd_ref[h] = jnp.exp(lg64 * (pos + 1.0))
                kd_ref[h] = jnp.exp(lg64 * (c - 1.0 - pos))
            else:
                qd_ref[h] = jnp.exp(lg64 * (c - pos))
                kd_ref[h] = jnp.exp(lg64 * pos)
            cd_ref[h] = jnp.exp(jax.nn.log_sigmoid(jnp.full((HEAD_DIM, HEAD_DIM), dec, F32)) * c)

    outs = []
    for h in range(N_HEADS):
        cols = slice(h * HEAD_DIM, (h + 1) * HEAD_DIM)
        q = q_ref[0, :, cols]
        k = k_ref[0, :, cols]
        v = v_ref[0, :, cols]
        state = st_ref[h]
        att = _dot_nt(q, k) * dm_ref[h]
        o = _dot(att.astype(BF16), v) + _dot(q, state.astype(BF16)) * qd_ref[h]
        kk = (k.astype(F32) * kd_ref[h]).astype(BF16)
        st_ref[h] = state * cd_ref[h] + _dot_tn(kk, v)
        outs.append(_rms(o) * _silu(g_ref[0, :, cols]))
    res = jnp.concatenate(outs, axis=-1)
    if has_prev:
        res = res + prev_ref[0]
    o_ref[0] = res.astype(o_ref.dtype)

    @pl.when(i == pl.num_programs(1) - 1)
    def _fin():
        sn_ref[0] = st_ref[...]


def _retention_pass(dec, q, k, v, gates, state0, prev, direction, chunk, out_dtype):
    b, t, _ = q.shape
    n = t // chunk
    if direction == 0:
        blk = lambda bi, i: (bi, i, 0)
        gblk = lambda bi, i: (bi, i, 0)
    else:
        blk = lambda bi, i: (bi, n - 1 - i, 0)
        gblk = lambda bi, i: (bi, n - 1 - i, 1)
    tok = pl.BlockSpec((1, chunk, GROUP_W), blk)
    st_spec = pl.BlockSpec((1, N_HEADS, HEAD_DIM, HEAD_DIM), lambda bi, i: (bi, 0, 0, 0))
    in_specs = [pl.BlockSpec(memory_space=pltpu.SMEM), tok, tok, tok,
                pl.BlockSpec((1, chunk, GROUP_W), gblk), st_spec]
    args = [dec, q, k, v, gates, state0]
    if prev is not None:
        in_specs.append(tok)
        args.append(prev)
    return pl.pallas_call(
        functools.partial(_ret_kernel, chunk=chunk, direction=direction, has_prev=prev is not None),
        grid=(b, n),
        in_specs=in_specs,
        out_specs=[tok, st_spec],
        out_shape=[jax.ShapeDtypeStruct((b, t, GROUP_W), out_dtype),
                   jax.ShapeDtypeStruct((b, N_HEADS, HEAD_DIM, HEAD_DIM), F32)],
        scratch_shapes=[pltpu.VMEM((N_HEADS, HEAD_DIM, HEAD_DIM), F32),
                        pltpu.VMEM((N_HEADS, chunk, chunk), F32),
                        pltpu.VMEM((N_HEADS, chunk, HEAD_DIM), F32),
                        pltpu.VMEM((N_HEADS, chunk, HEAD_DIM), F32),
                        pltpu.VMEM((N_HEADS, HEAD_DIM, HEAD_DIM), F32)],
        compiler_params=_cparams(("parallel", "arbitrary")),
        name="retention_fwd" if direction == 0 else "retention_bwd",
    )(*args)


def _retention(dec, xq, xk, xv, xg, yq, yk, yv, yg, chunk):
    b = xq.shape[0]
    zero = jnp.zeros((b, N_HEADS, HEAD_DIM, HEAD_DIM), F32)
    ychunk = min(chunk, yq.shape[1])
    yb, sb = _retention_pass(dec, yq, yk, yv, yg, zero, None, 1, ychunk, F32)
    y, sf = _retention_pass(dec, yq, yk, yv, yg, zero, yb, 0, ychunk, BF16)
    xb, _ = _retention_pass(dec, xq, xk, xv, xg, sb, None, 1, chunk, F32)
    x, _ = _retention_pass(dec, xq, xk, xv, xg, sf, xb, 0, chunk, BF16)
    return x, y


def _na_bias_kernel(rpb_ref, o_ref):
    h = pl.program_id(0)
    dr0 = pl.program_id(1)
    c = lax.broadcasted_iota(jnp.int32, (GRID_W, GRID_W), 0)
    kc = lax.broadcasted_iota(jnp.int32, (GRID_W, GRID_W), 1)
    c0 = jnp.clip(c - NA_KC // 2, 0, GRID_W - NA_KC)
    col_in = (kc >= c0) & (kc < c0 + NA_KC)
    dc = jnp.clip(kc - c, -(NA_KC - 1), NA_KC - 1) + NA_KC - 1
    n_dc = 2 * NA_KC - 1
    for j in range(NA_KR):
        base = (h * (2 * NA_KR - 1) + dr0 + j) * n_dc
        acc = jnp.zeros((GRID_W, GRID_W), F32)
        for d in range(n_dc):
            acc = jnp.where(dc == d, rpb_ref[base + d], acc)
        o_ref[0, 0, :, j * GRID_W:(j + 1) * GRID_W] = jnp.where(col_in, acc, NEG_INF)


def _na_bias_table(rpb):
    return pl.pallas_call(
        _na_bias_kernel,
        grid=(N_HEADS, NA_KR),
        in_specs=[pl.BlockSpec(memory_space=pltpu.SMEM)],
        out_specs=pl.BlockSpec((1, 1, GRID_W, NA_KR * GRID_W), lambda h, r: (h, r, 0, 0)),
        out_shape=jax.ShapeDtypeStruct((N_HEADS, NA_KR, GRID_W, NA_KR * GRID_W), F32),
        compiler_params=_cparams(("parallel", "parallel")),
        name="na_bias_table",
    )(rpb.reshape(-1))


def _na_kernel(q_ref, k_ref, v_ref, ky_ref, vy_ref, tb_ref, o_ref, *, rows_per_step, n_rows):
    r_base = pl.program_id(1) * rows_per_step
    win = NA_KR * GRID_W

    def row_body(i, carry):
        r = r_base + i
        r0 = jnp.clip(r - NA_KR // 2, 0, n_rows - NA_KR)
        dr0 = r0 - r + NA_KR - 1
        koff = pl.multiple_of(r0 * GRID_W, GRID_W)
        qoff = pl.multiple_of(i * GRID_W, GRID_W)
        outs = []
        for h in range(N_HEADS):
            cols = slice(h * HEAD_DIM, (h + 1) * HEAD_DIM)
            q = q_ref[0, pl.ds(qoff, GRID_W), cols]
            s = _dot_nt(q, k_ref[0, pl.ds(koff, win), cols]) + tb_ref[h, dr0]
            sc = _dot_nt(q, ky_ref[0, :, cols])
            m = jnp.maximum(jnp.max(s, axis=-1, keepdims=True), jnp.max(sc, axis=-1, keepdims=True))
            p = jnp.exp(s - m)
            pc = jnp.exp(sc - m)
            l = jnp.sum(p, axis=-1, keepdims=True) + jnp.sum(pc, axis=-1, keepdims=True)
            o = _dot(p.astype(BF16), v_ref[0, pl.ds(koff, win), cols]) + _dot(pc.astype(BF16), vy_ref[0, :, cols])
            outs.append(o / l)
        o_ref[0, pl.ds(qoff, GRID_W), :] = jnp.concatenate(outs, axis=-1).astype(o_ref.dtype)
        return carry

    lax.fori_loop(0, rows_per_step, row_body, 0)


def _na_attention(q, k, v, ky, vy, table, rows_per_step):
    b, s, _ = q.shape
    l = ky.shape[1]
    n_rows = s // GRID_W
    tq = rows_per_step * GRID_W
    seq = lambda n: pl.BlockSpec((1, n, GROUP_W), lambda bi, i: (bi, 0, 0))
    return pl.pallas_call(
        functools.partial(_na_kernel, rows_per_step=rows_per_step, n_rows=n_rows),
        grid=(b, n_rows // rows_per_step),
        in_specs=[pl.BlockSpec((1, tq, GROUP_W), lambda bi, i: (bi, i, 0)),
                  seq(s), seq(s), seq(l), seq(l), _const_spec(table.shape)],
        out_specs=pl.BlockSpec((1, tq, GROUP_W), lambda bi, i: (bi, i, 0)),
        out_shape=jax.ShapeDtypeStruct((b, s, GROUP_W), BF16),
        compiler_params=_cparams(("parallel", "arbitrary")),
        name="na_attention",
    )(q, k, v, ky, vy, table)


def _swa_kernel(sink_ref, q_ref, kp_ref, kc_ref, kn_ref, vp_ref, vc_ref, vn_ref, ky_ref, vy_ref, o_ref):
    n = pl.program_id(1)
    nb = pl.num_programs(1)
    bl = SWA_BLOCK
    g = N_HEADS // SWA_KV_HEADS
    qi = lax.broadcasted_iota(jnp.int32, (g * bl, 3 * bl), 0) % bl
    j = lax.broadcasted_iota(jnp.int32, (g * bl, 3 * bl), 1)
    delta = j - bl - qi
    blk = n - 1 + j // bl
    valid = (jnp.abs(delta) <= SWA_WINDOW) & (blk >= 0) & (blk < nb)
    half = lax.broadcasted_iota(jnp.int32, (g * bl, 1), 0) // bl
    outs = [None] * N_HEADS
    for kh in range(SWA_KV_HEADS):
        kcols = slice(kh * HEAD_DIM, (kh + 1) * HEAD_DIM)
        q = jnp.concatenate([q_ref[0, :, (kh * g + gi) * HEAD_DIM:(kh * g + gi + 1) * HEAD_DIM]
                             for gi in range(g)], axis=0)
        kw = jnp.concatenate([kp_ref[0, :, kcols], kc_ref[0, :, kcols], kn_ref[0, :, kcols]], axis=0)
        vw = jnp.concatenate([vp_ref[0, :, kcols], vc_ref[0, :, kcols], vn_ref[0, :, kcols]], axis=0)
        s = jnp.where(valid, _dot_nt(q, kw), NEG_INF)
        sc = _dot_nt(q, ky_ref[0, :, kcols])
        sink = jnp.full((g * bl, 1), sink_ref[kh * g], F32)
        for gi in range(1, g):
            sink = jnp.where(half == gi, sink_ref[kh * g + gi], sink)
        m = jnp.maximum(jnp.maximum(jnp.max(s, axis=-1, keepdims=True),
                                    jnp.max(sc, axis=-1, keepdims=True)), sink)
        p = jnp.exp(s - m)
        pc = jnp.exp(sc - m)
        l = (jnp.sum(p, axis=-1, keepdims=True) + jnp.sum(pc, axis=-1, keepdims=True)
             + jnp.exp(sink - m))
        o = (_dot(p.astype(BF16), vw) + _dot(pc.astype(BF16), vy_ref[0, :, kcols])) / l
        for gi in range(g):
            outs[kh * g + gi] = o[gi * bl:(gi + 1) * bl]
    o_ref[0] = jnp.concatenate(outs, axis=-1).astype(o_ref.dtype)


def _swa_attention(sink, q, k, v, ky, vy):
    b, s, _ = q.shape
    l = ky.shape[1]
    nb = s // SWA_BLOCK
    kvw = SWA_KV_HEADS * HEAD_DIM
    prev = pl.BlockSpec((1, SWA_BLOCK, kvw), lambda bi, i: (bi, jnp.maximum(i - 1, 0), 0))
    cur = pl.BlockSpec((1, SWA_BLOCK, kvw), lambda bi, i: (bi, i, 0))
    nxt = pl.BlockSpec((1, SWA_BLOCK, kvw), lambda bi, i: (bi, jnp.minimum(i + 1, nb - 1), 0))
    ctx = pl.BlockSpec((1, l, kvw), lambda bi, i: (bi, 0, 0))
    return pl.pallas_call(
        _swa_kernel,
        grid=(b, nb),
        in_specs=[pl.BlockSpec(memory_space=pltpu.SMEM),
                  pl.BlockSpec((1, SWA_BLOCK, GROUP_W), lambda bi, i: (bi, i, 0)),
                  prev, cur, nxt, prev, cur, nxt, ctx, ctx],
        out_specs=pl.BlockSpec((1, SWA_BLOCK, GROUP_W), lambda bi, i: (bi, i, 0)),
        out_shape=jax.ShapeDtypeStruct((b, s, GROUP_W), BF16),
        compiler_params=_cparams(("parallel", "arbitrary")),
        name="swa_attention",
    )(sink, q, k, k, k, v, v, v, ky, vy)


def _outffn_kernel(*refs, hidden_chunk, final):
    if final:
        (x_ref, m0_ref, m1_ref, m2_ref, m3_ref, wo_ref, g1_ref, n2_ref, sc_ref, sh_ref, g2_ref,
         w1_ref, w3_ref, w2_ref, fg_ref, o_ref) = refs
    else:
        (x_ref, m0_ref, m1_ref, m2_ref, m3_ref, wo_ref, g1_ref, n2_ref, sc_ref, sh_ref, g2_ref,
         w1_ref, w3_ref, w2_ref, o_ref) = refs
    mix = None
    for gi, m_ref in enumerate((m0_ref, m1_ref, m2_ref, m3_ref)):
        part = _dot(m_ref[0], wo_ref[gi * GROUP_W:(gi + 1) * GROUP_W, :])
        mix = part if mix is None else mix + part
    x1 = x_ref[0] + g1_ref[0] * mix
    hb = ((_rms(x1) * n2_ref[...]) * (1.0 + sc_ref[0]) + sh_ref[0]).astype(BF16)
    hidden = w1_ref.shape[1]
    acc = None
    for c0 in range(0, hidden, hidden_chunk):
        a = _dot(hb, w1_ref[:, c0:c0 + hidden_chunk])
        bgate = _dot(hb, w3_ref[:, c0:c0 + hidden_chunk])
        u = (_silu(a) * bgate).astype(BF16)
        part = _dot(u, w2_ref[c0:c0 + hidden_chunk, :])
        acc = part if acc is None else acc + part
    x2 = x1 + g2_ref[0] * acc
    if final:
        x2 = _rms(x2) * fg_ref[...]
    o_ref[0] = x2


def _outffn(x, mixes, wo, g1, n2, sc2, sh2, g2, w1, w3, w2, final_g, tm):
    b, t, d = x.shape
    tok = lambda wd: pl.BlockSpec((1, tm, wd), lambda bi, i: (bi, i, 0))
    vec = pl.BlockSpec((1, 1, d), lambda bi, i: (bi, 0, 0))
    in_specs = ([tok(d)] + [tok(GROUP_W)] * 4
                + [_const_spec(wo.shape), vec, _const_spec((1, d)), vec, vec, vec,
                   _const_spec(w1.shape), _const_spec(w3.shape), _const_spec(w2.shape)])
    args = [x, *mixes, wo, g1, n2, sc2, sh2, g2, w1, w3, w2]
    final = final_g is not None
    if final:
        in_specs.append(_const_spec((1, d)))
        args.append(final_g)
    return pl.pallas_call(
        functools.partial(_outffn_kernel, hidden_chunk=256, final=final),
        grid=(b, t // tm),
        in_specs=in_specs,
        out_specs=tok(d),
        out_shape=jax.ShapeDtypeStruct((b, t, d), F32),
        compiler_params=_cparams(("parallel", "parallel")),
        name="out_proj_ffn_final" if final else "out_proj_ffn",
    )(*args)


def _prep_weights(w_in, mla_w_uq, mla_w_ukv):
    depth, d, _ = w_in.shape
    offs = [0]
    for sz in IN_SIZES:
        offs.append(offs[-1] + sz)
    cols = [w_in[:, :, offs[i]:offs[i + 1]] for i in range(len(IN_SIZES))]
    cq, ckv, kr, rq, rk, rv, rgf, rgb, nq, nk, nv, sq, sk, sv = cols
    scale = HEAD_DIM ** -0.5
    zpad = jnp.zeros((depth, d, MLA_HEAD_PAD - MLA_NOPE - MLA_ROPE), F32)
    znope = jnp.zeros((depth, d, MLA_NOPE), F32)
    krep = jnp.concatenate([znope, kr, zpad] * N_HEADS, axis=-1)
    w = jnp.concatenate([cq, ckv, krep, rq, rk * scale, rv, rgf, rgb, nq * scale, nk, nv,
                         sq * scale, sk, sv], axis=-1).astype(BF16)

    qr = mla_w_uq.shape[1]
    uq = mla_w_uq.reshape(depth, qr, N_HEADS, MLA_NOPE + MLA_ROPE)
    wuq = jnp.concatenate([uq, jnp.zeros((depth, qr, N_HEADS, MLA_HEAD_PAD - MLA_NOPE - MLA_ROPE), F32)],
                          axis=-1).reshape(depth, qr, N_HEADS * MLA_HEAD_PAD).astype(BF16)
    kvr = mla_w_ukv.shape[1]
    ukv = mla_w_ukv.reshape(depth, kvr, N_HEADS, MLA_NOPE + MLA_V)
    zk = jnp.zeros((depth, kvr, N_HEADS, MLA_HEAD_PAD - MLA_NOPE), F32)
    wuk = jnp.concatenate([ukv[..., :MLA_NOPE], zk], axis=-1).reshape(depth, kvr, -1).astype(BF16)
    zv = jnp.zeros((depth, kvr, N_HEADS, MLA_HEAD_PAD - MLA_V), F32)
    wuv = jnp.concatenate([ukv[..., MLA_NOPE:], zv], axis=-1).reshape(depth, kvr, -1).astype(BF16)
    return w, wuq, wuk, wuv


def _ctx_head_specs():
    mla = tuple((h * MLA_HEAD_PAD, (h + 1) * MLA_HEAD_PAD, h * MLA_HEAD_PAD, (h + 1) * MLA_HEAD_PAD,
                 h * MLA_HEAD_PAD, h * MLA_HEAD_PAD + MLA_V, None) for h in range(N_HEADS))
    na = tuple((h * HEAD_DIM, (h + 1) * HEAD_DIM) * 3 + (None,) for h in range(N_HEADS))
    g = N_HEADS // SWA_KV_HEADS
    swa = tuple((h * HEAD_DIM, (h + 1) * HEAD_DIM, (h // g) * HEAD_DIM, (h // g + 1) * HEAD_DIM,
                 (h // g) * HEAD_DIM, (h // g + 1) * HEAD_DIM, h) for h in range(N_HEADS))
    return mla, na, swa


def kernel(x, c, ctx, c_ctx, ada_w, ada_b, norm1_g, w_in, mla_q_norm, mla_w_uq, mla_kv_norm, mla_w_ukv,
           ret_decay, na_rpb, swa_sink, w_out, norm2_g, ffn_w1, ffn_w3, ffn_w2, final_norm_g):
    b, s, d = x.shape
    l_ctx = ctx.shape[1]
    depth = ada_w.shape[0]
    assert b + 1 <= 8 and s % 512 == 0 and (s // GRID_W) % 8 == 0 and l_ctx % 128 == 0

    cond = jnp.concatenate([c, c_ctx[None, :], jnp.zeros((8 - b - 1, d), F32)], axis=0)
    mod = _modulation(cond, ada_w, ada_b)
    tables = _rope_tables(s)
    w_all, wuq_all, wuk_all, wuv_all = _prep_weights(w_in, mla_w_uq, mla_w_ukv)
    wo_all = w_out.astype(BF16)
    w1_all, w3_all, w2_all = ffn_w1.astype(BF16), ffn_w3.astype(BF16), ffn_w2.astype(BF16)
    mla_heads, na_heads, swa_heads = _ctx_head_specs()
    no_sink = jnp.zeros((N_HEADS,), F32)

    tm_x = 512
    tm_y = min(256, l_ctx)
    y = ctx
    for l in range(depth):
        mx = [mod[l, :b, j * d:(j + 1) * d][:, None, :] for j in range(6)]
        my = [jnp.broadcast_to(mod[l, b, j * d:(j + 1) * d][None, None, :], (b, 1, d)) for j in range(6)]
        n1 = norm1_g[l][None, :]
        n2 = norm2_g[l][None, :]
        qn = mla_q_norm[l][None, :]
        kvn = mla_kv_norm[l][None, :]
        lw = (w_all[l], qn, kvn, wuq_all[l], wuk_all[l], wuv_all[l])

        px = _inproj(x, n1, mx[1], mx[0], *lw, tables, tm_x)
        py = _inproj(y, n1, my[1], my[0], *lw, None, tm_y)
        (xmq, xmk, xmv, xrq, xrk, xrv, xrg, xnq, xnk, xnv, xsq, xsk, xsv) = px
        (ymq, ymk, ymv, yrq, yrk, yrv, yrg, ynq, ynk, ynv, ysq, ysk, ysv) = py

        mla_x = _mla_attention(xmq, xmk, xmv, ymk, ymv, tq=512, tk=512, n_sub=2)
        dec = ret_decay[l].reshape(-1)
        ret_x, ret_y = _retention(dec, xrq, xrk, xrv, xrg, yrq, yrk, yrv, yrg, chunk=128)
        table = _na_bias_table(na_rpb[l])
        na_x = _na_attention(xnq, xnk, xnv, ynk, ynv, table, rows_per_step=8)
        swa_x = _swa_attention(swa_sink[l], xsq, xsk, xsv, ysk, ysv)

        last = l == depth - 1
        x = _outffn(x, (mla_x, ret_x, na_x, swa_x), wo_all[l], mx[2], n2, mx[4], mx[3], mx[5],
                    w1_all[l], w3_all[l], w2_all[l], final_norm_g[None, :] if last else None, tm_x)
        if not last:
            mla_y = _ctx_attention(ymq, ymk, ymv, mla_heads, no_sink, "mla_ctx_attention", base2=True)
            na_y = _ctx_attention(ynq, ynk, ynv, na_heads, no_sink, "na_ctx_attention")
            swa_y = _ctx_attention(ysq, ysk, ysv, swa_heads, swa_sink[l], "swa_ctx_attention")
            y = _outffn(y, (mla_y, ret_y, na_y, swa_y), wo_all[l], my[2], n2, my[4], my[3], my[5],
                        w1_all[l], w3_all[l], w2_all[l], None, tm_y)
    return x
```

```python
import functools

import jax
import jax.numpy as jnp
from jax import lax
from jax.experimental import pallas as pl
from jax.experimental.pallas import tpu as pltpu

F32 = jnp.float32
BF16 = jnp.bfloat16

GRID_W = 64
HEAD_DIM = 64
N_HEADS = 4
GROUP_W = N_HEADS * HEAD_DIM
MLA_Q_RANK = 256
MLA_KV_RANK = 128
MLA_NOPE = 64
MLA_ROPE = 32
MLA_V = 64
MLA_HEAD_PAD = 128
MLA_V_ROWS = MLA_V + 16
NA_KR = 8
NA_KC = 16
SWA_KV_HEADS = 2
SWA_WINDOW = 128
SWA_BLOCK = 128
ROPE_THETA = 10000.0
EPS = 1e-6
NEG_INF = -1e30
LOG2_E = 1.4426950408889634
LANES = 128
VMEM_LIMIT = 56 * 1024 * 1024

IN_SIZES = (MLA_Q_RANK, MLA_KV_RANK, MLA_ROPE,
            GROUP_W, GROUP_W, GROUP_W, GROUP_W, GROUP_W,
            GROUP_W, GROUP_W, GROUP_W,
            GROUP_W, SWA_KV_HEADS * HEAD_DIM, SWA_KV_HEADS * HEAD_DIM)

_O_CQ = 0
_O_CKV = _O_CQ + MLA_Q_RANK
_O_KREP = _O_CKV + MLA_KV_RANK
_O_RQK = _O_KREP + N_HEADS * MLA_HEAD_PAD
_O_RV = _O_RQK + 2 * GROUP_W
_O_RG = _O_RV + GROUP_W
_O_NA = _O_RG + 2 * GROUP_W
_O_SQK = _O_NA + 3 * GROUP_W
_O_SV = _O_SQK + 2 * GROUP_W
_IN_COLS = _O_SV + GROUP_W


def _cparams(sem):
    return pltpu.CompilerParams(dimension_semantics=sem, vmem_limit_bytes=VMEM_LIMIT)


def _dot(a, b):
    return jnp.dot(a, b, preferred_element_type=F32)


def _dot_nt(a, b):
    return lax.dot_general(a, b, (((1,), (1,)), ((), ())), preferred_element_type=F32)


def _dot_tn(a, b):
    return lax.dot_general(a, b, (((0,), (0,)), ((), ())), preferred_element_type=F32)


def _rms(x):
    return x * lax.rsqrt(jnp.mean(x * x, axis=-1, keepdims=True) + EPS)


def _silu(x):
    return x * jax.nn.sigmoid(x)


def _lane_chunks(arrays):
    return [a[:, j * LANES:(j + 1) * LANES] for a in arrays for j in range(a.shape[-1] // LANES)]


def _row_max(*arrays):
    return jnp.max(functools.reduce(jnp.maximum, _lane_chunks(arrays)), axis=-1, keepdims=True)


def _row_sum(*arrays):
    return jnp.sum(functools.reduce(jnp.add, _lane_chunks(arrays)), axis=-1, keepdims=True)


def _mod_kernel(c_ref, w_ref, b_ref, o_ref):
    o_ref[0] = _dot(_silu(c_ref[...]), w_ref[0]) + b_ref[0]


def _modulation(cond, ada_w, ada_b):
    depth, d, d6 = ada_w.shape
    n = d6 // d
    return pl.pallas_call(
        _mod_kernel,
        grid=(depth, n),
        in_specs=[pl.BlockSpec((8, d), lambda l, j: (0, 0)),
                  pl.BlockSpec((1, d, d), lambda l, j: (l, 0, j)),
                  pl.BlockSpec((1, 1, d), lambda l, j: (l, 0, j))],
        out_specs=pl.BlockSpec((1, 8, d), lambda l, j: (l, 0, j)),
        out_shape=jax.ShapeDtypeStruct((depth, 8, d6), F32),
        compiler_params=_cparams(("parallel", "parallel")),
        name="ada_modulation",
    )(cond, ada_w, ada_b.reshape(depth, 1, d6))


def _rope_tables(seq):
    t = jnp.arange(seq)
    row = (t // GRID_W).astype(F32)[:, None]
    col = (t % GRID_W).astype(F32)[:, None]

    def parts(d):
        inv = ROPE_THETA ** (-jnp.arange(0, d, 2, dtype=F32) / d)
        ar, ac = row * inv[None, :], col * inv[None, :]
        z = jnp.zeros_like(ar)
        cos = jnp.concatenate([jnp.cos(ar), jnp.cos(ar), jnp.cos(ac), jnp.cos(ac)], axis=-1)
        s_next = jnp.concatenate([-jnp.sin(ar), z, -jnp.sin(ac), z], axis=-1)
        s_prev = jnp.concatenate([z, jnp.sin(ar), z, jnp.sin(ac)], axis=-1)
        return cos, s_prev, s_next

    c, sp, sn = parts(HEAD_DIM // 2)
    t64 = tuple(jnp.concatenate([a, a], axis=-1) for a in (c, sp, sn))
    c, sp, sn = parts(MLA_ROPE // 2)
    ones = jnp.ones((seq, MLA_NOPE), F32)
    zn = jnp.zeros((seq, MLA_NOPE), F32)
    pad1 = jnp.ones((seq, MLA_HEAD_PAD - MLA_NOPE - MLA_ROPE), F32)
    pad0 = jnp.zeros_like(pad1)
    tm = (jnp.concatenate([ones, c, pad1], axis=-1),
          jnp.concatenate([zn, sp, pad0], axis=-1),
          jnp.concatenate([zn, sn, pad0], axis=-1))
    return t64 + tm


def _rope(x, cos, s_prev, s_next, d):
    out = []
    for j in range(x.shape[-1] // LANES):
        xc = x[:, j * LANES:(j + 1) * LANES]
        out.append(xc * cos + pltpu.roll(xc, d, 1) * s_prev + pltpu.roll(xc, LANES - d, 1) * s_next)
    return out[0] if len(out) == 1 else jnp.concatenate(out, axis=-1)


def _inproj_kernel(*refs, rotate, mla_scale):
    if rotate:
        (x_ref, g_ref, sc_ref, sh_ref, w_ref, qn_ref, kvn_ref, wuq_ref, wuk_ref, wuv_ref,
         c64_ref, p64_ref, n64_ref, cm_ref, pm_ref, nm_ref,
         mq_ref, mk_ref, mv_ref, rq_ref, rk_ref, rv_ref, rg_ref,
         nq_ref, nk_ref, nv_ref, sq_ref, sk_ref, sv_ref) = refs
    else:
        (x_ref, g_ref, sc_ref, sh_ref, w_ref, qn_ref, kvn_ref, wuq_ref, wuk_ref, wuv_ref,
         mq_ref, mk_ref, mv_ref, rq_ref, rk_ref, rv_ref, rg_ref,
         nq_ref, nk_ref, nv_ref, sq_ref, sk_ref, sv_ref) = refs

    x = x_ref[0]
    h = (_rms(x) * g_ref[...]) * (1.0 + sc_ref[0]) + sh_ref[0]
    hb = h.astype(BF16)

    def proj(lo, hi):
        return _dot(hb, w_ref[:, lo:hi])

    def rope64(v):
        if not rotate:
            return v
        return _rope(v, c64_ref[...], p64_ref[...], n64_ref[...], HEAD_DIM // 4)

    def rope_mla(v):
        if not rotate:
            return v
        return _rope(v, cm_ref[...], pm_ref[...], nm_ref[...], MLA_ROPE // 4)

    cq = (_rms(proj(_O_CQ, _O_CKV)) * qn_ref[...]).astype(BF16)
    q = rope_mla(_dot(cq, wuq_ref[...])) * mla_scale
    mq_ref[0] = q.astype(BF16)
    ckv = (_rms(proj(_O_CKV, _O_KREP)) * kvn_ref[...]).astype(BF16)
    k = rope_mla(_dot(ckv, wuk_ref[...]) + proj(_O_KREP, _O_RQK))
    mk_ref[0] = k.astype(BF16)
    v = _dot(ckv, wuv_ref[...])
    lane = lax.broadcasted_iota(jnp.int32, v.shape, 1)
    mv_ref[0] = jnp.where(lane % MLA_HEAD_PAD >= MLA_V, 1.0, v).astype(BF16)

    rqk = proj(_O_RQK, _O_RV)
    rq_ref[0] = rope64(rqk[:, :GROUP_W]).astype(BF16)
    rk_ref[0] = rope64(rqk[:, GROUP_W:]).astype(BF16)
    rv_ref[0] = proj(_O_RV, _O_RG).astype(BF16)
    rg_ref[0] = proj(_O_RG, _O_NA)

    na = proj(_O_NA, _O_SQK)
    nq_ref[0] = na[:, :GROUP_W].astype(BF16)
    nk_ref[0] = na[:, GROUP_W:2 * GROUP_W].astype(BF16)
    nv_ref[0] = na[:, 2 * GROUP_W:].astype(BF16)

    sqk = proj(_O_SQK, _O_SV)
    sq_ref[0] = rope64(sqk[:, :GROUP_W]).astype(BF16)
    sk_ref[0] = rope64(sqk[:, GROUP_W:]).astype(BF16)
    sv_ref[0] = proj(_O_SV, _IN_COLS).astype(BF16)


def _const_spec(shape):
    nd = len(shape)
    return pl.BlockSpec(shape, lambda *_: (0,) * nd, pipeline_mode=pl.Buffered(1))


def _inproj(x, gain, scale, shift, w, qn, kvn, wuq, wuk, wuv, tables, tm):
    b, t, d = x.shape
    rotate = tables is not None
    kv_w = GROUP_W
    mla_w = N_HEADS * MLA_HEAD_PAD
    tok = lambda wd: pl.BlockSpec((1, tm, wd), lambda bi, i: (bi, i, 0))
    vec = pl.BlockSpec((1, 1, d), lambda bi, i: (bi, 0, 0))
    in_specs = [tok(d), _const_spec((1, d)), vec, vec, _const_spec(w.shape),
                _const_spec(qn.shape), _const_spec(kvn.shape), _const_spec(wuq.shape),
                _const_spec(wuk.shape), _const_spec(wuv.shape)]
    args = [x, gain, scale, shift, w, qn, kvn, wuq, wuk, wuv]
    if rotate:
        in_specs += [pl.BlockSpec((tm, LANES), lambda bi, i: (i, 0))] * 6
        args += list(tables)
    widths = [mla_w, mla_w, mla_w, GROUP_W, GROUP_W, GROUP_W, 2 * GROUP_W,
              GROUP_W, GROUP_W, GROUP_W, GROUP_W, kv_w, kv_w]
    dtypes = [BF16] * 6 + [F32] + [BF16] * 6
    return pl.pallas_call(
        functools.partial(_inproj_kernel, rotate=rotate, mla_scale=(MLA_NOPE + MLA_ROPE) ** -0.5 * LOG2_E),
        grid=(b, t // tm),
        in_specs=in_specs,
        out_specs=[tok(wd) for wd in widths],
        out_shape=[jax.ShapeDtypeStruct((b, t, wd), dt) for wd, dt in zip(widths, dtypes)],
        compiler_params=_cparams(("parallel", "parallel")),
        name="in_proj_rot" if rotate else "in_proj_ctx",
    )(*args)


def _mla_kernel(q_ref, kx_ref, vx_ref, ky_ref, vy_ref, o_ref, m_ref, acc_ref, sa_ref, sb_ref, sc_ref,
                *, tk, heads, n_sub):
    n_chunks = kx_ref.shape[1] // tk
    tq = q_ref.shape[1]
    ts = tq // n_sub
    chains = [(h, u) for h in range(heads) for u in range(n_sub)]
    for c in range(len(chains)):
        m_ref[c] = jnp.full(m_ref.shape[1:], NEG_INF, F32)
        acc_ref[c] = jnp.zeros(acc_ref.shape[1:], F32)

    def rows(h):
        return slice(h * MLA_HEAD_PAD, (h + 1) * MLA_HEAD_PAD)

    def scores(c, k, dst_ref):
        h, u = chains[c]
        dst_ref[c] = _dot_nt(k, q_ref[0, u * ts:(u + 1) * ts, rows(h)])

    def absorb(c, src_ref, vt):
        st = src_ref[c]
        m_old = m_ref[c]
        m_new = jnp.maximum(m_old, jnp.max(st, axis=0, keepdims=True))
        pt = jnp.exp2(st - m_new).astype(BF16)
        acc_ref[c, :MLA_V_ROWS] = (acc_ref[c, :MLA_V_ROWS] * jnp.exp2(m_old - m_new)
                                   + _dot(vt[:MLA_V_ROWS], pt))
        m_ref[c] = m_new

    def kx(j, h):
        return kx_ref[0, pl.ds(pl.multiple_of(j * tk, tk), tk), rows(h)]

    def stage(j_next, next_ref, j_cur, cur_ref):
        for c, (h, _) in enumerate(chains):
            scores(c, kx(j_next, h), next_ref)
            absorb(c, cur_ref, vx_ref[0, j_cur, rows(h), :])

    for c, (h, _) in enumerate(chains):
        scores(c, kx(0, h), sa_ref)

    def body(jj, carry):
        stage(2 * jj + 1, sb_ref, 2 * jj, sa_ref)
        stage(2 * jj + 2, sa_ref, 2 * jj + 1, sb_ref)
        return carry

    lax.fori_loop(0, n_chunks // 2 - 1, body, 0)
    stage(n_chunks - 1, sb_ref, n_chunks - 2, sa_ref)
    for c, (h, _) in enumerate(chains):
        scores(c, ky_ref[0, :, rows(h)], sc_ref)
        absorb(c, sb_ref, vx_ref[0, n_chunks - 1, rows(h), :])
    outs = [[None] * n_sub for _ in range(heads)]
    for c, (h, u) in enumerate(chains):
        absorb(c, sc_ref, vy_ref[0, 0, rows(h), :])
        acc = acc_ref[c].T
        outs[h][u] = acc[:, :MLA_V] / acc[:, MLA_V:MLA_V + 1]
    o_ref[0] = jnp.concatenate([jnp.concatenate(outs[h], axis=0) for h in range(heads)],
                               axis=-1).astype(o_ref.dtype)


def _mla_attention(q, kx, vx, ky, vy, tq, tk, n_sub):
    b, s, _ = q.shape
    l = ky.shape[1]
    hp = 2
    wd = hp * MLA_HEAD_PAD
    vxt = jnp.swapaxes(vx.reshape(b, s // tk, tk, -1), 2, 3)
    vyt = jnp.swapaxes(vy.reshape(b, 1, l, -1), 2, 3)
    ts = tq // n_sub
    return pl.pallas_call(
        functools.partial(_mla_kernel, tk=tk, heads=hp, n_sub=n_sub),
        grid=(b, N_HEADS // hp, s // tq),
        in_specs=[pl.BlockSpec((1, tq, wd), lambda bi, hi, i: (bi, i, hi)),
                  pl.BlockSpec((1, s, wd), lambda bi, hi, i: (bi, 0, hi)),
                  pl.BlockSpec((1, s // tk, wd, tk), lambda bi, hi, i: (bi, 0, hi, 0)),
                  pl.BlockSpec((1, l, wd), lambda bi, hi, i: (bi, 0, hi)),
                  pl.BlockSpec((1, 1, wd, l), lambda bi, hi, i: (bi, 0, hi, 0))],
        out_specs=pl.BlockSpec((1, tq, hp * MLA_V), lambda bi, hi, i: (bi, i, hi)),
        out_shape=jax.ShapeDtypeStruct((b, s, N_HEADS * MLA_V), BF16),
        scratch_shapes=[pltpu.VMEM((hp * n_sub, 1, ts), F32),
                        pltpu.VMEM((hp * n_sub, MLA_HEAD_PAD, ts), F32),
                        pltpu.VMEM((hp * n_sub, tk, ts), F32),
                        pltpu.VMEM((hp * n_sub, tk, ts), F32),
                        pltpu.VMEM((hp * n_sub, l, ts), F32)],
        compiler_params=_cparams(("parallel", "parallel", "arbitrary")),
        name="mla_attention",
    )(q, kx, vxt, ky, vyt)


def _ctx_attn_kernel(sink_ref, q_ref, k_ref, v_ref, o_ref, *, heads, base2):
    exp = jnp.exp2 if base2 else jnp.exp
    outs = []
    for (q0, q1, k0, k1, v0, v1, sink_idx) in heads:
        q = q_ref[0, :, q0:q1]
        s = _dot_nt(q, k_ref[0, :, k0:k1])
        m = jnp.max(s, axis=-1, keepdims=True)
        if sink_idx is not None:
            sink = jnp.full((1, 1), sink_ref[sink_idx], F32)
            m = jnp.maximum(m, sink)
        p = exp(s - m)
        l = jnp.sum(p, axis=-1, keepdims=True)
        if sink_idx is not None:
            l = l + exp(sink - m)
        outs.append(_dot(p.astype(BF16), v_ref[0, :, v0:v1]) / l)
    o_ref[0] = jnp.concatenate(outs, axis=-1).astype(o_ref.dtype)


def _ctx_attention(q, k, v, heads, sink, name, base2=False):
    b, l, _ = q.shape
    full = lambda a: pl.BlockSpec((1, l, a.shape[-1]), lambda bi: (bi, 0, 0))
    return pl.pallas_call(
        functools.partial(_ctx_attn_kernel, heads=heads, base2=base2),
        grid=(b,),
        in_specs=[pl.BlockSpec(memory_space=pltpu.SMEM), full(q), full(k), full(v)],
        out_specs=pl.BlockSpec((1, l, GROUP_W), lambda bi: (bi, 0, 0)),
        out_shape=jax.ShapeDtypeStruct((b, l, GROUP_W), BF16),
        compiler_params=_cparams(("parallel",)),
        name=name,
    )(sink, q, k, v)


def _ret_kernel(*refs, chunk, n_chunks, direction, has_prev):
    if has_prev:
        (dec_ref, q_ref, k_ref, kt_ref, v_ref, g_ref, s0_ref, prev_ref, o_ref, sn_ref,
         st_ref, dm_ref, qd_ref, kdt_ref, cd_ref) = refs
    else:
        (dec_ref, q_ref, k_ref, kt_ref, v_ref, g_ref, s0_ref, o_ref, sn_ref,
         st_ref, dm_ref, qd_ref, kdt_ref, cd_ref) = refs
        prev_ref = None
    i = pl.program_id(1)
    c = chunk
    fwd = direction == 0

    def head_of(shape, axis):
        return lax.broadcasted_iota(jnp.int32, shape, axis) // HEAD_DIM

    @pl.when(i == 0)
    def _init():
        st_ref[...] = s0_ref[0]
        ii = lax.broadcasted_iota(jnp.int32, (c, c), 0).astype(F32)
        jj = lax.broadcasted_iota(jnp.int32, (c, c), 1).astype(F32)
        diff = (ii - jj) if fwd else (jj - ii)
        pos = lax.broadcasted_iota(jnp.int32, (c, GROUP_W), 0).astype(F32)
        pos_t = lax.broadcasted_iota(jnp.int32, (GROUP_W, c), 1).astype(F32)
        q_steps = (pos + 1.0) if fwd else (c - pos)
        k_steps = (c - 1.0 - pos_t) if fwd else pos_t
        qd = jnp.zeros((c, GROUP_W), F32)
        kdt = jnp.zeros((GROUP_W, c), F32)
        cd = jnp.zeros((GROUP_W, GROUP_W), F32)
        for h in range(N_HEADS):
            dec = dec_ref[direction * N_HEADS + h]
            lg = jax.nn.log_sigmoid(jnp.full((c, c), dec, F32))
            dm_ref[h * c:(h + 1) * c] = jnp.where(diff >= 0, jnp.exp(lg * jnp.maximum(diff, 0.0)), 0.0)
            lgq = jax.nn.log_sigmoid(jnp.full((c, GROUP_W), dec, F32))
            qd = jnp.where(head_of((c, GROUP_W), 1) == h, jnp.exp(lgq * q_steps), qd)
            lgk = jax.nn.log_sigmoid(jnp.full((GROUP_W, c), dec, F32))
            kdt = jnp.where(head_of((GROUP_W, c), 0) == h, jnp.exp(lgk * k_steps), kdt)
            lgc = jax.nn.log_sigmoid(jnp.full((GROUP_W, GROUP_W), dec, F32))
            cd = jnp.where(head_of((GROUP_W, GROUP_W), 0) == h, jnp.exp(lgc * c), cd)
        qd_ref[...] = qd
        kdt_ref[...] = kdt
        cd_ref[...] = cd

    in_head = [head_of((c, GROUP_W), 1) == h for h in range(N_HEADS)]
    head_mask = [jnp.where(mk, 1.0, 0.0).astype(BF16) for mk in in_head]
    same_head = head_of((GROUP_W, GROUP_W), 0) == head_of((GROUP_W, GROUP_W), 1)

    def chunk_off(n):
        idx = n if fwd else n_chunks - 1 - n
        return pl.multiple_of(idx * c, c)

    def qk(n):
        off = chunk_off(n)
        q_all = q_ref[0, pl.ds(off, c), :]
        q4 = jnp.concatenate([q_all * head_mask[h] for h in range(N_HEADS)], axis=0)
        return _dot_nt(q4, k_ref[0, pl.ds(off, c), :])

    def body(n, att_raw):
        off = chunk_off(n)
        v = v_ref[0, pl.ds(off, c), :]
        nxt = qk(jnp.minimum(n + 1, n_chunks - 1))
        att = (att_raw * dm_ref[...]).astype(BF16)
        intra4 = _dot(att, v)
        intra = intra4[:c]
        for h in range(1, N_HEADS):
            intra = jnp.where(in_head[h], intra4[h * c:(h + 1) * c], intra)
        att_raw = nxt
        state = st_ref[...]
        o = intra + _dot(q_ref[0, pl.ds(off, c), :], state.astype(BF16)) * qd_ref[...]
        kk = (kt_ref[0, :, pl.ds(off, c)].astype(F32) * kdt_ref[...]).astype(BF16)
        st_ref[...] = state * cd_ref[...] + jnp.where(same_head, _dot(kk, v), 0.0)
        oo = o * o
        ms = jnp.zeros((c, GROUP_W), F32)
        for h in range(N_HEADS):
            ms_h = jnp.sum(jnp.where(in_head[h], oo, 0.0), axis=-1, keepdims=True) * (1.0 / HEAD_DIM)
            ms = jnp.where(in_head[h], ms_h, ms)
        res = o * lax.rsqrt(ms + EPS) * _silu(g_ref[0, pl.ds(off, c), :])
        if has_prev:
            res = res + prev_ref[0, pl.ds(off, c), :]
        o_ref[0, pl.ds(off, c), :] = res.astype(o_ref.dtype)
        return att_raw

    lax.fori_loop(0, n_chunks, body, qk(0))

    @pl.when(i == pl.num_programs(1) - 1)
    def _fin():
        sn_ref[0] = st_ref[...]


def _retention_pass(dec, q, k, v, gates, state0, prev, direction, chunk, n_chunks, out_dtype):
    b, t, _ = q.shape
    tb = chunk * n_chunks
    n = t // tb
    if direction == 0:
        blk = lambda bi, i: (bi, i, 0)
        gblk = lambda bi, i: (bi, i, 0)
    else:
        blk = lambda bi, i: (bi, n - 1 - i, 0)
        gblk = lambda bi, i: (bi, n - 1 - i, 1)
    tok = pl.BlockSpec((1, tb, GROUP_W), blk)
    tok_t = pl.BlockSpec((1, GROUP_W, tb), lambda bi, i: (bi, 0, blk(bi, i)[1]))
    st_spec = pl.BlockSpec((1, GROUP_W, GROUP_W), lambda bi, i: (bi, 0, 0))
    in_specs = [pl.BlockSpec(memory_space=pltpu.SMEM), tok, tok, tok_t, tok,
                pl.BlockSpec((1, tb, GROUP_W), gblk), st_spec]
    args = [dec, q, k, jnp.swapaxes(k, 1, 2), v, gates, state0]
    if prev is not None:
        in_specs.append(tok)
        args.append(prev)
    return pl.pallas_call(
        functools.partial(_ret_kernel, chunk=chunk, n_chunks=n_chunks, direction=direction,
                          has_prev=prev is not None),
        grid=(b, n),
        in_specs=in_specs,
        out_specs=[tok, st_spec],
        out_shape=[jax.ShapeDtypeStruct((b, t, GROUP_W), out_dtype),
                   jax.ShapeDtypeStruct((b, GROUP_W, GROUP_W), F32)],
        scratch_shapes=[pltpu.VMEM((GROUP_W, GROUP_W), F32),
                        pltpu.VMEM((N_HEADS * chunk, chunk), F32),
                        pltpu.VMEM((chunk, GROUP_W), F32),
                        pltpu.VMEM((GROUP_W, chunk), F32),
                        pltpu.VMEM((GROUP_W, GROUP_W), F32)],
        compiler_params=_cparams(("parallel", "arbitrary")),
        name="retention_fwd" if direction == 0 else "retention_bwd",
    )(*args)


def _retention(dec, xq, xk, xv, xg, yq, yk, yv, yg, chunk, chunks_per_step):
    b = xq.shape[0]
    zero = jnp.zeros((b, GROUP_W, GROUP_W), F32)
    ny = yq.shape[1] // chunk
    yb, sb = _retention_pass(dec, yq, yk, yv, yg, zero, None, 1, chunk, ny, F32)
    y, sf = _retention_pass(dec, yq, yk, yv, yg, zero, yb, 0, chunk, ny, BF16)
    xb, _ = _retention_pass(dec, xq, xk, xv, xg, sb, None, 1, chunk, chunks_per_step, F32)
    x, _ = _retention_pass(dec, xq, xk, xv, xg, sf, xb, 0, chunk, chunks_per_step, BF16)
    return x, y


def _na_bias_kernel(rpb_ref, o_ref):
    h = pl.program_id(0)
    dr0 = pl.program_id(1)
    c = lax.broadcasted_iota(jnp.int32, (GRID_W, GRID_W), 0)
    kc = lax.broadcasted_iota(jnp.int32, (GRID_W, GRID_W), 1)
    c0 = jnp.clip(c - NA_KC // 2, 0, GRID_W - NA_KC)
    col_in = (kc >= c0) & (kc < c0 + NA_KC)
    dc = jnp.clip(kc - c, -(NA_KC - 1), NA_KC - 1) + NA_KC - 1
    n_dc = 2 * NA_KC - 1
    for j in range(NA_KR):
        base = (h * (2 * NA_KR - 1) + dr0 + j) * n_dc
        acc = jnp.zeros((GRID_W, GRID_W), F32)
        for d in range(n_dc):
            acc = jnp.where(dc == d, rpb_ref[base + d], acc)
        o_ref[0, :, j * GRID_W:(j + 1) * GRID_W] = jnp.where(col_in, acc, NEG_INF)


def _na_bias_table(rpb):
    return pl.pallas_call(
        _na_bias_kernel,
        grid=(N_HEADS, NA_KR),
        in_specs=[pl.BlockSpec(memory_space=pltpu.SMEM)],
        out_specs=pl.BlockSpec((1, GRID_W, NA_KR * GRID_W), lambda h, r: (r, h, 0)),
        out_shape=jax.ShapeDtypeStruct((NA_KR, N_HEADS * GRID_W, NA_KR * GRID_W), F32),
        compiler_params=_cparams(("parallel", "parallel")),
        name="na_bias_table",
    )(rpb.reshape(-1))


def _na_kernel(q_ref, k_ref, v_ref, ky_ref, vy_ref, tb_ref, o_ref,
               sa_ref, sb_ref, pa_ref, pb_ref, la_ref, lb_ref, *, rows_per_step, n_rows):
    r_base = pl.program_id(1) * rows_per_step
    win = NA_KR * GRID_W

    head_of_lane = lax.broadcasted_iota(jnp.int32, (GRID_W, GROUP_W), 1) // HEAD_DIM
    in_head = [head_of_lane == h for h in range(N_HEADS)]
    head_mask = [jnp.where(mk, 1.0, 0.0).astype(BF16) for mk in in_head]

    def geometry(i):
        r = r_base + i
        r0 = jnp.clip(r - NA_KR // 2, 0, n_rows - NA_KR)
        return pl.multiple_of(r0 * GRID_W, GRID_W), r0 - r + NA_KR - 1

    def row_slice(i):
        return pl.ds(pl.multiple_of(i * GRID_W, GRID_W), GRID_W)

    def scores(i, s_ref):
        koff, dr0 = geometry(i)
        q_all = q_ref[0, row_slice(i), :]
        q4 = jnp.concatenate([q_all * head_mask[h] for h in range(N_HEADS)], axis=0)
        s_ref[:, :win] = _dot_nt(q4, k_ref[0, pl.ds(koff, win), :]) + tb_ref[dr0]
        s_ref[:, win:] = _dot_nt(q4, ky_ref[0])

    def softmax(s_ref, p_ref, l_ref):
        s = s_ref[...]
        p = jnp.exp(s - _row_max(s))
        l_ref[...] = 1.0 / _row_sum(p)
        p_ref[...] = p.astype(BF16)

    def values(i, p_ref, l_ref):
        koff, _ = geometry(i)
        o4 = (_dot(p_ref[:, :win], v_ref[0, pl.ds(koff, win), :])
              + _dot(p_ref[:, win:], vy_ref[0])) * l_ref[...]
        out = o4[:GRID_W]
        for h in range(1, N_HEADS):
            out = jnp.where(in_head[h], o4[h * GRID_W:(h + 1) * GRID_W], out)
        o_ref[0, row_slice(i), :] = out.astype(o_ref.dtype)

    pb_ref[...] = jnp.zeros(pb_ref.shape, BF16)
    lb_ref[...] = jnp.zeros(lb_ref.shape, F32)
    scores(0, sa_ref)

    def pair_body(tt, carry):
        t = 2 * tt
        values(jnp.maximum(t - 1, 0), pb_ref, lb_ref)
        scores(t + 1, sb_ref)
        softmax(sa_ref, pa_ref, la_ref)
        values(t, pa_ref, la_ref)
        scores(jnp.minimum(t + 2, rows_per_step - 1), sa_ref)
        softmax(sb_ref, pb_ref, lb_ref)
        return carry

    lax.fori_loop(0, rows_per_step // 2, pair_body, 0)
    values(rows_per_step - 1, pb_ref, lb_ref)


def _na_attention(q, k, v, ky, vy, table, rows_per_step):
    b, s, _ = q.shape
    l = ky.shape[1]
    n_rows = s // GRID_W
    tq = rows_per_step * GRID_W
    n_keys = NA_KR * GRID_W + l
    seq = lambda n: pl.BlockSpec((1, n, GROUP_W), lambda bi, i: (bi, 0, 0))
    return pl.pallas_call(
        functools.partial(_na_kernel, rows_per_step=rows_per_step, n_rows=n_rows),
        grid=(b, n_rows // rows_per_step),
        in_specs=[pl.BlockSpec((1, tq, GROUP_W), lambda bi, i: (bi, i, 0)),
                  seq(s), seq(s), seq(l), seq(l), _const_spec(table.shape)],
        out_specs=pl.BlockSpec((1, tq, GROUP_W), lambda bi, i: (bi, i, 0)),
        out_shape=jax.ShapeDtypeStruct((b, s, GROUP_W), BF16),
        scratch_shapes=([pltpu.VMEM((N_HEADS * GRID_W, n_keys), F32)] * 2
                        + [pltpu.VMEM((N_HEADS * GRID_W, n_keys), BF16)] * 2
                        + [pltpu.VMEM((N_HEADS * GRID_W, 1), F32)] * 2),
        compiler_params=_cparams(("parallel", "arbitrary")),
        name="na_attention",
    )(q, k, v, ky, vy, table)


def _swa_kernel(sink_ref, q_ref, kp_ref, kc_ref, kn_ref, vp_ref, vc_ref, vn_ref, ky_ref, vy_ref, o_ref,
                kw_ref, vw_ref, wb_ref, sa_ref, sb_ref, pa_ref, pb_ref, la_ref, lb_ref, *, blocks_per_step):
    step = pl.program_id(1)
    nb = pl.num_programs(1) * blocks_per_step
    bl = SWA_BLOCK
    g = N_HEADS // SWA_KV_HEADS
    tq = blocks_per_step * bl
    kw_ref[0:bl] = kp_ref[0]
    kw_ref[bl:bl + tq] = kc_ref[0]
    kw_ref[bl + tq:] = kn_ref[0]
    vw_ref[0:bl] = vp_ref[0]
    vw_ref[bl:bl + tq] = vc_ref[0]
    vw_ref[bl + tq:] = vn_ref[0]

    qi = lax.broadcasted_iota(jnp.int32, (g * bl, 3 * bl), 0) % bl
    jk = lax.broadcasted_iota(jnp.int32, (g * bl, 3 * bl), 1)
    wb_ref[...] = jnp.where(jnp.abs(jk - bl - qi) <= SWA_WINDOW, 0.0, NEG_INF)
    half = lax.broadcasted_iota(jnp.int32, (g * bl, 1), 0) // bl

    head_of_lane = lax.broadcasted_iota(jnp.int32, (bl, GROUP_W), 1) // HEAD_DIM
    in_head = [head_of_lane == h for h in range(N_HEADS)]
    head_mask = [jnp.where(mk, 1.0, 0.0).astype(BF16) for mk in in_head]

    def blk_slice(j):
        return pl.ds(pl.multiple_of(j * bl, bl), bl)

    def scores(j, s_ref):
        qoff = pl.multiple_of(j * bl, bl)
        q_all = q_ref[0, blk_slice(j), :]
        kw = kw_ref[pl.ds(qoff, 3 * bl), :]
        n = step * blocks_per_step + j
        lo_edge = jnp.where(n == 0, NEG_INF, 0.0)
        hi_edge = jnp.where(n == nb - 1, NEG_INF, 0.0)
        for kh in range(SWA_KV_HEADS):
            q = jnp.concatenate([q_all * head_mask[kh * g + gi] for gi in range(g)], axis=0)
            s = _dot_nt(q, kw) + wb_ref[...]
            s_ref[kh, :, :bl] = s[:, :bl] + lo_edge
            s_ref[kh, :, bl:2 * bl] = s[:, bl:2 * bl]
            s_ref[kh, :, 2 * bl:3 * bl] = s[:, 2 * bl:] + hi_edge
            s_ref[kh, :, 3 * bl:] = _dot_nt(q, ky_ref[0])

    def softmax(s_ref, p_ref, l_ref):
        for kh in range(SWA_KV_HEADS):
            s = s_ref[kh]
            sink = jnp.full((g * bl, 1), sink_ref[kh * g], F32)
            for gi in range(1, g):
                sink = jnp.where(half == gi, sink_ref[kh * g + gi], sink)
            m = jnp.maximum(_row_max(s), sink)
            p = jnp.exp(s - m)
            l_ref[kh] = 1.0 / (_row_sum(p) + jnp.exp(sink - m))
            p_ref[kh] = p.astype(BF16)

    def values(j, p_ref, l_ref):
        vw = vw_ref[pl.ds(pl.multiple_of(j * bl, bl), 3 * bl), :]
        out = jnp.zeros((bl, GROUP_W), F32)
        for kh in range(SWA_KV_HEADS):
            o = (_dot(p_ref[kh, :, :3 * bl], vw) + _dot(p_ref[kh, :, 3 * bl:], vy_ref[0])) * l_ref[kh]
            for gi in range(g):
                out = jnp.where(in_head[kh * g + gi], o[gi * bl:(gi + 1) * bl], out)
        o_ref[0, blk_slice(j), :] = out.astype(o_ref.dtype)

    pb_ref[...] = jnp.zeros(pb_ref.shape, BF16)
    lb_ref[...] = jnp.zeros(lb_ref.shape, F32)
    scores(0, sa_ref)

    def pair_body(tt, carry):
        t = 2 * tt
        values(jnp.maximum(t - 1, 0), pb_ref, lb_ref)
        scores(t + 1, sb_ref)
        softmax(sa_ref, pa_ref, la_ref)
        values(t, pa_ref, la_ref)
        scores(jnp.minimum(t + 2, blocks_per_step - 1), sa_ref)
        softmax(sb_ref, pb_ref, lb_ref)
        return carry

    lax.fori_loop(0, blocks_per_step // 2, pair_body, 0)
    values(blocks_per_step - 1, pb_ref, lb_ref)


def _swa_attention(sink, q, k, v, ky, vy, blocks_per_step):
    b, s, _ = q.shape
    l = ky.shape[1]
    tq = blocks_per_step * SWA_BLOCK
    n_steps = s // tq
    nb = s // SWA_BLOCK
    kvw = GROUP_W
    q_rows = N_HEADS // SWA_KV_HEADS * SWA_BLOCK
    n_keys = 3 * SWA_BLOCK + l
    prev = pl.BlockSpec((1, SWA_BLOCK, kvw), lambda bi, i: (bi, jnp.maximum(i * blocks_per_step - 1, 0), 0))
    cur = pl.BlockSpec((1, tq, kvw), lambda bi, i: (bi, i, 0))
    nxt = pl.BlockSpec((1, SWA_BLOCK, kvw), lambda bi, i: (bi, jnp.minimum((i + 1) * blocks_per_step, nb - 1), 0))
    ctx = pl.BlockSpec((1, l, kvw), lambda bi, i: (bi, 0, 0))
    return pl.pallas_call(
        functools.partial(_swa_kernel, blocks_per_step=blocks_per_step),
        grid=(b, n_steps),
        in_specs=[pl.BlockSpec(memory_space=pltpu.SMEM),
                  pl.BlockSpec((1, tq, GROUP_W), lambda bi, i: (bi, i, 0)),
                  prev, cur, nxt, prev, cur, nxt, ctx, ctx],
        out_specs=pl.BlockSpec((1, tq, GROUP_W), lambda bi, i: (bi, i, 0)),
        out_shape=jax.ShapeDtypeStruct((b, s, GROUP_W), BF16),
        scratch_shapes=[pltpu.VMEM((tq + 2 * SWA_BLOCK, kvw), BF16),
                        pltpu.VMEM((tq + 2 * SWA_BLOCK, kvw), BF16),
                        pltpu.VMEM((q_rows, 3 * SWA_BLOCK), F32)]
                       + [pltpu.VMEM((SWA_KV_HEADS, q_rows, n_keys), F32)] * 2
                       + [pltpu.VMEM((SWA_KV_HEADS, q_rows, n_keys), BF16)] * 2
                       + [pltpu.VMEM((SWA_KV_HEADS, q_rows, 1), F32)] * 2,
        compiler_params=_cparams(("parallel", "arbitrary")),
        name="swa_attention",
    )(sink, q, k, k, k, v, v, v, ky, vy)


def _outffn_kernel(*refs, hidden_chunk, final):
    if final:
        (x_ref, m0_ref, m1_ref, m2_ref, m3_ref, wo_ref, g1_ref, n2_ref, sc_ref, sh_ref, g2_ref,
         w1_ref, w3_ref, w2_ref, fg_ref, o_ref) = refs
    else:
        (x_ref, m0_ref, m1_ref, m2_ref, m3_ref, wo_ref, g1_ref, n2_ref, sc_ref, sh_ref, g2_ref,
         w1_ref, w3_ref, w2_ref, o_ref) = refs
    mix = None
    for gi, m_ref in enumerate((m0_ref, m1_ref, m2_ref, m3_ref)):
        part = _dot(m_ref[0], wo_ref[gi * GROUP_W:(gi + 1) * GROUP_W, :])
        mix = part if mix is None else mix + part
    x1 = x_ref[0] + g1_ref[0] * mix
    hb = ((_rms(x1) * n2_ref[...]) * (1.0 + sc_ref[0]) + sh_ref[0]).astype(BF16)
    hidden = w1_ref.shape[1]
    acc = None
    for c0 in range(0, hidden, hidden_chunk):
        a = _dot(hb, w1_ref[:, c0:c0 + hidden_chunk])
        bgate = _dot(hb, w3_ref[:, c0:c0 + hidden_chunk])
        u = (_silu(a) * bgate).astype(BF16)
        part = _dot(u, w2_ref[c0:c0 + hidden_chunk, :])
        acc = part if acc is None else acc + part
    x2 = x1 + g2_ref[0] * acc
    if final:
        x2 = _rms(x2) * fg_ref[...]
    o_ref[0] = x2


def _outffn(x, mixes, wo, g1, n2, sc2, sh2, g2, w1, w3, w2, final_g, tm):
    b, t, d = x.shape
    tok = lambda wd: pl.BlockSpec((1, tm, wd), lambda bi, i: (bi, i, 0))
    vec = pl.BlockSpec((1, 1, d), lambda bi, i: (bi, 0, 0))
    in_specs = ([tok(d)] + [tok(GROUP_W)] * 4
                + [_const_spec(wo.shape), vec, _const_spec((1, d)), vec, vec, vec,
                   _const_spec(w1.shape), _const_spec(w3.shape), _const_spec(w2.shape)])
    args = [x, *mixes, wo, g1, n2, sc2, sh2, g2, w1, w3, w2]
    final = final_g is not None
    if final:
        in_specs.append(_const_spec((1, d)))
        args.append(final_g)
    return pl.pallas_call(
        functools.partial(_outffn_kernel, hidden_chunk=256, final=final),
        grid=(b, t // tm),
        in_specs=in_specs,
        out_specs=tok(d),
        out_shape=jax.ShapeDtypeStruct((b, t, d), F32),
        compiler_params=_cparams(("parallel", "parallel")),
        name="out_proj_ffn_final" if final else "out_proj_ffn",
    )(*args)


def _prep_weights(w_in, mla_w_uq, mla_w_ukv):
    depth, d, _ = w_in.shape
    offs = [0]
    for sz in IN_SIZES:
        offs.append(offs[-1] + sz)
    cols = [w_in[:, :, offs[i]:offs[i + 1]] for i in range(len(IN_SIZES))]
    cq, ckv, kr, rq, rk, rv, rgf, rgb, nq, nk, nv, sq, sk, sv = cols
    scale = HEAD_DIM ** -0.5
    zpad = jnp.zeros((depth, d, MLA_HEAD_PAD - MLA_NOPE - MLA_ROPE), F32)
    znope = jnp.zeros((depth, d, MLA_NOPE), F32)
    krep = jnp.concatenate([znope, kr, zpad] * N_HEADS, axis=-1)

    def per_query_head(t):
        g = N_HEADS // SWA_KV_HEADS
        t = t.reshape(depth, d, SWA_KV_HEADS, 1, HEAD_DIM)
        return jnp.broadcast_to(t, (depth, d, SWA_KV_HEADS, g, HEAD_DIM)).reshape(depth, d, GROUP_W)

    w = jnp.concatenate([cq, ckv, krep, rq, rk * scale, rv, rgf, rgb, nq * scale, nk, nv,
                         sq * scale, per_query_head(sk), per_query_head(sv)], axis=-1).astype(BF16)

    qr = mla_w_uq.shape[1]
    uq = mla_w_uq.reshape(depth, qr, N_HEADS, MLA_NOPE + MLA_ROPE)
    wuq = jnp.concatenate([uq, jnp.zeros((depth, qr, N_HEADS, MLA_HEAD_PAD - MLA_NOPE - MLA_ROPE), F32)],
                          axis=-1).reshape(depth, qr, N_HEADS * MLA_HEAD_PAD).astype(BF16)
    kvr = mla_w_ukv.shape[1]
    ukv = mla_w_ukv.reshape(depth, kvr, N_HEADS, MLA_NOPE + MLA_V)
    zk = jnp.zeros((depth, kvr, N_HEADS, MLA_HEAD_PAD - MLA_NOPE), F32)
    wuk = jnp.concatenate([ukv[..., :MLA_NOPE], zk], axis=-1).reshape(depth, kvr, -1).astype(BF16)
    zv = jnp.zeros((depth, kvr, N_HEADS, MLA_HEAD_PAD - MLA_V), F32)
    wuv = jnp.concatenate([ukv[..., MLA_NOPE:], zv], axis=-1).reshape(depth, kvr, -1).astype(BF16)
    return w, wuq, wuk, wuv


def _ctx_head_specs():
    mla = tuple((h * MLA_HEAD_PAD, (h + 1) * MLA_HEAD_PAD, h * MLA_HEAD_PAD, (h + 1) * MLA_HEAD_PAD,
                 h * MLA_HEAD_PAD, h * MLA_HEAD_PAD + MLA_V, None) for h in range(N_HEADS))
    na = tuple((h * HEAD_DIM, (h + 1) * HEAD_DIM) * 3 + (None,) for h in range(N_HEADS))
    swa = tuple((h * HEAD_DIM, (h + 1) * HEAD_DIM) * 3 + (h,) for h in range(N_HEADS))
    return mla, na, swa


def kernel(x, c, ctx, c_ctx, ada_w, ada_b, norm1_g, w_in, mla_q_norm, mla_w_uq, mla_kv_norm, mla_w_ukv,
           ret_decay, na_rpb, swa_sink, w_out, norm2_g, ffn_w1, ffn_w3, ffn_w2, final_norm_g):
    b, s, d = x.shape
    l_ctx = ctx.shape[1]
    depth = ada_w.shape[0]
    assert b + 1 <= 8 and s % 1024 == 0 and l_ctx % 128 == 0

    cond = jnp.concatenate([c, c_ctx[None, :], jnp.zeros((8 - b - 1, d), F32)], axis=0)
    mod = _modulation(cond, ada_w, ada_b)
    tables = _rope_tables(s)
    w_all, wuq_all, wuk_all, wuv_all = _prep_weights(w_in, mla_w_uq, mla_w_ukv)
    wo_all = w_out.astype(BF16)
    w1_all, w3_all, w2_all = ffn_w1.astype(BF16), ffn_w3.astype(BF16), ffn_w2.astype(BF16)
    mla_heads, na_heads, swa_heads = _ctx_head_specs()
    no_sink = jnp.zeros((N_HEADS,), F32)

    tm_x = 512
    tm_y = min(256, l_ctx)
    y = ctx
    for l in range(depth):
        mx = [mod[l, :b, j * d:(j + 1) * d][:, None, :] for j in range(6)]
        my = [jnp.broadcast_to(mod[l, b, j * d:(j + 1) * d][None, None, :], (b, 1, d)) for j in range(6)]
        n1 = norm1_g[l][None, :]
        n2 = norm2_g[l][None, :]
        qn = mla_q_norm[l][None, :]
        kvn = mla_kv_norm[l][None, :]
        lw = (w_all[l], qn, kvn, wuq_all[l], wuk_all[l], wuv_all[l])

        px = _inproj(x, n1, mx[1], mx[0], *lw, tables, tm_x)
        py = _inproj(y, n1, my[1], my[0], *lw, None, tm_y)
        (xmq, xmk, xmv, xrq, xrk, xrv, xrg, xnq, xnk, xnv, xsq, xsk, xsv) = px
        (ymq, ymk, ymv, yrq, yrk, yrv, yrg, ynq, ynk, ynv, ysq, ysk, ysv) = py

        mla_x = _mla_attention(xmq, xmk, xmv, ymk, ymv, tq=512, tk=512, n_sub=2)
        dec = ret_decay[l].reshape(-1)
        ret_x, ret_y = _retention(dec, xrq, xrk, xrv, xrg, yrq, yrk, yrv, yrg, chunk=128, chunks_per_step=8)
        table = _na_bias_table(na_rpb[l])
        na_x = _na_attention(xnq, xnk, xnv, ynk, ynv, table, rows_per_step=16)
        swa_x = _swa_attention(swa_sink[l], xsq, xsk, xsv, ysk, ysv, blocks_per_step=8)

        last = l == depth - 1
        x = _outffn(x, (mla_x, ret_x, na_x, swa_x), wo_all[l], mx[2], n2, mx[4], mx[3], mx[5],
                    w1_all[l], w3_all[l], w2_all[l], final_norm_g[None, :] if last else None, tm_x)
        if not last:
            mla_y = _ctx_attention(ymq, ymk, ymv, mla_heads, no_sink, "mla_ctx_attention", base2=True)
            na_y = _ctx_attention(ynq, ynk, ynv, na_heads, no_sink, "na_ctx_attention")
            swa_y = _ctx_attention(ysq, ysk, ysv, swa_heads, swa_sink[l], "swa_ctx_attention")
            y = _outffn(y, (mla_y, ret_y, na_y, swa_y), wo_all[l], my[2], n2, my[4], my[3], my[5],
                        w1_all[l], w3_all[l], w2_all[l], None, tm_y)
    return x
```

```python
import functools

import jax
import jax.numpy as jnp
from jax import lax
from jax.experimental import pallas as pl
from jax.experimental.pallas import tpu as pltpu

F32 = jnp.float32
BF16 = jnp.bfloat16

GRID_W = 64
HEAD_DIM = 64
N_HEADS = 4
GROUP_W = N_HEADS * HEAD_DIM
MLA_Q_RANK = 256
MLA_KV_RANK = 128
MLA_NOPE = 64
MLA_ROPE = 32
MLA_V = 64
MLA_HEAD_PAD = 128
MLA_V_ROWS = MLA_V + 16
NA_KR = 8
NA_KC = 16
SWA_KV_HEADS = 2
SWA_WINDOW = 128
SWA_BLOCK = 128
ROPE_THETA = 10000.0
EPS = 1e-6
NEG_INF = -1e30
LOG2_E = 1.4426950408889634
LANES = 128
VMEM_LIMIT = 56 * 1024 * 1024

IN_SIZES = (MLA_Q_RANK, MLA_KV_RANK, MLA_ROPE,
            GROUP_W, GROUP_W, GROUP_W, GROUP_W, GROUP_W,
            GROUP_W, GROUP_W, GROUP_W,
            GROUP_W, SWA_KV_HEADS * HEAD_DIM, SWA_KV_HEADS * HEAD_DIM)

_O_CQ = 0
_O_CKV = _O_CQ + MLA_Q_RANK
_O_KREP = _O_CKV + MLA_KV_RANK
_O_RQK = _O_KREP + N_HEADS * MLA_HEAD_PAD
_O_RV = _O_RQK + 2 * GROUP_W
_O_RG = _O_RV + GROUP_W
_O_NA = _O_RG + 2 * GROUP_W
_O_SQK = _O_NA + 3 * GROUP_W
_O_SV = _O_SQK + 2 * GROUP_W
_IN_COLS = _O_SV + GROUP_W


def _cparams(sem):
    return pltpu.CompilerParams(dimension_semantics=sem, vmem_limit_bytes=VMEM_LIMIT)


def _dot(a, b):
    return jnp.dot(a, b, preferred_element_type=F32)


def _dot_nt(a, b):
    return lax.dot_general(a, b, (((1,), (1,)), ((), ())), preferred_element_type=F32)


def _dot_tn(a, b):
    return lax.dot_general(a, b, (((0,), (0,)), ((), ())), preferred_element_type=F32)


def _rms(x):
    return x * lax.rsqrt(jnp.mean(x * x, axis=-1, keepdims=True) + EPS)


def _silu(x):
    return x * jax.nn.sigmoid(x)


def _lane_chunks(arrays):
    return [a[:, j * LANES:(j + 1) * LANES] for a in arrays for j in range(a.shape[-1] // LANES)]


def _row_max(*arrays):
    return jnp.max(functools.reduce(jnp.maximum, _lane_chunks(arrays)), axis=-1, keepdims=True)


def _row_sum(*arrays):
    return jnp.sum(functools.reduce(jnp.add, _lane_chunks(arrays)), axis=-1, keepdims=True)


def _mod_kernel(c_ref, w_ref, b_ref, o_ref):
    o_ref[0] = _dot(_silu(c_ref[...]), w_ref[0]) + b_ref[0]


def _modulation(cond, ada_w, ada_b):
    depth, d, d6 = ada_w.shape
    n = d6 // d
    return pl.pallas_call(
        _mod_kernel,
        grid=(depth, n),
        in_specs=[pl.BlockSpec((8, d), lambda l, j: (0, 0)),
                  pl.BlockSpec((1, d, d), lambda l, j: (l, 0, j)),
                  pl.BlockSpec((1, 1, d), lambda l, j: (l, 0, j))],
        out_specs=pl.BlockSpec((1, 8, d), lambda l, j: (l, 0, j)),
        out_shape=jax.ShapeDtypeStruct((depth, 8, d6), F32),
        compiler_params=_cparams(("parallel", "parallel")),
        name="ada_modulation",
    )(cond, ada_w, ada_b.reshape(depth, 1, d6))


def _rope_tables(seq):
    n_rows = seq // GRID_W

    def parts(pos, d):
        inv = ROPE_THETA ** (-jnp.arange(0, d, 2, dtype=F32) / d)
        ang = pos.astype(F32)[:, None] * inv[None, :]
        z = jnp.zeros_like(ang)
        return (jnp.concatenate([jnp.cos(ang), jnp.cos(ang)], axis=-1),
                jnp.concatenate([z, jnp.sin(ang)], axis=-1),
                jnp.concatenate([-jnp.sin(ang), z], axis=-1))

    def expand(by_row, by_col):
        w = by_row.shape[-1]
        r = jnp.broadcast_to(by_row[:, None, :], (n_rows, GRID_W, w))
        c = jnp.broadcast_to(by_col[None, :, :], (n_rows, GRID_W, w))
        return jnp.concatenate([r, c], axis=-1).reshape(seq, 2 * w)

    def tables(d, fill):
        per_head = [expand(a, b) for a, b in zip(parts(jnp.arange(n_rows), d), parts(jnp.arange(GRID_W), d))]
        return [fill(t, i) for i, t in enumerate(per_head)]

    def two_heads(t, _):
        return jnp.concatenate([t, t], axis=-1)

    def mla_slot(t, i):
        lead = (jnp.ones if i == 0 else jnp.zeros)((seq, MLA_NOPE), F32)
        tail = (jnp.ones if i == 0 else jnp.zeros)((seq, MLA_HEAD_PAD - MLA_NOPE - MLA_ROPE), F32)
        return jnp.concatenate([lead, t, tail], axis=-1)

    return tuple(tables(HEAD_DIM // 2, two_heads) + tables(MLA_ROPE // 2, mla_slot))


def _rope(x, cos, s_prev, s_next, d):
    out = []
    for j in range(x.shape[-1] // LANES):
        xc = x[:, j * LANES:(j + 1) * LANES]
        out.append(xc * cos + pltpu.roll(xc, d, 1) * s_prev + pltpu.roll(xc, LANES - d, 1) * s_next)
    return out[0] if len(out) == 1 else jnp.concatenate(out, axis=-1)


def _inproj_kernel(*refs, rotate, mla_scale):
    if rotate:
        (x_ref, g_ref, sc_ref, sh_ref, w_ref, qn_ref, kvn_ref, wuq_ref, wuk_ref, wuv_ref,
         c64_ref, p64_ref, n64_ref, cm_ref, pm_ref, nm_ref,
         mq_ref, mk_ref, mv_ref, rq_ref, rk_ref, rv_ref, rg_ref,
         nq_ref, nk_ref, nv_ref, sq_ref, sk_ref, sv_ref) = refs
    else:
        (x_ref, g_ref, sc_ref, sh_ref, w_ref, qn_ref, kvn_ref, wuq_ref, wuk_ref, wuv_ref,
         mq_ref, mk_ref, mv_ref, rq_ref, rk_ref, rv_ref, rg_ref,
         nq_ref, nk_ref, nv_ref, sq_ref, sk_ref, sv_ref) = refs

    x = x_ref[0]
    h = (_rms(x) * g_ref[...]) * (1.0 + sc_ref[0]) + sh_ref[0]
    hb = h.astype(BF16)

    def proj(lo, hi):
        return _dot(hb, w_ref[:, lo:hi])

    def rope64(v):
        if not rotate:
            return v
        return _rope(v, c64_ref[...], p64_ref[...], n64_ref[...], HEAD_DIM // 4)

    def rope_mla(v):
        if not rotate:
            return v
        return _rope(v, cm_ref[...], pm_ref[...], nm_ref[...], MLA_ROPE // 4)

    cq = (_rms(proj(_O_CQ, _O_CKV)) * qn_ref[...]).astype(BF16)
    q = rope_mla(_dot(cq, wuq_ref[...])) * mla_scale
    mq_ref[0] = q.astype(BF16)
    ckv = (_rms(proj(_O_CKV, _O_KREP)) * kvn_ref[...]).astype(BF16)
    k = rope_mla(_dot(ckv, wuk_ref[...]) + proj(_O_KREP, _O_RQK))
    mk_ref[0] = k.astype(BF16)
    v = _dot(ckv, wuv_ref[...])
    lane = lax.broadcasted_iota(jnp.int32, v.shape, 1)
    mv_ref[0] = jnp.where(lane % MLA_HEAD_PAD >= MLA_V, 1.0, v).astype(BF16)

    rqk = proj(_O_RQK, _O_RV)
    rq_ref[0] = rope64(rqk[:, :GROUP_W]).astype(BF16)
    rk_ref[0] = rope64(rqk[:, GROUP_W:]).astype(BF16)
    rv_ref[0] = proj(_O_RV, _O_RG).astype(BF16)
    rg_ref[0] = proj(_O_RG, _O_NA)

    na = proj(_O_NA, _O_SQK)
    nq_ref[0] = na[:, :GROUP_W].astype(BF16)
    nk_ref[0] = na[:, GROUP_W:2 * GROUP_W].astype(BF16)
    nv_ref[0] = na[:, 2 * GROUP_W:].astype(BF16)

    sqk = proj(_O_SQK, _O_SV)
    sq_ref[0] = rope64(sqk[:, :GROUP_W]).astype(BF16)
    sk_ref[0] = rope64(sqk[:, GROUP_W:]).astype(BF16)
    sv_ref[0] = proj(_O_SV, _IN_COLS).astype(BF16)


def _const_spec(shape):
    nd = len(shape)
    return pl.BlockSpec(shape, lambda *_: (0,) * nd, pipeline_mode=pl.Buffered(1))


def _inproj(x, gain, scale, shift, w, qn, kvn, wuq, wuk, wuv, tables, tm):
    b, t, d = x.shape
    rotate = tables is not None
    kv_w = GROUP_W
    mla_w = N_HEADS * MLA_HEAD_PAD
    tok = lambda wd: pl.BlockSpec((1, tm, wd), lambda bi, i: (bi, i, 0))
    vec = pl.BlockSpec((1, 1, d), lambda bi, i: (bi, 0, 0))
    in_specs = [tok(d), _const_spec((1, d)), vec, vec, _const_spec(w.shape),
                _const_spec(qn.shape), _const_spec(kvn.shape), _const_spec(wuq.shape),
                _const_spec(wuk.shape), _const_spec(wuv.shape)]
    args = [x, gain, scale, shift, w, qn, kvn, wuq, wuk, wuv]
    if rotate:
        in_specs += [pl.BlockSpec((tm, LANES), lambda bi, i: (i, 0))] * 6
        args += list(tables)
    widths = [mla_w, mla_w, mla_w, GROUP_W, GROUP_W, GROUP_W, 2 * GROUP_W,
              GROUP_W, GROUP_W, GROUP_W, GROUP_W, kv_w, kv_w]
    dtypes = [BF16] * 6 + [F32] + [BF16] * 6
    return pl.pallas_call(
        functools.partial(_inproj_kernel, rotate=rotate, mla_scale=(MLA_NOPE + MLA_ROPE) ** -0.5 * LOG2_E),
        grid=(b, t // tm),
        in_specs=in_specs,
        out_specs=[tok(wd) for wd in widths],
        out_shape=[jax.ShapeDtypeStruct((b, t, wd), dt) for wd, dt in zip(widths, dtypes)],
        compiler_params=_cparams(("parallel", "parallel")),
        name="in_proj_rot" if rotate else "in_proj_ctx",
    )(*args)


def _mla_kernel(q_ref, kx_ref, vx_ref, ky_ref, vy_ref, o_ref, m_ref, acc_ref, sa_ref, sb_ref, sc_ref,
                ma_ref, mb_ref, mc_ref, *, tk, heads, n_sub, unroll):
    n_chunks = kx_ref.shape[1] // tk
    tq = q_ref.shape[1]
    ts = tq // n_sub
    chains = [(h, u) for h in range(heads) for u in range(n_sub)]
    for c in range(len(chains)):
        m_ref[c] = jnp.full(m_ref.shape[1:], NEG_INF, F32)
        acc_ref[c] = jnp.zeros(acc_ref.shape[1:], F32)

    def rows(h):
        return slice(h * MLA_HEAD_PAD, (h + 1) * MLA_HEAD_PAD)

    def vrows(h):
        return slice(h * MLA_HEAD_PAD, h * MLA_HEAD_PAD + MLA_V_ROWS)

    buf_a, buf_b, buf_c = (sa_ref, ma_ref), (sb_ref, mb_ref), (sc_ref, mc_ref)

    def scores(c, k, dst):
        h, u = chains[c]
        st = _dot_nt(k, q_ref[0, u * ts:(u + 1) * ts, rows(h)])
        dst[0][c] = st
        dst[1][c] = jnp.max(st, axis=0, keepdims=True)

    def absorb(c, src, vt):
        m_old = m_ref[c]
        m_new = jnp.maximum(m_old, src[1][c])
        pt = jnp.exp2(src[0][c] - m_new).astype(BF16)
        acc_ref[c, :MLA_V_ROWS] = acc_ref[c, :MLA_V_ROWS] * jnp.exp2(m_old - m_new) + _dot(vt, pt)
        m_ref[c] = m_new

    def kx(j, h):
        return kx_ref[0, pl.ds(pl.multiple_of(j * tk, tk), tk), rows(h)]

    def stage(j_next, nxt, j_cur, cur):
        for c, (h, _) in enumerate(chains):
            scores(c, kx(j_next, h), nxt)
            absorb(c, cur, vx_ref[0, j_cur, vrows(h), :])

    for c, (h, _) in enumerate(chains):
        scores(c, kx(0, h), buf_a)

    def stage_at(j, parity):
        cur, nxt = (buf_a, buf_b) if parity == 0 else (buf_b, buf_a)
        stage(j + 1, nxt, j, cur)

    def body(jj, carry):
        for i in range(unroll):
            stage_at(unroll * jj + i, i % 2)
        return carry

    trips = (n_chunks - 1) // unroll
    lax.fori_loop(0, trips, body, 0)
    for j in range(unroll * trips, n_chunks - 1):
        stage_at(j, j % 2)
    last = buf_a if (n_chunks - 1) % 2 == 0 else buf_b
    for c, (h, _) in enumerate(chains):
        scores(c, ky_ref[0, :, rows(h)], buf_c)
        absorb(c, last, vx_ref[0, n_chunks - 1, vrows(h), :])
    outs = [[None] * n_sub for _ in range(heads)]
    for c, (h, u) in enumerate(chains):
        absorb(c, buf_c, vy_ref[0, 0, vrows(h), :])
        acc = acc_ref[c].T
        outs[h][u] = acc[:, :MLA_V] / acc[:, MLA_V:MLA_V + 1]
    o_ref[0] = jnp.concatenate([jnp.concatenate(outs[h], axis=0) for h in range(heads)],
                               axis=-1).astype(o_ref.dtype)


def _mla_attention(q, kx, vx, ky, vy, tq, tk, n_sub, unroll):
    b, s, _ = q.shape
    l = ky.shape[1]
    hp = 2
    wd = hp * MLA_HEAD_PAD
    vxt = jnp.swapaxes(vx.reshape(b, s // tk, tk, -1), 2, 3)
    vyt = jnp.swapaxes(vy.reshape(b, 1, l, -1), 2, 3)
    ts = tq // n_sub
    return pl.pallas_call(
        functools.partial(_mla_kernel, tk=tk, heads=hp, n_sub=n_sub, unroll=unroll),
        grid=(b, N_HEADS // hp, s // tq),
        in_specs=[pl.BlockSpec((1, tq, wd), lambda bi, hi, i: (bi, i, hi)),
                  pl.BlockSpec((1, s, wd), lambda bi, hi, i: (bi, 0, hi)),
                  pl.BlockSpec((1, s // tk, wd, tk), lambda bi, hi, i: (bi, 0, hi, 0)),
                  pl.BlockSpec((1, l, wd), lambda bi, hi, i: (bi, 0, hi)),
                  pl.BlockSpec((1, 1, wd, l), lambda bi, hi, i: (bi, 0, hi, 0))],
        out_specs=pl.BlockSpec((1, tq, hp * MLA_V), lambda bi, hi, i: (bi, i, hi)),
        out_shape=jax.ShapeDtypeStruct((b, s, N_HEADS * MLA_V), BF16),
        scratch_shapes=[pltpu.VMEM((hp * n_sub, 1, ts), F32),
                        pltpu.VMEM((hp * n_sub, MLA_HEAD_PAD, ts), F32),
                        pltpu.VMEM((hp * n_sub, tk, ts), F32),
                        pltpu.VMEM((hp * n_sub, tk, ts), F32),
                        pltpu.VMEM((hp * n_sub, l, ts), F32)]
                       + [pltpu.VMEM((hp * n_sub, 1, ts), F32)] * 3,
        compiler_params=_cparams(("parallel", "parallel", "arbitrary")),
        name="mla_attention",
    )(q, kx, vxt, ky, vyt)


def _ctx_attn_kernel(sink_ref, q_ref, k_ref, v_ref, o_ref, *, heads, base2):
    exp = jnp.exp2 if base2 else jnp.exp
    outs = []
    for (q0, q1, k0, k1, v0, v1, sink_idx) in heads:
        q = q_ref[0, :, q0:q1]
        s = _dot_nt(q, k_ref[0, :, k0:k1])
        m = jnp.max(s, axis=-1, keepdims=True)
        if sink_idx is not None:
            sink = jnp.full((1, 1), sink_ref[sink_idx], F32)
            m = jnp.maximum(m, sink)
        p = exp(s - m)
        l = jnp.sum(p, axis=-1, keepdims=True)
        if sink_idx is not None:
            l = l + exp(sink - m)
        outs.append(_dot(p.astype(BF16), v_ref[0, :, v0:v1]) / l)
    o_ref[0] = jnp.concatenate(outs, axis=-1).astype(o_ref.dtype)


def _ctx_attention(q, k, v, heads, sink, name, base2=False):
    b, l, _ = q.shape
    full = lambda a: pl.BlockSpec((1, l, a.shape[-1]), lambda bi: (bi, 0, 0))
    return pl.pallas_call(
        functools.partial(_ctx_attn_kernel, heads=heads, base2=base2),
        grid=(b,),
        in_specs=[pl.BlockSpec(memory_space=pltpu.SMEM), full(q), full(k), full(v)],
        out_specs=pl.BlockSpec((1, l, GROUP_W), lambda bi: (bi, 0, 0)),
        out_shape=jax.ShapeDtypeStruct((b, l, GROUP_W), BF16),
        compiler_params=_cparams(("parallel",)),
        name=name,
    )(sink, q, k, v)


def _ret_kernel(*refs, chunk, n_chunks, direction, has_prev):
    if has_prev:
        (dec_ref, q_ref, k_ref, kt_ref, v_ref, g_ref, s0_ref, prev_ref, o_ref, sn_ref,
         st_ref, dm_ref, qd_ref, kdt_ref, cd_ref) = refs
    else:
        (dec_ref, q_ref, k_ref, kt_ref, v_ref, g_ref, s0_ref, o_ref, sn_ref,
         st_ref, dm_ref, qd_ref, kdt_ref, cd_ref) = refs
        prev_ref = None
    i = pl.program_id(1)
    c = chunk
    fwd = direction == 0

    def head_of(shape, axis):
        return lax.broadcasted_iota(jnp.int32, shape, axis) // HEAD_DIM

    @pl.when(i == 0)
    def _init():
        st_ref[...] = s0_ref[0]
        ii = lax.broadcasted_iota(jnp.int32, (c, c), 0).astype(F32)
        jj = lax.broadcasted_iota(jnp.int32, (c, c), 1).astype(F32)
        diff = (ii - jj) if fwd else (jj - ii)
        pos = lax.broadcasted_iota(jnp.int32, (c, GROUP_W), 0).astype(F32)
        pos_t = lax.broadcasted_iota(jnp.int32, (GROUP_W, c), 1).astype(F32)
        q_steps = (pos + 1.0) if fwd else (c - pos)
        k_steps = (c - 1.0 - pos_t) if fwd else pos_t
        qd = jnp.zeros((c, GROUP_W), F32)
        kdt = jnp.zeros((GROUP_W, c), F32)
        cd = jnp.zeros((GROUP_W, GROUP_W), F32)
        for h in range(N_HEADS):
            dec = dec_ref[direction * N_HEADS + h]
            lg = jax.nn.log_sigmoid(jnp.full((c, c), dec, F32))
            dm_ref[h * c:(h + 1) * c] = jnp.where(diff >= 0, jnp.exp(lg * jnp.maximum(diff, 0.0)), 0.0)
            lgq = jax.nn.log_sigmoid(jnp.full((c, GROUP_W), dec, F32))
            qd = jnp.where(head_of((c, GROUP_W), 1) == h, jnp.exp(lgq * q_steps), qd)
            lgk = jax.nn.log_sigmoid(jnp.full((GROUP_W, c), dec, F32))
            kdt = jnp.where(head_of((GROUP_W, c), 0) == h, jnp.exp(lgk * k_steps), kdt)
            lgc = jax.nn.log_sigmoid(jnp.full((GROUP_W, GROUP_W), dec, F32))
            cd = jnp.where(head_of((GROUP_W, GROUP_W), 0) == h, jnp.exp(lgc * c), cd)
        qd_ref[...] = qd
        kdt_ref[...] = kdt
        cd_ref[...] = cd

    in_head = [head_of((c, GROUP_W), 1) == h for h in range(N_HEADS)]
    head_mask = [jnp.where(mk, 1.0, 0.0).astype(BF16) for mk in in_head]
    same_head = head_of((GROUP_W, GROUP_W), 0) == head_of((GROUP_W, GROUP_W), 1)

    def chunk_off(n):
        idx = n if fwd else n_chunks - 1 - n
        return pl.multiple_of(idx * c, c)

    def qk(n):
        off = chunk_off(n)
        q_all = q_ref[0, pl.ds(off, c), :]
        q4 = jnp.concatenate([q_all * head_mask[h] for h in range(N_HEADS)], axis=0)
        return _dot_nt(q4, k_ref[0, pl.ds(off, c), :])

    def body(n, att_raw):
        off = chunk_off(n)
        v = v_ref[0, pl.ds(off, c), :]
        nxt = qk(jnp.minimum(n + 1, n_chunks - 1))
        att = (att_raw * dm_ref[...]).astype(BF16)
        intra4 = _dot(att, v)
        intra = intra4[:c]
        for h in range(1, N_HEADS):
            intra = jnp.where(in_head[h], intra4[h * c:(h + 1) * c], intra)
        att_raw = nxt
        state = st_ref[...]
        o = intra + _dot(q_ref[0, pl.ds(off, c), :], state.astype(BF16)) * qd_ref[...]
        kk = (kt_ref[0, :, pl.ds(off, c)].astype(F32) * kdt_ref[...]).astype(BF16)
        st_ref[...] = state * cd_ref[...] + jnp.where(same_head, _dot(kk, v), 0.0)
        oo = o * o
        ms = jnp.zeros((c, GROUP_W), F32)
        for h in range(N_HEADS):
            ms_h = jnp.sum(jnp.where(in_head[h], oo, 0.0), axis=-1, keepdims=True) * (1.0 / HEAD_DIM)
            ms = jnp.where(in_head[h], ms_h, ms)
        res = o * lax.rsqrt(ms + EPS) * _silu(g_ref[0, pl.ds(off, c), :])
        if has_prev:
            res = res + prev_ref[0, pl.ds(off, c), :]
        o_ref[0, pl.ds(off, c), :] = res.astype(o_ref.dtype)
        return att_raw

    lax.fori_loop(0, n_chunks, body, qk(0))

    @pl.when(i == pl.num_programs(1) - 1)
    def _fin():
        sn_ref[0] = st_ref[...]


def _retention_pass(dec, q, k, v, gates, state0, prev, direction, chunk, n_chunks, out_dtype):
    b, t, _ = q.shape
    tb = chunk * n_chunks
    n = t // tb
    if direction == 0:
        blk = lambda bi, i: (bi, i, 0)
        gblk = lambda bi, i: (bi, i, 0)
    else:
        blk = lambda bi, i: (bi, n - 1 - i, 0)
        gblk = lambda bi, i: (bi, n - 1 - i, 1)
    tok = pl.BlockSpec((1, tb, GROUP_W), blk)
    tok_t = pl.BlockSpec((1, GROUP_W, tb), lambda bi, i: (bi, 0, blk(bi, i)[1]))
    st_spec = pl.BlockSpec((1, GROUP_W, GROUP_W), lambda bi, i: (bi, 0, 0))
    in_specs = [pl.BlockSpec(memory_space=pltpu.SMEM), tok, tok, tok_t, tok,
                pl.BlockSpec((1, tb, GROUP_W), gblk), st_spec]
    args = [dec, q, k, jnp.swapaxes(k, 1, 2), v, gates, state0]
    if prev is not None:
        in_specs.append(tok)
        args.append(prev)
    return pl.pallas_call(
        functools.partial(_ret_kernel, chunk=chunk, n_chunks=n_chunks, direction=direction,
                          has_prev=prev is not None),
        grid=(b, n),
        in_specs=in_specs,
        out_specs=[tok, st_spec],
        out_shape=[jax.ShapeDtypeStruct((b, t, GROUP_W), out_dtype),
                   jax.ShapeDtypeStruct((b, GROUP_W, GROUP_W), F32)],
        scratch_shapes=[pltpu.VMEM((GROUP_W, GROUP_W), F32),
                        pltpu.VMEM((N_HEADS * chunk, chunk), F32),
                        pltpu.VMEM((chunk, GROUP_W), F32),
                        pltpu.VMEM((GROUP_W, chunk), F32),
                        pltpu.VMEM((GROUP_W, GROUP_W), F32)],
        compiler_params=_cparams(("parallel", "arbitrary")),
        name="retention_fwd" if direction == 0 else "retention_bwd",
    )(*args)


def _retention(dec, xq, xk, xv, xg, yq, yk, yv, yg, chunk, chunks_per_step):
    b = xq.shape[0]
    zero = jnp.zeros((b, GROUP_W, GROUP_W), F32)
    ny = yq.shape[1] // chunk
    yb, sb = _retention_pass(dec, yq, yk, yv, yg, zero, None, 1, chunk, ny, F32)
    y, sf = _retention_pass(dec, yq, yk, yv, yg, zero, yb, 0, chunk, ny, BF16)
    xb, _ = _retention_pass(dec, xq, xk, xv, xg, sb, None, 1, chunk, chunks_per_step, F32)
    x, _ = _retention_pass(dec, xq, xk, xv, xg, sf, xb, 0, chunk, chunks_per_step, BF16)
    return x, y


def _na_bias_kernel(rpb_ref, o_ref):
    h = pl.program_id(0)
    dr0 = pl.program_id(1)
    c = lax.broadcasted_iota(jnp.int32, (GRID_W, GRID_W), 0)
    kc = lax.broadcasted_iota(jnp.int32, (GRID_W, GRID_W), 1)
    c0 = jnp.clip(c - NA_KC // 2, 0, GRID_W - NA_KC)
    col_in = (kc >= c0) & (kc < c0 + NA_KC)
    dc = jnp.clip(kc - c, -(NA_KC - 1), NA_KC - 1) + NA_KC - 1
    n_dc = 2 * NA_KC - 1
    for j in range(NA_KR):
        base = (h * (2 * NA_KR - 1) + dr0 + j) * n_dc
        acc = jnp.zeros((GRID_W, GRID_W), F32)
        for d in range(n_dc):
            acc = jnp.where(dc == d, rpb_ref[base + d], acc)
        o_ref[0, :, j * GRID_W:(j + 1) * GRID_W] = jnp.where(col_in, acc, NEG_INF)


def _na_bias_table(rpb):
    return pl.pallas_call(
        _na_bias_kernel,
        grid=(N_HEADS, NA_KR),
        in_specs=[pl.BlockSpec(memory_space=pltpu.SMEM)],
        out_specs=pl.BlockSpec((1, GRID_W, NA_KR * GRID_W), lambda h, r: (r, h, 0)),
        out_shape=jax.ShapeDtypeStruct((NA_KR, N_HEADS * GRID_W, NA_KR * GRID_W), F32),
        compiler_params=_cparams(("parallel", "parallel")),
        name="na_bias_table",
    )(rpb.reshape(-1))


def _na_kernel(q_ref, k_ref, v_ref, ky_ref, vy_ref, tb_ref, o_ref,
               sa_ref, sb_ref, pa_ref, pb_ref, la_ref, lb_ref, *, rows_per_step, n_rows):
    r_base = pl.program_id(1) * rows_per_step
    win = NA_KR * GRID_W

    head_of_lane = lax.broadcasted_iota(jnp.int32, (GRID_W, GROUP_W), 1) // HEAD_DIM
    in_head = [head_of_lane == h for h in range(N_HEADS)]
    head_mask = [jnp.where(mk, 1.0, 0.0).astype(BF16) for mk in in_head]

    def geometry(i):
        r = r_base + i
        r0 = jnp.clip(r - NA_KR // 2, 0, n_rows - NA_KR)
        return pl.multiple_of(r0 * GRID_W, GRID_W), r0 - r + NA_KR - 1

    def row_slice(i):
        return pl.ds(pl.multiple_of(i * GRID_W, GRID_W), GRID_W)

    def scores(i, s_ref):
        koff, dr0 = geometry(i)
        q_all = q_ref[0, row_slice(i), :]
        q4 = jnp.concatenate([q_all * head_mask[h] for h in range(N_HEADS)], axis=0)
        s_ref[:, :win] = _dot_nt(q4, k_ref[0, pl.ds(koff, win), :]) + tb_ref[dr0]
        s_ref[:, win:] = _dot_nt(q4, ky_ref[0])

    def softmax(s_ref, p_ref, l_ref):
        s = s_ref[...]
        p = jnp.exp(s - _row_max(s))
        l_ref[...] = 1.0 / _row_sum(p)
        p_ref[...] = p.astype(BF16)

    def values(i, p_ref, l_ref):
        koff, _ = geometry(i)
        o4 = (_dot(p_ref[:, :win], v_ref[0, pl.ds(koff, win), :])
              + _dot(p_ref[:, win:], vy_ref[0])) * l_ref[...]
        out = o4[:GRID_W]
        for h in range(1, N_HEADS):
            out = jnp.where(in_head[h], o4[h * GRID_W:(h + 1) * GRID_W], out)
        o_ref[0, row_slice(i), :] = out.astype(o_ref.dtype)

    pb_ref[...] = jnp.zeros(pb_ref.shape, BF16)
    lb_ref[...] = jnp.zeros(lb_ref.shape, F32)
    scores(0, sa_ref)

    def pair_body(tt, carry):
        t = 2 * tt
        values(jnp.maximum(t - 1, 0), pb_ref, lb_ref)
        scores(t + 1, sb_ref)
        softmax(sa_ref, pa_ref, la_ref)
        values(t, pa_ref, la_ref)
        scores(jnp.minimum(t + 2, rows_per_step - 1), sa_ref)
        softmax(sb_ref, pb_ref, lb_ref)
        return carry

    lax.fori_loop(0, rows_per_step // 2, pair_body, 0)
    values(rows_per_step - 1, pb_ref, lb_ref)


def _na_attention(q, k, v, ky, vy, table, rows_per_step):
    b, s, _ = q.shape
    l = ky.shape[1]
    n_rows = s // GRID_W
    tq = rows_per_step * GRID_W
    n_keys = NA_KR * GRID_W + l
    seq = lambda n: pl.BlockSpec((1, n, GROUP_W), lambda bi, i: (bi, 0, 0))
    return pl.pallas_call(
        functools.partial(_na_kernel, rows_per_step=rows_per_step, n_rows=n_rows),
        grid=(b, n_rows // rows_per_step),
        in_specs=[pl.BlockSpec((1, tq, GROUP_W), lambda bi, i: (bi, i, 0)),
                  seq(s), seq(s), seq(l), seq(l), _const_spec(table.shape)],
        out_specs=pl.BlockSpec((1, tq, GROUP_W), lambda bi, i: (bi, i, 0)),
        out_shape=jax.ShapeDtypeStruct((b, s, GROUP_W), BF16),
        scratch_shapes=([pltpu.VMEM((N_HEADS * GRID_W, n_keys), F32)] * 2
                        + [pltpu.VMEM((N_HEADS * GRID_W, n_keys), BF16)] * 2
                        + [pltpu.VMEM((N_HEADS * GRID_W, 1), F32)] * 2),
        compiler_params=_cparams(("parallel", "arbitrary")),
        name="na_attention",
    )(q, k, v, ky, vy, table)


def _swa_kernel(sink_ref, q_ref, kp_ref, kc_ref, kn_ref, vp_ref, vc_ref, vn_ref, ky_ref, vy_ref, o_ref,
                kw_ref, vw_ref, wb_ref, sa_ref, sb_ref, pa_ref, pb_ref, la_ref, lb_ref, *, blocks_per_step):
    step = pl.program_id(1)
    nb = pl.num_programs(1) * blocks_per_step
    bl = SWA_BLOCK
    g = N_HEADS // SWA_KV_HEADS
    tq = blocks_per_step * bl
    kw_ref[0:bl] = kp_ref[0]
    kw_ref[bl:bl + tq] = kc_ref[0]
    kw_ref[bl + tq:] = kn_ref[0]
    vw_ref[0:bl] = vp_ref[0]
    vw_ref[bl:bl + tq] = vc_ref[0]
    vw_ref[bl + tq:] = vn_ref[0]

    qi = lax.broadcasted_iota(jnp.int32, (g * bl, 3 * bl), 0) % bl
    jk = lax.broadcasted_iota(jnp.int32, (g * bl, 3 * bl), 1)
    wb_ref[...] = jnp.where(jnp.abs(jk - bl - qi) <= SWA_WINDOW, 0.0, NEG_INF)
    half = lax.broadcasted_iota(jnp.int32, (g * bl, 1), 0) // bl

    head_of_lane = lax.broadcasted_iota(jnp.int32, (bl, GROUP_W), 1) // HEAD_DIM
    in_head = [head_of_lane == h for h in range(N_HEADS)]
    head_mask = [jnp.where(mk, 1.0, 0.0).astype(BF16) for mk in in_head]

    def blk_slice(j):
        return pl.ds(pl.multiple_of(j * bl, bl), bl)

    def scores(j, s_ref):
        qoff = pl.multiple_of(j * bl, bl)
        q_all = q_ref[0, blk_slice(j), :]
        kw = kw_ref[pl.ds(qoff, 3 * bl), :]
        n = step * blocks_per_step + j
        lo_edge = jnp.where(n == 0, NEG_INF, 0.0)
        hi_edge = jnp.where(n == nb - 1, NEG_INF, 0.0)
        for kh in range(SWA_KV_HEADS):
            q = jnp.concatenate([q_all * head_mask[kh * g + gi] for gi in range(g)], axis=0)
            s = _dot_nt(q, kw) + wb_ref[...]
            s_ref[kh, :, :bl] = s[:, :bl] + lo_edge
            s_ref[kh, :, bl:2 * bl] = s[:, bl:2 * bl]
            s_ref[kh, :, 2 * bl:3 * bl] = s[:, 2 * bl:] + hi_edge
            s_ref[kh, :, 3 * bl:] = _dot_nt(q, ky_ref[0])

    def softmax(s_ref, p_ref, l_ref):
        for kh in range(SWA_KV_HEADS):
            s = s_ref[kh]
            sink = jnp.full((g * bl, 1), sink_ref[kh * g], F32)
            for gi in range(1, g):
                sink = jnp.where(half == gi, sink_ref[kh * g + gi], sink)
            m = jnp.maximum(_row_max(s), sink)
            p = jnp.exp(s - m)
            l_ref[kh] = 1.0 / (_row_sum(p) + jnp.exp(sink - m))
            p_ref[kh] = p.astype(BF16)

    def values(j, p_ref, l_ref):
        vw = vw_ref[pl.ds(pl.multiple_of(j * bl, bl), 3 * bl), :]
        out = jnp.zeros((bl, GROUP_W), F32)
        for kh in range(SWA_KV_HEADS):
            o = (_dot(p_ref[kh, :, :3 * bl], vw) + _dot(p_ref[kh, :, 3 * bl:], vy_ref[0])) * l_ref[kh]
            for gi in range(g):
                out = jnp.where(in_head[kh * g + gi], o[gi * bl:(gi + 1) * bl], out)
        o_ref[0, blk_slice(j), :] = out.astype(o_ref.dtype)

    pb_ref[...] = jnp.zeros(pb_ref.shape, BF16)
    lb_ref[...] = jnp.zeros(lb_ref.shape, F32)
    scores(0, sa_ref)

    def pair_body(tt, carry):
        t = 2 * tt
        values(jnp.maximum(t - 1, 0), pb_ref, lb_ref)
        scores(t + 1, sb_ref)
        softmax(sa_ref, pa_ref, la_ref)
        values(t, pa_ref, la_ref)
        scores(jnp.minimum(t + 2, blocks_per_step - 1), sa_ref)
        softmax(sb_ref, pb_ref, lb_ref)
        return carry

    lax.fori_loop(0, blocks_per_step // 2, pair_body, 0)
    values(blocks_per_step - 1, pb_ref, lb_ref)


def _swa_attention(sink, q, k, v, ky, vy, blocks_per_step):
    b, s, _ = q.shape
    l = ky.shape[1]
    tq = blocks_per_step * SWA_BLOCK
    n_steps = s // tq
    nb = s // SWA_BLOCK
    kvw = GROUP_W
    q_rows = N_HEADS // SWA_KV_HEADS * SWA_BLOCK
    n_keys = 3 * SWA_BLOCK + l
    prev = pl.BlockSpec((1, SWA_BLOCK, kvw), lambda bi, i: (bi, jnp.maximum(i * blocks_per_step - 1, 0), 0))
    cur = pl.BlockSpec((1, tq, kvw), lambda bi, i: (bi, i, 0))
    nxt = pl.BlockSpec((1, SWA_BLOCK, kvw), lambda bi, i: (bi, jnp.minimum((i + 1) * blocks_per_step, nb - 1), 0))
    ctx = pl.BlockSpec((1, l, kvw), lambda bi, i: (bi, 0, 0))
    return pl.pallas_call(
        functools.partial(_swa_kernel, blocks_per_step=blocks_per_step),
        grid=(b, n_steps),
        in_specs=[pl.BlockSpec(memory_space=pltpu.SMEM),
                  pl.BlockSpec((1, tq, GROUP_W), lambda bi, i: (bi, i, 0)),
                  prev, cur, nxt, prev, cur, nxt, ctx, ctx],
        out_specs=pl.BlockSpec((1, tq, GROUP_W), lambda bi, i: (bi, i, 0)),
        out_shape=jax.ShapeDtypeStruct((b, s, GROUP_W), BF16),
        scratch_shapes=[pltpu.VMEM((tq + 2 * SWA_BLOCK, kvw), BF16),
                        pltpu.VMEM((tq + 2 * SWA_BLOCK, kvw), BF16),
                        pltpu.VMEM((q_rows, 3 * SWA_BLOCK), F32)]
                       + [pltpu.VMEM((SWA_KV_HEADS, q_rows, n_keys), F32)] * 2
                       + [pltpu.VMEM((SWA_KV_HEADS, q_rows, n_keys), BF16)] * 2
                       + [pltpu.VMEM((SWA_KV_HEADS, q_rows, 1), F32)] * 2,
        compiler_params=_cparams(("parallel", "arbitrary")),
        name="swa_attention",
    )(sink, q, k, k, k, v, v, v, ky, vy)


def _outffn_kernel(*refs, hidden_chunk, final):
    if final:
        (x_ref, m0_ref, m1_ref, m2_ref, m3_ref, wo_ref, g1_ref, n2_ref, sc_ref, sh_ref, g2_ref,
         w1_ref, w3_ref, w2_ref, fg_ref, o_ref) = refs
    else:
        (x_ref, m0_ref, m1_ref, m2_ref, m3_ref, wo_ref, g1_ref, n2_ref, sc_ref, sh_ref, g2_ref,
         w1_ref, w3_ref, w2_ref, o_ref) = refs
    mix = None
    for gi, m_ref in enumerate((m0_ref, m1_ref, m2_ref, m3_ref)):
        part = _dot(m_ref[0], wo_ref[gi * GROUP_W:(gi + 1) * GROUP_W, :])
        mix = part if mix is None else mix + part
    x1 = x_ref[0] + g1_ref[0] * mix
    hb = ((_rms(x1) * n2_ref[...]) * (1.0 + sc_ref[0]) + sh_ref[0]).astype(BF16)
    hidden = w1_ref.shape[1]
    acc = None
    for c0 in range(0, hidden, hidden_chunk):
        a = _dot(hb, w1_ref[:, c0:c0 + hidden_chunk])
        bgate = _dot(hb, w3_ref[:, c0:c0 + hidden_chunk])
        u = (_silu(a) * bgate).astype(BF16)
        part = _dot(u, w2_ref[c0:c0 + hidden_chunk, :])
        acc = part if acc is None else acc + part
    x2 = x1 + g2_ref[0] * acc
    if final:
        x2 = _rms(x2) * fg_ref[...]
    o_ref[0] = x2


def _outffn(x, mixes, wo, g1, n2, sc2, sh2, g2, w1, w3, w2, final_g, tm):
    b, t, d = x.shape
    tok = lambda wd: pl.BlockSpec((1, tm, wd), lambda bi, i: (bi, i, 0))
    vec = pl.BlockSpec((1, 1, d), lambda bi, i: (bi, 0, 0))
    in_specs = ([tok(d)] + [tok(GROUP_W)] * 4
                + [_const_spec(wo.shape), vec, _const_spec((1, d)), vec, vec, vec,
                   _const_spec(w1.shape), _const_spec(w3.shape), _const_spec(w2.shape)])
    args = [x, *mixes, wo, g1, n2, sc2, sh2, g2, w1, w3, w2]
    final = final_g is not None
    if final:
        in_specs.append(_const_spec((1, d)))
        args.append(final_g)
    return pl.pallas_call(
        functools.partial(_outffn_kernel, hidden_chunk=256, final=final),
        grid=(b, t // tm),
        in_specs=in_specs,
        out_specs=tok(d),
        out_shape=jax.ShapeDtypeStruct((b, t, d), F32),
        compiler_params=_cparams(("parallel", "parallel")),
        name="out_proj_ffn_final" if final else "out_proj_ffn",
    )(*args)


def _prep_weights(w_in, mla_w_uq, mla_w_ukv):
    depth, d, _ = w_in.shape
    offs = [0]
    for sz in IN_SIZES:
        offs.append(offs[-1] + sz)
    w_bf = w_in.astype(BF16)
    cols = [w_bf[:, :, offs[i]:offs[i + 1]] for i in range(len(IN_SIZES))]
    cq, ckv, kr, rq, rk, rv, rgf, rgb, nq, nk, nv, sq, sk, sv = cols
    scale = HEAD_DIM ** -0.5
    zpad = jnp.zeros((depth, d, MLA_HEAD_PAD - MLA_NOPE - MLA_ROPE), BF16)
    znope = jnp.zeros((depth, d, MLA_NOPE), BF16)
    krep = jnp.concatenate([znope, kr, zpad] * N_HEADS, axis=-1)

    def per_query_head(t):
        g = N_HEADS // SWA_KV_HEADS
        t = t.reshape(depth, d, SWA_KV_HEADS, 1, HEAD_DIM)
        return jnp.broadcast_to(t, (depth, d, SWA_KV_HEADS, g, HEAD_DIM)).reshape(depth, d, GROUP_W)

    w = jnp.concatenate([cq, ckv, krep, rq, rk * scale, rv, rgf, rgb, nq * scale, nk, nv,
                         sq * scale, per_query_head(sk), per_query_head(sv)], axis=-1).astype(BF16)

    qr = mla_w_uq.shape[1]
    uq = mla_w_uq.reshape(depth, qr, N_HEADS, MLA_NOPE + MLA_ROPE)
    wuq = jnp.concatenate([uq, jnp.zeros((depth, qr, N_HEADS, MLA_HEAD_PAD - MLA_NOPE - MLA_ROPE), F32)],
                          axis=-1).reshape(depth, qr, N_HEADS * MLA_HEAD_PAD).astype(BF16)
    kvr = mla_w_ukv.shape[1]
    ukv = mla_w_ukv.reshape(depth, kvr, N_HEADS, MLA_NOPE + MLA_V)
    zk = jnp.zeros((depth, kvr, N_HEADS, MLA_HEAD_PAD - MLA_NOPE), F32)
    wuk = jnp.concatenate([ukv[..., :MLA_NOPE], zk], axis=-1).reshape(depth, kvr, -1).astype(BF16)
    zv = jnp.zeros((depth, kvr, N_HEADS, MLA_HEAD_PAD - MLA_V), F32)
    wuv = jnp.concatenate([ukv[..., MLA_NOPE:], zv], axis=-1).reshape(depth, kvr, -1).astype(BF16)
    return w, wuq, wuk, wuv


def _ctx_head_specs():
    mla = tuple((h * MLA_HEAD_PAD, (h + 1) * MLA_HEAD_PAD, h * MLA_HEAD_PAD, (h + 1) * MLA_HEAD_PAD,
                 h * MLA_HEAD_PAD, h * MLA_HEAD_PAD + MLA_V, None) for h in range(N_HEADS))
    na = tuple((h * HEAD_DIM, (h + 1) * HEAD_DIM) * 3 + (None,) for h in range(N_HEADS))
    swa = tuple((h * HEAD_DIM, (h + 1) * HEAD_DIM) * 3 + (h,) for h in range(N_HEADS))
    return mla, na, swa


def kernel(x, c, ctx, c_ctx, ada_w, ada_b, norm1_g, w_in, mla_q_norm, mla_w_uq, mla_kv_norm, mla_w_ukv,
           ret_decay, na_rpb, swa_sink, w_out, norm2_g, ffn_w1, ffn_w3, ffn_w2, final_norm_g):
    b, s, d = x.shape
    l_ctx = ctx.shape[1]
    depth = ada_w.shape[0]
    assert b + 1 <= 8 and s % 1024 == 0 and l_ctx % 128 == 0

    cond = jnp.concatenate([c, c_ctx[None, :], jnp.zeros((8 - b - 1, d), F32)], axis=0)
    mod = _modulation(cond, ada_w, ada_b)
    tables = _rope_tables(s)
    w_all, wuq_all, wuk_all, wuv_all = _prep_weights(w_in, mla_w_uq, mla_w_ukv)
    wo_all = w_out.astype(BF16)
    w1_all, w3_all, w2_all = ffn_w1.astype(BF16), ffn_w3.astype(BF16), ffn_w2.astype(BF16)
    mla_heads, na_heads, swa_heads = _ctx_head_specs()
    no_sink = jnp.zeros((N_HEADS,), F32)

    tm_x = 512
    tm_y = min(256, l_ctx)
    y = ctx
    for l in range(depth):
        mx = [mod[l, :b, j * d:(j + 1) * d][:, None, :] for j in range(6)]
        my = [jnp.broadcast_to(mod[l, b, j * d:(j + 1) * d][None, None, :], (b, 1, d)) for j in range(6)]
        n1 = norm1_g[l][None, :]
        n2 = norm2_g[l][None, :]
        qn = mla_q_norm[l][None, :]
        kvn = mla_kv_norm[l][None, :]
        lw = (w_all[l], qn, kvn, wuq_all[l], wuk_all[l], wuv_all[l])

        px = _inproj(x, n1, mx[1], mx[0], *lw, tables, tm_x)
        py = _inproj(y, n1, my[1], my[0], *lw, None, tm_y)
        (xmq, xmk, xmv, xrq, xrk, xrv, xrg, xnq, xnk, xnv, xsq, xsk, xsv) = px
        (ymq, ymk, ymv, yrq, yrk, yrv, yrg, ynq, ynk, ynv, ysq, ysk, ysv) = py

        mla_x = _mla_attention(xmq, xmk, xmv, ymk, ymv, tq=512, tk=512, n_sub=2, unroll=8)
        dec = ret_decay[l].reshape(-1)
        ret_x, ret_y = _retention(dec, xrq, xrk, xrv, xrg, yrq, yrk, yrv, yrg, chunk=128, chunks_per_step=8)
        table = _na_bias_table(na_rpb[l])
        na_x = _na_attention(xnq, xnk, xnv, ynk, ynv, table, rows_per_step=16)
        swa_x = _swa_attention(swa_sink[l], xsq, xsk, xsv, ysk, ysv, blocks_per_step=8)

        last = l == depth - 1
        x = _outffn(x, (mla_x, ret_x, na_x, swa_x), wo_all[l], mx[2], n2, mx[4], mx[3], mx[5],
                    w1_all[l], w3_all[l], w2_all[l], final_norm_g[None, :] if last else None, tm_x)
        if not last:
            mla_y = _ctx_attention(ymq, ymk, ymv, mla_heads, no_sink, "mla_ctx_attention", base2=True)
            na_y = _ctx_attention(ynq, ynk, ynv, na_heads, no_sink, "na_ctx_attention")
            swa_y = _ctx_attention(ysq, ysk, ysv, swa_heads, swa_sink[l], "swa_ctx_attention")
            y = _outffn(y, (mla_y, ret_y, na_y, swa_y), wo_all[l], my[2], n2, my[4], my[3], my[5],
                        w1_all[l], w3_all[l], w2_all[l], None, tm_y)
    return x
```

```python
import functools

import jax
import jax.numpy as jnp
from jax import lax
from jax.experimental import pallas as pl
from jax.experimental.pallas import tpu as pltpu

F32 = jnp.float32
BF16 = jnp.bfloat16

GRID_W = 64
HEAD_DIM = 64
N_HEADS = 4
GROUP_W = N_HEADS * HEAD_DIM
MLA_Q_RANK = 256
MLA_KV_RANK = 128
MLA_NOPE = 64
MLA_ROPE = 32
MLA_V = 64
MLA_HEAD_PAD = 128
MLA_V_ROWS = MLA_V + 16
NA_KR = 8
NA_KC = 16
SWA_KV_HEADS = 2
SWA_WINDOW = 128
SWA_BLOCK = 128
ROPE_THETA = 10000.0
EPS = 1e-6
NEG_INF = -1e30
LOG2_E = 1.4426950408889634
LANES = 128
VMEM_LIMIT = 56 * 1024 * 1024

IN_SIZES = (MLA_Q_RANK, MLA_KV_RANK, MLA_ROPE,
            GROUP_W, GROUP_W, GROUP_W, GROUP_W, GROUP_W,
            GROUP_W, GROUP_W, GROUP_W,
            GROUP_W, SWA_KV_HEADS * HEAD_DIM, SWA_KV_HEADS * HEAD_DIM)

_O_CQ = 0
_O_CKV = _O_CQ + MLA_Q_RANK
_O_KR = _O_CKV + MLA_KV_RANK
_O_RQK = _O_KR + LANES
_O_RV = _O_RQK + 2 * GROUP_W
_O_RG = _O_RV + GROUP_W
_O_NA = _O_RG + 2 * GROUP_W
_O_SQK = _O_NA + 3 * GROUP_W
_O_SV = _O_SQK + 2 * GROUP_W
_IN_COLS = _O_SV + GROUP_W


def _cparams(sem):
    return pltpu.CompilerParams(dimension_semantics=sem, vmem_limit_bytes=VMEM_LIMIT)


def _dot(a, b):
    return jnp.dot(a, b, preferred_element_type=F32)


def _dot_nt(a, b):
    return lax.dot_general(a, b, (((1,), (1,)), ((), ())), preferred_element_type=F32)


def _dot_tn(a, b):
    return lax.dot_general(a, b, (((0,), (0,)), ((), ())), preferred_element_type=F32)


def _rms(x):
    return x * lax.rsqrt(jnp.mean(x * x, axis=-1, keepdims=True) + EPS)


def _silu(x):
    return x * jax.nn.sigmoid(x)


def _lane_chunks(arrays):
    return [a[:, j * LANES:(j + 1) * LANES] for a in arrays for j in range(a.shape[-1] // LANES)]


def _row_max(*arrays):
    return jnp.max(functools.reduce(jnp.maximum, _lane_chunks(arrays)), axis=-1, keepdims=True)


def _row_sum(*arrays):
    return jnp.sum(functools.reduce(jnp.add, _lane_chunks(arrays)), axis=-1, keepdims=True)


def _mod_kernel(c_ref, w_ref, b_ref, o_ref):
    o_ref[0] = _dot(_silu(c_ref[...]), w_ref[0]) + b_ref[0]


def _modulation(cond, ada_w, ada_b):
    depth, d, d6 = ada_w.shape
    n = d6 // d
    return pl.pallas_call(
        _mod_kernel,
        grid=(depth, n),
        in_specs=[pl.BlockSpec((8, d), lambda l, j: (0, 0)),
                  pl.BlockSpec((1, d, d), lambda l, j: (l, 0, j)),
                  pl.BlockSpec((1, 1, d), lambda l, j: (l, 0, j))],
        out_specs=pl.BlockSpec((1, 8, d), lambda l, j: (l, 0, j)),
        out_shape=jax.ShapeDtypeStruct((depth, 8, d6), F32),
        compiler_params=_cparams(("parallel", "parallel")),
        name="ada_modulation",
    )(cond, ada_w, ada_b.reshape(depth, 1, d6))


def _rope_tables(seq):
    n_rows = seq // GRID_W

    def parts(pos, d):
        inv = ROPE_THETA ** (-jnp.arange(0, d, 2, dtype=F32) / d)
        ang = pos.astype(F32)[:, None] * inv[None, :]
        z = jnp.zeros_like(ang)
        return (jnp.concatenate([jnp.cos(ang), jnp.cos(ang)], axis=-1),
                jnp.concatenate([z, jnp.sin(ang)], axis=-1),
                jnp.concatenate([-jnp.sin(ang), z], axis=-1))

    def expand(by_row, by_col):
        w = by_row.shape[-1]
        r = jnp.broadcast_to(by_row[:, None, :], (n_rows, GRID_W, w))
        c = jnp.broadcast_to(by_col[None, :, :], (n_rows, GRID_W, w))
        return jnp.concatenate([r, c], axis=-1).reshape(seq, 2 * w)

    def tables(d, fill):
        per_head = [expand(a, b) for a, b in zip(parts(jnp.arange(n_rows), d), parts(jnp.arange(GRID_W), d))]
        return [fill(t, i) for i, t in enumerate(per_head)]

    def two_heads(t, _):
        return jnp.concatenate([t, t], axis=-1)

    def mla_slot(t, i):
        lead = (jnp.ones if i == 0 else jnp.zeros)((seq, MLA_NOPE), F32)
        tail = (jnp.ones if i == 0 else jnp.zeros)((seq, MLA_HEAD_PAD - MLA_NOPE - MLA_ROPE), F32)
        return jnp.concatenate([lead, t, tail], axis=-1)

    def kr_slot(t, i):
        tail = (jnp.ones if i == 0 else jnp.zeros)((seq, LANES - MLA_ROPE), F32)
        return jnp.concatenate([t, tail], axis=-1)

    return tuple(tables(HEAD_DIM // 2, two_heads) + tables(MLA_ROPE // 2, mla_slot)
                 + tables(MLA_ROPE // 2, kr_slot))


def _rope(x, cos, s_prev, s_next, d):
    out = []
    for j in range(x.shape[-1] // LANES):
        xc = x[:, j * LANES:(j + 1) * LANES]
        out.append(xc * cos + pltpu.roll(xc, d, 1) * s_prev + pltpu.roll(xc, LANES - d, 1) * s_next)
    return out[0] if len(out) == 1 else jnp.concatenate(out, axis=-1)


def _inproj_kernel(*refs, rotate, mla_scale):
    if rotate:
        (x_ref, g_ref, sc_ref, sh_ref, w_ref, qn_ref, kvn_ref, wuq_ref, wuk_ref, wuv_ref, place_ref,
         c64_ref, p64_ref, n64_ref, cm_ref, pm_ref, nm_ref, ckr_ref, pkr_ref, nkr_ref,
         mq_ref, mk_ref, mv_ref, rq_ref, rk_ref, rv_ref, rg_ref,
         nq_ref, nk_ref, nv_ref, sq_ref, sk_ref, sv_ref) = refs
    else:
        (x_ref, g_ref, sc_ref, sh_ref, w_ref, qn_ref, kvn_ref, wuq_ref, wuk_ref, wuv_ref, place_ref,
         mq_ref, mk_ref, mv_ref, rq_ref, rk_ref, rv_ref, rg_ref,
         nq_ref, nk_ref, nv_ref, sq_ref, sk_ref, sv_ref) = refs

    x = x_ref[0]
    h = (_rms(x) * g_ref[...]) * (1.0 + sc_ref[0]) + sh_ref[0]
    hb = h.astype(BF16)

    def proj(lo, hi):
        return _dot(hb, w_ref[:, lo:hi])

    def rope64(v):
        if not rotate:
            return v
        return _rope(v, c64_ref[...], p64_ref[...], n64_ref[...], HEAD_DIM // 4)

    def rope_mla(v):
        if not rotate:
            return v
        return _rope(v, cm_ref[...], pm_ref[...], nm_ref[...], MLA_ROPE // 4)

    cq = (_rms(proj(_O_CQ, _O_CKV)) * qn_ref[...]).astype(BF16)
    ckv_kr = proj(_O_CKV, _O_RQK)
    ckv = (_rms(ckv_kr[:, :MLA_KV_RANK]) * kvn_ref[...]).astype(BF16)
    kr = ckv_kr[:, MLA_KV_RANK:]
    if rotate:
        kr = _rope(kr, ckr_ref[...], pkr_ref[...], nkr_ref[...], MLA_ROPE // 4)
    kr = kr.astype(BF16)

    rqk = proj(_O_RQK, _O_RV)
    rq_ref[0] = rope64(rqk[:, :GROUP_W]).astype(BF16)
    rk_ref[0] = rope64(rqk[:, GROUP_W:]).astype(BF16)
    rv_ref[0] = proj(_O_RV, _O_RG).astype(BF16)
    rg_ref[0] = proj(_O_RG, _O_NA)

    na = proj(_O_NA, _O_SQK)
    nq_ref[0] = na[:, :GROUP_W].astype(BF16)
    nk_ref[0] = na[:, GROUP_W:2 * GROUP_W].astype(BF16)
    nv_ref[0] = na[:, 2 * GROUP_W:].astype(BF16)

    sqk = proj(_O_SQK, _O_SV)
    sq_ref[0] = rope64(sqk[:, :GROUP_W]).astype(BF16)
    sk_ref[0] = rope64(sqk[:, GROUP_W:]).astype(BF16)
    sv_ref[0] = proj(_O_SV, _IN_COLS).astype(BF16)

    q = rope_mla(_dot(cq, wuq_ref[...])) * mla_scale
    mq_ref[0] = q.astype(BF16)
    k = _dot(ckv, wuk_ref[...]) + _dot(kr, place_ref[...])
    mk_ref[0] = k.astype(BF16)
    v = _dot(ckv, wuv_ref[...])
    lane = lax.broadcasted_iota(jnp.int32, v.shape, 1)
    mv_ref[0] = jnp.where(lane % MLA_HEAD_PAD >= MLA_V, 1.0, v).astype(BF16)


def _const_spec(shape):
    nd = len(shape)
    return pl.BlockSpec(shape, lambda *_: (0,) * nd, pipeline_mode=pl.Buffered(1))


def _inproj(x, gain, scale, shift, w, qn, kvn, wuq, wuk, wuv, place, tables, tm):
    b, t, d = x.shape
    rotate = tables is not None
    kv_w = GROUP_W
    mla_w = N_HEADS * MLA_HEAD_PAD
    tok = lambda wd: pl.BlockSpec((1, tm, wd), lambda bi, i: (bi, i, 0))
    vec = pl.BlockSpec((1, 1, d), lambda bi, i: (bi, 0, 0))
    in_specs = [tok(d), _const_spec((1, d)), vec, vec, _const_spec(w.shape),
                _const_spec(qn.shape), _const_spec(kvn.shape), _const_spec(wuq.shape),
                _const_spec(wuk.shape), _const_spec(wuv.shape), _const_spec(place.shape)]
    args = [x, gain, scale, shift, w, qn, kvn, wuq, wuk, wuv, place]
    if rotate:
        in_specs += [pl.BlockSpec((tm, LANES), lambda bi, i: (i, 0))] * len(tables)
        args += list(tables)
    widths = [mla_w, mla_w, mla_w, GROUP_W, GROUP_W, GROUP_W, 2 * GROUP_W,
              GROUP_W, GROUP_W, GROUP_W, GROUP_W, kv_w, kv_w]
    dtypes = [BF16] * 6 + [F32] + [BF16] * 6
    return pl.pallas_call(
        functools.partial(_inproj_kernel, rotate=rotate, mla_scale=(MLA_NOPE + MLA_ROPE) ** -0.5 * LOG2_E),
        grid=(b, t // tm),
        in_specs=in_specs,
        out_specs=[tok(wd) for wd in widths],
        out_shape=[jax.ShapeDtypeStruct((b, t, wd), dt) for wd, dt in zip(widths, dtypes)],
        compiler_params=_cparams(("parallel", "parallel")),
        name="in_proj_rot" if rotate else "in_proj_ctx",
    )(*args)


def _mla_kernel(q_ref, kx_ref, vx_ref, ky_ref, vy_ref, o_ref, m_ref, acc_ref, sa_ref, sb_ref, sc_ref,
                ma_ref, mb_ref, mc_ref, *, tk, heads, n_sub, unroll):
    n_chunks = kx_ref.shape[1] // tk
    tq = q_ref.shape[1]
    ts = tq // n_sub
    chains = [(h, u) for h in range(heads) for u in range(n_sub)]
    for c in range(len(chains)):
        m_ref[c] = jnp.full(m_ref.shape[1:], NEG_INF, F32)
        acc_ref[c] = jnp.zeros(acc_ref.shape[1:], F32)

    def rows(h):
        return slice(h * MLA_HEAD_PAD, (h + 1) * MLA_HEAD_PAD)

    def vrows(h):
        return slice(h * MLA_HEAD_PAD, h * MLA_HEAD_PAD + MLA_V_ROWS)

    buf_a, buf_b, buf_c = (sa_ref, ma_ref), (sb_ref, mb_ref), (sc_ref, mc_ref)

    def scores(c, k, dst):
        h, u = chains[c]
        st = _dot_nt(k, q_ref[0, u * ts:(u + 1) * ts, rows(h)])
        dst[0][c] = st
        dst[1][c] = jnp.max(st, axis=0, keepdims=True)

    def absorb(c, src, vt):
        m_old = m_ref[c]
        m_new = jnp.maximum(m_old, src[1][c])
        pt = jnp.exp2(src[0][c] - m_new).astype(BF16)
        acc_ref[c, :MLA_V_ROWS] = acc_ref[c, :MLA_V_ROWS] * jnp.exp2(m_old - m_new) + _dot(vt, pt)
        m_ref[c] = m_new

    def kx(j, h):
        return kx_ref[0, pl.ds(pl.multiple_of(j * tk, tk), tk), rows(h)]

    def stage(j_next, nxt, j_cur, cur):
        for c, (h, _) in enumerate(chains):
            scores(c, kx(j_next, h), nxt)
            absorb(c, cur, vx_ref[0, j_cur, vrows(h), :])

    for c, (h, _) in enumerate(chains):
        scores(c, kx(0, h), buf_a)

    def stage_at(j, parity):
        cur, nxt = (buf_a, buf_b) if parity == 0 else (buf_b, buf_a)
        stage(j + 1, nxt, j, cur)

    def body(jj, carry):
        for i in range(unroll):
            stage_at(unroll * jj + i, i % 2)
        return carry

    trips = (n_chunks - 1) // unroll
    lax.fori_loop(0, trips, body, 0)
    for j in range(unroll * trips, n_chunks - 1):
        stage_at(j, j % 2)
    last = buf_a if (n_chunks - 1) % 2 == 0 else buf_b
    for c, (h, _) in enumerate(chains):
        scores(c, ky_ref[0, :, rows(h)], buf_c)
        absorb(c, last, vx_ref[0, n_chunks - 1, vrows(h), :])
    outs = [[None] * n_sub for _ in range(heads)]
    for c, (h, u) in enumerate(chains):
        absorb(c, buf_c, vy_ref[0, 0, vrows(h), :])
        acc = acc_ref[c].T
        outs[h][u] = acc[:, :MLA_V] / acc[:, MLA_V:MLA_V + 1]
    o_ref[0] = jnp.concatenate([jnp.concatenate(outs[h], axis=0) for h in range(heads)],
                               axis=-1).astype(o_ref.dtype)


def _mla_attention(q, kx, vx, ky, vy, tq, tk, n_sub, unroll):
    b, s, _ = q.shape
    l = ky.shape[1]
    hp = 2
    wd = hp * MLA_HEAD_PAD
    vxt = jnp.swapaxes(vx.reshape(b, s // tk, tk, -1), 2, 3)
    vyt = jnp.swapaxes(vy.reshape(b, 1, l, -1), 2, 3)
    ts = tq // n_sub
    return pl.pallas_call(
        functools.partial(_mla_kernel, tk=tk, heads=hp, n_sub=n_sub, unroll=unroll),
        grid=(b, N_HEADS // hp, s // tq),
        in_specs=[pl.BlockSpec((1, tq, wd), lambda bi, hi, i: (bi, i, hi)),
                  pl.BlockSpec((1, s, wd), lambda bi, hi, i: (bi, 0, hi)),
                  pl.BlockSpec((1, s // tk, wd, tk), lambda bi, hi, i: (bi, 0, hi, 0)),
                  pl.BlockSpec((1, l, wd), lambda bi, hi, i: (bi, 0, hi)),
                  pl.BlockSpec((1, 1, wd, l), lambda bi, hi, i: (bi, 0, hi, 0))],
        out_specs=pl.BlockSpec((1, tq, hp * MLA_V), lambda bi, hi, i: (bi, i, hi)),
        out_shape=jax.ShapeDtypeStruct((b, s, N_HEADS * MLA_V), BF16),
        scratch_shapes=[pltpu.VMEM((hp * n_sub, 1, ts), F32),
                        pltpu.VMEM((hp * n_sub, MLA_HEAD_PAD, ts), F32),
                        pltpu.VMEM((hp * n_sub, tk, ts), F32),
                        pltpu.VMEM((hp * n_sub, tk, ts), F32),
                        pltpu.VMEM((hp * n_sub, l, ts), F32)]
                       + [pltpu.VMEM((hp * n_sub, 1, ts), F32)] * 3,
        compiler_params=_cparams(("parallel", "parallel", "arbitrary")),
        name="mla_attention",
    )(q, kx, vxt, ky, vyt)


def _ctx_attn_kernel(sink_ref, q_ref, k_ref, v_ref, o_ref, *, heads, base2):
    exp = jnp.exp2 if base2 else jnp.exp
    outs = []
    for (q0, q1, k0, k1, v0, v1, sink_idx) in heads:
        q = q_ref[0, :, q0:q1]
        s = _dot_nt(q, k_ref[0, :, k0:k1])
        m = jnp.max(s, axis=-1, keepdims=True)
        if sink_idx is not None:
            sink = jnp.full((1, 1), sink_ref[sink_idx], F32)
            m = jnp.maximum(m, sink)
        p = exp(s - m)
        l = jnp.sum(p, axis=-1, keepdims=True)
        if sink_idx is not None:
            l = l + exp(sink - m)
        outs.append(_dot(p.astype(BF16), v_ref[0, :, v0:v1]) / l)
    o_ref[0] = jnp.concatenate(outs, axis=-1).astype(o_ref.dtype)


def _ctx_attention(q, k, v, heads, sink, name, base2=False):
    b, l, _ = q.shape
    full = lambda a: pl.BlockSpec((1, l, a.shape[-1]), lambda bi: (bi, 0, 0))
    return pl.pallas_call(
        functools.partial(_ctx_attn_kernel, heads=heads, base2=base2),
        grid=(b,),
        in_specs=[pl.BlockSpec(memory_space=pltpu.SMEM), full(q), full(k), full(v)],
        out_specs=pl.BlockSpec((1, l, GROUP_W), lambda bi: (bi, 0, 0)),
        out_shape=jax.ShapeDtypeStruct((b, l, GROUP_W), BF16),
        compiler_params=_cparams(("parallel",)),
        name=name,
    )(sink, q, k, v)


def _ret_kernel(*refs, chunk, n_chunks, direction, has_prev):
    if has_prev:
        (dec_ref, q_ref, k_ref, kt_ref, v_ref, g_ref, s0_ref, prev_ref, o_ref, sn_ref,
         st_ref, dm_ref, qd_ref, kdt_ref, cd_ref, ob_ref) = refs
    else:
        (dec_ref, q_ref, k_ref, kt_ref, v_ref, g_ref, s0_ref, o_ref, sn_ref,
         st_ref, dm_ref, qd_ref, kdt_ref, cd_ref, ob_ref) = refs
        prev_ref = None
    i = pl.program_id(1)
    c = chunk
    fwd = direction == 0

    def head_of(shape, axis):
        return lax.broadcasted_iota(jnp.int32, shape, axis) // HEAD_DIM

    @pl.when(i == 0)
    def _init():
        st_ref[...] = s0_ref[0]
        ii = lax.broadcasted_iota(jnp.int32, (c, c), 0).astype(F32)
        jj = lax.broadcasted_iota(jnp.int32, (c, c), 1).astype(F32)
        diff = (ii - jj) if fwd else (jj - ii)
        pos = lax.broadcasted_iota(jnp.int32, (c, GROUP_W), 0).astype(F32)
        pos_t = lax.broadcasted_iota(jnp.int32, (GROUP_W, c), 1).astype(F32)
        q_steps = (pos + 1.0) if fwd else (c - pos)
        k_steps = (c - 1.0 - pos_t) if fwd else pos_t
        qd = jnp.zeros((c, GROUP_W), F32)
        kdt = jnp.zeros((GROUP_W, c), F32)
        cd = jnp.zeros((GROUP_W, GROUP_W), F32)
        for h in range(N_HEADS):
            dec = dec_ref[direction * N_HEADS + h]
            lg = jax.nn.log_sigmoid(jnp.full((c, c), dec, F32))
            dm_ref[h * c:(h + 1) * c] = jnp.where(diff >= 0, jnp.exp(lg * jnp.maximum(diff, 0.0)), 0.0)
            lgq = jax.nn.log_sigmoid(jnp.full((c, GROUP_W), dec, F32))
            qd = jnp.where(head_of((c, GROUP_W), 1) == h, jnp.exp(lgq * q_steps), qd)
            lgk = jax.nn.log_sigmoid(jnp.full((GROUP_W, c), dec, F32))
            kdt = jnp.where(head_of((GROUP_W, c), 0) == h, jnp.exp(lgk * k_steps), kdt)
            lgc = jax.nn.log_sigmoid(jnp.full((GROUP_W, GROUP_W), dec, F32))
            cd = jnp.where(head_of((GROUP_W, GROUP_W), 0) == h, jnp.exp(lgc * c), cd)
        qd_ref[...] = qd
        kdt_ref[...] = kdt
        cd_ref[...] = cd

    in_head = [head_of((c, GROUP_W), 1) == h for h in range(N_HEADS)]
    head_mask = [jnp.where(mk, 1.0, 0.0).astype(BF16) for mk in in_head]
    same_head = head_of((GROUP_W, GROUP_W), 0) == head_of((GROUP_W, GROUP_W), 1)

    def chunk_off(n):
        idx = n if fwd else n_chunks - 1 - n
        return pl.multiple_of(idx * c, c)

    def qk(n):
        off = chunk_off(n)
        q_all = q_ref[0, pl.ds(off, c), :]
        q4 = jnp.concatenate([q_all * head_mask[h] for h in range(N_HEADS)], axis=0)
        return _dot_nt(q4, k_ref[0, pl.ds(off, c), :])

    def gate_and_store(n):
        off = chunk_off(n)
        o = ob_ref[...]
        oo = o * o
        ms = jnp.zeros((c, GROUP_W), F32)
        for h in range(N_HEADS):
            ms_h = jnp.sum(jnp.where(in_head[h], oo, 0.0), axis=-1, keepdims=True) * (1.0 / HEAD_DIM)
            ms = jnp.where(in_head[h], ms_h, ms)
        res = o * lax.rsqrt(ms + EPS) * _silu(g_ref[0, pl.ds(off, c), :])
        if has_prev:
            res = res + prev_ref[0, pl.ds(off, c), :]
        o_ref[0, pl.ds(off, c), :] = res.astype(o_ref.dtype)

    ob_ref[...] = jnp.zeros(ob_ref.shape, F32)

    def body(n, att_raw):
        gate_and_store(jnp.maximum(n - 1, 0))
        off = chunk_off(n)
        v = v_ref[0, pl.ds(off, c), :]
        nxt = qk(jnp.minimum(n + 1, n_chunks - 1))
        att = (att_raw * dm_ref[...]).astype(BF16)
        intra4 = _dot(att, v)
        intra = intra4[:c]
        for h in range(1, N_HEADS):
            intra = jnp.where(in_head[h], intra4[h * c:(h + 1) * c], intra)
        state = st_ref[...]
        ob_ref[...] = intra + _dot(q_ref[0, pl.ds(off, c), :], state.astype(BF16)) * qd_ref[...]
        kk = (kt_ref[0, :, pl.ds(off, c)].astype(F32) * kdt_ref[...]).astype(BF16)
        st_ref[...] = state * cd_ref[...] + jnp.where(same_head, _dot(kk, v), 0.0)
        return nxt

    lax.fori_loop(0, n_chunks, body, qk(0))
    gate_and_store(n_chunks - 1)

    @pl.when(i == pl.num_programs(1) - 1)
    def _fin():
        sn_ref[0] = st_ref[...]


def _retention_pass(dec, q, k, v, gates, state0, prev, direction, chunk, n_chunks, out_dtype):
    b, t, _ = q.shape
    tb = chunk * n_chunks
    n = t // tb
    if direction == 0:
        blk = lambda bi, i: (bi, i, 0)
        gblk = lambda bi, i: (bi, i, 0)
    else:
        blk = lambda bi, i: (bi, n - 1 - i, 0)
        gblk = lambda bi, i: (bi, n - 1 - i, 1)
    tok = pl.BlockSpec((1, tb, GROUP_W), blk)
    tok_t = pl.BlockSpec((1, GROUP_W, tb), lambda bi, i: (bi, 0, blk(bi, i)[1]))
    st_spec = pl.BlockSpec((1, GROUP_W, GROUP_W), lambda bi, i: (bi, 0, 0))
    in_specs = [pl.BlockSpec(memory_space=pltpu.SMEM), tok, tok, tok_t, tok,
                pl.BlockSpec((1, tb, GROUP_W), gblk), st_spec]
    args = [dec, q, k, jnp.swapaxes(k, 1, 2), v, gates, state0]
    if prev is not None:
        in_specs.append(tok)
        args.append(prev)
    return pl.pallas_call(
        functools.partial(_ret_kernel, chunk=chunk, n_chunks=n_chunks, direction=direction,
                          has_prev=prev is not None),
        grid=(b, n),
        in_specs=in_specs,
        out_specs=[tok, st_spec],
        out_shape=[jax.ShapeDtypeStruct((b, t, GROUP_W), out_dtype),
                   jax.ShapeDtypeStruct((b, GROUP_W, GROUP_W), F32)],
        scratch_shapes=[pltpu.VMEM((GROUP_W, GROUP_W), F32),
                        pltpu.VMEM((N_HEADS * chunk, chunk), F32),
                        pltpu.VMEM((chunk, GROUP_W), F32),
                        pltpu.VMEM((GROUP_W, chunk), F32),
                        pltpu.VMEM((GROUP_W, GROUP_W), F32),
                        pltpu.VMEM((chunk, GROUP_W), F32)],
        compiler_params=_cparams(("parallel", "arbitrary")),
        name="retention_fwd" if direction == 0 else "retention_bwd",
    )(*args)


def _retention(dec, xq, xk, xv, xg, yq, yk, yv, yg, chunk, chunks_per_step):
    b = xq.shape[0]
    zero = jnp.zeros((b, GROUP_W, GROUP_W), F32)
    ny = yq.shape[1] // chunk
    yb, sb = _retention_pass(dec, yq, yk, yv, yg, zero, None, 1, chunk, ny, F32)
    y, sf = _retention_pass(dec, yq, yk, yv, yg, zero, yb, 0, chunk, ny, BF16)
    xb, _ = _retention_pass(dec, xq, xk, xv, xg, sb, None, 1, chunk, chunks_per_step, F32)
    x, _ = _retention_pass(dec, xq, xk, xv, xg, sf, xb, 0, chunk, chunks_per_step, BF16)
    return x, y


def _na_bias_kernel(rpb_ref, o_ref):
    h = pl.program_id(0)
    dr0 = pl.program_id(1)
    c = lax.broadcasted_iota(jnp.int32, (GRID_W, GRID_W), 0)
    kc = lax.broadcasted_iota(jnp.int32, (GRID_W, GRID_W), 1)
    c0 = jnp.clip(c - NA_KC // 2, 0, GRID_W - NA_KC)
    col_in = (kc >= c0) & (kc < c0 + NA_KC)
    dc = jnp.clip(kc - c, -(NA_KC - 1), NA_KC - 1) + NA_KC - 1
    n_dc = 2 * NA_KC - 1
    for j in range(NA_KR):
        base = (h * (2 * NA_KR - 1) + dr0 + j) * n_dc
        acc = jnp.zeros((GRID_W, GRID_W), F32)
        for d in range(n_dc):
            acc = jnp.where(dc == d, rpb_ref[base + d], acc)
        o_ref[0, :, j * GRID_W:(j + 1) * GRID_W] = jnp.where(col_in, acc, NEG_INF)


def _na_bias_table(rpb):
    return pl.pallas_call(
        _na_bias_kernel,
        grid=(N_HEADS, NA_KR),
        in_specs=[pl.BlockSpec(memory_space=pltpu.SMEM)],
        out_specs=pl.BlockSpec((1, GRID_W, NA_KR * GRID_W), lambda h, r: (r, h, 0)),
        out_shape=jax.ShapeDtypeStruct((NA_KR, N_HEADS * GRID_W, NA_KR * GRID_W), F32),
        compiler_params=_cparams(("parallel", "parallel")),
        name="na_bias_table",
    )(rpb.reshape(-1))


def _na_kernel(q_ref, k_ref, v_ref, ky_ref, vy_ref, tb_ref, o_ref,
               sa_ref, sb_ref, pa_ref, pb_ref, la_ref, lb_ref, *, rows_per_step, n_rows):
    r_base = pl.program_id(1) * rows_per_step
    win = NA_KR * GRID_W

    head_of_lane = lax.broadcasted_iota(jnp.int32, (GRID_W, GROUP_W), 1) // HEAD_DIM
    in_head = [head_of_lane == h for h in range(N_HEADS)]
    head_mask = [jnp.where(mk, 1.0, 0.0).astype(BF16) for mk in in_head]

    def geometry(i):
        r = r_base + i
        r0 = jnp.clip(r - NA_KR // 2, 0, n_rows - NA_KR)
        return pl.multiple_of(r0 * GRID_W, GRID_W), r0 - r + NA_KR - 1

    def row_slice(i):
        return pl.ds(pl.multiple_of(i * GRID_W, GRID_W), GRID_W)

    def scores(i, s_ref):
        koff, dr0 = geometry(i)
        q_all = q_ref[0, row_slice(i), :]
        q4 = jnp.concatenate([q_all * head_mask[h] for h in range(N_HEADS)], axis=0)
        s_ref[:, :win] = _dot_nt(q4, k_ref[0, pl.ds(koff, win), :]) + tb_ref[dr0]
        s_ref[:, win:] = _dot_nt(q4, ky_ref[0])

    def softmax(s_ref, p_ref, l_ref):
        s = s_ref[...]
        p = jnp.exp(s - _row_max(s))
        l_ref[...] = 1.0 / _row_sum(p)
        p_ref[...] = p.astype(BF16)

    def values(i, p_ref, l_ref):
        koff, _ = geometry(i)
        o4 = (_dot(p_ref[:, :win], v_ref[0, pl.ds(koff, win), :])
              + _dot(p_ref[:, win:], vy_ref[0])) * l_ref[...]
        out = o4[:GRID_W]
        for h in range(1, N_HEADS):
            out = jnp.where(in_head[h], o4[h * GRID_W:(h + 1) * GRID_W], out)
        o_ref[0, row_slice(i), :] = out.astype(o_ref.dtype)

    pb_ref[...] = jnp.zeros(pb_ref.shape, BF16)
    lb_ref[...] = jnp.zeros(lb_ref.shape, F32)
    scores(0, sa_ref)

    def pair_body(tt, carry):
        t = 2 * tt
        values(jnp.maximum(t - 1, 0), pb_ref, lb_ref)
        scores(t + 1, sb_ref)
        softmax(sa_ref, pa_ref, la_ref)
        values(t, pa_ref, la_ref)
        scores(jnp.minimum(t + 2, rows_per_step - 1), sa_ref)
        softmax(sb_ref, pb_ref, lb_ref)
        return carry

    lax.fori_loop(0, rows_per_step // 2, pair_body, 0)
    values(rows_per_step - 1, pb_ref, lb_ref)


def _na_attention(q, k, v, ky, vy, table, rows_per_step):
    b, s, _ = q.shape
    l = ky.shape[1]
    n_rows = s // GRID_W
    tq = rows_per_step * GRID_W
    n_keys = NA_KR * GRID_W + l
    seq = lambda n: pl.BlockSpec((1, n, GROUP_W), lambda bi, i: (bi, 0, 0))
    return pl.pallas_call(
        functools.partial(_na_kernel, rows_per_step=rows_per_step, n_rows=n_rows),
        grid=(b, n_rows // rows_per_step),
        in_specs=[pl.BlockSpec((1, tq, GROUP_W), lambda bi, i: (bi, i, 0)),
                  seq(s), seq(s), seq(l), seq(l), _const_spec(table.shape)],
        out_specs=pl.BlockSpec((1, tq, GROUP_W), lambda bi, i: (bi, i, 0)),
        out_shape=jax.ShapeDtypeStruct((b, s, GROUP_W), BF16),
        scratch_shapes=([pltpu.VMEM((N_HEADS * GRID_W, n_keys), F32)] * 2
                        + [pltpu.VMEM((N_HEADS * GRID_W, n_keys), BF16)] * 2
                        + [pltpu.VMEM((N_HEADS * GRID_W, 1), F32)] * 2),
        compiler_params=_cparams(("parallel", "arbitrary")),
        name="na_attention",
    )(q, k, v, ky, vy, table)


def _swa_kernel(sink_ref, q_ref, kp_ref, kc_ref, kn_ref, vp_ref, vc_ref, vn_ref, ky_ref, vy_ref, o_ref,
                kw_ref, vw_ref, wb_ref, sa_ref, sb_ref, pa_ref, pb_ref, la_ref, lb_ref, *, blocks_per_step):
    step = pl.program_id(1)
    nb = pl.num_programs(1) * blocks_per_step
    bl = SWA_BLOCK
    g = N_HEADS // SWA_KV_HEADS
    tq = blocks_per_step * bl
    kw_ref[0:bl] = kp_ref[0]
    kw_ref[bl:bl + tq] = kc_ref[0]
    kw_ref[bl + tq:] = kn_ref[0]
    vw_ref[0:bl] = vp_ref[0]
    vw_ref[bl:bl + tq] = vc_ref[0]
    vw_ref[bl + tq:] = vn_ref[0]

    qi = lax.broadcasted_iota(jnp.int32, (g * bl, 3 * bl), 0) % bl
    jk = lax.broadcasted_iota(jnp.int32, (g * bl, 3 * bl), 1)
    wb_ref[...] = jnp.where(jnp.abs(jk - bl - qi) <= SWA_WINDOW, 0.0, NEG_INF)
    half = lax.broadcasted_iota(jnp.int32, (g * bl, 1), 0) // bl

    head_of_lane = lax.broadcasted_iota(jnp.int32, (bl, GROUP_W), 1) // HEAD_DIM
    in_head = [head_of_lane == h for h in range(N_HEADS)]
    head_mask = [jnp.where(mk, 1.0, 0.0).astype(BF16) for mk in in_head]

    def blk_slice(j):
        return pl.ds(pl.multiple_of(j * bl, bl), bl)

    def scores(j, s_ref):
        qoff = pl.multiple_of(j * bl, bl)
        q_all = q_ref[0, blk_slice(j), :]
        kw = kw_ref[pl.ds(qoff, 3 * bl), :]
        n = step * blocks_per_step + j
        lo_edge = jnp.where(n == 0, NEG_INF, 0.0)
        hi_edge = jnp.where(n == nb - 1, NEG_INF, 0.0)
        for kh in range(SWA_KV_HEADS):
            q = jnp.concatenate([q_all * head_mask[kh * g + gi] for gi in range(g)], axis=0)
            s = _dot_nt(q, kw) + wb_ref[...]
            s_ref[kh, :, :bl] = s[:, :bl] + lo_edge
            s_ref[kh, :, bl:2 * bl] = s[:, bl:2 * bl]
            s_ref[kh, :, 2 * bl:3 * bl] = s[:, 2 * bl:] + hi_edge
            s_ref[kh, :, 3 * bl:] = _dot_nt(q, ky_ref[0])

    def softmax(s_ref, p_ref, l_ref):
        for kh in range(SWA_KV_HEADS):
            s = s_ref[kh]
            sink = jnp.full((g * bl, 1), sink_ref[kh * g], F32)
            for gi in range(1, g):
                sink = jnp.where(half == gi, sink_ref[kh * g + gi], sink)
            m = jnp.maximum(_row_max(s), sink)
            p = jnp.exp(s - m)
            l_ref[kh] = 1.0 / (_row_sum(p) + jnp.exp(sink - m))
            p_ref[kh] = p.astype(BF16)

    def values(j, p_ref, l_ref):
        vw = vw_ref[pl.ds(pl.multiple_of(j * bl, bl), 3 * bl), :]
        out = jnp.zeros((bl, GROUP_W), F32)
        for kh in range(SWA_KV_HEADS):
            o = (_dot(p_ref[kh, :, :3 * bl], vw) + _dot(p_ref[kh, :, 3 * bl:], vy_ref[0])) * l_ref[kh]
            for gi in range(g):
                out = jnp.where(in_head[kh * g + gi], o[gi * bl:(gi + 1) * bl], out)
        o_ref[0, blk_slice(j), :] = out.astype(o_ref.dtype)

    pb_ref[...] = jnp.zeros(pb_ref.shape, BF16)
    lb_ref[...] = jnp.zeros(lb_ref.shape, F32)
    scores(0, sa_ref)

    def pair_body(tt, carry):
        t = 2 * tt
        values(jnp.maximum(t - 1, 0), pb_ref, lb_ref)
        scores(t + 1, sb_ref)
        softmax(sa_ref, pa_ref, la_ref)
        values(t, pa_ref, la_ref)
        scores(jnp.minimum(t + 2, blocks_per_step - 1), sa_ref)
        softmax(sb_ref, pb_ref, lb_ref)
        return carry

    lax.fori_loop(0, blocks_per_step // 2, pair_body, 0)
    values(blocks_per_step - 1, pb_ref, lb_ref)


def _swa_attention(sink, q, k, v, ky, vy, blocks_per_step):
    b, s, _ = q.shape
    l = ky.shape[1]
    tq = blocks_per_step * SWA_BLOCK
    n_steps = s // tq
    nb = s // SWA_BLOCK
    kvw = GROUP_W
    q_rows = N_HEADS // SWA_KV_HEADS * SWA_BLOCK
    n_keys = 3 * SWA_BLOCK + l
    prev = pl.BlockSpec((1, SWA_BLOCK, kvw), lambda bi, i: (bi, jnp.maximum(i * blocks_per_step - 1, 0), 0))
    cur = pl.BlockSpec((1, tq, kvw), lambda bi, i: (bi, i, 0))
    nxt = pl.BlockSpec((1, SWA_BLOCK, kvw), lambda bi, i: (bi, jnp.minimum((i + 1) * blocks_per_step, nb - 1), 0))
    ctx = pl.BlockSpec((1, l, kvw), lambda bi, i: (bi, 0, 0))
    return pl.pallas_call(
        functools.partial(_swa_kernel, blocks_per_step=blocks_per_step),
        grid=(b, n_steps),
        in_specs=[pl.BlockSpec(memory_space=pltpu.SMEM),
                  pl.BlockSpec((1, tq, GROUP_W), lambda bi, i: (bi, i, 0)),
                  prev, cur, nxt, prev, cur, nxt, ctx, ctx],
        out_specs=pl.BlockSpec((1, tq, GROUP_W), lambda bi, i: (bi, i, 0)),
        out_shape=jax.ShapeDtypeStruct((b, s, GROUP_W), BF16),
        scratch_shapes=[pltpu.VMEM((tq + 2 * SWA_BLOCK, kvw), BF16),
                        pltpu.VMEM((tq + 2 * SWA_BLOCK, kvw), BF16),
                        pltpu.VMEM((q_rows, 3 * SWA_BLOCK), F32)]
                       + [pltpu.VMEM((SWA_KV_HEADS, q_rows, n_keys), F32)] * 2
                       + [pltpu.VMEM((SWA_KV_HEADS, q_rows, n_keys), BF16)] * 2
                       + [pltpu.VMEM((SWA_KV_HEADS, q_rows, 1), F32)] * 2,
        compiler_params=_cparams(("parallel", "arbitrary")),
        name="swa_attention",
    )(sink, q, k, k, k, v, v, v, ky, vy)


def _outffn_kernel(*refs, hidden_chunk, final):
    if final:
        (x_ref, m0_ref, m1_ref, m2_ref, m3_ref, wo_ref, g1_ref, n2_ref, sc_ref, sh_ref, g2_ref,
         w1_ref, w3_ref, w2_ref, fg_ref, o_ref) = refs
    else:
        (x_ref, m0_ref, m1_ref, m2_ref, m3_ref, wo_ref, g1_ref, n2_ref, sc_ref, sh_ref, g2_ref,
         w1_ref, w3_ref, w2_ref, o_ref) = refs
    mix = None
    for gi, m_ref in enumerate((m0_ref, m1_ref, m2_ref, m3_ref)):
        part = _dot(m_ref[0], wo_ref[gi * GROUP_W:(gi + 1) * GROUP_W, :])
        mix = part if mix is None else mix + part
    x1 = x_ref[0] + g1_ref[0] * mix
    hb = ((_rms(x1) * n2_ref[...]) * (1.0 + sc_ref[0]) + sh_ref[0]).astype(BF16)
    hidden = w1_ref.shape[1]
    acc = None
    for c0 in range(0, hidden, hidden_chunk):
        a = _dot(hb, w1_ref[:, c0:c0 + hidden_chunk])
        bgate = _dot(hb, w3_ref[:, c0:c0 + hidden_chunk])
        u = (_silu(a) * bgate).astype(BF16)
        part = _dot(u, w2_ref[c0:c0 + hidden_chunk, :])
        acc = part if acc is None else acc + part
    x2 = x1 + g2_ref[0] * acc
    if final:
        x2 = _rms(x2) * fg_ref[...]
    o_ref[0] = x2


def _outffn(x, mixes, wo, g1, n2, sc2, sh2, g2, w1, w3, w2, final_g, tm):
    b, t, d = x.shape
    tok = lambda wd: pl.BlockSpec((1, tm, wd), lambda bi, i: (bi, i, 0))
    vec = pl.BlockSpec((1, 1, d), lambda bi, i: (bi, 0, 0))
    in_specs = ([tok(d)] + [tok(GROUP_W)] * 4
                + [_const_spec(wo.shape), vec, _const_spec((1, d)), vec, vec, vec,
                   _const_spec(w1.shape), _const_spec(w3.shape), _const_spec(w2.shape)])
    args = [x, *mixes, wo, g1, n2, sc2, sh2, g2, w1, w3, w2]
    final = final_g is not None
    if final:
        in_specs.append(_const_spec((1, d)))
        args.append(final_g)
    return pl.pallas_call(
        functools.partial(_outffn_kernel, hidden_chunk=256, final=final),
        grid=(b, t // tm),
        in_specs=in_specs,
        out_specs=tok(d),
        out_shape=jax.ShapeDtypeStruct((b, t, d), F32),
        compiler_params=_cparams(("parallel", "parallel")),
        name="out_proj_ffn_final" if final else "out_proj_ffn",
    )(*args)


def _prep_weights(w_in, mla_w_uq, mla_w_ukv):
    depth, d, _ = w_in.shape
    offs = [0]
    for sz in IN_SIZES:
        offs.append(offs[-1] + sz)
    w_bf = w_in.astype(BF16)
    cols = [w_bf[:, :, offs[i]:offs[i + 1]] for i in range(len(IN_SIZES))]
    cq, ckv, kr, rq, rk, rv, rgf, rgb, nq, nk, nv, sq, sk, sv = cols
    scale = HEAD_DIM ** -0.5
    kr_slot = jnp.concatenate([kr, jnp.zeros((depth, d, LANES - MLA_ROPE), BF16)], axis=-1)
    src = jnp.arange(LANES)[:, None]
    dst = jnp.arange(N_HEADS * MLA_HEAD_PAD)[None, :]
    place = ((src < MLA_ROPE) & (dst % MLA_HEAD_PAD == src + MLA_NOPE)).astype(BF16)

    def per_query_head(t):
        g = N_HEADS // SWA_KV_HEADS
        t = t.reshape(depth, d, SWA_KV_HEADS, 1, HEAD_DIM)
        return jnp.broadcast_to(t, (depth, d, SWA_KV_HEADS, g, HEAD_DIM)).reshape(depth, d, GROUP_W)

    w = jnp.concatenate([cq, ckv, kr_slot, rq, rk * scale, rv, rgf, rgb, nq * scale, nk, nv,
                         sq * scale, per_query_head(sk), per_query_head(sv)], axis=-1).astype(BF16)

    qr = mla_w_uq.shape[1]
    uq = mla_w_uq.reshape(depth, qr, N_HEADS, MLA_NOPE + MLA_ROPE)
    wuq = jnp.concatenate([uq, jnp.zeros((depth, qr, N_HEADS, MLA_HEAD_PAD - MLA_NOPE - MLA_ROPE), F32)],
                          axis=-1).reshape(depth, qr, N_HEADS * MLA_HEAD_PAD).astype(BF16)
    kvr = mla_w_ukv.shape[1]
    ukv = mla_w_ukv.reshape(depth, kvr, N_HEADS, MLA_NOPE + MLA_V)
    zk = jnp.zeros((depth, kvr, N_HEADS, MLA_HEAD_PAD - MLA_NOPE), F32)
    wuk = jnp.concatenate([ukv[..., :MLA_NOPE], zk], axis=-1).reshape(depth, kvr, -1).astype(BF16)
    zv = jnp.zeros((depth, kvr, N_HEADS, MLA_HEAD_PAD - MLA_V), F32)
    wuv = jnp.concatenate([ukv[..., MLA_NOPE:], zv], axis=-1).reshape(depth, kvr, -1).astype(BF16)
    return w, wuq, wuk, wuv, place


def _ctx_head_specs():
    mla = tuple((h * MLA_HEAD_PAD, (h + 1) * MLA_HEAD_PAD, h * MLA_HEAD_PAD, (h + 1) * MLA_HEAD_PAD,
                 h * MLA_HEAD_PAD, h * MLA_HEAD_PAD + MLA_V, None) for h in range(N_HEADS))
    na = tuple((h * HEAD_DIM, (h + 1) * HEAD_DIM) * 3 + (None,) for h in range(N_HEADS))
    swa = tuple((h * HEAD_DIM, (h + 1) * HEAD_DIM) * 3 + (h,) for h in range(N_HEADS))
    return mla, na, swa


def kernel(x, c, ctx, c_ctx, ada_w, ada_b, norm1_g, w_in, mla_q_norm, mla_w_uq, mla_kv_norm, mla_w_ukv,
           ret_decay, na_rpb, swa_sink, w_out, norm2_g, ffn_w1, ffn_w3, ffn_w2, final_norm_g):
    b, s, d = x.shape
    l_ctx = ctx.shape[1]
    depth = ada_w.shape[0]
    assert b + 1 <= 8 and s % 1024 == 0 and l_ctx % 128 == 0

    cond = jnp.concatenate([c, c_ctx[None, :], jnp.zeros((8 - b - 1, d), F32)], axis=0)
    mod = _modulation(cond, ada_w, ada_b)
    tables = _rope_tables(s)
    w_all, wuq_all, wuk_all, wuv_all, place = _prep_weights(w_in, mla_w_uq, mla_w_ukv)
    wo_all = w_out.astype(BF16)
    w1_all, w3_all, w2_all = ffn_w1.astype(BF16), ffn_w3.astype(BF16), ffn_w2.astype(BF16)
    mla_heads, na_heads, swa_heads = _ctx_head_specs()
    no_sink = jnp.zeros((N_HEADS,), F32)

    tm_x = 512
    tm_y = min(256, l_ctx)
    y = ctx
    for l in range(depth):
        mx = [mod[l, :b, j * d:(j + 1) * d][:, None, :] for j in range(6)]
        my = [jnp.broadcast_to(mod[l, b, j * d:(j + 1) * d][None, None, :], (b, 1, d)) for j in range(6)]
        n1 = norm1_g[l][None, :]
        n2 = norm2_g[l][None, :]
        qn = mla_q_norm[l][None, :]
        kvn = mla_kv_norm[l][None, :]
        lw = (w_all[l], qn, kvn, wuq_all[l], wuk_all[l], wuv_all[l], place)

        px = _inproj(x, n1, mx[1], mx[0], *lw, tables, tm_x)
        py = _inproj(y, n1, my[1], my[0], *lw, None, tm_y)
        (xmq, xmk, xmv, xrq, xrk, xrv, xrg, xnq, xnk, xnv, xsq, xsk, xsv) = px
        (ymq, ymk, ymv, yrq, yrk, yrv, yrg, ynq, ynk, ynv, ysq, ysk, ysv) = py

        mla_x = _mla_attention(xmq, xmk, xmv, ymk, ymv, tq=512, tk=512, n_sub=2, unroll=8)
        dec = ret_decay[l].reshape(-1)
        ret_x, ret_y = _retention(dec, xrq, xrk, xrv, xrg, yrq, yrk, yrv, yrg, chunk=128, chunks_per_step=8)
        table = _na_bias_table(na_rpb[l])
        na_x = _na_attention(xnq, xnk, xnv, ynk, ynv, table, rows_per_step=16)
        swa_x = _swa_attention(swa_sink[l], xsq, xsk, xsv, ysk, ysv, blocks_per_step=8)

        last = l == depth - 1
        x = _outffn(x, (mla_x, ret_x, na_x, swa_x), wo_all[l], mx[2], n2, mx[4], mx[3], mx[5],
                    w1_all[l], w3_all[l], w2_all[l], final_norm_g[None, :] if last else None, tm_x)
        if not last:
            mla_y = _ctx_attention(ymq, ymk, ymv, mla_heads, no_sink, "mla_ctx_attention", base2=True)
            na_y = _ctx_attention(ynq, ynk, ynv, na_heads, no_sink, "na_ctx_attention")
            swa_y = _ctx_attention(ysq, ysk, ysv, swa_heads, swa_sink[l], "swa_ctx_attention")
            y = _outffn(y, (mla_y, ret_y, na_y, swa_y), wo_all[l], my[2], n2, my[4], my[3], my[5],
                        w1_all[l], w3_all[l], w2_all[l], None, tm_y)
    return x
```

```python
import functools

import jax
import jax.numpy as jnp
from jax import lax
from jax.experimental import pallas as pl
from jax.experimental.pallas import tpu as pltpu

F32 = jnp.float32
BF16 = jnp.bfloat16

GRID_W = 64
HEAD_DIM = 64
N_HEADS = 4
GROUP_W = N_HEADS * HEAD_DIM
MLA_Q_RANK = 256
MLA_KV_RANK = 128
MLA_NOPE = 64
MLA_ROPE = 32
MLA_V = 64
MLA_HEAD_PAD = 128
MLA_V_ROWS = MLA_V + 16
NA_KR = 8
NA_KC = 16
SWA_KV_HEADS = 2
SWA_WINDOW = 128
SWA_BLOCK = 128
ROPE_THETA = 10000.0
EPS = 1e-6
NEG_INF = -1e30
LOG2_E = 1.4426950408889634
LANES = 128
VMEM_LIMIT = 56 * 1024 * 1024

IN_SIZES = (MLA_Q_RANK, MLA_KV_RANK, MLA_ROPE,
            GROUP_W, GROUP_W, GROUP_W, GROUP_W, GROUP_W,
            GROUP_W, GROUP_W, GROUP_W,
            GROUP_W, SWA_KV_HEADS * HEAD_DIM, SWA_KV_HEADS * HEAD_DIM)

_O_CQ = 0
_O_CKV = _O_CQ + MLA_Q_RANK
_O_KR = _O_CKV + MLA_KV_RANK
_O_RQK = _O_KR + LANES
_O_RV = _O_RQK + 2 * GROUP_W
_O_RG = _O_RV + GROUP_W
_O_NA = _O_RG + 2 * GROUP_W
_O_SQK = _O_NA + 3 * GROUP_W
_O_SV = _O_SQK + 2 * GROUP_W
_IN_COLS = _O_SV + GROUP_W


def _cparams(sem):
    return pltpu.CompilerParams(dimension_semantics=sem, vmem_limit_bytes=VMEM_LIMIT)


def _dot(a, b):
    return jnp.dot(a, b, preferred_element_type=F32)


def _dot_nt(a, b):
    return lax.dot_general(a, b, (((1,), (1,)), ((), ())), preferred_element_type=F32)


def _dot_tn(a, b):
    return lax.dot_general(a, b, (((0,), (0,)), ((), ())), preferred_element_type=F32)


def _rms(x):
    return x * lax.rsqrt(jnp.mean(x * x, axis=-1, keepdims=True) + EPS)


def _silu(x):
    return x * jax.nn.sigmoid(x)


def _lane_chunks(arrays):
    return [a[:, j * LANES:(j + 1) * LANES] for a in arrays for j in range(a.shape[-1] // LANES)]


def _row_max(*arrays):
    return jnp.max(functools.reduce(jnp.maximum, _lane_chunks(arrays)), axis=-1, keepdims=True)


def _row_sum(*arrays):
    return jnp.sum(functools.reduce(jnp.add, _lane_chunks(arrays)), axis=-1, keepdims=True)


def _mod_kernel(c_ref, w_ref, b_ref, o_ref):
    o_ref[0] = _dot(_silu(c_ref[...]), w_ref[0]) + b_ref[0]


def _modulation(cond, ada_w, ada_b):
    depth, d, d6 = ada_w.shape
    n = d6 // d
    return pl.pallas_call(
        _mod_kernel,
        grid=(depth, n),
        in_specs=[pl.BlockSpec((8, d), lambda l, j: (0, 0)),
                  pl.BlockSpec((1, d, d), lambda l, j: (l, 0, j)),
                  pl.BlockSpec((1, 1, d), lambda l, j: (l, 0, j))],
        out_specs=pl.BlockSpec((1, 8, d), lambda l, j: (l, 0, j)),
        out_shape=jax.ShapeDtypeStruct((depth, 8, d6), F32),
        compiler_params=_cparams(("parallel", "parallel")),
        name="ada_modulation",
    )(cond, ada_w, ada_b.reshape(depth, 1, d6))


def _rope_tables(seq):
    n_rows = seq // GRID_W

    def parts(pos, d):
        inv = ROPE_THETA ** (-jnp.arange(0, d, 2, dtype=F32) / d)
        ang = pos.astype(F32)[:, None] * inv[None, :]
        z = jnp.zeros_like(ang)
        return (jnp.concatenate([jnp.cos(ang), jnp.cos(ang)], axis=-1),
                jnp.concatenate([z, jnp.sin(ang)], axis=-1),
                jnp.concatenate([-jnp.sin(ang), z], axis=-1))

    def expand(by_row, by_col):
        w = by_row.shape[-1]
        r = jnp.broadcast_to(by_row[:, None, :], (n_rows, GRID_W, w))
        c = jnp.broadcast_to(by_col[None, :, :], (n_rows, GRID_W, w))
        return jnp.concatenate([r, c], axis=-1).reshape(seq, 2 * w)

    def tables(d, fill):
        per_head = [expand(a, b) for a, b in zip(parts(jnp.arange(n_rows), d), parts(jnp.arange(GRID_W), d))]
        return [fill(t, i) for i, t in enumerate(per_head)]

    def two_heads(t, _):
        return jnp.concatenate([t, t], axis=-1)

    def mla_slot(t, i):
        lead = (jnp.ones if i == 0 else jnp.zeros)((seq, MLA_NOPE), F32)
        tail = (jnp.ones if i == 0 else jnp.zeros)((seq, MLA_HEAD_PAD - MLA_NOPE - MLA_ROPE), F32)
        return jnp.concatenate([lead, t, tail], axis=-1)

    def kr_slot(t, i):
        tail = (jnp.ones if i == 0 else jnp.zeros)((seq, LANES - MLA_ROPE), F32)
        return jnp.concatenate([t, tail], axis=-1)

    return tuple(tables(HEAD_DIM // 2, two_heads) + tables(MLA_ROPE // 2, mla_slot)
                 + tables(MLA_ROPE // 2, kr_slot))


def _rope(x, cos, s_prev, s_next, d):
    out = []
    for j in range(x.shape[-1] // LANES):
        xc = x[:, j * LANES:(j + 1) * LANES]
        out.append(xc * cos + pltpu.roll(xc, d, 1) * s_prev + pltpu.roll(xc, LANES - d, 1) * s_next)
    return out[0] if len(out) == 1 else jnp.concatenate(out, axis=-1)


def _inproj_kernel(*refs, rotate, mla_scale):
    if rotate:
        (x_ref, g_ref, sc_ref, sh_ref, w_ref, qn_ref, kvn_ref, wuq_ref, wuk_ref, wuv_ref, place_ref,
         c64_ref, p64_ref, n64_ref, cm_ref, pm_ref, nm_ref, ckr_ref, pkr_ref, nkr_ref,
         mq_ref, mk_ref, mv_ref, rq_ref, rk_ref, rv_ref, rg_ref,
         nq_ref, nk_ref, nv_ref, sq_ref, sk_ref, sv_ref) = refs
    else:
        (x_ref, g_ref, sc_ref, sh_ref, w_ref, qn_ref, kvn_ref, wuq_ref, wuk_ref, wuv_ref, place_ref,
         mq_ref, mk_ref, mv_ref, rq_ref, rk_ref, rv_ref, rg_ref,
         nq_ref, nk_ref, nv_ref, sq_ref, sk_ref, sv_ref) = refs

    x = x_ref[0]
    h = (_rms(x) * g_ref[...]) * (1.0 + sc_ref[0]) + sh_ref[0]
    hb = h.astype(BF16)

    def proj(lo, hi):
        return _dot(hb, w_ref[:, lo:hi])

    def rope64(v):
        if not rotate:
            return v
        return _rope(v, c64_ref[...], p64_ref[...], n64_ref[...], HEAD_DIM // 4)

    def rope_mla(v):
        if not rotate:
            return v
        return _rope(v, cm_ref[...], pm_ref[...], nm_ref[...], MLA_ROPE // 4)

    cq = (_rms(proj(_O_CQ, _O_CKV)) * qn_ref[...]).astype(BF16)
    ckv_kr = proj(_O_CKV, _O_RQK)
    ckv = (_rms(ckv_kr[:, :MLA_KV_RANK]) * kvn_ref[...]).astype(BF16)
    kr = ckv_kr[:, MLA_KV_RANK:]
    if rotate:
        kr = _rope(kr, ckr_ref[...], pkr_ref[...], nkr_ref[...], MLA_ROPE // 4)
    kr = kr.astype(BF16)

    rqk = proj(_O_RQK, _O_RV)
    rq_ref[0] = rope64(rqk[:, :GROUP_W]).astype(BF16)
    rk_ref[0] = rope64(rqk[:, GROUP_W:]).astype(BF16)
    rv_ref[0] = proj(_O_RV, _O_RG).astype(BF16)
    rg_ref[0] = proj(_O_RG, _O_NA)

    na = proj(_O_NA, _O_SQK)
    nq_ref[0] = na[:, :GROUP_W].astype(BF16)
    nk_ref[0] = na[:, GROUP_W:2 * GROUP_W].astype(BF16)
    nv_ref[0] = na[:, 2 * GROUP_W:].astype(BF16)

    sqk = proj(_O_SQK, _O_SV)
    sq_ref[0] = rope64(sqk[:, :GROUP_W]).astype(BF16)
    sk_ref[0] = rope64(sqk[:, GROUP_W:]).astype(BF16)
    sv_ref[0] = proj(_O_SV, _IN_COLS).astype(BF16)

    q = rope_mla(_dot(cq, wuq_ref[...])) * mla_scale
    mq_ref[0] = q.astype(BF16)
    k = _dot(ckv, wuk_ref[...]) + _dot(kr, place_ref[...])
    mk_ref[0] = k.astype(BF16)
    v = _dot(ckv, wuv_ref[...])
    lane = lax.broadcasted_iota(jnp.int32, v.shape, 1)
    mv_ref[0] = jnp.where(lane % MLA_HEAD_PAD >= MLA_V, 1.0, v).astype(BF16)


def _const_spec(shape):
    nd = len(shape)
    return pl.BlockSpec(shape, lambda *_: (0,) * nd, pipeline_mode=pl.Buffered(1))


def _inproj(x, gain, scale, shift, w, qn, kvn, wuq, wuk, wuv, place, tables, tm):
    b, t, d = x.shape
    rotate = tables is not None
    kv_w = GROUP_W
    mla_w = N_HEADS * MLA_HEAD_PAD
    tok = lambda wd: pl.BlockSpec((1, tm, wd), lambda bi, i: (bi, i, 0))
    vec = pl.BlockSpec((1, 1, d), lambda bi, i: (bi, 0, 0))
    in_specs = [tok(d), _const_spec((1, d)), vec, vec, _const_spec(w.shape),
                _const_spec(qn.shape), _const_spec(kvn.shape), _const_spec(wuq.shape),
                _const_spec(wuk.shape), _const_spec(wuv.shape), _const_spec(place.shape)]
    args = [x, gain, scale, shift, w, qn, kvn, wuq, wuk, wuv, place]
    if rotate:
        in_specs += [pl.BlockSpec((tm, LANES), lambda bi, i: (i, 0))] * len(tables)
        args += list(tables)
    widths = [mla_w, mla_w, mla_w, GROUP_W, GROUP_W, GROUP_W, 2 * GROUP_W,
              GROUP_W, GROUP_W, GROUP_W, GROUP_W, kv_w, kv_w]
    dtypes = [BF16] * 6 + [F32] + [BF16] * 6
    return pl.pallas_call(
        functools.partial(_inproj_kernel, rotate=rotate, mla_scale=(MLA_NOPE + MLA_ROPE) ** -0.5 * LOG2_E),
        grid=(b, t // tm),
        in_specs=in_specs,
        out_specs=[tok(wd) for wd in widths],
        out_shape=[jax.ShapeDtypeStruct((b, t, wd), dt) for wd, dt in zip(widths, dtypes)],
        compiler_params=_cparams(("parallel", "parallel")),
        name="in_proj_rot" if rotate else "in_proj_ctx",
    )(*args)


def _mla_kernel(q_ref, kx_ref, vx_ref, ky_ref, vy_ref, o_ref, m_ref, acc_ref, sa_ref, sb_ref, sc_ref,
                ma_ref, mb_ref, mc_ref, *, tk, heads, n_sub, unroll):
    n_chunks = kx_ref.shape[1] // tk
    tq = q_ref.shape[1]
    ts = tq // n_sub
    chains = [(h, u) for h in range(heads) for u in range(n_sub)]
    for c in range(len(chains)):
        m_ref[c] = jnp.full(m_ref.shape[1:], NEG_INF, F32)
        acc_ref[c] = jnp.zeros(acc_ref.shape[1:], F32)

    def rows(h):
        return slice(h * MLA_HEAD_PAD, (h + 1) * MLA_HEAD_PAD)

    def vrows(h):
        return slice(h * MLA_HEAD_PAD, h * MLA_HEAD_PAD + MLA_V_ROWS)

    buf_a, buf_b, buf_c = (sa_ref, ma_ref), (sb_ref, mb_ref), (sc_ref, mc_ref)

    def scores(c, k, dst):
        h, u = chains[c]
        st = _dot_nt(k, q_ref[0, u * ts:(u + 1) * ts, rows(h)])
        dst[0][c] = st
        dst[1][c] = jnp.max(st, axis=0, keepdims=True)

    def absorb(c, src, vt):
        m_old = m_ref[c]
        m_new = jnp.maximum(m_old, src[1][c])
        pt = jnp.exp2(src[0][c] - m_new).astype(BF16)
        acc_ref[c, :MLA_V_ROWS] = acc_ref[c, :MLA_V_ROWS] * jnp.exp2(m_old - m_new) + _dot(vt, pt)
        m_ref[c] = m_new

    def kx(j, h):
        return kx_ref[0, pl.ds(pl.multiple_of(j * tk, tk), tk), rows(h)]

    def stage(j_next, nxt, j_cur, cur):
        for c, (h, _) in enumerate(chains):
            scores(c, kx(j_next, h), nxt)
            absorb(c, cur, vx_ref[0, j_cur, vrows(h), :])

    for c, (h, _) in enumerate(chains):
        scores(c, kx(0, h), buf_a)

    def stage_at(j, parity):
        cur, nxt = (buf_a, buf_b) if parity == 0 else (buf_b, buf_a)
        stage(j + 1, nxt, j, cur)

    def body(jj, carry):
        for i in range(unroll):
            stage_at(unroll * jj + i, i % 2)
        return carry

    trips = (n_chunks - 1) // unroll
    lax.fori_loop(0, trips, body, 0)
    for j in range(unroll * trips, n_chunks - 1):
        stage_at(j, j % 2)
    last = buf_a if (n_chunks - 1) % 2 == 0 else buf_b
    for c, (h, _) in enumerate(chains):
        scores(c, ky_ref[0, :, rows(h)], buf_c)
        absorb(c, last, vx_ref[0, n_chunks - 1, vrows(h), :])
    outs = [[None] * n_sub for _ in range(heads)]
    for c, (h, u) in enumerate(chains):
        absorb(c, buf_c, vy_ref[0, 0, vrows(h), :])
        acc = acc_ref[c].T
        outs[h][u] = acc[:, :MLA_V] / acc[:, MLA_V:MLA_V + 1]
    o_ref[0] = jnp.concatenate([jnp.concatenate(outs[h], axis=0) for h in range(heads)],
                               axis=-1).astype(o_ref.dtype)


def _mla_attention(q, kx, vx, ky, vy, tq, tk, n_sub, unroll):
    b, s, _ = q.shape
    l = ky.shape[1]
    hp = 2
    wd = hp * MLA_HEAD_PAD
    vxt = jnp.swapaxes(vx.reshape(b, s // tk, tk, -1), 2, 3)
    vyt = jnp.swapaxes(vy.reshape(b, 1, l, -1), 2, 3)
    ts = tq // n_sub
    return pl.pallas_call(
        functools.partial(_mla_kernel, tk=tk, heads=hp, n_sub=n_sub, unroll=unroll),
        grid=(b, N_HEADS // hp, s // tq),
        in_specs=[pl.BlockSpec((1, tq, wd), lambda bi, hi, i: (bi, i, hi)),
                  pl.BlockSpec((1, s, wd), lambda bi, hi, i: (bi, 0, hi)),
                  pl.BlockSpec((1, s // tk, wd, tk), lambda bi, hi, i: (bi, 0, hi, 0)),
                  pl.BlockSpec((1, l, wd), lambda bi, hi, i: (bi, 0, hi)),
                  pl.BlockSpec((1, 1, wd, l), lambda bi, hi, i: (bi, 0, hi, 0))],
        out_specs=pl.BlockSpec((1, tq, hp * MLA_V), lambda bi, hi, i: (bi, i, hi)),
        out_shape=jax.ShapeDtypeStruct((b, s, N_HEADS * MLA_V), BF16),
        scratch_shapes=[pltpu.VMEM((hp * n_sub, 1, ts), F32),
                        pltpu.VMEM((hp * n_sub, MLA_HEAD_PAD, ts), F32),
                        pltpu.VMEM((hp * n_sub, tk, ts), F32),
                        pltpu.VMEM((hp * n_sub, tk, ts), F32),
                        pltpu.VMEM((hp * n_sub, l, ts), F32)]
                       + [pltpu.VMEM((hp * n_sub, 1, ts), F32)] * 3,
        compiler_params=_cparams(("parallel", "parallel", "arbitrary")),
        name="mla_attention",
    )(q, kx, vxt, ky, vyt)


def _ctx_attn_kernel(sink_ref, q_ref, k_ref, v_ref, o_ref, *, heads, base2):
    exp = jnp.exp2 if base2 else jnp.exp
    outs = []
    for (q0, q1, k0, k1, v0, v1, sink_idx) in heads:
        q = q_ref[0, :, q0:q1]
        s = _dot_nt(q, k_ref[0, :, k0:k1])
        m = jnp.max(s, axis=-1, keepdims=True)
        if sink_idx is not None:
            sink = jnp.full((1, 1), sink_ref[sink_idx], F32)
            m = jnp.maximum(m, sink)
        p = exp(s - m)
        l = jnp.sum(p, axis=-1, keepdims=True)
        if sink_idx is not None:
            l = l + exp(sink - m)
        outs.append(_dot(p.astype(BF16), v_ref[0, :, v0:v1]) / l)
    o_ref[0] = jnp.concatenate(outs, axis=-1).astype(o_ref.dtype)


def _ctx_attention(q, k, v, heads, sink, name, base2=False):
    b, l, _ = q.shape
    full = lambda a: pl.BlockSpec((1, l, a.shape[-1]), lambda bi: (bi, 0, 0))
    return pl.pallas_call(
        functools.partial(_ctx_attn_kernel, heads=heads, base2=base2),
        grid=(b,),
        in_specs=[pl.BlockSpec(memory_space=pltpu.SMEM), full(q), full(k), full(v)],
        out_specs=pl.BlockSpec((1, l, GROUP_W), lambda bi: (bi, 0, 0)),
        out_shape=jax.ShapeDtypeStruct((b, l, GROUP_W), BF16),
        compiler_params=_cparams(("parallel",)),
        name=name,
    )(sink, q, k, v)


def _ret_kernel(*refs, chunk, n_chunks, direction, has_prev):
    if has_prev:
        (dec_ref, q_ref, k_ref, kt_ref, v_ref, g_ref, s0_ref, prev_ref, o_ref, sn_ref,
         st_ref, dm_ref, qd_ref, kdt_ref, cd_ref, ob_ref, qk_ref) = refs
    else:
        (dec_ref, q_ref, k_ref, kt_ref, v_ref, g_ref, s0_ref, o_ref, sn_ref,
         st_ref, dm_ref, qd_ref, kdt_ref, cd_ref, ob_ref, qk_ref) = refs
        prev_ref = None
    i = pl.program_id(0)
    c = chunk
    fwd = direction == 0
    batch = range(q_ref.shape[0])

    def head_of(shape, axis):
        return lax.broadcasted_iota(jnp.int32, shape, axis) // HEAD_DIM

    @pl.when(i == 0)
    def _init():
        st_ref[...] = s0_ref[...]
        ii = lax.broadcasted_iota(jnp.int32, (c, c), 0).astype(F32)
        jj = lax.broadcasted_iota(jnp.int32, (c, c), 1).astype(F32)
        diff = (ii - jj) if fwd else (jj - ii)
        pos = lax.broadcasted_iota(jnp.int32, (c, GROUP_W), 0).astype(F32)
        pos_t = lax.broadcasted_iota(jnp.int32, (GROUP_W, c), 1).astype(F32)
        q_steps = (pos + 1.0) if fwd else (c - pos)
        k_steps = (c - 1.0 - pos_t) if fwd else pos_t
        qd = jnp.zeros((c, GROUP_W), F32)
        kdt = jnp.zeros((GROUP_W, c), F32)
        cd = jnp.zeros((GROUP_W, GROUP_W), F32)
        for h in range(N_HEADS):
            dec = dec_ref[direction * N_HEADS + h]
            lg = jax.nn.log_sigmoid(jnp.full((c, c), dec, F32))
            dm_ref[h * c:(h + 1) * c] = jnp.where(diff >= 0, jnp.exp(lg * jnp.maximum(diff, 0.0)), 0.0)
            lgq = jax.nn.log_sigmoid(jnp.full((c, GROUP_W), dec, F32))
            qd = jnp.where(head_of((c, GROUP_W), 1) == h, jnp.exp(lgq * q_steps), qd)
            lgk = jax.nn.log_sigmoid(jnp.full((GROUP_W, c), dec, F32))
            kdt = jnp.where(head_of((GROUP_W, c), 0) == h, jnp.exp(lgk * k_steps), kdt)
            lgc = jax.nn.log_sigmoid(jnp.full((GROUP_W, GROUP_W), dec, F32))
            cd = jnp.where(head_of((GROUP_W, GROUP_W), 0) == h, jnp.exp(lgc * c), cd)
        qd_ref[...] = qd
        kdt_ref[...] = kdt
        cd_ref[...] = cd

    in_head = [head_of((c, GROUP_W), 1) == h for h in range(N_HEADS)]
    head_mask = [jnp.where(mk, 1.0, 0.0).astype(BF16) for mk in in_head]
    same_head = head_of((GROUP_W, GROUP_W), 0) == head_of((GROUP_W, GROUP_W), 1)

    def chunk_off(n):
        idx = n if fwd else n_chunks - 1 - n
        return pl.multiple_of(idx * c, c)

    def qk(bb, n):
        off = chunk_off(n)
        q_all = q_ref[bb, pl.ds(off, c), :]
        q4 = jnp.concatenate([q_all * head_mask[h] for h in range(N_HEADS)], axis=0)
        return _dot_nt(q4, k_ref[bb, pl.ds(off, c), :])

    def gate_and_store(bb, n):
        off = chunk_off(n)
        o = ob_ref[bb]
        oo = o * o
        ms = jnp.zeros((c, GROUP_W), F32)
        for h in range(N_HEADS):
            ms_h = jnp.sum(jnp.where(in_head[h], oo, 0.0), axis=-1, keepdims=True) * (1.0 / HEAD_DIM)
            ms = jnp.where(in_head[h], ms_h, ms)
        res = o * lax.rsqrt(ms + EPS) * _silu(g_ref[bb, pl.ds(off, c), :])
        if has_prev:
            res = res + prev_ref[bb, pl.ds(off, c), :]
        o_ref[bb, pl.ds(off, c), :] = res.astype(o_ref.dtype)

    ob_ref[...] = jnp.zeros(ob_ref.shape, F32)

    for bb in batch:
        qk_ref[bb] = qk(bb, 0)

    def body(n, carry):
        for bb in batch:
            gate_and_store(bb, jnp.maximum(n - 1, 0))
        off = chunk_off(n)
        atts = [(qk_ref[bb] * dm_ref[...]).astype(BF16) for bb in batch]
        for bb in batch:
            qk_ref[bb] = qk(bb, jnp.minimum(n + 1, n_chunks - 1))
        for bb in batch:
            v = v_ref[bb, pl.ds(off, c), :]
            att = atts[bb]
            intra4 = _dot(att, v)
            intra = intra4[:c]
            for h in range(1, N_HEADS):
                intra = jnp.where(in_head[h], intra4[h * c:(h + 1) * c], intra)
            state = st_ref[bb]
            ob_ref[bb] = intra + _dot(q_ref[bb, pl.ds(off, c), :], state.astype(BF16)) * qd_ref[...]
            kk = (kt_ref[bb, :, pl.ds(off, c)].astype(F32) * kdt_ref[...]).astype(BF16)
            st_ref[bb] = state * cd_ref[...] + jnp.where(same_head, _dot(kk, v), 0.0)
        return carry

    lax.fori_loop(0, n_chunks, body, 0)
    for bb in batch:
        gate_and_store(bb, n_chunks - 1)

    @pl.when(i == pl.num_programs(0) - 1)
    def _fin():
        sn_ref[...] = st_ref[...]


def _retention_pass(dec, q, k, v, gates, state0, prev, direction, chunk, n_chunks, out_dtype):
    b, t, _ = q.shape
    tb = chunk * n_chunks
    n = t // tb
    blk = (lambda i: i) if direction == 0 else (lambda i: n - 1 - i)
    tok = pl.BlockSpec((b, tb, GROUP_W), lambda i: (0, blk(i), 0))
    tok_t = pl.BlockSpec((b, GROUP_W, tb), lambda i: (0, 0, blk(i)))
    gate = pl.BlockSpec((b, tb, GROUP_W), lambda i: (0, blk(i), direction))
    st_spec = pl.BlockSpec((b, GROUP_W, GROUP_W), lambda i: (0, 0, 0))
    in_specs = [pl.BlockSpec(memory_space=pltpu.SMEM), tok, tok, tok_t, tok, gate, st_spec]
    args = [dec, q, k, jnp.swapaxes(k, 1, 2), v, gates, state0]
    if prev is not None:
        in_specs.append(tok)
        args.append(prev)
    return pl.pallas_call(
        functools.partial(_ret_kernel, chunk=chunk, n_chunks=n_chunks, direction=direction,
                          has_prev=prev is not None),
        grid=(n,),
        in_specs=in_specs,
        out_specs=[tok, st_spec],
        out_shape=[jax.ShapeDtypeStruct((b, t, GROUP_W), out_dtype),
                   jax.ShapeDtypeStruct((b, GROUP_W, GROUP_W), F32)],
        scratch_shapes=[pltpu.VMEM((b, GROUP_W, GROUP_W), F32),
                        pltpu.VMEM((N_HEADS * chunk, chunk), F32),
                        pltpu.VMEM((chunk, GROUP_W), F32),
                        pltpu.VMEM((GROUP_W, chunk), F32),
                        pltpu.VMEM((GROUP_W, GROUP_W), F32),
                        pltpu.VMEM((b, chunk, GROUP_W), F32),
                        pltpu.VMEM((b, N_HEADS * chunk, chunk), F32)],
        compiler_params=_cparams(("arbitrary",)),
        name="retention_fwd" if direction == 0 else "retention_bwd",
    )(*args)


def _retention(dec, xq, xk, xv, xg, yq, yk, yv, yg, chunk, chunks_per_step):
    b = xq.shape[0]
    zero = jnp.zeros((b, GROUP_W, GROUP_W), F32)
    ny = yq.shape[1] // chunk
    yb, sb = _retention_pass(dec, yq, yk, yv, yg, zero, None, 1, chunk, ny, F32)
    y, sf = _retention_pass(dec, yq, yk, yv, yg, zero, yb, 0, chunk, ny, BF16)
    xb, _ = _retention_pass(dec, xq, xk, xv, xg, sb, None, 1, chunk, chunks_per_step, F32)
    x, _ = _retention_pass(dec, xq, xk, xv, xg, sf, xb, 0, chunk, chunks_per_step, BF16)
    return x, y


def _na_bias_kernel(rpb_ref, o_ref):
    h = pl.program_id(0)
    dr0 = pl.program_id(1)
    c = lax.broadcasted_iota(jnp.int32, (GRID_W, GRID_W), 0)
    kc = lax.broadcasted_iota(jnp.int32, (GRID_W, GRID_W), 1)
    c0 = jnp.clip(c - NA_KC // 2, 0, GRID_W - NA_KC)
    col_in = (kc >= c0) & (kc < c0 + NA_KC)
    dc = jnp.clip(kc - c, -(NA_KC - 1), NA_KC - 1) + NA_KC - 1
    n_dc = 2 * NA_KC - 1
    for j in range(NA_KR):
        base = (h * (2 * NA_KR - 1) + dr0 + j) * n_dc
        acc = jnp.zeros((GRID_W, GRID_W), F32)
        for d in range(n_dc):
            acc = jnp.where(dc == d, rpb_ref[base + d], acc)
        o_ref[0, :, j * GRID_W:(j + 1) * GRID_W] = jnp.where(col_in, acc, NEG_INF)


def _na_bias_table(rpb):
    return pl.pallas_call(
        _na_bias_kernel,
        grid=(N_HEADS, NA_KR),
        in_specs=[pl.BlockSpec(memory_space=pltpu.SMEM)],
        out_specs=pl.BlockSpec((1, GRID_W, NA_KR * GRID_W), lambda h, r: (r, h, 0)),
        out_shape=jax.ShapeDtypeStruct((NA_KR, N_HEADS * GRID_W, NA_KR * GRID_W), F32),
        compiler_params=_cparams(("parallel", "parallel")),
        name="na_bias_table",
    )(rpb.reshape(-1))


def _na_kernel(q_ref, k_ref, v_ref, ky_ref, vy_ref, tb_ref, o_ref,
               sa_ref, sb_ref, pa_ref, pb_ref, la_ref, lb_ref, *, rows_per_step, n_rows):
    r_base = pl.program_id(1) * rows_per_step
    win = NA_KR * GRID_W

    head_of_lane = lax.broadcasted_iota(jnp.int32, (GRID_W, GROUP_W), 1) // HEAD_DIM
    in_head = [head_of_lane == h for h in range(N_HEADS)]
    head_mask = [jnp.where(mk, 1.0, 0.0).astype(BF16) for mk in in_head]

    def geometry(i):
        r = r_base + i
        r0 = jnp.clip(r - NA_KR // 2, 0, n_rows - NA_KR)
        return pl.multiple_of(r0 * GRID_W, GRID_W), r0 - r + NA_KR - 1

    def row_slice(i):
        return pl.ds(pl.multiple_of(i * GRID_W, GRID_W), GRID_W)

    def scores(i, s_ref):
        koff, dr0 = geometry(i)
        q_all = q_ref[0, row_slice(i), :]
        q4 = jnp.concatenate([q_all * head_mask[h] for h in range(N_HEADS)], axis=0)
        s_ref[:, :win] = _dot_nt(q4, k_ref[0, pl.ds(koff, win), :]) + tb_ref[dr0]
        s_ref[:, win:] = _dot_nt(q4, ky_ref[0])

    def softmax(s_ref, p_ref, l_ref):
        s = s_ref[...]
        p = jnp.exp(s - _row_max(s))
        l_ref[...] = 1.0 / _row_sum(p)
        p_ref[...] = p.astype(BF16)

    def values(i, p_ref, l_ref):
        koff, _ = geometry(i)
        o4 = (_dot(p_ref[:, :win], v_ref[0, pl.ds(koff, win), :])
              + _dot(p_ref[:, win:], vy_ref[0])) * l_ref[...]
        out = o4[:GRID_W]
        for h in range(1, N_HEADS):
            out = jnp.where(in_head[h], o4[h * GRID_W:(h + 1) * GRID_W], out)
        o_ref[0, row_slice(i), :] = out.astype(o_ref.dtype)

    pb_ref[...] = jnp.zeros(pb_ref.shape, BF16)
    lb_ref[...] = jnp.zeros(lb_ref.shape, F32)
    scores(0, sa_ref)

    def pair_body(tt, carry):
        t = 2 * tt
        values(jnp.maximum(t - 1, 0), pb_ref, lb_ref)
        scores(t + 1, sb_ref)
        softmax(sa_ref, pa_ref, la_ref)
        values(t, pa_ref, la_ref)
        scores(jnp.minimum(t + 2, rows_per_step - 1), sa_ref)
        softmax(sb_ref, pb_ref, lb_ref)
        return carry

    lax.fori_loop(0, rows_per_step // 2, pair_body, 0)
    values(rows_per_step - 1, pb_ref, lb_ref)


def _na_attention(q, k, v, ky, vy, table, rows_per_step):
    b, s, _ = q.shape
    l = ky.shape[1]
    n_rows = s // GRID_W
    tq = rows_per_step * GRID_W
    n_keys = NA_KR * GRID_W + l
    seq = lambda n: pl.BlockSpec((1, n, GROUP_W), lambda bi, i: (bi, 0, 0))
    return pl.pallas_call(
        functools.partial(_na_kernel, rows_per_step=rows_per_step, n_rows=n_rows),
        grid=(b, n_rows // rows_per_step),
        in_specs=[pl.BlockSpec((1, tq, GROUP_W), lambda bi, i: (bi, i, 0)),
                  seq(s), seq(s), seq(l), seq(l), _const_spec(table.shape)],
        out_specs=pl.BlockSpec((1, tq, GROUP_W), lambda bi, i: (bi, i, 0)),
        out_shape=jax.ShapeDtypeStruct((b, s, GROUP_W), BF16),
        scratch_shapes=([pltpu.VMEM((N_HEADS * GRID_W, n_keys), F32)] * 2
                        + [pltpu.VMEM((N_HEADS * GRID_W, n_keys), BF16)] * 2
                        + [pltpu.VMEM((N_HEADS * GRID_W, 1), F32)] * 2),
        compiler_params=_cparams(("parallel", "arbitrary")),
        name="na_attention",
    )(q, k, v, ky, vy, table)


def _swa_kernel(sink_ref, q_ref, kp_ref, kc_ref, kn_ref, vp_ref, vc_ref, vn_ref, ky_ref, vy_ref, o_ref,
                kw_ref, vw_ref, wb_ref, sa_ref, sb_ref, pa_ref, pb_ref, la_ref, lb_ref, *, blocks_per_step):
    step = pl.program_id(1)
    nb = pl.num_programs(1) * blocks_per_step
    bl = SWA_BLOCK
    g = N_HEADS // SWA_KV_HEADS
    tq = blocks_per_step * bl
    kw_ref[0:bl] = kp_ref[0]
    kw_ref[bl:bl + tq] = kc_ref[0]
    kw_ref[bl + tq:] = kn_ref[0]
    vw_ref[0:bl] = vp_ref[0]
    vw_ref[bl:bl + tq] = vc_ref[0]
    vw_ref[bl + tq:] = vn_ref[0]

    qi = lax.broadcasted_iota(jnp.int32, (g * bl, 3 * bl), 0) % bl
    jk = lax.broadcasted_iota(jnp.int32, (g * bl, 3 * bl), 1)
    wb_ref[...] = jnp.where(jnp.abs(jk - bl - qi) <= SWA_WINDOW, 0.0, NEG_INF)
    half = lax.broadcasted_iota(jnp.int32, (g * bl, 1), 0) // bl

    head_of_lane = lax.broadcasted_iota(jnp.int32, (bl, GROUP_W), 1) // HEAD_DIM
    in_head = [head_of_lane == h for h in range(N_HEADS)]
    head_mask = [jnp.where(mk, 1.0, 0.0).astype(BF16) for mk in in_head]

    def blk_slice(j):
        return pl.ds(pl.multiple_of(j * bl, bl), bl)

    def scores(j, s_ref):
        qoff = pl.multiple_of(j * bl, bl)
        q_all = q_ref[0, blk_slice(j), :]
        kw = kw_ref[pl.ds(qoff, 3 * bl), :]
        n = step * blocks_per_step + j
        lo_edge = jnp.where(n == 0, NEG_INF, 0.0)
        hi_edge = jnp.where(n == nb - 1, NEG_INF, 0.0)
        for kh in range(SWA_KV_HEADS):
            q = jnp.concatenate([q_all * head_mask[kh * g + gi] for gi in range(g)], axis=0)
            s = _dot_nt(q, kw) + wb_ref[...]
            s_ref[kh, :, :bl] = s[:, :bl] + lo_edge
            s_ref[kh, :, bl:2 * bl] = s[:, bl:2 * bl]
            s_ref[kh, :, 2 * bl:3 * bl] = s[:, 2 * bl:] + hi_edge
            s_ref[kh, :, 3 * bl:] = _dot_nt(q, ky_ref[0])

    def softmax(s_ref, p_ref, l_ref):
        for kh in range(SWA_KV_HEADS):
            s = s_ref[kh]
            sink = jnp.full((g * bl, 1), sink_ref[kh * g], F32)
            for gi in range(1, g):
                sink = jnp.where(half == gi, sink_ref[kh * g + gi], sink)
            m = jnp.maximum(_row_max(s), sink)
            p = jnp.exp(s - m)
            l_ref[kh] = 1.0 / (_row_sum(p) + jnp.exp(sink - m))
            p_ref[kh] = p.astype(BF16)

    def values(j, p_ref, l_ref):
        vw = vw_ref[pl.ds(pl.multiple_of(j * bl, bl), 3 * bl), :]
        out = jnp.zeros((bl, GROUP_W), F32)
        for kh in range(SWA_KV_HEADS):
            o = (_dot(p_ref[kh, :, :3 * bl], vw) + _dot(p_ref[kh, :, 3 * bl:], vy_ref[0])) * l_ref[kh]
            for gi in range(g):
                out = jnp.where(in_head[kh * g + gi], o[gi * bl:(gi + 1) * bl], out)
        o_ref[0, blk_slice(j), :] = out.astype(o_ref.dtype)

    pb_ref[...] = jnp.zeros(pb_ref.shape, BF16)
    lb_ref[...] = jnp.zeros(lb_ref.shape, F32)
    scores(0, sa_ref)

    def pair_body(tt, carry):
        t = 2 * tt
        values(jnp.maximum(t - 1, 0), pb_ref, lb_ref)
        scores(t + 1, sb_ref)
        softmax(sa_ref, pa_ref, la_ref)
        values(t, pa_ref, la_ref)
        scores(jnp.minimum(t + 2, blocks_per_step - 1), sa_ref)
        softmax(sb_ref, pb_ref, lb_ref)
        return carry

    lax.fori_loop(0, blocks_per_step // 2, pair_body, 0)
    values(blocks_per_step - 1, pb_ref, lb_ref)


def _swa_attention(sink, q, k, v, ky, vy, blocks_per_step):
    b, s, _ = q.shape
    l = ky.shape[1]
    tq = blocks_per_step * SWA_BLOCK
    n_steps = s // tq
    nb = s // SWA_BLOCK
    kvw = GROUP_W
    q_rows = N_HEADS // SWA_KV_HEADS * SWA_BLOCK
    n_keys = 3 * SWA_BLOCK + l
    prev = pl.BlockSpec((1, SWA_BLOCK, kvw), lambda bi, i: (bi, jnp.maximum(i * blocks_per_step - 1, 0), 0))
    cur = pl.BlockSpec((1, tq, kvw), lambda bi, i: (bi, i, 0))
    nxt = pl.BlockSpec((1, SWA_BLOCK, kvw), lambda bi, i: (bi, jnp.minimum((i + 1) * blocks_per_step, nb - 1), 0))
    ctx = pl.BlockSpec((1, l, kvw), lambda bi, i: (bi, 0, 0))
    return pl.pallas_call(
        functools.partial(_swa_kernel, blocks_per_step=blocks_per_step),
        grid=(b, n_steps),
        in_specs=[pl.BlockSpec(memory_space=pltpu.SMEM),
                  pl.BlockSpec((1, tq, GROUP_W), lambda bi, i: (bi, i, 0)),
                  prev, cur, nxt, prev, cur, nxt, ctx, ctx],
        out_specs=pl.BlockSpec((1, tq, GROUP_W), lambda bi, i: (bi, i, 0)),
        out_shape=jax.ShapeDtypeStruct((b, s, GROUP_W), BF16),
        scratch_shapes=[pltpu.VMEM((tq + 2 * SWA_BLOCK, kvw), BF16),
                        pltpu.VMEM((tq + 2 * SWA_BLOCK, kvw), BF16),
                        pltpu.VMEM((q_rows, 3 * SWA_BLOCK), F32)]
                       + [pltpu.VMEM((SWA_KV_HEADS, q_rows, n_keys), F32)] * 2
                       + [pltpu.VMEM((SWA_KV_HEADS, q_rows, n_keys), BF16)] * 2
                       + [pltpu.VMEM((SWA_KV_HEADS, q_rows, 1), F32)] * 2,
        compiler_params=_cparams(("parallel", "arbitrary")),
        name="swa_attention",
    )(sink, q, k, k, k, v, v, v, ky, vy)


def _outffn_kernel(*refs, hidden_chunk, final):
    if final:
        (x_ref, m0_ref, m1_ref, m2_ref, m3_ref, wo_ref, g1_ref, n2_ref, sc_ref, sh_ref, g2_ref,
         w1_ref, w3_ref, w2_ref, fg_ref, o_ref) = refs
    else:
        (x_ref, m0_ref, m1_ref, m2_ref, m3_ref, wo_ref, g1_ref, n2_ref, sc_ref, sh_ref, g2_ref,
         w1_ref, w3_ref, w2_ref, o_ref) = refs
    mix = None
    for gi, m_ref in enumerate((m0_ref, m1_ref, m2_ref, m3_ref)):
        part = _dot(m_ref[0], wo_ref[gi * GROUP_W:(gi + 1) * GROUP_W, :])
        mix = part if mix is None else mix + part
    x1 = x_ref[0] + g1_ref[0] * mix
    hb = ((_rms(x1) * n2_ref[...]) * (1.0 + sc_ref[0]) + sh_ref[0]).astype(BF16)
    hidden = w1_ref.shape[1]
    acc = None
    for c0 in range(0, hidden, hidden_chunk):
        a = _dot(hb, w1_ref[:, c0:c0 + hidden_chunk])
        bgate = _dot(hb, w3_ref[:, c0:c0 + hidden_chunk])
        u = (_silu(a) * bgate).astype(BF16)
        part = _dot(u, w2_ref[c0:c0 + hidden_chunk, :])
        acc = part if acc is None else acc + part
    x2 = x1 + g2_ref[0] * acc
    if final:
        x2 = _rms(x2) * fg_ref[...]
    o_ref[0] = x2


def _outffn(x, mixes, wo, g1, n2, sc2, sh2, g2, w1, w3, w2, final_g, tm):
    b, t, d = x.shape
    tok = lambda wd: pl.BlockSpec((1, tm, wd), lambda bi, i: (bi, i, 0))
    vec = pl.BlockSpec((1, 1, d), lambda bi, i: (bi, 0, 0))
    in_specs = ([tok(d)] + [tok(GROUP_W)] * 4
                + [_const_spec(wo.shape), vec, _const_spec((1, d)), vec, vec, vec,
                   _const_spec(w1.shape), _const_spec(w3.shape), _const_spec(w2.shape)])
    args = [x, *mixes, wo, g1, n2, sc2, sh2, g2, w1, w3, w2]
    final = final_g is not None
    if final:
        in_specs.append(_const_spec((1, d)))
        args.append(final_g)
    return pl.pallas_call(
        functools.partial(_outffn_kernel, hidden_chunk=256, final=final),
        grid=(b, t // tm),
        in_specs=in_specs,
        out_specs=tok(d),
        out_shape=jax.ShapeDtypeStruct((b, t, d), F32),
        compiler_params=_cparams(("parallel", "parallel")),
        name="out_proj_ffn_final" if final else "out_proj_ffn",
    )(*args)


def _prep_weights(w_in, mla_w_uq, mla_w_ukv):
    depth, d, _ = w_in.shape
    offs = [0]
    for sz in IN_SIZES:
        offs.append(offs[-1] + sz)
    w_bf = w_in.astype(BF16)
    cols = [w_bf[:, :, offs[i]:offs[i + 1]] for i in range(len(IN_SIZES))]
    cq, ckv, kr, rq, rk, rv, rgf, rgb, nq, nk, nv, sq, sk, sv = cols
    scale = HEAD_DIM ** -0.5
    kr_slot = jnp.concatenate([kr, jnp.zeros((depth, d, LANES - MLA_ROPE), BF16)], axis=-1)
    src = jnp.arange(LANES)[:, None]
    dst = jnp.arange(N_HEADS * MLA_HEAD_PAD)[None, :]
    place = ((src < MLA_ROPE) & (dst % MLA_HEAD_PAD == src + MLA_NOPE)).astype(BF16)

    def per_query_head(t):
        g = N_HEADS // SWA_KV_HEADS
        t = t.reshape(depth, d, SWA_KV_HEADS, 1, HEAD_DIM)
        return jnp.broadcast_to(t, (depth, d, SWA_KV_HEADS, g, HEAD_DIM)).reshape(depth, d, GROUP_W)

    w = jnp.concatenate([cq, ckv, kr_slot, rq, rk * scale, rv, rgf, rgb, nq * scale, nk, nv,
                         sq * scale, per_query_head(sk), per_query_head(sv)], axis=-1).astype(BF16)

    qr = mla_w_uq.shape[1]
    uq = mla_w_uq.reshape(depth, qr, N_HEADS, MLA_NOPE + MLA_ROPE)
    wuq = jnp.concatenate([uq, jnp.zeros((depth, qr, N_HEADS, MLA_HEAD_PAD - MLA_NOPE - MLA_ROPE), F32)],
                          axis=-1).reshape(depth, qr, N_HEADS * MLA_HEAD_PAD).astype(BF16)
    kvr = mla_w_ukv.shape[1]
    ukv = mla_w_ukv.reshape(depth, kvr, N_HEADS, MLA_NOPE + MLA_V)
    zk = jnp.zeros((depth, kvr, N_HEADS, MLA_HEAD_PAD - MLA_NOPE), F32)
    wuk = jnp.concatenate([ukv[..., :MLA_NOPE], zk], axis=-1).reshape(depth, kvr, -1).astype(BF16)
    zv = jnp.zeros((depth, kvr, N_HEADS, MLA_HEAD_PAD - MLA_V), F32)
    wuv = jnp.concatenate([ukv[..., MLA_NOPE:], zv], axis=-1).reshape(depth, kvr, -1).astype(BF16)
    return w, wuq, wuk, wuv, place


def _ctx_head_specs():
    mla = tuple((h * MLA_HEAD_PAD, (h + 1) * MLA_HEAD_PAD, h * MLA_HEAD_PAD, (h + 1) * MLA_HEAD_PAD,
                 h * MLA_HEAD_PAD, h * MLA_HEAD_PAD + MLA_V, None) for h in range(N_HEADS))
    na = tuple((h * HEAD_DIM, (h + 1) * HEAD_DIM) * 3 + (None,) for h in range(N_HEADS))
    swa = tuple((h * HEAD_DIM, (h + 1) * HEAD_DIM) * 3 + (h,) for h in range(N_HEADS))
    return mla, na, swa


def kernel(x, c, ctx, c_ctx, ada_w, ada_b, norm1_g, w_in, mla_q_norm, mla_w_uq, mla_kv_norm, mla_w_ukv,
           ret_decay, na_rpb, swa_sink, w_out, norm2_g, ffn_w1, ffn_w3, ffn_w2, final_norm_g):
    b, s, d = x.shape
    l_ctx = ctx.shape[1]
    depth = ada_w.shape[0]
    assert b + 1 <= 8 and s % 1024 == 0 and l_ctx % 128 == 0

    cond = jnp.concatenate([c, c_ctx[None, :], jnp.zeros((8 - b - 1, d), F32)], axis=0)
    mod = _modulation(cond, ada_w, ada_b)
    tables = _rope_tables(s)
    w_all, wuq_all, wuk_all, wuv_all, place = _prep_weights(w_in, mla_w_uq, mla_w_ukv)
    wo_all = w_out.astype(BF16)
    w1_all, w3_all, w2_all = ffn_w1.astype(BF16), ffn_w3.astype(BF16), ffn_w2.astype(BF16)
    mla_heads, na_heads, swa_heads = _ctx_head_specs()
    no_sink = jnp.zeros((N_HEADS,), F32)

    tm_x = 512
    tm_y = min(256, l_ctx)
    y = ctx
    for l in range(depth):
        mx = [mod[l, :b, j * d:(j + 1) * d][:, None, :] for j in range(6)]
        my = [jnp.broadcast_to(mod[l, b, j * d:(j + 1) * d][None, None, :], (b, 1, d)) for j in range(6)]
        n1 = norm1_g[l][None, :]
        n2 = norm2_g[l][None, :]
        qn = mla_q_norm[l][None, :]
        kvn = mla_kv_norm[l][None, :]
        lw = (w_all[l], qn, kvn, wuq_all[l], wuk_all[l], wuv_all[l], place)

        px = _inproj(x, n1, mx[1], mx[0], *lw, tables, tm_x)
        py = _inproj(y, n1, my[1], my[0], *lw, None, tm_y)
        (xmq, xmk, xmv, xrq, xrk, xrv, xrg, xnq, xnk, xnv, xsq, xsk, xsv) = px
        (ymq, ymk, ymv, yrq, yrk, yrv, yrg, ynq, ynk, ynv, ysq, ysk, ysv) = py

        mla_x = _mla_attention(xmq, xmk, xmv, ymk, ymv, tq=512, tk=512, n_sub=2, unroll=8)
        dec = ret_decay[l].reshape(-1)
        ret_x, ret_y = _retention(dec, xrq, xrk, xrv, xrg, yrq, yrk, yrv, yrg, chunk=128, chunks_per_step=8)
        table = _na_bias_table(na_rpb[l])
        na_x = _na_attention(xnq, xnk, xnv, ynk, ynv, table, rows_per_step=16)
        swa_x = _swa_attention(swa_sink[l], xsq, xsk, xsv, ysk, ysv, blocks_per_step=8)

        last = l == depth - 1
        x = _outffn(x, (mla_x, ret_x, na_x, swa_x), wo_all[l], mx[2], n2, mx[4], mx[3], mx[5],
                    w1_all[l], w3_all[l], w2_all[l], final_norm_g[None, :] if last else None, tm_x)
        if not last:
            mla_y = _ctx_attention(ymq, ymk, ymv, mla_heads, no_sink, "mla_ctx_attention", base2=True)
            na_y = _ctx_attention(ynq, ynk, ynv, na_heads, no_sink, "na_ctx_attention")
            swa_y = _ctx_attention(ysq, ysk, ysv, swa_heads, swa_sink[l], "swa_ctx_attention")
            y = _outffn(y, (mla_y, ret_y, na_y, swa_y), wo_all[l], my[2], n2, my[4], my[3], my[5],
                        w1_all[l], w3_all[l], w2_all[l], None, tm_y)
    return x
```

```python
import functools

import jax
import jax.numpy as jnp
from jax import lax
from jax.experimental import pallas as pl
from jax.experimental.pallas import tpu as pltpu

F32 = jnp.float32
BF16 = jnp.bfloat16

GRID_W = 64
HEAD_DIM = 64
N_HEADS = 4
GROUP_W = N_HEADS * HEAD_DIM
MLA_Q_RANK = 256
MLA_KV_RANK = 128
MLA_NOPE = 64
MLA_ROPE = 32
MLA_V = 64
MLA_HEAD_PAD = 128
MLA_V_ROWS = MLA_V + 16
NA_KR = 8
NA_KC = 16
SWA_KV_HEADS = 2
SWA_WINDOW = 128
SWA_BLOCK = 128
ROPE_THETA = 10000.0
EPS = 1e-6
NEG_INF = -1e30
LOG2_E = 1.4426950408889634
LANES = 128
VMEM_LIMIT = 56 * 1024 * 1024

IN_SIZES = (MLA_Q_RANK, MLA_KV_RANK, MLA_ROPE,
            GROUP_W, GROUP_W, GROUP_W, GROUP_W, GROUP_W,
            GROUP_W, GROUP_W, GROUP_W,
            GROUP_W, SWA_KV_HEADS * HEAD_DIM, SWA_KV_HEADS * HEAD_DIM)

_O_CQ = 0
_O_CKV = _O_CQ + MLA_Q_RANK
_O_KR = _O_CKV + MLA_KV_RANK
_O_RQK = _O_KR + LANES
_O_RV = _O_RQK + 2 * GROUP_W
_O_RG = _O_RV + GROUP_W
_O_NA = _O_RG + 2 * GROUP_W
_O_SQK = _O_NA + 3 * GROUP_W
_O_SV = _O_SQK + 2 * GROUP_W
_IN_COLS = _O_SV + GROUP_W


def _cparams(sem):
    return pltpu.CompilerParams(dimension_semantics=sem, vmem_limit_bytes=VMEM_LIMIT)


def _dot(a, b):
    return jnp.dot(a, b, preferred_element_type=F32)


def _dot_nt(a, b):
    return lax.dot_general(a, b, (((1,), (1,)), ((), ())), preferred_element_type=F32)


def _dot_tn(a, b):
    return lax.dot_general(a, b, (((0,), (0,)), ((), ())), preferred_element_type=F32)


def _rms(x):
    return x * lax.rsqrt(jnp.mean(x * x, axis=-1, keepdims=True) + EPS)


def _silu(x):
    return x * jax.nn.sigmoid(x)


def _lane_chunks(arrays):
    return [a[:, j * LANES:(j + 1) * LANES] for a in arrays for j in range(a.shape[-1] // LANES)]


def _row_max(*arrays):
    return jnp.max(functools.reduce(jnp.maximum, _lane_chunks(arrays)), axis=-1, keepdims=True)


def _row_sum(*arrays):
    return jnp.sum(functools.reduce(jnp.add, _lane_chunks(arrays)), axis=-1, keepdims=True)


def _mod_kernel(c_ref, w_ref, b_ref, o_ref):
    o_ref[0] = _dot(_silu(c_ref[...]), w_ref[0]) + b_ref[0]


def _modulation(cond, ada_w, ada_b):
    depth, d, d6 = ada_w.shape
    n = d6 // d
    return pl.pallas_call(
        _mod_kernel,
        grid=(depth, n),
        in_specs=[pl.BlockSpec((8, d), lambda l, j: (0, 0)),
                  pl.BlockSpec((1, d, d), lambda l, j: (l, 0, j)),
                  pl.BlockSpec((1, 1, d), lambda l, j: (l, 0, j))],
        out_specs=pl.BlockSpec((1, 8, d), lambda l, j: (l, 0, j)),
        out_shape=jax.ShapeDtypeStruct((depth, 8, d6), F32),
        compiler_params=_cparams(("parallel", "parallel")),
        name="ada_modulation",
    )(cond, ada_w, ada_b.reshape(depth, 1, d6))


def _rope_tables(seq):
    n_rows = seq // GRID_W

    def parts(pos, d):
        inv = ROPE_THETA ** (-jnp.arange(0, d, 2, dtype=F32) / d)
        ang = pos.astype(F32)[:, None] * inv[None, :]
        z = jnp.zeros_like(ang)
        return (jnp.concatenate([jnp.cos(ang), jnp.cos(ang)], axis=-1),
                jnp.concatenate([z, jnp.sin(ang)], axis=-1),
                jnp.concatenate([-jnp.sin(ang), z], axis=-1))

    def expand(by_row, by_col):
        w = by_row.shape[-1]
        r = jnp.broadcast_to(by_row[:, None, :], (n_rows, GRID_W, w))
        c = jnp.broadcast_to(by_col[None, :, :], (n_rows, GRID_W, w))
        return jnp.concatenate([r, c], axis=-1).reshape(seq, 2 * w)

    def tables(d, fill):
        per_head = [expand(a, b) for a, b in zip(parts(jnp.arange(n_rows), d), parts(jnp.arange(GRID_W), d))]
        return [fill(t, i) for i, t in enumerate(per_head)]

    def two_heads(t, _):
        return jnp.concatenate([t, t], axis=-1)

    def mla_slot(t, i):
        lead = (jnp.ones if i == 0 else jnp.zeros)((seq, MLA_NOPE), F32)
        tail = (jnp.ones if i == 0 else jnp.zeros)((seq, MLA_HEAD_PAD - MLA_NOPE - MLA_ROPE), F32)
        return jnp.concatenate([lead, t, tail], axis=-1)

    def kr_slot(t, i):
        tail = (jnp.ones if i == 0 else jnp.zeros)((seq, LANES - MLA_ROPE), F32)
        return jnp.concatenate([t, tail], axis=-1)

    return tuple(tables(HEAD_DIM // 2, two_heads) + tables(MLA_ROPE // 2, mla_slot)
                 + tables(MLA_ROPE // 2, kr_slot))


def _rope(x, cos, s_prev, s_next, d):
    out = []
    for j in range(x.shape[-1] // LANES):
        xc = x[:, j * LANES:(j + 1) * LANES]
        out.append(xc * cos + pltpu.roll(xc, d, 1) * s_prev + pltpu.roll(xc, LANES - d, 1) * s_next)
    return out[0] if len(out) == 1 else jnp.concatenate(out, axis=-1)


def _inproj_kernel(*refs, rotate, mla_scale):
    if rotate:
        (x_ref, g_ref, sc_ref, sh_ref, w_ref, qn_ref, kvn_ref, wuq_ref, wuk_ref, wuv_ref, place_ref,
         c64_ref, p64_ref, n64_ref, cm_ref, pm_ref, nm_ref, ckr_ref, pkr_ref, nkr_ref,
         mq_ref, mk_ref, mv_ref, rq_ref, rk_ref, rv_ref, rg_ref,
         nq_ref, nk_ref, nv_ref, sq_ref, sk_ref, sv_ref) = refs
    else:
        (x_ref, g_ref, sc_ref, sh_ref, w_ref, qn_ref, kvn_ref, wuq_ref, wuk_ref, wuv_ref, place_ref,
         mq_ref, mk_ref, mv_ref, rq_ref, rk_ref, rv_ref, rg_ref,
         nq_ref, nk_ref, nv_ref, sq_ref, sk_ref, sv_ref) = refs

    x = x_ref[0]
    h = (_rms(x) * g_ref[...]) * (1.0 + sc_ref[0]) + sh_ref[0]
    hb = h.astype(BF16)

    def proj(lo, hi):
        return _dot(hb, w_ref[:, lo:hi])

    def rope64(v):
        if not rotate:
            return v
        return _rope(v, c64_ref[...], p64_ref[...], n64_ref[...], HEAD_DIM // 4)

    def rope_mla(v):
        if not rotate:
            return v
        return _rope(v, cm_ref[...], pm_ref[...], nm_ref[...], MLA_ROPE // 4)

    cq = (_rms(proj(_O_CQ, _O_CKV)) * qn_ref[...]).astype(BF16)
    ckv_kr = proj(_O_CKV, _O_RQK)
    ckv = (_rms(ckv_kr[:, :MLA_KV_RANK]) * kvn_ref[...]).astype(BF16)
    kr = ckv_kr[:, MLA_KV_RANK:]
    if rotate:
        kr = _rope(kr, ckr_ref[...], pkr_ref[...], nkr_ref[...], MLA_ROPE // 4)
    kr = kr.astype(BF16)

    rqk = proj(_O_RQK, _O_RV)
    rq_ref[0] = rope64(rqk[:, :GROUP_W]).astype(BF16)
    rk_ref[0] = rope64(rqk[:, GROUP_W:]).astype(BF16)
    rv_ref[0] = proj(_O_RV, _O_RG).astype(BF16)
    rg_ref[0] = proj(_O_RG, _O_NA)

    na = proj(_O_NA, _O_SQK)
    nq_ref[0] = na[:, :GROUP_W].astype(BF16)
    nk_ref[0] = na[:, GROUP_W:2 * GROUP_W].astype(BF16)
    nv_ref[0] = na[:, 2 * GROUP_W:].astype(BF16)

    sqk = proj(_O_SQK, _O_SV)
    sq_ref[0] = rope64(sqk[:, :GROUP_W]).astype(BF16)
    sk_ref[0] = rope64(sqk[:, GROUP_W:]).astype(BF16)
    sv_ref[0] = proj(_O_SV, _IN_COLS).astype(BF16)

    q = rope_mla(_dot(cq, wuq_ref[...])) * mla_scale
    mq_ref[0] = q.astype(BF16)
    k = _dot(ckv, wuk_ref[...]) + _dot(kr, place_ref[...])
    mk_ref[0] = k.astype(BF16)
    v = _dot(ckv, wuv_ref[...])
    lane = lax.broadcasted_iota(jnp.int32, v.shape, 1)
    mv_ref[0] = jnp.where(lane % MLA_HEAD_PAD >= MLA_V, 1.0, v).astype(BF16)


def _const_spec(shape):
    nd = len(shape)
    return pl.BlockSpec(shape, lambda *_: (0,) * nd, pipeline_mode=pl.Buffered(1))


def _inproj(x, gain, scale, shift, w, qn, kvn, wuq, wuk, wuv, place, tables, tm):
    b, t, d = x.shape
    rotate = tables is not None
    kv_w = GROUP_W
    mla_w = N_HEADS * MLA_HEAD_PAD
    tok = lambda wd: pl.BlockSpec((1, tm, wd), lambda bi, i: (bi, i, 0))
    vec = pl.BlockSpec((1, 1, d), lambda bi, i: (bi, 0, 0))
    in_specs = [tok(d), _const_spec((1, d)), vec, vec, _const_spec(w.shape),
                _const_spec(qn.shape), _const_spec(kvn.shape), _const_spec(wuq.shape),
                _const_spec(wuk.shape), _const_spec(wuv.shape), _const_spec(place.shape)]
    args = [x, gain, scale, shift, w, qn, kvn, wuq, wuk, wuv, place]
    if rotate:
        in_specs += [pl.BlockSpec((tm, LANES), lambda bi, i: (i, 0))] * len(tables)
        args += list(tables)
    widths = [mla_w, mla_w, mla_w, GROUP_W, GROUP_W, GROUP_W, 2 * GROUP_W,
              GROUP_W, GROUP_W, GROUP_W, GROUP_W, kv_w, kv_w]
    dtypes = [BF16] * 6 + [F32] + [BF16] * 6
    return pl.pallas_call(
        functools.partial(_inproj_kernel, rotate=rotate, mla_scale=(MLA_NOPE + MLA_ROPE) ** -0.5 * LOG2_E),
        grid=(b, t // tm),
        in_specs=in_specs,
        out_specs=[tok(wd) for wd in widths],
        out_shape=[jax.ShapeDtypeStruct((b, t, wd), dt) for wd, dt in zip(widths, dtypes)],
        compiler_params=_cparams(("parallel", "parallel")),
        name="in_proj_rot" if rotate else "in_proj_ctx",
    )(*args)


def _mla_kernel(q_ref, kx_ref, vx_ref, ky_ref, vy_ref, o_ref, m_ref, acc_ref, sa_ref, sb_ref, sc_ref,
                ma_ref, mb_ref, mc_ref, *, tk, heads, n_tiles, n_sub, unroll):
    n_chunks = kx_ref.shape[1] // tk
    assert n_chunks % 2 == 0
    tq = q_ref.shape[1] // n_tiles
    ts = tq // n_sub
    chains = [(h, u) for h in range(heads) for u in range(n_sub)]

    def reset():
        for c in range(len(chains)):
            m_ref[c] = jnp.full(m_ref.shape[1:], NEG_INF, F32)
            acc_ref[c] = jnp.zeros(acc_ref.shape[1:], F32)

    def rows(h):
        return slice(h * MLA_HEAD_PAD, (h + 1) * MLA_HEAD_PAD)

    def vrows(h):
        return slice(h * MLA_HEAD_PAD, h * MLA_HEAD_PAD + MLA_V_ROWS)

    buf_a, buf_b, buf_c = (sa_ref, ma_ref), (sb_ref, mb_ref), (sc_ref, mc_ref)

    def scores(t, c, k, dst):
        h, u = chains[c]
        q0 = t * tq + u * ts
        st = _dot_nt(k, q_ref[0, q0:q0 + ts, rows(h)])
        dst[0][c] = st
        dst[1][c] = jnp.max(st, axis=0, keepdims=True)

    def absorb(c, src, vt):
        m_old = m_ref[c]
        m_new = jnp.maximum(m_old, src[1][c])
        pt = jnp.exp2(src[0][c] - m_new).astype(BF16)
        acc_ref[c, :MLA_V_ROWS] = acc_ref[c, :MLA_V_ROWS] * jnp.exp2(m_old - m_new) + _dot(vt, pt)
        m_ref[c] = m_new

    def kx(j, h):
        return kx_ref[0, pl.ds(pl.multiple_of(j * tk, tk), tk), rows(h)]

    def stage_at(t, j, parity):
        cur, nxt = (buf_a, buf_b) if parity == 0 else (buf_b, buf_a)
        for c, (h, _) in enumerate(chains):
            scores(t, c, kx(j + 1, h), nxt)
            absorb(c, cur, vx_ref[0, j, vrows(h), :])

    trips = (n_chunks - 1) // unroll
    reset()
    for c, (h, _) in enumerate(chains):
        scores(0, c, kx(0, h), buf_a)
    for t in range(n_tiles):
        def body(jj, carry):
            for i in range(unroll):
                stage_at(t, unroll * jj + i, i % 2)
            return carry

        lax.fori_loop(0, trips, body, 0)
        for j in range(unroll * trips, n_chunks - 1):
            stage_at(t, j, j % 2)
        for c, (h, _) in enumerate(chains):
            scores(t, c, ky_ref[0, :, rows(h)], buf_c)
            absorb(c, buf_b, vx_ref[0, n_chunks - 1, vrows(h), :])
        outs = [[None] * n_sub for _ in range(heads)]
        for c, (h, u) in enumerate(chains):
            if t + 1 < n_tiles:
                scores(t + 1, c, kx(0, h), buf_a)
            absorb(c, buf_c, vy_ref[0, 0, vrows(h), :])
            acc = acc_ref[c].T
            outs[h][u] = acc[:, :MLA_V] / acc[:, MLA_V:MLA_V + 1]
        o_ref[0, t * tq:(t + 1) * tq, :] = jnp.concatenate(
            [jnp.concatenate(outs[h], axis=0) for h in range(heads)], axis=-1).astype(o_ref.dtype)
        if t + 1 < n_tiles:
            reset()


def _mla_attention(q, kx, vx, ky, vy, tq, tk, n_tiles, n_sub, unroll):
    b, s, _ = q.shape
    l = ky.shape[1]
    hp = 2
    wd = hp * MLA_HEAD_PAD
    vxt = jnp.swapaxes(vx.reshape(b, s // tk, tk, -1), 2, 3)
    vyt = jnp.swapaxes(vy.reshape(b, 1, l, -1), 2, 3)
    ts = tq // n_tiles // n_sub
    return pl.pallas_call(
        functools.partial(_mla_kernel, tk=tk, heads=hp, n_tiles=n_tiles, n_sub=n_sub, unroll=unroll),
        grid=(b, N_HEADS // hp, s // tq),
        in_specs=[pl.BlockSpec((1, tq, wd), lambda bi, hi, i: (bi, i, hi)),
                  pl.BlockSpec((1, s, wd), lambda bi, hi, i: (bi, 0, hi)),
                  pl.BlockSpec((1, s // tk, wd, tk), lambda bi, hi, i: (bi, 0, hi, 0)),
                  pl.BlockSpec((1, l, wd), lambda bi, hi, i: (bi, 0, hi)),
                  pl.BlockSpec((1, 1, wd, l), lambda bi, hi, i: (bi, 0, hi, 0))],
        out_specs=pl.BlockSpec((1, tq, hp * MLA_V), lambda bi, hi, i: (bi, i, hi)),
        out_shape=jax.ShapeDtypeStruct((b, s, N_HEADS * MLA_V), BF16),
        scratch_shapes=[pltpu.VMEM((hp * n_sub, 1, ts), F32),
                        pltpu.VMEM((hp * n_sub, MLA_HEAD_PAD, ts), F32),
                        pltpu.VMEM((hp * n_sub, tk, ts), F32),
                        pltpu.VMEM((hp * n_sub, tk, ts), F32),
                        pltpu.VMEM((hp * n_sub, l, ts), F32)]
                       + [pltpu.VMEM((hp * n_sub, 1, ts), F32)] * 3,
        compiler_params=_cparams(("parallel", "parallel", "arbitrary")),
        name="mla_attention",
    )(q, kx, vxt, ky, vyt)


def _ctx_attn_kernel(sink_ref, q_ref, k_ref, v_ref, o_ref, *, heads, base2):
    exp = jnp.exp2 if base2 else jnp.exp
    outs = []
    for (q0, q1, k0, k1, v0, v1, sink_idx) in heads:
        q = q_ref[0, :, q0:q1]
        s = _dot_nt(q, k_ref[0, :, k0:k1])
        m = jnp.max(s, axis=-1, keepdims=True)
        if sink_idx is not None:
            sink = jnp.full((1, 1), sink_ref[sink_idx], F32)
            m = jnp.maximum(m, sink)
        p = exp(s - m)
        l = jnp.sum(p, axis=-1, keepdims=True)
        if sink_idx is not None:
            l = l + exp(sink - m)
        outs.append(_dot(p.astype(BF16), v_ref[0, :, v0:v1]) / l)
    o_ref[0] = jnp.concatenate(outs, axis=-1).astype(o_ref.dtype)


def _ctx_attention(q, k, v, heads, sink, name, base2=False):
    b, l, _ = q.shape
    full = lambda a: pl.BlockSpec((1, l, a.shape[-1]), lambda bi: (bi, 0, 0))
    return pl.pallas_call(
        functools.partial(_ctx_attn_kernel, heads=heads, base2=base2),
        grid=(b,),
        in_specs=[pl.BlockSpec(memory_space=pltpu.SMEM), full(q), full(k), full(v)],
        out_specs=pl.BlockSpec((1, l, GROUP_W), lambda bi: (bi, 0, 0)),
        out_shape=jax.ShapeDtypeStruct((b, l, GROUP_W), BF16),
        compiler_params=_cparams(("parallel",)),
        name=name,
    )(sink, q, k, v)


def _ret_kernel(*refs, chunk, n_chunks, direction, has_prev):
    if has_prev:
        (dec_ref, q_ref, k_ref, kt_ref, v_ref, g_ref, s0_ref, prev_ref, o_ref, sn_ref,
         st_ref, dm_ref, qd_ref, kdt_ref, cd_ref, ob_ref, qk_ref) = refs
    else:
        (dec_ref, q_ref, k_ref, kt_ref, v_ref, g_ref, s0_ref, o_ref, sn_ref,
         st_ref, dm_ref, qd_ref, kdt_ref, cd_ref, ob_ref, qk_ref) = refs
        prev_ref = None
    i = pl.program_id(0)
    c = chunk
    fwd = direction == 0
    batch = range(q_ref.shape[0])

    def head_of(shape, axis):
        return lax.broadcasted_iota(jnp.int32, shape, axis) // HEAD_DIM

    @pl.when(i == 0)
    def _init():
        st_ref[...] = s0_ref[...]
        ii = lax.broadcasted_iota(jnp.int32, (c, c), 0).astype(F32)
        jj = lax.broadcasted_iota(jnp.int32, (c, c), 1).astype(F32)
        diff = (ii - jj) if fwd else (jj - ii)
        pos = lax.broadcasted_iota(jnp.int32, (c, GROUP_W), 0).astype(F32)
        pos_t = lax.broadcasted_iota(jnp.int32, (GROUP_W, c), 1).astype(F32)
        q_steps = (pos + 1.0) if fwd else (c - pos)
        k_steps = (c - 1.0 - pos_t) if fwd else pos_t
        qd = jnp.zeros((c, GROUP_W), F32)
        kdt = jnp.zeros((GROUP_W, c), F32)
        cd = jnp.zeros((GROUP_W, GROUP_W), F32)
        for h in range(N_HEADS):
            dec = dec_ref[direction * N_HEADS + h]
            lg = jax.nn.log_sigmoid(jnp.full((c, c), dec, F32))
            dm_ref[h * c:(h + 1) * c] = jnp.where(diff >= 0, jnp.exp(lg * jnp.maximum(diff, 0.0)), 0.0)
            lgq = jax.nn.log_sigmoid(jnp.full((c, GROUP_W), dec, F32))
            qd = jnp.where(head_of((c, GROUP_W), 1) == h, jnp.exp(lgq * q_steps), qd)
            lgk = jax.nn.log_sigmoid(jnp.full((GROUP_W, c), dec, F32))
            kdt = jnp.where(head_of((GROUP_W, c), 0) == h, jnp.exp(lgk * k_steps), kdt)
            lgc = jax.nn.log_sigmoid(jnp.full((GROUP_W, GROUP_W), dec, F32))
            cd = jnp.where(head_of((GROUP_W, GROUP_W), 0) == h, jnp.exp(lgc * c), cd)
        qd_ref[...] = qd
        kdt_ref[...] = kdt
        cd_ref[...] = cd

    in_head = [head_of((c, GROUP_W), 1) == h for h in range(N_HEADS)]
    head_mask = [jnp.where(mk, 1.0, 0.0).astype(BF16) for mk in in_head]
    same_head = head_of((GROUP_W, GROUP_W), 0) == head_of((GROUP_W, GROUP_W), 1)

    def chunk_off(n):
        idx = n if fwd else n_chunks - 1 - n
        return pl.multiple_of(idx * c, c)

    def qk(bb, n):
        off = chunk_off(n)
        q_all = q_ref[bb, pl.ds(off, c), :]
        q4 = jnp.concatenate([q_all * head_mask[h] for h in range(N_HEADS)], axis=0)
        return _dot_nt(q4, k_ref[bb, pl.ds(off, c), :])

    def gate_and_store(bb, n):
        off = chunk_off(n)
        o = ob_ref[bb]
        oo = o * o
        ms = jnp.zeros((c, GROUP_W), F32)
        for h in range(N_HEADS):
            ms_h = jnp.sum(jnp.where(in_head[h], oo, 0.0), axis=-1, keepdims=True) * (1.0 / HEAD_DIM)
            ms = jnp.where(in_head[h], ms_h, ms)
        res = o * lax.rsqrt(ms + EPS) * _silu(g_ref[bb, pl.ds(off, c), :])
        if has_prev:
            res = res + prev_ref[bb, pl.ds(off, c), :]
        o_ref[bb, pl.ds(off, c), :] = res.astype(o_ref.dtype)

    ob_ref[...] = jnp.zeros(ob_ref.shape, F32)

    for bb in batch:
        qk_ref[bb] = qk(bb, 0)

    def body(n, carry):
        for bb in batch:
            gate_and_store(bb, jnp.maximum(n - 1, 0))
        off = chunk_off(n)
        atts = [(qk_ref[bb] * dm_ref[...]).astype(BF16) for bb in batch]
        for bb in batch:
            qk_ref[bb] = qk(bb, jnp.minimum(n + 1, n_chunks - 1))
        for bb in batch:
            v = v_ref[bb, pl.ds(off, c), :]
            att = atts[bb]
            intra4 = _dot(att, v)
            intra = intra4[:c]
            for h in range(1, N_HEADS):
                intra = jnp.where(in_head[h], intra4[h * c:(h + 1) * c], intra)
            state = st_ref[bb]
            ob_ref[bb] = intra + _dot(q_ref[bb, pl.ds(off, c), :], state.astype(BF16)) * qd_ref[...]
            kk = (kt_ref[bb, :, pl.ds(off, c)].astype(F32) * kdt_ref[...]).astype(BF16)
            st_ref[bb] = state * cd_ref[...] + jnp.where(same_head, _dot(kk, v), 0.0)
        return carry

    lax.fori_loop(0, n_chunks, body, 0)
    for bb in batch:
        gate_and_store(bb, n_chunks - 1)

    @pl.when(i == pl.num_programs(0) - 1)
    def _fin():
        sn_ref[...] = st_ref[...]


def _retention_pass(dec, q, k, v, gates, state0, prev, direction, chunk, n_chunks, out_dtype):
    b, t, _ = q.shape
    tb = chunk * n_chunks
    n = t // tb
    blk = (lambda i: i) if direction == 0 else (lambda i: n - 1 - i)
    tok = pl.BlockSpec((b, tb, GROUP_W), lambda i: (0, blk(i), 0))
    tok_t = pl.BlockSpec((b, GROUP_W, tb), lambda i: (0, 0, blk(i)))
    gate = pl.BlockSpec((b, tb, GROUP_W), lambda i: (0, blk(i), direction))
    st_spec = pl.BlockSpec((b, GROUP_W, GROUP_W), lambda i: (0, 0, 0))
    in_specs = [pl.BlockSpec(memory_space=pltpu.SMEM), tok, tok, tok_t, tok, gate, st_spec]
    args = [dec, q, k, jnp.swapaxes(k, 1, 2), v, gates, state0]
    if prev is not None:
        in_specs.append(tok)
        args.append(prev)
    return pl.pallas_call(
        functools.partial(_ret_kernel, chunk=chunk, n_chunks=n_chunks, direction=direction,
                          has_prev=prev is not None),
        grid=(n,),
        in_specs=in_specs,
        out_specs=[tok, st_spec],
        out_shape=[jax.ShapeDtypeStruct((b, t, GROUP_W), out_dtype),
                   jax.ShapeDtypeStruct((b, GROUP_W, GROUP_W), F32)],
        scratch_shapes=[pltpu.VMEM((b, GROUP_W, GROUP_W), F32),
                        pltpu.VMEM((N_HEADS * chunk, chunk), F32),
                        pltpu.VMEM((chunk, GROUP_W), F32),
                        pltpu.VMEM((GROUP_W, chunk), F32),
                        pltpu.VMEM((GROUP_W, GROUP_W), F32),
                        pltpu.VMEM((b, chunk, GROUP_W), F32),
                        pltpu.VMEM((b, N_HEADS * chunk, chunk), F32)],
        compiler_params=_cparams(("arbitrary",)),
        name="retention_fwd" if direction == 0 else "retention_bwd",
    )(*args)


def _retention(dec, xq, xk, xv, xg, yq, yk, yv, yg, chunk, chunks_per_step):
    b = xq.shape[0]
    zero = jnp.zeros((b, GROUP_W, GROUP_W), F32)
    ny = yq.shape[1] // chunk
    yb, sb = _retention_pass(dec, yq, yk, yv, yg, zero, None, 1, chunk, ny, F32)
    y, sf = _retention_pass(dec, yq, yk, yv, yg, zero, yb, 0, chunk, ny, BF16)
    xb, _ = _retention_pass(dec, xq, xk, xv, xg, sb, None, 1, chunk, chunks_per_step, F32)
    x, _ = _retention_pass(dec, xq, xk, xv, xg, sf, xb, 0, chunk, chunks_per_step, BF16)
    return x, y


def _na_bias_kernel(rpb_ref, o_ref):
    h = pl.program_id(0)
    dr0 = pl.program_id(1)
    c = lax.broadcasted_iota(jnp.int32, (GRID_W, GRID_W), 0)
    kc = lax.broadcasted_iota(jnp.int32, (GRID_W, GRID_W), 1)
    c0 = jnp.clip(c - NA_KC // 2, 0, GRID_W - NA_KC)
    col_in = (kc >= c0) & (kc < c0 + NA_KC)
    dc = jnp.clip(kc - c, -(NA_KC - 1), NA_KC - 1) + NA_KC - 1
    n_dc = 2 * NA_KC - 1
    for j in range(NA_KR):
        base = (h * (2 * NA_KR - 1) + dr0 + j) * n_dc
        acc = jnp.zeros((GRID_W, GRID_W), F32)
        for d in range(n_dc):
            acc = jnp.where(dc == d, rpb_ref[base + d], acc)
        o_ref[0, :, j * GRID_W:(j + 1) * GRID_W] = jnp.where(col_in, acc, NEG_INF)


def _na_bias_table(rpb):
    return pl.pallas_call(
        _na_bias_kernel,
        grid=(N_HEADS, NA_KR),
        in_specs=[pl.BlockSpec(memory_space=pltpu.SMEM)],
        out_specs=pl.BlockSpec((1, GRID_W, NA_KR * GRID_W), lambda h, r: (r, h, 0)),
        out_shape=jax.ShapeDtypeStruct((NA_KR, N_HEADS * GRID_W, NA_KR * GRID_W), F32),
        compiler_params=_cparams(("parallel", "parallel")),
        name="na_bias_table",
    )(rpb.reshape(-1))


def _na_kernel(q_ref, k_ref, v_ref, ky_ref, vy_ref, tb_ref, o_ref,
               sa_ref, sb_ref, pa_ref, pb_ref, la_ref, lb_ref, *, rows_per_step, n_rows):
    r_base = pl.program_id(1) * rows_per_step
    win = NA_KR * GRID_W

    head_of_lane = lax.broadcasted_iota(jnp.int32, (GRID_W, GROUP_W), 1) // HEAD_DIM
    in_head = [head_of_lane == h for h in range(N_HEADS)]
    head_mask = [jnp.where(mk, 1.0, 0.0).astype(BF16) for mk in in_head]

    def geometry(i):
        r = r_base + i
        r0 = jnp.clip(r - NA_KR // 2, 0, n_rows - NA_KR)
        return pl.multiple_of(r0 * GRID_W, GRID_W), r0 - r + NA_KR - 1

    def row_slice(i):
        return pl.ds(pl.multiple_of(i * GRID_W, GRID_W), GRID_W)

    def scores(i, s_ref):
        koff, dr0 = geometry(i)
        q_all = q_ref[0, row_slice(i), :]
        q4 = jnp.concatenate([q_all * head_mask[h] for h in range(N_HEADS)], axis=0)
        s_ref[:, :win] = _dot_nt(q4, k_ref[0, pl.ds(koff, win), :]) + tb_ref[dr0]
        s_ref[:, win:] = _dot_nt(q4, ky_ref[0])

    def softmax(s_ref, p_ref, l_ref):
        s = s_ref[...]
        p = jnp.exp(s - _row_max(s))
        l_ref[...] = 1.0 / _row_sum(p)
        p_ref[...] = p.astype(BF16)

    def values(i, p_ref, l_ref):
        koff, _ = geometry(i)
        o4 = (_dot(p_ref[:, :win], v_ref[0, pl.ds(koff, win), :])
              + _dot(p_ref[:, win:], vy_ref[0])) * l_ref[...]
        out = o4[:GRID_W]
        for h in range(1, N_HEADS):
            out = jnp.where(in_head[h], o4[h * GRID_W:(h + 1) * GRID_W], out)
        o_ref[0, row_slice(i), :] = out.astype(o_ref.dtype)

    pb_ref[...] = jnp.zeros(pb_ref.shape, BF16)
    lb_ref[...] = jnp.zeros(lb_ref.shape, F32)
    scores(0, sa_ref)

    def pair_body(tt, carry):
        t = 2 * tt
        values(jnp.maximum(t - 1, 0), pb_ref, lb_ref)
        scores(t + 1, sb_ref)
        softmax(sa_ref, pa_ref, la_ref)
        values(t, pa_ref, la_ref)
        scores(jnp.minimum(t + 2, rows_per_step - 1), sa_ref)
        softmax(sb_ref, pb_ref, lb_ref)
        return carry

    lax.fori_loop(0, rows_per_step // 2, pair_body, 0)
    values(rows_per_step - 1, pb_ref, lb_ref)


def _na_attention(q, k, v, ky, vy, table, rows_per_step):
    b, s, _ = q.shape
    l = ky.shape[1]
    n_rows = s // GRID_W
    tq = rows_per_step * GRID_W
    n_keys = NA_KR * GRID_W + l
    seq = lambda n: pl.BlockSpec((1, n, GROUP_W), lambda bi, i: (bi, 0, 0))
    return pl.pallas_call(
        functools.partial(_na_kernel, rows_per_step=rows_per_step, n_rows=n_rows),
        grid=(b, n_rows // rows_per_step),
        in_specs=[pl.BlockSpec((1, tq, GROUP_W), lambda bi, i: (bi, i, 0)),
                  seq(s), seq(s), seq(l), seq(l), _const_spec(table.shape)],
        out_specs=pl.BlockSpec((1, tq, GROUP_W), lambda bi, i: (bi, i, 0)),
        out_shape=jax.ShapeDtypeStruct((b, s, GROUP_W), BF16),
        scratch_shapes=([pltpu.VMEM((N_HEADS * GRID_W, n_keys), F32)] * 2
                        + [pltpu.VMEM((N_HEADS * GRID_W, n_keys), BF16)] * 2
                        + [pltpu.VMEM((N_HEADS * GRID_W, 1), F32)] * 2),
        compiler_params=_cparams(("parallel", "arbitrary")),
        name="na_attention",
    )(q, k, v, ky, vy, table)


def _swa_kernel(sink_ref, q_ref, kp_ref, kc_ref, kn_ref, vp_ref, vc_ref, vn_ref, ky_ref, vy_ref, o_ref,
                kw_ref, vw_ref, wb_ref, sa_ref, sb_ref, pa_ref, pb_ref, la_ref, lb_ref, *, blocks_per_step):
    step = pl.program_id(1)
    nb = pl.num_programs(1) * blocks_per_step
    bl = SWA_BLOCK
    g = N_HEADS // SWA_KV_HEADS
    tq = blocks_per_step * bl
    kw_ref[0:bl] = kp_ref[0]
    kw_ref[bl:bl + tq] = kc_ref[0]
    kw_ref[bl + tq:] = kn_ref[0]
    vw_ref[0:bl] = vp_ref[0]
    vw_ref[bl:bl + tq] = vc_ref[0]
    vw_ref[bl + tq:] = vn_ref[0]

    qi = lax.broadcasted_iota(jnp.int32, (g * bl, 3 * bl), 0) % bl
    jk = lax.broadcasted_iota(jnp.int32, (g * bl, 3 * bl), 1)
    wb_ref[...] = jnp.where(jnp.abs(jk - bl - qi) <= SWA_WINDOW, 0.0, NEG_INF)
    half = lax.broadcasted_iota(jnp.int32, (g * bl, 1), 0) // bl

    head_of_lane = lax.broadcasted_iota(jnp.int32, (bl, GROUP_W), 1) // HEAD_DIM
    in_head = [head_of_lane == h for h in range(N_HEADS)]
    head_mask = [jnp.where(mk, 1.0, 0.0).astype(BF16) for mk in in_head]

    def blk_slice(j):
        return pl.ds(pl.multiple_of(j * bl, bl), bl)

    def scores(j, s_ref):
        qoff = pl.multiple_of(j * bl, bl)
        q_all = q_ref[0, blk_slice(j), :]
        kw = kw_ref[pl.ds(qoff, 3 * bl), :]
        n = step * blocks_per_step + j
        lo_edge = jnp.where(n == 0, NEG_INF, 0.0)
        hi_edge = jnp.where(n == nb - 1, NEG_INF, 0.0)
        for kh in range(SWA_KV_HEADS):
            q = jnp.concatenate([q_all * head_mask[kh * g + gi] for gi in range(g)], axis=0)
            s = _dot_nt(q, kw) + wb_ref[...]
            s_ref[kh, :, :bl] = s[:, :bl] + lo_edge
            s_ref[kh, :, bl:2 * bl] = s[:, bl:2 * bl]
            s_ref[kh, :, 2 * bl:3 * bl] = s[:, 2 * bl:] + hi_edge
            s_ref[kh, :, 3 * bl:] = _dot_nt(q, ky_ref[0])

    def softmax(s_ref, p_ref, l_ref):
        for kh in range(SWA_KV_HEADS):
            s = s_ref[kh]
            sink = jnp.full((g * bl, 1), sink_ref[kh * g], F32)
            for gi in range(1, g):
                sink = jnp.where(half == gi, sink_ref[kh * g + gi], sink)
            m = jnp.maximum(_row_max(s), sink)
            p = jnp.exp(s - m)
            l_ref[kh] = 1.0 / (_row_sum(p) + jnp.exp(sink - m))
            p_ref[kh] = p.astype(BF16)

    def values(j, p_ref, l_ref):
        vw = vw_ref[pl.ds(pl.multiple_of(j * bl, bl), 3 * bl), :]
        out = jnp.zeros((bl, GROUP_W), F32)
        for kh in range(SWA_KV_HEADS):
            o = (_dot(p_ref[kh, :, :3 * bl], vw) + _dot(p_ref[kh, :, 3 * bl:], vy_ref[0])) * l_ref[kh]
            for gi in range(g):
                out = jnp.where(in_head[kh * g + gi], o[gi * bl:(gi + 1) * bl], out)
        o_ref[0, blk_slice(j), :] = out.astype(o_ref.dtype)

    pb_ref[...] = jnp.zeros(pb_ref.shape, BF16)
    lb_ref[...] = jnp.zeros(lb_ref.shape, F32)
    scores(0, sa_ref)

    def pair_body(tt, carry):
        t = 2 * tt
        values(jnp.maximum(t - 1, 0), pb_ref, lb_ref)
        scores(t + 1, sb_ref)
        softmax(sa_ref, pa_ref, la_ref)
        values(t, pa_ref, la_ref)
        scores(jnp.minimum(t + 2, blocks_per_step - 1), sa_ref)
        softmax(sb_ref, pb_ref, lb_ref)
        return carry

    lax.fori_loop(0, blocks_per_step // 2, pair_body, 0)
    values(blocks_per_step - 1, pb_ref, lb_ref)


def _swa_attention(sink, q, k, v, ky, vy, blocks_per_step):
    b, s, _ = q.shape
    l = ky.shape[1]
    tq = blocks_per_step * SWA_BLOCK
    n_steps = s // tq
    nb = s // SWA_BLOCK
    kvw = GROUP_W
    q_rows = N_HEADS // SWA_KV_HEADS * SWA_BLOCK
    n_keys = 3 * SWA_BLOCK + l
    prev = pl.BlockSpec((1, SWA_BLOCK, kvw), lambda bi, i: (bi, jnp.maximum(i * blocks_per_step - 1, 0), 0))
    cur = pl.BlockSpec((1, tq, kvw), lambda bi, i: (bi, i, 0))
    nxt = pl.BlockSpec((1, SWA_BLOCK, kvw), lambda bi, i: (bi, jnp.minimum((i + 1) * blocks_per_step, nb - 1), 0))
    ctx = pl.BlockSpec((1, l, kvw), lambda bi, i: (bi, 0, 0))
    return pl.pallas_call(
        functools.partial(_swa_kernel, blocks_per_step=blocks_per_step),
        grid=(b, n_steps),
        in_specs=[pl.BlockSpec(memory_space=pltpu.SMEM),
                  pl.BlockSpec((1, tq, GROUP_W), lambda bi, i: (bi, i, 0)),
                  prev, cur, nxt, prev, cur, nxt, ctx, ctx],
        out_specs=pl.BlockSpec((1, tq, GROUP_W), lambda bi, i: (bi, i, 0)),
        out_shape=jax.ShapeDtypeStruct((b, s, GROUP_W), BF16),
        scratch_shapes=[pltpu.VMEM((tq + 2 * SWA_BLOCK, kvw), BF16),
                        pltpu.VMEM((tq + 2 * SWA_BLOCK, kvw), BF16),
                        pltpu.VMEM((q_rows, 3 * SWA_BLOCK), F32)]
                       + [pltpu.VMEM((SWA_KV_HEADS, q_rows, n_keys), F32)] * 2
                       + [pltpu.VMEM((SWA_KV_HEADS, q_rows, n_keys), BF16)] * 2
                       + [pltpu.VMEM((SWA_KV_HEADS, q_rows, 1), F32)] * 2,
        compiler_params=_cparams(("parallel", "arbitrary")),
        name="swa_attention",
    )(sink, q, k, k, k, v, v, v, ky, vy)


def _outffn_kernel(*refs, hidden_chunk, final):
    if final:
        (x_ref, m0_ref, m1_ref, m2_ref, m3_ref, wo_ref, g1_ref, n2_ref, sc_ref, sh_ref, g2_ref,
         w1_ref, w3_ref, w2_ref, fg_ref, o_ref) = refs
    else:
        (x_ref, m0_ref, m1_ref, m2_ref, m3_ref, wo_ref, g1_ref, n2_ref, sc_ref, sh_ref, g2_ref,
         w1_ref, w3_ref, w2_ref, o_ref) = refs
    mix = None
    for gi, m_ref in enumerate((m0_ref, m1_ref, m2_ref, m3_ref)):
        part = _dot(m_ref[0], wo_ref[gi * GROUP_W:(gi + 1) * GROUP_W, :])
        mix = part if mix is None else mix + part
    x1 = x_ref[0] + g1_ref[0] * mix
    hb = ((_rms(x1) * n2_ref[...]) * (1.0 + sc_ref[0]) + sh_ref[0]).astype(BF16)
    hidden = w1_ref.shape[1]
    acc = None
    for c0 in range(0, hidden, hidden_chunk):
        a = _dot(hb, w1_ref[:, c0:c0 + hidden_chunk])
        bgate = _dot(hb, w3_ref[:, c0:c0 + hidden_chunk])
        u = (_silu(a) * bgate).astype(BF16)
        part = _dot(u, w2_ref[c0:c0 + hidden_chunk, :])
        acc = part if acc is None else acc + part
    x2 = x1 + g2_ref[0] * acc
    if final:
        x2 = _rms(x2) * fg_ref[...]
    o_ref[0] = x2


def _outffn(x, mixes, wo, g1, n2, sc2, sh2, g2, w1, w3, w2, final_g, tm):
    b, t, d = x.shape
    tok = lambda wd: pl.BlockSpec((1, tm, wd), lambda bi, i: (bi, i, 0))
    vec = pl.BlockSpec((1, 1, d), lambda bi, i: (bi, 0, 0))
    in_specs = ([tok(d)] + [tok(GROUP_W)] * 4
                + [_const_spec(wo.shape), vec, _const_spec((1, d)), vec, vec, vec,
                   _const_spec(w1.shape), _const_spec(w3.shape), _const_spec(w2.shape)])
    args = [x, *mixes, wo, g1, n2, sc2, sh2, g2, w1, w3, w2]
    final = final_g is not None
    if final:
        in_specs.append(_const_spec((1, d)))
        args.append(final_g)
    return pl.pallas_call(
        functools.partial(_outffn_kernel, hidden_chunk=256, final=final),
        grid=(b, t // tm),
        in_specs=in_specs,
        out_specs=tok(d),
        out_shape=jax.ShapeDtypeStruct((b, t, d), F32),
        compiler_params=_cparams(("parallel", "parallel")),
        name="out_proj_ffn_final" if final else "out_proj_ffn",
    )(*args)


def _prep_weights(w_in, mla_w_uq, mla_w_ukv):
    depth, d, _ = w_in.shape
    offs = [0]
    for sz in IN_SIZES:
        offs.append(offs[-1] + sz)
    w_bf = w_in.astype(BF16)
    cols = [w_bf[:, :, offs[i]:offs[i + 1]] for i in range(len(IN_SIZES))]
    cq, ckv, kr, rq, rk, rv, rgf, rgb, nq, nk, nv, sq, sk, sv = cols
    scale = HEAD_DIM ** -0.5
    kr_slot = jnp.concatenate([kr, jnp.zeros((depth, d, LANES - MLA_ROPE), BF16)], axis=-1)
    src = jnp.arange(LANES)[:, None]
    dst = jnp.arange(N_HEADS * MLA_HEAD_PAD)[None, :]
    place = ((src < MLA_ROPE) & (dst % MLA_HEAD_PAD == src + MLA_NOPE)).astype(BF16)

    def per_query_head(t):
        g = N_HEADS // SWA_KV_HEADS
        t = t.reshape(depth, d, SWA_KV_HEADS, 1, HEAD_DIM)
        return jnp.broadcast_to(t, (depth, d, SWA_KV_HEADS, g, HEAD_DIM)).reshape(depth, d, GROUP_W)

    w = jnp.concatenate([cq, ckv, kr_slot, rq, rk * scale, rv, rgf, rgb, nq * scale, nk, nv,
                         sq * scale, per_query_head(sk), per_query_head(sv)], axis=-1).astype(BF16)

    qr = mla_w_uq.shape[1]
    uq = mla_w_uq.reshape(depth, qr, N_HEADS, MLA_NOPE + MLA_ROPE)
    wuq = jnp.concatenate([uq, jnp.zeros((depth, qr, N_HEADS, MLA_HEAD_PAD - MLA_NOPE - MLA_ROPE), F32)],
                          axis=-1).reshape(depth, qr, N_HEADS * MLA_HEAD_PAD).astype(BF16)
    kvr = mla_w_ukv.shape[1]
    ukv = mla_w_ukv.reshape(depth, kvr, N_HEADS, MLA_NOPE + MLA_V)
    zk = jnp.zeros((depth, kvr, N_HEADS, MLA_HEAD_PAD - MLA_NOPE), F32)
    wuk = jnp.concatenate([ukv[..., :MLA_NOPE], zk], axis=-1).reshape(depth, kvr, -1).astype(BF16)
    zv = jnp.zeros((depth, kvr, N_HEADS, MLA_HEAD_PAD - MLA_V), F32)
    wuv = jnp.concatenate([ukv[..., MLA_NOPE:], zv], axis=-1).reshape(depth, kvr, -1).astype(BF16)
    return w, wuq, wuk, wuv, place


def _ctx_head_specs():
    mla = tuple((h * MLA_HEAD_PAD, (h + 1) * MLA_HEAD_PAD, h * MLA_HEAD_PAD, (h + 1) * MLA_HEAD_PAD,
                 h * MLA_HEAD_PAD, h * MLA_HEAD_PAD + MLA_V, None) for h in range(N_HEADS))
    na = tuple((h * HEAD_DIM, (h + 1) * HEAD_DIM) * 3 + (None,) for h in range(N_HEADS))
    swa = tuple((h * HEAD_DIM, (h + 1) * HEAD_DIM) * 3 + (h,) for h in range(N_HEADS))
    return mla, na, swa


def kernel(x, c, ctx, c_ctx, ada_w, ada_b, norm1_g, w_in, mla_q_norm, mla_w_uq, mla_kv_norm, mla_w_ukv,
           ret_decay, na_rpb, swa_sink, w_out, norm2_g, ffn_w1, ffn_w3, ffn_w2, final_norm_g):
    b, s, d = x.shape
    l_ctx = ctx.shape[1]
    depth = ada_w.shape[0]
    assert b + 1 <= 8 and s % 1024 == 0 and l_ctx % 128 == 0

    cond = jnp.concatenate([c, c_ctx[None, :], jnp.zeros((8 - b - 1, d), F32)], axis=0)
    mod = _modulation(cond, ada_w, ada_b)
    tables = _rope_tables(s)
    w_all, wuq_all, wuk_all, wuv_all, place = _prep_weights(w_in, mla_w_uq, mla_w_ukv)
    wo_all = w_out.astype(BF16)
    w1_all, w3_all, w2_all = ffn_w1.astype(BF16), ffn_w3.astype(BF16), ffn_w2.astype(BF16)
    mla_heads, na_heads, swa_heads = _ctx_head_specs()
    no_sink = jnp.zeros((N_HEADS,), F32)

    tm_x = 512
    tm_y = min(256, l_ctx)
    y = ctx
    for l in range(depth):
        mx = [mod[l, :b, j * d:(j + 1) * d][:, None, :] for j in range(6)]
        my = [jnp.broadcast_to(mod[l, b, j * d:(j + 1) * d][None, None, :], (b, 1, d)) for j in range(6)]
        n1 = norm1_g[l][None, :]
        n2 = norm2_g[l][None, :]
        qn = mla_q_norm[l][None, :]
        kvn = mla_kv_norm[l][None, :]
        lw = (w_all[l], qn, kvn, wuq_all[l], wuk_all[l], wuv_all[l], place)

        px = _inproj(x, n1, mx[1], mx[0], *lw, tables, tm_x)
        py = _inproj(y, n1, my[1], my[0], *lw, None, tm_y)
        (xmq, xmk, xmv, xrq, xrk, xrv, xrg, xnq, xnk, xnv, xsq, xsk, xsv) = px
        (ymq, ymk, ymv, yrq, yrk, yrv, yrg, ynq, ynk, ynv, ysq, ysk, ysv) = py

        mla_x = _mla_attention(xmq, xmk, xmv, ymk, ymv, tq=1024, tk=512, n_tiles=2, n_sub=2, unroll=8)
        dec = ret_decay[l].reshape(-1)
        ret_x, ret_y = _retention(dec, xrq, xrk, xrv, xrg, yrq, yrk, yrv, yrg, chunk=128, chunks_per_step=8)
        table = _na_bias_table(na_rpb[l])
        na_x = _na_attention(xnq, xnk, xnv, ynk, ynv, table, rows_per_step=16)
        swa_x = _swa_attention(swa_sink[l], xsq, xsk, xsv, ysk, ysv, blocks_per_step=8)

        last = l == depth - 1
        x = _outffn(x, (mla_x, ret_x, na_x, swa_x), wo_all[l], mx[2], n2, mx[4], mx[3], mx[5],
                    w1_all[l], w3_all[l], w2_all[l], final_norm_g[None, :] if last else None, tm_x)
        if not last:
            mla_y = _ctx_attention(ymq, ymk, ymv, mla_heads, no_sink, "mla_ctx_attention", base2=True)
            na_y = _ctx_attention(ynq, ynk, ynv, na_heads, no_sink, "na_ctx_attention")
            swa_y = _ctx_attention(ysq, ysk, ysv, swa_heads, swa_sink[l], "swa_ctx_attention")
            y = _outffn(y, (mla_y, ret_y, na_y, swa_y), wo_all[l], my[2], n2, my[4], my[3], my[5],
                        w1_all[l], w3_all[l], w2_all[l], None, tm_y)
    return x
```

```python
import functools

import jax
import jax.numpy as jnp
from jax import lax
from jax.experimental import pallas as pl
from jax.experimental.pallas import tpu as pltpu

F32 = jnp.float32
BF16 = jnp.bfloat16

GRID_W = 64
HEAD_DIM = 64
N_HEADS = 4
GROUP_W = N_HEADS * HEAD_DIM
MLA_Q_RANK = 256
MLA_KV_RANK = 128
MLA_NOPE = 64
MLA_ROPE = 32
MLA_V = 64
MLA_HEAD_PAD = 128
MLA_V_ROWS = MLA_V + 16
NA_KR = 8
NA_KC = 16
SWA_KV_HEADS = 2
SWA_WINDOW = 128
SWA_BLOCK = 128
ROPE_THETA = 10000.0
EPS = 1e-6
NEG_INF = -1e30
LOG2_E = 1.4426950408889634
LANES = 128
VMEM_LIMIT = 56 * 1024 * 1024

IN_SIZES = (MLA_Q_RANK, MLA_KV_RANK, MLA_ROPE,
            GROUP_W, GROUP_W, GROUP_W, GROUP_W, GROUP_W,
            GROUP_W, GROUP_W, GROUP_W,
            GROUP_W, SWA_KV_HEADS * HEAD_DIM, SWA_KV_HEADS * HEAD_DIM)

_O_CQ = 0
_O_CKV = _O_CQ + MLA_Q_RANK
_O_KR = _O_CKV + MLA_KV_RANK
_O_RQK = _O_KR + LANES
_O_RV = _O_RQK + 2 * GROUP_W
_O_RG = _O_RV + GROUP_W
_O_NA = _O_RG + 2 * GROUP_W
_O_SQK = _O_NA + 3 * GROUP_W
_O_SV = _O_SQK + 2 * GROUP_W
_IN_COLS = _O_SV + GROUP_W


def _cparams(sem):
    return pltpu.CompilerParams(dimension_semantics=sem, vmem_limit_bytes=VMEM_LIMIT)


def _dot(a, b):
    return jnp.dot(a, b, preferred_element_type=F32)


def _dot_nt(a, b):
    return lax.dot_general(a, b, (((1,), (1,)), ((), ())), preferred_element_type=F32)


def _rms(x):
    return x * lax.rsqrt(jnp.mean(x * x, axis=-1, keepdims=True) + EPS)


def _silu(x):
    return x * jax.nn.sigmoid(x)


def _lane_chunks(arrays):
    return [a[:, j * LANES:(j + 1) * LANES] for a in arrays for j in range(a.shape[-1] // LANES)]


def _row_max(*arrays):
    return jnp.max(functools.reduce(jnp.maximum, _lane_chunks(arrays)), axis=-1, keepdims=True)


def _row_sum(*arrays):
    return jnp.sum(functools.reduce(jnp.add, _lane_chunks(arrays)), axis=-1, keepdims=True)


def _mod_kernel(c_ref, w_ref, b_ref, o_ref):
    o_ref[0] = _dot(_silu(c_ref[...]), w_ref[0]) + b_ref[0]


def _modulation(cond, ada_w, ada_b):
    depth, d, d6 = ada_w.shape
    n = d6 // d
    return pl.pallas_call(
        _mod_kernel,
        grid=(depth, n),
        in_specs=[pl.BlockSpec((8, d), lambda l, j: (0, 0)),
                  pl.BlockSpec((1, d, d), lambda l, j: (l, 0, j)),
                  pl.BlockSpec((1, 1, d), lambda l, j: (l, 0, j))],
        out_specs=pl.BlockSpec((1, 8, d), lambda l, j: (l, 0, j)),
        out_shape=jax.ShapeDtypeStruct((depth, 8, d6), F32),
        compiler_params=_cparams(("parallel", "parallel")),
        name="ada_modulation",
    )(cond, ada_w, ada_b.reshape(depth, 1, d6))


def _rope_tables(seq):
    n_rows = seq // GRID_W

    def parts(pos, d):
        inv = ROPE_THETA ** (-jnp.arange(0, d, 2, dtype=F32) / d)
        ang = pos.astype(F32)[:, None] * inv[None, :]
        z = jnp.zeros_like(ang)
        return (jnp.concatenate([jnp.cos(ang), jnp.cos(ang)], axis=-1),
                jnp.concatenate([z, jnp.sin(ang)], axis=-1),
                jnp.concatenate([-jnp.sin(ang), z], axis=-1))

    def expand(by_row, by_col):
        w = by_row.shape[-1]
        r = jnp.broadcast_to(by_row[:, None, :], (n_rows, GRID_W, w))
        c = jnp.broadcast_to(by_col[None, :, :], (n_rows, GRID_W, w))
        return jnp.concatenate([r, c], axis=-1).reshape(seq, 2 * w)

    def tables(d, fill):
        per_head = [expand(a, b) for a, b in zip(parts(jnp.arange(n_rows), d), parts(jnp.arange(GRID_W), d))]
        return [fill(t, i) for i, t in enumerate(per_head)]

    def two_heads(t, _):
        return jnp.concatenate([t, t], axis=-1)

    def mla_slot(t, i):
        lead = (jnp.ones if i == 0 else jnp.zeros)((seq, MLA_NOPE), F32)
        tail = (jnp.ones if i == 0 else jnp.zeros)((seq, MLA_HEAD_PAD - MLA_NOPE - MLA_ROPE), F32)
        return jnp.concatenate([lead, t, tail], axis=-1)

    def kr_slot(t, i):
        tail = (jnp.ones if i == 0 else jnp.zeros)((seq, LANES - MLA_ROPE), F32)
        return jnp.concatenate([t, tail], axis=-1)

    return tuple(tables(HEAD_DIM // 2, two_heads) + tables(MLA_ROPE // 2, mla_slot)
                 + tables(MLA_ROPE // 2, kr_slot))


def _rope(x, cos, s_prev, s_next, d):
    out = []
    for j in range(x.shape[-1] // LANES):
        xc = x[:, j * LANES:(j + 1) * LANES]
        out.append(xc * cos + pltpu.roll(xc, d, 1) * s_prev + pltpu.roll(xc, LANES - d, 1) * s_next)
    return out[0] if len(out) == 1 else jnp.concatenate(out, axis=-1)


def _inproj_kernel(*refs, rotate, mla_scale):
    if rotate:
        (x_ref, g_ref, sc_ref, sh_ref, w_ref, qn_ref, kvn_ref, wuq_ref, wuk_ref, wuv_ref, place_ref,
         c64_ref, p64_ref, n64_ref, cm_ref, pm_ref, nm_ref, ckr_ref, pkr_ref, nkr_ref,
         mq_ref, mk_ref, mv_ref, rq_ref, rk_ref, rv_ref, rg_ref,
         nq_ref, nk_ref, nv_ref, sq_ref, sk_ref, sv_ref) = refs
    else:
        (x_ref, g_ref, sc_ref, sh_ref, w_ref, qn_ref, kvn_ref, wuq_ref, wuk_ref, wuv_ref, place_ref,
         mq_ref, mk_ref, mv_ref, rq_ref, rk_ref, rv_ref, rg_ref,
         nq_ref, nk_ref, nv_ref, sq_ref, sk_ref, sv_ref) = refs

    x = x_ref[0]
    h = _rms(x) * (g_ref[...] * (1.0 + sc_ref[0])) + sh_ref[0]
    hb = h.astype(BF16)

    def proj(lo, hi):
        return _dot(hb, w_ref[:, lo:hi])

    def rope64(v):
        if not rotate:
            return v
        return _rope(v, c64_ref[...], p64_ref[...], n64_ref[...], HEAD_DIM // 4)

    def rope_mla(v):
        if not rotate:
            return v
        return _rope(v, cm_ref[...], pm_ref[...], nm_ref[...], MLA_ROPE // 4)

    cq = (_rms(proj(_O_CQ, _O_CKV)) * qn_ref[...]).astype(BF16)
    ckv_kr = proj(_O_CKV, _O_RQK)
    ckv = (_rms(ckv_kr[:, :MLA_KV_RANK]) * kvn_ref[...]).astype(BF16)
    kr = ckv_kr[:, MLA_KV_RANK:]
    if rotate:
        kr = _rope(kr, ckr_ref[...], pkr_ref[...], nkr_ref[...], MLA_ROPE // 4)
    kr = kr.astype(BF16)

    rqk = proj(_O_RQK, _O_RV)
    rq_ref[0] = rope64(rqk[:, :GROUP_W]).astype(BF16)
    rk_ref[0] = rope64(rqk[:, GROUP_W:]).astype(BF16)
    rv_ref[0] = proj(_O_RV, _O_RG).astype(BF16)
    rg_ref[0] = proj(_O_RG, _O_NA)

    na = proj(_O_NA, _O_SQK)
    nq_ref[0] = na[:, :GROUP_W].astype(BF16)
    nk_ref[0] = na[:, GROUP_W:2 * GROUP_W].astype(BF16)
    nv_ref[0] = na[:, 2 * GROUP_W:].astype(BF16)

    sqk = proj(_O_SQK, _O_SV)
    sq_ref[0] = rope64(sqk[:, :GROUP_W]).astype(BF16)
    sk_ref[0] = rope64(sqk[:, GROUP_W:]).astype(BF16)
    sv_ref[0] = proj(_O_SV, _IN_COLS).astype(BF16)

    q = rope_mla(_dot(cq, wuq_ref[...])) * mla_scale
    mq_ref[0] = q.astype(BF16)
    k = _dot(ckv, wuk_ref[...]) + _dot(kr, place_ref[...])
    mk_ref[0] = k.astype(BF16)
    v = _dot(ckv, wuv_ref[...])
    lane = lax.broadcasted_iota(jnp.int32, v.shape, 1)
    mv_ref[0] = jnp.where(lane % MLA_HEAD_PAD >= MLA_V, 1.0, v).astype(BF16)


def _const_spec(shape):
    nd = len(shape)
    return pl.BlockSpec(shape, lambda *_: (0,) * nd, pipeline_mode=pl.Buffered(1))


def _inproj(x, gain, scale, shift, w, qn, kvn, wuq, wuk, wuv, place, tables, tm):
    b, t, d = x.shape
    rotate = tables is not None
    kv_w = GROUP_W
    mla_w = N_HEADS * MLA_HEAD_PAD
    tok = lambda wd: pl.BlockSpec((1, tm, wd), lambda bi, i: (bi, i, 0))
    vec = pl.BlockSpec((1, 1, d), lambda bi, i: (bi, 0, 0))
    in_specs = [tok(d), _const_spec((1, d)), vec, vec, _const_spec(w.shape),
                _const_spec(qn.shape), _const_spec(kvn.shape), _const_spec(wuq.shape),
                _const_spec(wuk.shape), _const_spec(wuv.shape), _const_spec(place.shape)]
    args = [x, gain, scale, shift, w, qn, kvn, wuq, wuk, wuv, place]
    if rotate:
        in_specs += [pl.BlockSpec((tm, LANES), lambda bi, i: (i, 0))] * len(tables)
        args += list(tables)
    widths = [mla_w, mla_w, mla_w, GROUP_W, GROUP_W, GROUP_W, 2 * GROUP_W,
              GROUP_W, GROUP_W, GROUP_W, GROUP_W, kv_w, kv_w]
    dtypes = [BF16] * 6 + [F32] + [BF16] * 6
    return pl.pallas_call(
        functools.partial(_inproj_kernel, rotate=rotate, mla_scale=(MLA_NOPE + MLA_ROPE) ** -0.5 * LOG2_E),
        grid=(b, t // tm),
        in_specs=in_specs,
        out_specs=[tok(wd) for wd in widths],
        out_shape=[jax.ShapeDtypeStruct((b, t, wd), dt) for wd, dt in zip(widths, dtypes)],
        compiler_params=_cparams(("parallel", "parallel")),
        name="in_proj_rot" if rotate else "in_proj_ctx",
    )(*args)


def _mla_kernel(q_ref, kx_ref, vx_ref, ky_ref, vy_ref, o_ref, m_ref, acc_ref, sa_ref, sb_ref, sc_ref,
                ma_ref, mb_ref, mc_ref, *, tk, heads, n_tiles, n_sub, unroll):
    n_chunks = kx_ref.shape[1] // tk
    assert n_chunks % 2 == 0
    tq = q_ref.shape[1] // n_tiles
    ts = tq // n_sub
    chains = [(h, u) for h in range(heads) for u in range(n_sub)]

    def reset():
        for c in range(len(chains)):
            m_ref[c] = jnp.full(m_ref.shape[1:], NEG_INF, F32)
            acc_ref[c] = jnp.zeros(acc_ref.shape[1:], F32)

    def rows(h):
        return slice(h * MLA_HEAD_PAD, (h + 1) * MLA_HEAD_PAD)

    def vrows(h):
        return slice(h * MLA_HEAD_PAD, h * MLA_HEAD_PAD + MLA_V_ROWS)

    buf_a, buf_b, buf_c = (sa_ref, ma_ref), (sb_ref, mb_ref), (sc_ref, mc_ref)

    def scores(t, c, k, dst):
        h, u = chains[c]
        q0 = t * tq + u * ts
        st = _dot_nt(k, q_ref[0, q0:q0 + ts, rows(h)])
        dst[0][c] = st
        dst[1][c] = jnp.max(st, axis=0, keepdims=True)

    def absorb(c, src, vt):
        m_old = m_ref[c]
        m_new = jnp.maximum(m_old, src[1][c])
        pt = jnp.exp2(src[0][c] - m_new).astype(BF16)
        acc_ref[c, :MLA_V_ROWS] = acc_ref[c, :MLA_V_ROWS] * jnp.exp2(m_old - m_new) + _dot(vt, pt)
        m_ref[c] = m_new

    def kx(j, h):
        return kx_ref[0, pl.ds(pl.multiple_of(j * tk, tk), tk), rows(h)]

    def stage_at(t, j, parity):
        cur, nxt = (buf_a, buf_b) if parity == 0 else (buf_b, buf_a)
        for c, (h, _) in enumerate(chains):
            scores(t, c, kx(j + 1, h), nxt)
            absorb(c, cur, vx_ref[0, j, vrows(h), :])

    trips = (n_chunks - 1) // unroll
    reset()
    for c, (h, _) in enumerate(chains):
        scores(0, c, kx(0, h), buf_a)
    for t in range(n_tiles):
        def body(jj, carry):
            for i in range(unroll):
                stage_at(t, unroll * jj + i, i % 2)
            return carry

        lax.fori_loop(0, trips, body, 0)
        for j in range(unroll * trips, n_chunks - 1):
            stage_at(t, j, j % 2)
        for c, (h, _) in enumerate(chains):
            scores(t, c, ky_ref[0, :, rows(h)], buf_c)
            absorb(c, buf_b, vx_ref[0, n_chunks - 1, vrows(h), :])
        outs = [[None] * n_sub for _ in range(heads)]
        for c, (h, u) in enumerate(chains):
            if t + 1 < n_tiles:
                scores(t + 1, c, kx(0, h), buf_a)
            absorb(c, buf_c, vy_ref[0, 0, vrows(h), :])
            acc = acc_ref[c].T
            outs[h][u] = acc[:, :MLA_V] / acc[:, MLA_V:MLA_V + 1]
        o_ref[0, t * tq:(t + 1) * tq, :] = jnp.concatenate(
            [jnp.concatenate(outs[h], axis=0) for h in range(heads)], axis=-1).astype(o_ref.dtype)
        if t + 1 < n_tiles:
            reset()


def _mla_attention(q, kx, vx, ky, vy, tq, tk, n_tiles, n_sub, unroll):
    b, s, _ = q.shape
    l = ky.shape[1]
    hp = 2
    wd = hp * MLA_HEAD_PAD
    vxt = jnp.swapaxes(vx.reshape(b, s // tk, tk, -1), 2, 3)
    vyt = jnp.swapaxes(vy.reshape(b, 1, l, -1), 2, 3)
    ts = tq // n_tiles // n_sub
    return pl.pallas_call(
        functools.partial(_mla_kernel, tk=tk, heads=hp, n_tiles=n_tiles, n_sub=n_sub, unroll=unroll),
        grid=(b, N_HEADS // hp, s // tq),
        in_specs=[pl.BlockSpec((1, tq, wd), lambda bi, hi, i: (bi, i, hi)),
                  pl.BlockSpec((1, s, wd), lambda bi, hi, i: (bi, 0, hi)),
                  pl.BlockSpec((1, s // tk, wd, tk), lambda bi, hi, i: (bi, 0, hi, 0)),
                  pl.BlockSpec((1, l, wd), lambda bi, hi, i: (bi, 0, hi)),
                  pl.BlockSpec((1, 1, wd, l), lambda bi, hi, i: (bi, 0, hi, 0))],
        out_specs=pl.BlockSpec((1, tq, hp * MLA_V), lambda bi, hi, i: (bi, i, hi)),
        out_shape=jax.ShapeDtypeStruct((b, s, N_HEADS * MLA_V), BF16),
        scratch_shapes=[pltpu.VMEM((hp * n_sub, 1, ts), F32),
                        pltpu.VMEM((hp * n_sub, MLA_HEAD_PAD, ts), F32),
                        pltpu.VMEM((hp * n_sub, tk, ts), F32),
                        pltpu.VMEM((hp * n_sub, tk, ts), F32),
                        pltpu.VMEM((hp * n_sub, l, ts), F32)]
                       + [pltpu.VMEM((hp * n_sub, 1, ts), F32)] * 3,
        compiler_params=_cparams(("parallel", "parallel", "arbitrary")),
        name="mla_attention",
    )(q, kx, vxt, ky, vyt)


def _ctx_attn_kernel(sink_ref, q_ref, k_ref, v_ref, o_ref, *, heads, base2):
    exp = jnp.exp2 if base2 else jnp.exp
    outs = []
    for (q0, q1, k0, k1, v0, v1, sink_idx) in heads:
        q = q_ref[0, :, q0:q1]
        s = _dot_nt(q, k_ref[0, :, k0:k1])
        m = jnp.max(s, axis=-1, keepdims=True)
        if sink_idx is not None:
            sink = jnp.full((1, 1), sink_ref[sink_idx], F32)
            m = jnp.maximum(m, sink)
        p = exp(s - m)
        l = jnp.sum(p, axis=-1, keepdims=True)
        if sink_idx is not None:
            l = l + exp(sink - m)
        outs.append(_dot(p.astype(BF16), v_ref[0, :, v0:v1]) / l)
    o_ref[0] = jnp.concatenate(outs, axis=-1).astype(o_ref.dtype)


def _ctx_attention(q, k, v, heads, sink, name, base2=False):
    b, l, _ = q.shape
    full = lambda a: pl.BlockSpec((1, l, a.shape[-1]), lambda bi: (bi, 0, 0))
    return pl.pallas_call(
        functools.partial(_ctx_attn_kernel, heads=heads, base2=base2),
        grid=(b,),
        in_specs=[pl.BlockSpec(memory_space=pltpu.SMEM), full(q), full(k), full(v)],
        out_specs=pl.BlockSpec((1, l, GROUP_W), lambda bi: (bi, 0, 0)),
        out_shape=jax.ShapeDtypeStruct((b, l, GROUP_W), BF16),
        compiler_params=_cparams(("parallel",)),
        name=name,
    )(sink, q, k, v)


def _ret_kernel(*refs, chunk, n_chunks, direction, has_prev):
    if has_prev:
        (dec_ref, q_ref, k_ref, kt_ref, v_ref, g_ref, s0_ref, prev_ref, o_ref, sn_ref,
         st_ref, dm_ref, qd_ref, kdt_ref, cd_ref, ob_ref, qk_ref) = refs
    else:
        (dec_ref, q_ref, k_ref, kt_ref, v_ref, g_ref, s0_ref, o_ref, sn_ref,
         st_ref, dm_ref, qd_ref, kdt_ref, cd_ref, ob_ref, qk_ref) = refs
        prev_ref = None
    i = pl.program_id(0)
    c = chunk
    fwd = direction == 0
    batch = range(q_ref.shape[0])

    def head_of(shape, axis):
        return lax.broadcasted_iota(jnp.int32, shape, axis) // HEAD_DIM

    @pl.when(i == 0)
    def _init():
        st_ref[...] = s0_ref[...]
        ii = lax.broadcasted_iota(jnp.int32, (c, c), 0).astype(F32)
        jj = lax.broadcasted_iota(jnp.int32, (c, c), 1).astype(F32)
        diff = (ii - jj) if fwd else (jj - ii)
        pos = lax.broadcasted_iota(jnp.int32, (c, GROUP_W), 0).astype(F32)
        pos_t = lax.broadcasted_iota(jnp.int32, (GROUP_W, c), 1).astype(F32)
        q_steps = (pos + 1.0) if fwd else (c - pos)
        k_steps = (c - 1.0 - pos_t) if fwd else pos_t
        qd = jnp.zeros((c, GROUP_W), F32)
        kdt = jnp.zeros((GROUP_W, c), F32)
        cd = jnp.zeros((GROUP_W, GROUP_W), F32)
        for h in range(N_HEADS):
            dec = dec_ref[direction * N_HEADS + h]
            lg = jax.nn.log_sigmoid(jnp.full((c, c), dec, F32))
            dm_ref[h * c:(h + 1) * c] = jnp.where(diff >= 0, jnp.exp(lg * jnp.maximum(diff, 0.0)), 0.0)
            lgq = jax.nn.log_sigmoid(jnp.full((c, GROUP_W), dec, F32))
            qd = jnp.where(head_of((c, GROUP_W), 1) == h, jnp.exp(lgq * q_steps), qd)
            lgk = jax.nn.log_sigmoid(jnp.full((GROUP_W, c), dec, F32))
            kdt = jnp.where(head_of((GROUP_W, c), 0) == h, jnp.exp(lgk * k_steps), kdt)
            lgc = jax.nn.log_sigmoid(jnp.full((GROUP_W, GROUP_W), dec, F32))
            cd = jnp.where(head_of((GROUP_W, GROUP_W), 0) == h, jnp.exp(lgc * c), cd)
        qd_ref[...] = qd
        kdt_ref[...] = kdt
        cd_ref[...] = cd

    in_head = [head_of((c, GROUP_W), 1) == h for h in range(N_HEADS)]
    head_mask = [jnp.where(mk, 1.0, 0.0).astype(BF16) for mk in in_head]
    same_head = head_of((GROUP_W, GROUP_W), 0) == head_of((GROUP_W, GROUP_W), 1)

    def chunk_off(n):
        idx = n if fwd else n_chunks - 1 - n
        return pl.multiple_of(idx * c, c)

    def qk(bb, n):
        off = chunk_off(n)
        q_all = q_ref[bb, pl.ds(off, c), :]
        q4 = jnp.concatenate([q_all * head_mask[h] for h in range(N_HEADS)], axis=0)
        return _dot_nt(q4, k_ref[bb, pl.ds(off, c), :])

    def gate_and_store(bb, n):
        off = chunk_off(n)
        o = ob_ref[bb]
        oo = o * o
        ms = jnp.zeros((c, GROUP_W), F32)
        for h in range(N_HEADS):
            ms_h = jnp.sum(jnp.where(in_head[h], oo, 0.0), axis=-1, keepdims=True) * (1.0 / HEAD_DIM)
            ms = jnp.where(in_head[h], ms_h, ms)
        res = o * lax.rsqrt(ms + EPS) * _silu(g_ref[bb, pl.ds(off, c), :])
        if has_prev:
            res = res + prev_ref[bb, pl.ds(off, c), :]
        o_ref[bb, pl.ds(off, c), :] = res.astype(o_ref.dtype)

    ob_ref[...] = jnp.zeros(ob_ref.shape, F32)

    for bb in batch:
        qk_ref[bb] = qk(bb, 0)

    def body(n, carry):
        for bb in batch:
            gate_and_store(bb, jnp.maximum(n - 1, 0))
        off = chunk_off(n)
        atts = [(qk_ref[bb] * dm_ref[...]).astype(BF16) for bb in batch]
        for bb in batch:
            qk_ref[bb] = qk(bb, jnp.minimum(n + 1, n_chunks - 1))
        for bb in batch:
            v = v_ref[bb, pl.ds(off, c), :]
            att = atts[bb]
            intra4 = _dot(att, v)
            intra = intra4[:c]
            for h in range(1, N_HEADS):
                intra = jnp.where(in_head[h], intra4[h * c:(h + 1) * c], intra)
            state = st_ref[bb]
            ob_ref[bb] = intra + _dot(q_ref[bb, pl.ds(off, c), :], state.astype(BF16)) * qd_ref[...]
            kk = (kt_ref[bb, :, pl.ds(off, c)].astype(F32) * kdt_ref[...]).astype(BF16)
            st_ref[bb] = state * cd_ref[...] + jnp.where(same_head, _dot(kk, v), 0.0)
        return carry

    lax.fori_loop(0, n_chunks, body, 0)
    for bb in batch:
        gate_and_store(bb, n_chunks - 1)

    @pl.when(i == pl.num_programs(0) - 1)
    def _fin():
        sn_ref[...] = st_ref[...]


def _retention_pass(dec, q, k, v, gates, state0, prev, direction, chunk, n_chunks, out_dtype):
    b, t, _ = q.shape
    tb = chunk * n_chunks
    n = t // tb
    blk = (lambda i: i) if direction == 0 else (lambda i: n - 1 - i)
    tok = pl.BlockSpec((b, tb, GROUP_W), lambda i: (0, blk(i), 0))
    tok_t = pl.BlockSpec((b, GROUP_W, tb), lambda i: (0, 0, blk(i)))
    gate = pl.BlockSpec((b, tb, GROUP_W), lambda i: (0, blk(i), direction))
    st_spec = pl.BlockSpec((b, GROUP_W, GROUP_W), lambda i: (0, 0, 0))
    in_specs = [pl.BlockSpec(memory_space=pltpu.SMEM), tok, tok, tok_t, tok, gate, st_spec]
    args = [dec, q, k, jnp.swapaxes(k, 1, 2), v, gates, state0]
    if prev is not None:
        in_specs.append(tok)
        args.append(prev)
    return pl.pallas_call(
        functools.partial(_ret_kernel, chunk=chunk, n_chunks=n_chunks, direction=direction,
                          has_prev=prev is not None),
        grid=(n,),
        in_specs=in_specs,
        out_specs=[tok, st_spec],
        out_shape=[jax.ShapeDtypeStruct((b, t, GROUP_W), out_dtype),
                   jax.ShapeDtypeStruct((b, GROUP_W, GROUP_W), F32)],
        scratch_shapes=[pltpu.VMEM((b, GROUP_W, GROUP_W), F32),
                        pltpu.VMEM((N_HEADS * chunk, chunk), F32),
                        pltpu.VMEM((chunk, GROUP_W), F32),
                        pltpu.VMEM((GROUP_W, chunk), F32),
                        pltpu.VMEM((GROUP_W, GROUP_W), F32),
                        pltpu.VMEM((b, chunk, GROUP_W), F32),
                        pltpu.VMEM((b, N_HEADS * chunk, chunk), F32)],
        compiler_params=_cparams(("arbitrary",)),
        name="retention_fwd" if direction == 0 else "retention_bwd",
    )(*args)


def _retention(dec, xq, xk, xv, xg, yq, yk, yv, yg, chunk, chunks_per_step):
    b = xq.shape[0]
    zero = jnp.zeros((b, GROUP_W, GROUP_W), F32)
    ny = yq.shape[1] // chunk
    yb, sb = _retention_pass(dec, yq, yk, yv, yg, zero, None, 1, chunk, ny, F32)
    y, sf = _retention_pass(dec, yq, yk, yv, yg, zero, yb, 0, chunk, ny, BF16)
    xb, _ = _retention_pass(dec, xq, xk, xv, xg, sb, None, 1, chunk, chunks_per_step, F32)
    x, _ = _retention_pass(dec, xq, xk, xv, xg, sf, xb, 0, chunk, chunks_per_step, BF16)
    return x, y


def _na_bias_kernel(rpb_ref, o_ref):
    h = pl.program_id(0)
    dr0 = pl.program_id(1)
    c = lax.broadcasted_iota(jnp.int32, (GRID_W, GRID_W), 0)
    kc = lax.broadcasted_iota(jnp.int32, (GRID_W, GRID_W), 1)
    c0 = jnp.clip(c - NA_KC // 2, 0, GRID_W - NA_KC)
    col_in = (kc >= c0) & (kc < c0 + NA_KC)
    dc = jnp.clip(kc - c, -(NA_KC - 1), NA_KC - 1) + NA_KC - 1
    n_dc = 2 * NA_KC - 1
    accs = [jnp.zeros((GRID_W, GRID_W), F32) for _ in range(NA_KR)]
    for d in range(n_dc):
        at_d = dc == d
        for j in range(NA_KR):
            accs[j] = jnp.where(at_d, rpb_ref[(h * (2 * NA_KR - 1) + dr0 + j) * n_dc + d], accs[j])
    for j in range(NA_KR):
        o_ref[0, :, j * GRID_W:(j + 1) * GRID_W] = jnp.where(col_in, accs[j], NEG_INF)


def _na_bias_table(rpb):
    return pl.pallas_call(
        _na_bias_kernel,
        grid=(N_HEADS, NA_KR),
        in_specs=[pl.BlockSpec(memory_space=pltpu.SMEM)],
        out_specs=pl.BlockSpec((1, GRID_W, NA_KR * GRID_W), lambda h, r: (r, h, 0)),
        out_shape=jax.ShapeDtypeStruct((NA_KR, N_HEADS * GRID_W, NA_KR * GRID_W), F32),
        compiler_params=_cparams(("parallel", "parallel")),
        name="na_bias_table",
    )(rpb.reshape(-1))


def _na_kernel(q_ref, k_ref, v_ref, ky_ref, vy_ref, tb_ref, o_ref,
               sa_ref, sb_ref, pa_ref, pb_ref, la_ref, lb_ref, *, rows_per_step, n_rows):
    r_base = pl.program_id(1) * rows_per_step
    win = NA_KR * GRID_W

    head_of_lane = lax.broadcasted_iota(jnp.int32, (GRID_W, GROUP_W), 1) // HEAD_DIM
    in_head = [head_of_lane == h for h in range(N_HEADS)]
    head_mask = [jnp.where(mk, 1.0, 0.0).astype(BF16) for mk in in_head]

    def geometry(i):
        r = r_base + i
        r0 = jnp.clip(r - NA_KR // 2, 0, n_rows - NA_KR)
        return pl.multiple_of(r0 * GRID_W, GRID_W), r0 - r + NA_KR - 1

    def row_slice(i):
        return pl.ds(pl.multiple_of(i * GRID_W, GRID_W), GRID_W)

    def scores(i, s_ref):
        koff, dr0 = geometry(i)
        q_all = q_ref[0, row_slice(i), :]
        q4 = jnp.concatenate([q_all * head_mask[h] for h in range(N_HEADS)], axis=0)
        s_ref[:, :win] = _dot_nt(q4, k_ref[0, pl.ds(koff, win), :]) + tb_ref[dr0]
        s_ref[:, win:] = _dot_nt(q4, ky_ref[0])

    def softmax(s_ref, p_ref, l_ref):
        s = s_ref[...]
        p = jnp.exp(s - _row_max(s))
        l_ref[...] = 1.0 / _row_sum(p)
        p_ref[...] = p.astype(BF16)

    def values(i, p_ref, l_ref):
        koff, _ = geometry(i)
        o4 = (_dot(p_ref[:, :win], v_ref[0, pl.ds(koff, win), :])
              + _dot(p_ref[:, win:], vy_ref[0])) * l_ref[...]
        out = o4[:GRID_W]
        for h in range(1, N_HEADS):
            out = jnp.where(in_head[h], o4[h * GRID_W:(h + 1) * GRID_W], out)
        o_ref[0, row_slice(i), :] = out.astype(o_ref.dtype)

    pb_ref[...] = jnp.zeros(pb_ref.shape, BF16)
    lb_ref[...] = jnp.zeros(lb_ref.shape, F32)
    scores(0, sa_ref)

    def pair_body(tt, carry):
        t = 2 * tt
        values(jnp.maximum(t - 1, 0), pb_ref, lb_ref)
        scores(t + 1, sb_ref)
        softmax(sa_ref, pa_ref, la_ref)
        values(t, pa_ref, la_ref)
        scores(jnp.minimum(t + 2, rows_per_step - 1), sa_ref)
        softmax(sb_ref, pb_ref, lb_ref)
        return carry

    lax.fori_loop(0, rows_per_step // 2, pair_body, 0)
    values(rows_per_step - 1, pb_ref, lb_ref)


def _na_attention(q, k, v, ky, vy, table, rows_per_step):
    b, s, _ = q.shape
    l = ky.shape[1]
    n_rows = s // GRID_W
    tq = rows_per_step * GRID_W
    n_keys = NA_KR * GRID_W + l
    seq = lambda n: pl.BlockSpec((1, n, GROUP_W), lambda bi, i: (bi, 0, 0))
    return pl.pallas_call(
        functools.partial(_na_kernel, rows_per_step=rows_per_step, n_rows=n_rows),
        grid=(b, n_rows // rows_per_step),
        in_specs=[pl.BlockSpec((1, tq, GROUP_W), lambda bi, i: (bi, i, 0)),
                  seq(s), seq(s), seq(l), seq(l), _const_spec(table.shape)],
        out_specs=pl.BlockSpec((1, tq, GROUP_W), lambda bi, i: (bi, i, 0)),
        out_shape=jax.ShapeDtypeStruct((b, s, GROUP_W), BF16),
        scratch_shapes=([pltpu.VMEM((N_HEADS * GRID_W, n_keys), F32)] * 2
                        + [pltpu.VMEM((N_HEADS * GRID_W, n_keys), BF16)] * 2
                        + [pltpu.VMEM((N_HEADS * GRID_W, 1), F32)] * 2),
        compiler_params=_cparams(("parallel", "arbitrary")),
        name="na_attention",
    )(q, k, v, ky, vy, table)


def _swa_kernel(sink_ref, q_ref, kp_ref, kc_ref, kn_ref, vp_ref, vc_ref, vn_ref, ky_ref, vy_ref, o_ref,
                kw_ref, vw_ref, wb_ref, sa_ref, sb_ref, pa_ref, pb_ref, la_ref, lb_ref, *, blocks_per_step):
    step = pl.program_id(1)
    nb = pl.num_programs(1) * blocks_per_step
    bl = SWA_BLOCK
    g = N_HEADS // SWA_KV_HEADS
    tq = blocks_per_step * bl
    kw_ref[0:bl] = kp_ref[0]
    kw_ref[bl:bl + tq] = kc_ref[0]
    kw_ref[bl + tq:] = kn_ref[0]
    vw_ref[0:bl] = vp_ref[0]
    vw_ref[bl:bl + tq] = vc_ref[0]
    vw_ref[bl + tq:] = vn_ref[0]

    qi = lax.broadcasted_iota(jnp.int32, (g * bl, 3 * bl), 0) % bl
    jk = lax.broadcasted_iota(jnp.int32, (g * bl, 3 * bl), 1)
    wb_ref[...] = jnp.where(jnp.abs(jk - bl - qi) <= SWA_WINDOW, 0.0, NEG_INF)
    half = lax.broadcasted_iota(jnp.int32, (g * bl, 1), 0) // bl

    head_of_lane = lax.broadcasted_iota(jnp.int32, (bl, GROUP_W), 1) // HEAD_DIM
    in_head = [head_of_lane == h for h in range(N_HEADS)]
    head_mask = [jnp.where(mk, 1.0, 0.0).astype(BF16) for mk in in_head]

    def blk_slice(j):
        return pl.ds(pl.multiple_of(j * bl, bl), bl)

    def scores(j, s_ref):
        qoff = pl.multiple_of(j * bl, bl)
        q_all = q_ref[0, blk_slice(j), :]
        kw = kw_ref[pl.ds(qoff, 3 * bl), :]
        n = step * blocks_per_step + j
        lo_edge = jnp.where(n == 0, NEG_INF, 0.0)
        hi_edge = jnp.where(n == nb - 1, NEG_INF, 0.0)
        for kh in range(SWA_KV_HEADS):
            q = jnp.concatenate([q_all * head_mask[kh * g + gi] for gi in range(g)], axis=0)
            s = _dot_nt(q, kw) + wb_ref[...]
            s_ref[kh, :, :bl] = s[:, :bl] + lo_edge
            s_ref[kh, :, bl:2 * bl] = s[:, bl:2 * bl]
            s_ref[kh, :, 2 * bl:3 * bl] = s[:, 2 * bl:] + hi_edge
            s_ref[kh, :, 3 * bl:] = _dot_nt(q, ky_ref[0])

    def softmax(s_ref, p_ref, l_ref):
        for kh in range(SWA_KV_HEADS):
            s = s_ref[kh]
            sink = jnp.full((g * bl, 1), sink_ref[kh * g], F32)
            for gi in range(1, g):
                sink = jnp.where(half == gi, sink_ref[kh * g + gi], sink)
            m = jnp.maximum(_row_max(s), sink)
            p = jnp.exp(s - m)
            l_ref[kh] = 1.0 / (_row_sum(p) + jnp.exp(sink - m))
            p_ref[kh] = p.astype(BF16)

    def values(j, p_ref, l_ref):
        vw = vw_ref[pl.ds(pl.multiple_of(j * bl, bl), 3 * bl), :]
        out = jnp.zeros((bl, GROUP_W), F32)
        for kh in range(SWA_KV_HEADS):
            o = (_dot(p_ref[kh, :, :3 * bl], vw) + _dot(p_ref[kh, :, 3 * bl:], vy_ref[0])) * l_ref[kh]
            for gi in range(g):
                out = jnp.where(in_head[kh * g + gi], o[gi * bl:(gi + 1) * bl], out)
        o_ref[0, blk_slice(j), :] = out.astype(o_ref.dtype)

    pb_ref[...] = jnp.zeros(pb_ref.shape, BF16)
    lb_ref[...] = jnp.zeros(lb_ref.shape, F32)
    scores(0, sa_ref)

    def pair_body(tt, carry):
        t = 2 * tt
        values(jnp.maximum(t - 1, 0), pb_ref, lb_ref)
        scores(t + 1, sb_ref)
        softmax(sa_ref, pa_ref, la_ref)
        values(t, pa_ref, la_ref)
        scores(jnp.minimum(t + 2, blocks_per_step - 1), sa_ref)
        softmax(sb_ref, pb_ref, lb_ref)
        return carry

    lax.fori_loop(0, blocks_per_step // 2, pair_body, 0)
    values(blocks_per_step - 1, pb_ref, lb_ref)


def _swa_attention(sink, q, k, v, ky, vy, blocks_per_step):
    b, s, _ = q.shape
    l = ky.shape[1]
    tq = blocks_per_step * SWA_BLOCK
    n_steps = s // tq
    nb = s // SWA_BLOCK
    kvw = GROUP_W
    q_rows = N_HEADS // SWA_KV_HEADS * SWA_BLOCK
    n_keys = 3 * SWA_BLOCK + l
    prev = pl.BlockSpec((1, SWA_BLOCK, kvw), lambda bi, i: (bi, jnp.maximum(i * blocks_per_step - 1, 0), 0))
    cur = pl.BlockSpec((1, tq, kvw), lambda bi, i: (bi, i, 0))
    nxt = pl.BlockSpec((1, SWA_BLOCK, kvw), lambda bi, i: (bi, jnp.minimum((i + 1) * blocks_per_step, nb - 1), 0))
    ctx = pl.BlockSpec((1, l, kvw), lambda bi, i: (bi, 0, 0))
    return pl.pallas_call(
        functools.partial(_swa_kernel, blocks_per_step=blocks_per_step),
        grid=(b, n_steps),
        in_specs=[pl.BlockSpec(memory_space=pltpu.SMEM),
                  pl.BlockSpec((1, tq, GROUP_W), lambda bi, i: (bi, i, 0)),
                  prev, cur, nxt, prev, cur, nxt, ctx, ctx],
        out_specs=pl.BlockSpec((1, tq, GROUP_W), lambda bi, i: (bi, i, 0)),
        out_shape=jax.ShapeDtypeStruct((b, s, GROUP_W), BF16),
        scratch_shapes=[pltpu.VMEM((tq + 2 * SWA_BLOCK, kvw), BF16),
                        pltpu.VMEM((tq + 2 * SWA_BLOCK, kvw), BF16),
                        pltpu.VMEM((q_rows, 3 * SWA_BLOCK), F32)]
                       + [pltpu.VMEM((SWA_KV_HEADS, q_rows, n_keys), F32)] * 2
                       + [pltpu.VMEM((SWA_KV_HEADS, q_rows, n_keys), BF16)] * 2
                       + [pltpu.VMEM((SWA_KV_HEADS, q_rows, 1), F32)] * 2,
        compiler_params=_cparams(("parallel", "arbitrary")),
        name="swa_attention",
    )(sink, q, k, k, k, v, v, v, ky, vy)


def _outffn_kernel(*refs, hidden_chunk, final):
    if final:
        (x_ref, m0_ref, m1_ref, m2_ref, m3_ref, wo_ref, g1_ref, n2_ref, sc_ref, sh_ref, g2_ref,
         w1_ref, w3_ref, w2_ref, fg_ref, o_ref) = refs
    else:
        (x_ref, m0_ref, m1_ref, m2_ref, m3_ref, wo_ref, g1_ref, n2_ref, sc_ref, sh_ref, g2_ref,
         w1_ref, w3_ref, w2_ref, o_ref) = refs
    mix = None
    for gi, m_ref in enumerate((m0_ref, m1_ref, m2_ref, m3_ref)):
        part = _dot(m_ref[0], wo_ref[gi * GROUP_W:(gi + 1) * GROUP_W, :])
        mix = part if mix is None else mix + part
    x1 = x_ref[0] + g1_ref[0] * mix
    hb = (_rms(x1) * (n2_ref[...] * (1.0 + sc_ref[0])) + sh_ref[0]).astype(BF16)
    hidden = w1_ref.shape[1]
    acc = None
    for c0 in range(0, hidden, hidden_chunk):
        a = _dot(hb, w1_ref[:, c0:c0 + hidden_chunk])
        bgate = _dot(hb, w3_ref[:, c0:c0 + hidden_chunk])
        u = (_silu(a) * bgate).astype(BF16)
        part = _dot(u, w2_ref[c0:c0 + hidden_chunk, :])
        acc = part if acc is None else acc + part
    x2 = x1 + g2_ref[0] * acc
    if final:
        x2 = _rms(x2) * fg_ref[...]
    o_ref[0] = x2


def _outffn(x, mixes, wo, g1, n2, sc2, sh2, g2, w1, w3, w2, final_g, tm):
    b, t, d = x.shape
    tok = lambda wd: pl.BlockSpec((1, tm, wd), lambda bi, i: (bi, i, 0))
    vec = pl.BlockSpec((1, 1, d), lambda bi, i: (bi, 0, 0))
    in_specs = ([tok(d)] + [tok(GROUP_W)] * 4
                + [_const_spec(wo.shape), vec, _const_spec((1, d)), vec, vec, vec,
                   _const_spec(w1.shape), _const_spec(w3.shape), _const_spec(w2.shape)])
    args = [x, *mixes, wo, g1, n2, sc2, sh2, g2, w1, w3, w2]
    final = final_g is not None
    if final:
        in_specs.append(_const_spec((1, d)))
        args.append(final_g)
    return pl.pallas_call(
        functools.partial(_outffn_kernel, hidden_chunk=256, final=final),
        grid=(b, t // tm),
        in_specs=in_specs,
        out_specs=tok(d),
        out_shape=jax.ShapeDtypeStruct((b, t, d), F32),
        compiler_params=_cparams(("parallel", "parallel")),
        name="out_proj_ffn_final" if final else "out_proj_ffn",
    )(*args)


def _prep_weights(w_in, mla_w_uq, mla_w_ukv):
    depth, d, _ = w_in.shape
    offs = [0]
    for sz in IN_SIZES:
        offs.append(offs[-1] + sz)
    w_bf = w_in.astype(BF16)
    cols = [w_bf[:, :, offs[i]:offs[i + 1]] for i in range(len(IN_SIZES))]
    cq, ckv, kr, rq, rk, rv, rgf, rgb, nq, nk, nv, sq, sk, sv = cols
    scale = HEAD_DIM ** -0.5
    kr_slot = jnp.concatenate([kr, jnp.zeros((depth, d, LANES - MLA_ROPE), BF16)], axis=-1)
    src = jnp.arange(LANES)[:, None]
    dst = jnp.arange(N_HEADS * MLA_HEAD_PAD)[None, :]
    place = ((src < MLA_ROPE) & (dst % MLA_HEAD_PAD == src + MLA_NOPE)).astype(BF16)

    def per_query_head(t):
        g = N_HEADS // SWA_KV_HEADS
        t = t.reshape(depth, d, SWA_KV_HEADS, 1, HEAD_DIM)
        return jnp.broadcast_to(t, (depth, d, SWA_KV_HEADS, g, HEAD_DIM)).reshape(depth, d, GROUP_W)

    w = jnp.concatenate([cq, ckv, kr_slot, rq, rk * scale, rv, rgf, rgb, nq * scale, nk, nv,
                         sq * scale, per_query_head(sk), per_query_head(sv)], axis=-1).astype(BF16)

    qr = mla_w_uq.shape[1]
    uq = mla_w_uq.reshape(depth, qr, N_HEADS, MLA_NOPE + MLA_ROPE)
    wuq = jnp.concatenate([uq, jnp.zeros((depth, qr, N_HEADS, MLA_HEAD_PAD - MLA_NOPE - MLA_ROPE), F32)],
                          axis=-1).reshape(depth, qr, N_HEADS * MLA_HEAD_PAD).astype(BF16)
    kvr = mla_w_ukv.shape[1]
    ukv = mla_w_ukv.reshape(depth, kvr, N_HEADS, MLA_NOPE + MLA_V)
    zk = jnp.zeros((depth, kvr, N_HEADS, MLA_HEAD_PAD - MLA_NOPE), F32)
    wuk = jnp.concatenate([ukv[..., :MLA_NOPE], zk], axis=-1).reshape(depth, kvr, -1).astype(BF16)
    zv = jnp.zeros((depth, kvr, N_HEADS, MLA_HEAD_PAD - MLA_V), F32)
    wuv = jnp.concatenate([ukv[..., MLA_NOPE:], zv], axis=-1).reshape(depth, kvr, -1).astype(BF16)
    return w, wuq, wuk, wuv, place


def _ctx_head_specs():
    mla = tuple((h * MLA_HEAD_PAD, (h + 1) * MLA_HEAD_PAD, h * MLA_HEAD_PAD, (h + 1) * MLA_HEAD_PAD,
                 h * MLA_HEAD_PAD, h * MLA_HEAD_PAD + MLA_V, None) for h in range(N_HEADS))
    na = tuple((h * HEAD_DIM, (h + 1) * HEAD_DIM) * 3 + (None,) for h in range(N_HEADS))
    swa = tuple((h * HEAD_DIM, (h + 1) * HEAD_DIM) * 3 + (h,) for h in range(N_HEADS))
    return mla, na, swa


def kernel(x, c, ctx, c_ctx, ada_w, ada_b, norm1_g, w_in, mla_q_norm, mla_w_uq, mla_kv_norm, mla_w_ukv,
           ret_decay, na_rpb, swa_sink, w_out, norm2_g, ffn_w1, ffn_w3, ffn_w2, final_norm_g):
    b, s, d = x.shape
    l_ctx = ctx.shape[1]
    depth = ada_w.shape[0]
    assert b + 1 <= 8 and s % 1024 == 0 and l_ctx % 128 == 0

    cond = jnp.concatenate([c, c_ctx[None, :], jnp.zeros((8 - b - 1, d), F32)], axis=0)
    mod = _modulation(cond, ada_w, ada_b)
    tables = _rope_tables(s)
    w_all, wuq_all, wuk_all, wuv_all, place = _prep_weights(w_in, mla_w_uq, mla_w_ukv)
    wo_all = w_out.astype(BF16)
    w1_all, w3_all, w2_all = ffn_w1.astype(BF16), ffn_w3.astype(BF16), ffn_w2.astype(BF16)
    mla_heads, na_heads, swa_heads = _ctx_head_specs()
    no_sink = jnp.zeros((N_HEADS,), F32)

    tm_x = 512
    tm_y = min(256, l_ctx)
    y = ctx
    for l in range(depth):
        mx = [mod[l, :b, j * d:(j + 1) * d][:, None, :] for j in range(6)]
        my = [jnp.broadcast_to(mod[l, b, j * d:(j + 1) * d][None, None, :], (b, 1, d)) for j in range(6)]
        n1 = norm1_g[l][None, :]
        n2 = norm2_g[l][None, :]
        qn = mla_q_norm[l][None, :]
        kvn = mla_kv_norm[l][None, :]
        lw = (w_all[l], qn, kvn, wuq_all[l], wuk_all[l], wuv_all[l], place)

        px = _inproj(x, n1, mx[1], mx[0], *lw, tables, tm_x)
        py = _inproj(y, n1, my[1], my[0], *lw, None, tm_y)
        (xmq, xmk, xmv, xrq, xrk, xrv, xrg, xnq, xnk, xnv, xsq, xsk, xsv) = px
        (ymq, ymk, ymv, yrq, yrk, yrv, yrg, ynq, ynk, ynv, ysq, ysk, ysv) = py

        mla_x = _mla_attention(xmq, xmk, xmv, ymk, ymv, tq=1024, tk=512, n_tiles=2, n_sub=2, unroll=8)
        dec = ret_decay[l].reshape(-1)
        ret_x, ret_y = _retention(dec, xrq, xrk, xrv, xrg, yrq, yrk, yrv, yrg, chunk=128, chunks_per_step=8)
        table = _na_bias_table(na_rpb[l])
        na_x = _na_attention(xnq, xnk, xnv, ynk, ynv, table, rows_per_step=16)
        swa_x = _swa_attention(swa_sink[l], xsq, xsk, xsv, ysk, ysv, blocks_per_step=8)

        last = l == depth - 1
        x = _outffn(x, (mla_x, ret_x, na_x, swa_x), wo_all[l], mx[2], n2, mx[4], mx[3], mx[5],
                    w1_all[l], w3_all[l], w2_all[l], final_norm_g[None, :] if last else None, tm_x)
        if not last:
            mla_y = _ctx_attention(ymq, ymk, ymv, mla_heads, no_sink, "mla_ctx_attention", base2=True)
            na_y = _ctx_attention(ynq, ynk, ynv, na_heads, no_sink, "na_ctx_attention")
            swa_y = _ctx_attention(ysq, ysk, ysv, swa_heads, swa_sink[l], "swa_ctx_attention")
            y = _outffn(y, (mla_y, ret_y, na_y, swa_y), wo_all[l], my[2], n2, my[4], my[3], my[5],
                        w1_all[l], w3_all[l], w2_all[l], None, tm_y)
    return x
```

```python
import functools

import jax
import jax.numpy as jnp
from jax import lax
from jax.experimental import pallas as pl
from jax.experimental.pallas import tpu as pltpu

F32 = jnp.float32
BF16 = jnp.bfloat16

GRID_W = 64
HEAD_DIM = 64
N_HEADS = 4
GROUP_W = N_HEADS * HEAD_DIM
MLA_Q_RANK = 256
MLA_KV_RANK = 128
MLA_NOPE = 64
MLA_ROPE = 32
MLA_V = 64
MLA_HEAD_PAD = 128
MLA_V_ROWS = MLA_V + 16
NA_KR = 8
NA_KC = 16
SWA_KV_HEADS = 2
SWA_WINDOW = 128
SWA_BLOCK = 128
ROPE_THETA = 10000.0
EPS = 1e-6
NEG_INF = -1e30
LOG2_E = 1.4426950408889634
LANES = 128
VMEM_LIMIT = 56 * 1024 * 1024

IN_SIZES = (MLA_Q_RANK, MLA_KV_RANK, MLA_ROPE,
            GROUP_W, GROUP_W, GROUP_W, GROUP_W, GROUP_W,
            GROUP_W, GROUP_W, GROUP_W,
            GROUP_W, SWA_KV_HEADS * HEAD_DIM, SWA_KV_HEADS * HEAD_DIM)

_O_CQ = 0
_O_CKV = _O_CQ + MLA_Q_RANK
_O_KR = _O_CKV + MLA_KV_RANK
_O_RQK = _O_KR + LANES
_O_RV = _O_RQK + 2 * GROUP_W
_O_RG = _O_RV + GROUP_W
_O_NA = _O_RG + 2 * GROUP_W
_O_SQK = _O_NA + 3 * GROUP_W
_O_SV = _O_SQK + 2 * GROUP_W
_IN_COLS = _O_SV + GROUP_W


def _cparams(sem):
    return pltpu.CompilerParams(dimension_semantics=sem, vmem_limit_bytes=VMEM_LIMIT)


def _dot(a, b):
    return jnp.dot(a, b, preferred_element_type=F32)


def _dot_nt(a, b):
    return lax.dot_general(a, b, (((1,), (1,)), ((), ())), preferred_element_type=F32)


def _rms(x):
    return x * lax.rsqrt(jnp.mean(x * x, axis=-1, keepdims=True) + EPS)


def _silu(x):
    return x * jax.nn.sigmoid(x)


def _lane_chunks(arrays):
    return [a[:, j * LANES:(j + 1) * LANES] for a in arrays for j in range(a.shape[-1] // LANES)]


def _row_max(*arrays):
    return jnp.max(functools.reduce(jnp.maximum, _lane_chunks(arrays)), axis=-1, keepdims=True)


def _row_sum(*arrays):
    return jnp.sum(functools.reduce(jnp.add, _lane_chunks(arrays)), axis=-1, keepdims=True)


def _mod_kernel(c_ref, w_ref, b_ref, o_ref):
    o_ref[0] = _dot(_silu(c_ref[...]), w_ref[0]) + b_ref[0]


def _modulation(cond, ada_w, ada_b):
    depth, d, d6 = ada_w.shape
    n = d6 // d
    return pl.pallas_call(
        _mod_kernel,
        grid=(depth, n),
        in_specs=[pl.BlockSpec((8, d), lambda l, j: (0, 0)),
                  pl.BlockSpec((1, d, d), lambda l, j: (l, 0, j)),
                  pl.BlockSpec((1, 1, d), lambda l, j: (l, 0, j))],
        out_specs=pl.BlockSpec((1, 8, d), lambda l, j: (l, 0, j)),
        out_shape=jax.ShapeDtypeStruct((depth, 8, d6), F32),
        compiler_params=_cparams(("parallel", "parallel")),
        name="ada_modulation",
    )(cond, ada_w, ada_b.reshape(depth, 1, d6))


def _rope_tables(seq):
    n_rows = seq // GRID_W

    def parts(pos, d):
        inv = ROPE_THETA ** (-jnp.arange(0, d, 2, dtype=F32) / d)
        ang = pos.astype(F32)[:, None] * inv[None, :]
        z = jnp.zeros_like(ang)
        return (jnp.concatenate([jnp.cos(ang), jnp.cos(ang)], axis=-1),
                jnp.concatenate([z, jnp.sin(ang)], axis=-1),
                jnp.concatenate([-jnp.sin(ang), z], axis=-1))

    def expand(by_row, by_col):
        w = by_row.shape[-1]
        r = jnp.broadcast_to(by_row[:, None, :], (n_rows, GRID_W, w))
        c = jnp.broadcast_to(by_col[None, :, :], (n_rows, GRID_W, w))
        return jnp.concatenate([r, c], axis=-1).reshape(seq, 2 * w)

    def tables(d, fill):
        per_head = [expand(a, b) for a, b in zip(parts(jnp.arange(n_rows), d), parts(jnp.arange(GRID_W), d))]
        return [fill(t, i) for i, t in enumerate(per_head)]

    def two_heads(t, _):
        return jnp.concatenate([t, t], axis=-1)

    def mla_slot(t, i):
        lead = (jnp.ones if i == 0 else jnp.zeros)((seq, MLA_NOPE), F32)
        tail = (jnp.ones if i == 0 else jnp.zeros)((seq, MLA_HEAD_PAD - MLA_NOPE - MLA_ROPE), F32)
        return jnp.concatenate([lead, t, tail], axis=-1)

    def kr_slot(t, i):
        tail = (jnp.ones if i == 0 else jnp.zeros)((seq, LANES - MLA_ROPE), F32)
        return jnp.concatenate([t, tail], axis=-1)

    return tuple(tables(HEAD_DIM // 2, two_heads) + tables(MLA_ROPE // 2, mla_slot)
                 + tables(MLA_ROPE // 2, kr_slot))


def _rope(x, cos, s_prev, s_next, d):
    out = []
    for j in range(x.shape[-1] // LANES):
        xc = x[:, j * LANES:(j + 1) * LANES]
        out.append(xc * cos + pltpu.roll(xc, d, 1) * s_prev + pltpu.roll(xc, LANES - d, 1) * s_next)
    return out[0] if len(out) == 1 else jnp.concatenate(out, axis=-1)


def _inproj_kernel(*refs, rotate, mla_scale):
    if rotate:
        (x_ref, g_ref, sc_ref, sh_ref, w_ref, qn_ref, kvn_ref, wuq_ref, wuk_ref, wuv_ref, place_ref,
         c64_ref, p64_ref, n64_ref, cm_ref, pm_ref, nm_ref, ckr_ref, pkr_ref, nkr_ref,
         mq_ref, mk_ref, mv_ref, rq_ref, rk_ref, rv_ref, rg_ref,
         nq_ref, nk_ref, nv_ref, sq_ref, sk_ref, sv_ref) = refs
    else:
        (x_ref, g_ref, sc_ref, sh_ref, w_ref, qn_ref, kvn_ref, wuq_ref, wuk_ref, wuv_ref, place_ref,
         mq_ref, mk_ref, mv_ref, rq_ref, rk_ref, rv_ref, rg_ref,
         nq_ref, nk_ref, nv_ref, sq_ref, sk_ref, sv_ref) = refs

    x = x_ref[0]
    h = _rms(x) * (g_ref[...] * (1.0 + sc_ref[0])) + sh_ref[0]
    hb = h.astype(BF16)

    def proj(lo, hi):
        return _dot(hb, w_ref[:, lo:hi])

    def rope64(v):
        if not rotate:
            return v
        return _rope(v, c64_ref[...], p64_ref[...], n64_ref[...], HEAD_DIM // 4)

    def rope_mla(v):
        if not rotate:
            return v
        return _rope(v, cm_ref[...], pm_ref[...], nm_ref[...], MLA_ROPE // 4)

    cq = (_rms(proj(_O_CQ, _O_CKV)) * qn_ref[...]).astype(BF16)
    ckv_kr = proj(_O_CKV, _O_RQK)
    ckv = (_rms(ckv_kr[:, :MLA_KV_RANK]) * kvn_ref[...]).astype(BF16)
    kr = ckv_kr[:, MLA_KV_RANK:]
    if rotate:
        kr = _rope(kr, ckr_ref[...], pkr_ref[...], nkr_ref[...], MLA_ROPE // 4)
    kr = kr.astype(BF16)

    rqk = proj(_O_RQK, _O_RV)
    rq_ref[0] = rope64(rqk[:, :GROUP_W]).astype(BF16)
    rk_ref[0] = rope64(rqk[:, GROUP_W:]).astype(BF16)
    rv_ref[0] = proj(_O_RV, _O_RG).astype(BF16)
    rg_ref[0] = proj(_O_RG, _O_NA)

    na = proj(_O_NA, _O_SQK)
    nq_ref[0] = na[:, :GROUP_W].astype(BF16)
    nk_ref[0] = na[:, GROUP_W:2 * GROUP_W].astype(BF16)
    nv_ref[0] = na[:, 2 * GROUP_W:].astype(BF16)

    sqk = proj(_O_SQK, _O_SV)
    sq_ref[0] = rope64(sqk[:, :GROUP_W]).astype(BF16)
    sk_ref[0] = rope64(sqk[:, GROUP_W:]).astype(BF16)
    sv_ref[0] = proj(_O_SV, _IN_COLS).astype(BF16)

    q = rope_mla(_dot(cq, wuq_ref[...])) * mla_scale
    mq_ref[0] = q.astype(BF16)
    k = _dot(ckv, wuk_ref[...]) + _dot(kr, place_ref[...])
    mk_ref[0] = k.astype(BF16)
    v = _dot(ckv, wuv_ref[...])
    lane = lax.broadcasted_iota(jnp.int32, v.shape, 1)
    mv_ref[0] = jnp.where(lane % MLA_HEAD_PAD >= MLA_V, 1.0, v).astype(BF16)


def _const_spec(shape):
    nd = len(shape)
    return pl.BlockSpec(shape, lambda *_: (0,) * nd, pipeline_mode=pl.Buffered(1))


def _inproj(x, gain, scale, shift, w, qn, kvn, wuq, wuk, wuv, place, tables, tm):
    b, t, d = x.shape
    rotate = tables is not None
    kv_w = GROUP_W
    mla_w = N_HEADS * MLA_HEAD_PAD
    tok = lambda wd: pl.BlockSpec((1, tm, wd), lambda bi, i: (bi, i, 0))
    vec = pl.BlockSpec((1, 1, d), lambda bi, i: (bi, 0, 0))
    in_specs = [tok(d), _const_spec((1, d)), vec, vec, _const_spec(w.shape),
                _const_spec(qn.shape), _const_spec(kvn.shape), _const_spec(wuq.shape),
                _const_spec(wuk.shape), _const_spec(wuv.shape), _const_spec(place.shape)]
    args = [x, gain, scale, shift, w, qn, kvn, wuq, wuk, wuv, place]
    if rotate:
        in_specs += [pl.BlockSpec((tm, LANES), lambda bi, i: (i, 0))] * len(tables)
        args += list(tables)
    widths = [mla_w, mla_w, mla_w, GROUP_W, GROUP_W, GROUP_W, 2 * GROUP_W,
              GROUP_W, GROUP_W, GROUP_W, GROUP_W, kv_w, kv_w]
    dtypes = [BF16] * 6 + [F32] + [BF16] * 6
    return pl.pallas_call(
        functools.partial(_inproj_kernel, rotate=rotate, mla_scale=(MLA_NOPE + MLA_ROPE) ** -0.5 * LOG2_E),
        grid=(b, t // tm),
        in_specs=in_specs,
        out_specs=[tok(wd) for wd in widths],
        out_shape=[jax.ShapeDtypeStruct((b, t, wd), dt) for wd, dt in zip(widths, dtypes)],
        compiler_params=_cparams(("parallel", "parallel")),
        name="in_proj_rot" if rotate else "in_proj_ctx",
    )(*args)


def _mla_kernel(q_ref, kx_ref, vx_ref, ky_ref, vy_ref, o_ref, m_ref, acc_ref, sa_ref, sb_ref, sc_ref,
                ma_ref, mb_ref, mc_ref, *, tk, heads, n_tiles, n_sub, unroll):
    n_chunks = kx_ref.shape[1] // tk
    assert n_chunks % 2 == 0
    tq = q_ref.shape[1] // n_tiles
    ts = tq // n_sub
    chains = [(h, u) for h in range(heads) for u in range(n_sub)]

    def reset():
        for c in range(len(chains)):
            m_ref[c] = jnp.full(m_ref.shape[1:], NEG_INF, F32)
            acc_ref[c] = jnp.zeros(acc_ref.shape[1:], F32)

    def rows(h):
        return slice(h * MLA_HEAD_PAD, (h + 1) * MLA_HEAD_PAD)

    def vrows(h):
        return slice(h * MLA_HEAD_PAD, h * MLA_HEAD_PAD + MLA_V_ROWS)

    buf_a, buf_b, buf_c = (sa_ref, ma_ref), (sb_ref, mb_ref), (sc_ref, mc_ref)

    def scores(t, c, k, dst):
        h, u = chains[c]
        q0 = t * tq + u * ts
        st = _dot_nt(k, q_ref[0, q0:q0 + ts, rows(h)])
        dst[0][c] = st
        dst[1][c] = jnp.max(st, axis=0, keepdims=True)

    def absorb(c, src, vt):
        m_old = m_ref[c]
        m_new = jnp.maximum(m_old, src[1][c])
        pt = jnp.exp2(src[0][c] - m_new).astype(BF16)
        acc_ref[c, :MLA_V_ROWS] = acc_ref[c, :MLA_V_ROWS] * jnp.exp2(m_old - m_new) + _dot(vt, pt)
        m_ref[c] = m_new

    def kx(j, h):
        return kx_ref[0, pl.ds(pl.multiple_of(j * tk, tk), tk), rows(h)]

    def stage_at(t, j, parity):
        cur, nxt = (buf_a, buf_b) if parity == 0 else (buf_b, buf_a)
        for c, (h, _) in enumerate(chains):
            scores(t, c, kx(j + 1, h), nxt)
            absorb(c, cur, vx_ref[0, j, vrows(h), :])

    trips = (n_chunks - 1) // unroll
    reset()
    for c, (h, _) in enumerate(chains):
        scores(0, c, kx(0, h), buf_a)
    for t in range(n_tiles):
        def body(jj, carry):
            for i in range(unroll):
                stage_at(t, unroll * jj + i, i % 2)
            return carry

        lax.fori_loop(0, trips, body, 0)
        for j in range(unroll * trips, n_chunks - 1):
            stage_at(t, j, j % 2)
        for c, (h, _) in enumerate(chains):
            scores(t, c, ky_ref[0, :, rows(h)], buf_c)
            absorb(c, buf_b, vx_ref[0, n_chunks - 1, vrows(h), :])
        outs = [[None] * n_sub for _ in range(heads)]
        for c, (h, u) in enumerate(chains):
            if t + 1 < n_tiles:
                scores(t + 1, c, kx(0, h), buf_a)
            absorb(c, buf_c, vy_ref[0, 0, vrows(h), :])
            acc = acc_ref[c].T
            outs[h][u] = acc[:, :MLA_V] / acc[:, MLA_V:MLA_V + 1]
        o_ref[0, t * tq:(t + 1) * tq, :] = jnp.concatenate(
            [jnp.concatenate(outs[h], axis=0) for h in range(heads)], axis=-1).astype(o_ref.dtype)
        if t + 1 < n_tiles:
            reset()


def _mla_attention(q, kx, vx, ky, vy, tq, tk, n_tiles, n_sub, unroll):
    b, s, _ = q.shape
    l = ky.shape[1]
    hp = 2
    wd = hp * MLA_HEAD_PAD
    vxt = jnp.swapaxes(vx.reshape(b, s // tk, tk, -1), 2, 3)
    vyt = jnp.swapaxes(vy.reshape(b, 1, l, -1), 2, 3)
    ts = tq // n_tiles // n_sub
    return pl.pallas_call(
        functools.partial(_mla_kernel, tk=tk, heads=hp, n_tiles=n_tiles, n_sub=n_sub, unroll=unroll),
        grid=(b, N_HEADS // hp, s // tq),
        in_specs=[pl.BlockSpec((1, tq, wd), lambda bi, hi, i: (bi, i, hi)),
                  pl.BlockSpec((1, s, wd), lambda bi, hi, i: (bi, 0, hi)),
                  pl.BlockSpec((1, s // tk, wd, tk), lambda bi, hi, i: (bi, 0, hi, 0)),
                  pl.BlockSpec((1, l, wd), lambda bi, hi, i: (bi, 0, hi)),
                  pl.BlockSpec((1, 1, wd, l), lambda bi, hi, i: (bi, 0, hi, 0))],
        out_specs=pl.BlockSpec((1, tq, hp * MLA_V), lambda bi, hi, i: (bi, i, hi)),
        out_shape=jax.ShapeDtypeStruct((b, s, N_HEADS * MLA_V), BF16),
        scratch_shapes=[pltpu.VMEM((hp * n_sub, 1, ts), F32),
                        pltpu.VMEM((hp * n_sub, MLA_HEAD_PAD, ts), F32),
                        pltpu.VMEM((hp * n_sub, tk, ts), F32),
                        pltpu.VMEM((hp * n_sub, tk, ts), F32),
                        pltpu.VMEM((hp * n_sub, l, ts), F32)]
                       + [pltpu.VMEM((hp * n_sub, 1, ts), F32)] * 3,
        compiler_params=_cparams(("parallel", "parallel", "arbitrary")),
        name="mla_attention",
    )(q, kx, vxt, ky, vyt)


def _ctx_attn_kernel(sink_ref, q_ref, k_ref, v_ref, o_ref, *, heads, base2):
    exp = jnp.exp2 if base2 else jnp.exp
    outs = []
    for (q0, q1, k0, k1, v0, v1, sink_idx) in heads:
        q = q_ref[0, :, q0:q1]
        s = _dot_nt(q, k_ref[0, :, k0:k1])
        m = jnp.max(s, axis=-1, keepdims=True)
        if sink_idx is not None:
            sink = jnp.full((1, 1), sink_ref[sink_idx], F32)
            m = jnp.maximum(m, sink)
        p = exp(s - m)
        l = jnp.sum(p, axis=-1, keepdims=True)
        if sink_idx is not None:
            l = l + exp(sink - m)
        outs.append(_dot(p.astype(BF16), v_ref[0, :, v0:v1]) / l)
    o_ref[0] = jnp.concatenate(outs, axis=-1).astype(o_ref.dtype)


def _ctx_attention(q, k, v, heads, sink, name, base2=False):
    b, l, _ = q.shape
    full = lambda a: pl.BlockSpec((1, l, a.shape[-1]), lambda bi: (bi, 0, 0))
    return pl.pallas_call(
        functools.partial(_ctx_attn_kernel, heads=heads, base2=base2),
        grid=(b,),
        in_specs=[pl.BlockSpec(memory_space=pltpu.SMEM), full(q), full(k), full(v)],
        out_specs=pl.BlockSpec((1, l, GROUP_W), lambda bi: (bi, 0, 0)),
        out_shape=jax.ShapeDtypeStruct((b, l, GROUP_W), BF16),
        compiler_params=_cparams(("parallel",)),
        name=name,
    )(sink, q, k, v)


def _ret_kernel(*refs, chunk, n_chunks, direction, has_prev):
    if has_prev:
        (dec_ref, q_ref, k_ref, kt_ref, v_ref, g_ref, s0_ref, prev_ref, o_ref, sn_ref,
         st_ref, dm_ref, qd_ref, kdt_ref, cd_ref, ob_ref, qk_ref) = refs
    else:
        (dec_ref, q_ref, k_ref, kt_ref, v_ref, g_ref, s0_ref, o_ref, sn_ref,
         st_ref, dm_ref, qd_ref, kdt_ref, cd_ref, ob_ref, qk_ref) = refs
        prev_ref = None
    i = pl.program_id(0)
    c = chunk
    fwd = direction == 0
    batch = range(q_ref.shape[0])

    def head_of(shape, axis):
        return lax.broadcasted_iota(jnp.int32, shape, axis) // HEAD_DIM

    @pl.when(i == 0)
    def _init():
        st_ref[...] = s0_ref[...]
        ii = lax.broadcasted_iota(jnp.int32, (c, c), 0).astype(F32)
        jj = lax.broadcasted_iota(jnp.int32, (c, c), 1).astype(F32)
        diff = (ii - jj) if fwd else (jj - ii)
        pos = lax.broadcasted_iota(jnp.int32, (c, GROUP_W), 0).astype(F32)
        pos_t = lax.broadcasted_iota(jnp.int32, (GROUP_W, c), 1).astype(F32)
        q_steps = (pos + 1.0) if fwd else (c - pos)
        k_steps = (c - 1.0 - pos_t) if fwd else pos_t
        qd = jnp.zeros((c, GROUP_W), F32)
        kdt = jnp.zeros((GROUP_W, c), F32)
        cd = jnp.zeros((GROUP_W, GROUP_W), F32)
        for h in range(N_HEADS):
            dec = dec_ref[direction * N_HEADS + h]
            lg = jax.nn.log_sigmoid(jnp.full((c, c), dec, F32))
            dm_ref[h * c:(h + 1) * c] = jnp.where(diff >= 0, jnp.exp(lg * jnp.maximum(diff, 0.0)), 0.0)
            lgq = jax.nn.log_sigmoid(jnp.full((c, GROUP_W), dec, F32))
            qd = jnp.where(head_of((c, GROUP_W), 1) == h, jnp.exp(lgq * q_steps), qd)
            lgk = jax.nn.log_sigmoid(jnp.full((GROUP_W, c), dec, F32))
            kdt = jnp.where(head_of((GROUP_W, c), 0) == h, jnp.exp(lgk * k_steps), kdt)
            lgc = jax.nn.log_sigmoid(jnp.full((GROUP_W, GROUP_W), dec, F32))
            cd = jnp.where(head_of((GROUP_W, GROUP_W), 0) == h, jnp.exp(lgc * c), cd)
        qd_ref[...] = qd
        kdt_ref[...] = kdt
        cd_ref[...] = cd

    in_head = [head_of((c, GROUP_W), 1) == h for h in range(N_HEADS)]
    head_mask = [jnp.where(mk, 1.0, 0.0).astype(BF16) for mk in in_head]
    same_head = head_of((GROUP_W, GROUP_W), 0) == head_of((GROUP_W, GROUP_W), 1)

    def chunk_off(n):
        idx = n if fwd else n_chunks - 1 - n
        return pl.multiple_of(idx * c, c)

    def qk(bb, n):
        off = chunk_off(n)
        q_all = q_ref[bb, pl.ds(off, c), :]
        q4 = jnp.concatenate([q_all * head_mask[h] for h in range(N_HEADS)], axis=0)
        return _dot_nt(q4, k_ref[bb, pl.ds(off, c), :])

    def gate_and_store(bb, n):
        off = chunk_off(n)
        o = ob_ref[bb]
        oo = o * o
        ms = jnp.zeros((c, GROUP_W), F32)
        for h in range(N_HEADS):
            ms_h = jnp.sum(jnp.where(in_head[h], oo, 0.0), axis=-1, keepdims=True) * (1.0 / HEAD_DIM)
            ms = jnp.where(in_head[h], ms_h, ms)
        res = o * lax.rsqrt(ms + EPS) * _silu(g_ref[bb, pl.ds(off, c), :])
        if has_prev:
            res = res + prev_ref[bb, pl.ds(off, c), :]
        o_ref[bb, pl.ds(off, c), :] = res.astype(o_ref.dtype)

    ob_ref[...] = jnp.zeros(ob_ref.shape, F32)

    for bb in batch:
        qk_ref[bb] = qk(bb, 0)

    def body(n, carry):
        for bb in batch:
            gate_and_store(bb, jnp.maximum(n - 1, 0))
        off = chunk_off(n)
        atts = [(qk_ref[bb] * dm_ref[...]).astype(BF16) for bb in batch]
        for bb in batch:
            qk_ref[bb] = qk(bb, jnp.minimum(n + 1, n_chunks - 1))
        for bb in batch:
            v = v_ref[bb, pl.ds(off, c), :]
            att = atts[bb]
            intra4 = _dot(att, v)
            intra = intra4[:c]
            for h in range(1, N_HEADS):
                intra = jnp.where(in_head[h], intra4[h * c:(h + 1) * c], intra)
            state = st_ref[bb]
            ob_ref[bb] = intra + _dot(q_ref[bb, pl.ds(off, c), :], state.astype(BF16)) * qd_ref[...]
            kk = (kt_ref[bb, :, pl.ds(off, c)].astype(F32) * kdt_ref[...]).astype(BF16)
            st_ref[bb] = state * cd_ref[...] + jnp.where(same_head, _dot(kk, v), 0.0)
        return carry

    lax.fori_loop(0, n_chunks, body, 0)
    for bb in batch:
        gate_and_store(bb, n_chunks - 1)

    @pl.when(i == pl.num_programs(0) - 1)
    def _fin():
        sn_ref[...] = st_ref[...]


def _retention_pass(dec, q, k, v, gates, state0, prev, direction, chunk, n_chunks, out_dtype):
    b, t, _ = q.shape
    tb = chunk * n_chunks
    n = t // tb
    blk = (lambda i: i) if direction == 0 else (lambda i: n - 1 - i)
    tok = pl.BlockSpec((b, tb, GROUP_W), lambda i: (0, blk(i), 0))
    tok_t = pl.BlockSpec((b, GROUP_W, tb), lambda i: (0, 0, blk(i)))
    gate = pl.BlockSpec((b, tb, GROUP_W), lambda i: (0, blk(i), direction))
    st_spec = pl.BlockSpec((b, GROUP_W, GROUP_W), lambda i: (0, 0, 0))
    in_specs = [pl.BlockSpec(memory_space=pltpu.SMEM), tok, tok, tok_t, tok, gate, st_spec]
    args = [dec, q, k, jnp.swapaxes(k, 1, 2), v, gates, state0]
    if prev is not None:
        in_specs.append(tok)
        args.append(prev)
    return pl.pallas_call(
        functools.partial(_ret_kernel, chunk=chunk, n_chunks=n_chunks, direction=direction,
                          has_prev=prev is not None),
        grid=(n,),
        in_specs=in_specs,
        out_specs=[tok, st_spec],
        out_shape=[jax.ShapeDtypeStruct((b, t, GROUP_W), out_dtype),
                   jax.ShapeDtypeStruct((b, GROUP_W, GROUP_W), F32)],
        scratch_shapes=[pltpu.VMEM((b, GROUP_W, GROUP_W), F32),
                        pltpu.VMEM((N_HEADS * chunk, chunk), F32),
                        pltpu.VMEM((chunk, GROUP_W), F32),
                        pltpu.VMEM((GROUP_W, chunk), F32),
                        pltpu.VMEM((GROUP_W, GROUP_W), F32),
                        pltpu.VMEM((b, chunk, GROUP_W), F32),
                        pltpu.VMEM((b, N_HEADS * chunk, chunk), F32)],
        compiler_params=_cparams(("arbitrary",)),
        name="retention_fwd" if direction == 0 else "retention_bwd",
    )(*args)


def _retention(dec, xq, xk, xv, xg, yq, yk, yv, yg, chunk, chunks_per_step):
    b = xq.shape[0]
    zero = jnp.zeros((b, GROUP_W, GROUP_W), F32)
    ny = yq.shape[1] // chunk
    yb, sb = _retention_pass(dec, yq, yk, yv, yg, zero, None, 1, chunk, ny, F32)
    y, sf = _retention_pass(dec, yq, yk, yv, yg, zero, yb, 0, chunk, ny, BF16)
    xb, _ = _retention_pass(dec, xq, xk, xv, xg, sb, None, 1, chunk, chunks_per_step, F32)
    x, _ = _retention_pass(dec, xq, xk, xv, xg, sf, xb, 0, chunk, chunks_per_step, BF16)
    return x, y


def _na_bias_kernel(rpb_ref, o_ref):
    h = pl.program_id(0)
    dr0 = pl.program_id(1)
    c = lax.broadcasted_iota(jnp.int32, (GRID_W, GRID_W), 0)
    kc = lax.broadcasted_iota(jnp.int32, (GRID_W, GRID_W), 1)
    c0 = jnp.clip(c - NA_KC // 2, 0, GRID_W - NA_KC)
    col_in = (kc >= c0) & (kc < c0 + NA_KC)
    dc = jnp.clip(kc - c, -(NA_KC - 1), NA_KC - 1) + NA_KC - 1
    n_dc = 2 * NA_KC - 1
    accs = [jnp.zeros((GRID_W, GRID_W), F32) for _ in range(NA_KR)]
    for d in range(n_dc):
        at_d = dc == d
        for j in range(NA_KR):
            accs[j] = jnp.where(at_d, rpb_ref[(h * (2 * NA_KR - 1) + dr0 + j) * n_dc + d], accs[j])
    for j in range(NA_KR):
        o_ref[0, :, j * GRID_W:(j + 1) * GRID_W] = jnp.where(col_in, accs[j], NEG_INF)


def _na_bias_table(rpb):
    return pl.pallas_call(
        _na_bias_kernel,
        grid=(N_HEADS, NA_KR),
        in_specs=[pl.BlockSpec(memory_space=pltpu.SMEM)],
        out_specs=pl.BlockSpec((1, GRID_W, NA_KR * GRID_W), lambda h, r: (r, h, 0)),
        out_shape=jax.ShapeDtypeStruct((NA_KR, N_HEADS * GRID_W, NA_KR * GRID_W), F32),
        compiler_params=_cparams(("parallel", "parallel")),
        name="na_bias_table",
    )(rpb.reshape(-1))


def _na_kernel(q_ref, k_ref, v_ref, ky_ref, vy_ref, tb_ref, o_ref,
               sa_ref, sb_ref, pa_ref, pb_ref, la_ref, lb_ref, *, rows_per_step, n_rows):
    r_base = pl.program_id(1) * rows_per_step
    win = NA_KR * GRID_W

    head_of_lane = lax.broadcasted_iota(jnp.int32, (GRID_W, GROUP_W), 1) // HEAD_DIM
    in_head = [head_of_lane == h for h in range(N_HEADS)]
    head_mask = [jnp.where(mk, 1.0, 0.0).astype(BF16) for mk in in_head]

    def geometry(i):
        r = r_base + i
        r0 = jnp.clip(r - NA_KR // 2, 0, n_rows - NA_KR)
        return pl.multiple_of(r0 * GRID_W, GRID_W), r0 - r + NA_KR - 1

    def row_slice(i):
        return pl.ds(pl.multiple_of(i * GRID_W, GRID_W), GRID_W)

    def scores(i, s_ref):
        koff, dr0 = geometry(i)
        q_all = q_ref[0, row_slice(i), :]
        q4 = jnp.concatenate([q_all * head_mask[h] for h in range(N_HEADS)], axis=0)
        s_ref[:, :win] = _dot_nt(q4, k_ref[0, pl.ds(koff, win), :]) + tb_ref[dr0]
        s_ref[:, win:] = _dot_nt(q4, ky_ref[0])

    def softmax(s_ref, p_ref, l_ref):
        s = s_ref[...]
        p = jnp.exp(s - _row_max(s))
        l_ref[...] = 1.0 / _row_sum(p)
        p_ref[...] = p.astype(BF16)

    def values(i, p_ref, l_ref):
        koff, _ = geometry(i)
        o4 = (_dot(p_ref[:, :win], v_ref[0, pl.ds(koff, win), :])
              + _dot(p_ref[:, win:], vy_ref[0])) * l_ref[...]
        out = o4[:GRID_W]
        for h in range(1, N_HEADS):
            out = jnp.where(in_head[h], o4[h * GRID_W:(h + 1) * GRID_W], out)
        o_ref[0, row_slice(i), :] = out.astype(o_ref.dtype)

    pb_ref[...] = jnp.zeros(pb_ref.shape, BF16)
    lb_ref[...] = jnp.zeros(lb_ref.shape, F32)
    scores(0, sa_ref)

    def pair_body(tt, carry):
        t = 2 * tt
        values(jnp.maximum(t - 1, 0), pb_ref, lb_ref)
        scores(t + 1, sb_ref)
        softmax(sa_ref, pa_ref, la_ref)
        values(t, pa_ref, la_ref)
        scores(jnp.minimum(t + 2, rows_per_step - 1), sa_ref)
        softmax(sb_ref, pb_ref, lb_ref)
        return carry

    lax.fori_loop(0, rows_per_step // 2, pair_body, 0)
    values(rows_per_step - 1, pb_ref, lb_ref)


def _na_attention(q, k, v, ky, vy, table, rows_per_step):
    b, s, _ = q.shape
    l = ky.shape[1]
    n_rows = s // GRID_W
    tq = rows_per_step * GRID_W
    n_keys = NA_KR * GRID_W + l
    seq = lambda n: pl.BlockSpec((1, n, GROUP_W), lambda bi, i: (bi, 0, 0))
    return pl.pallas_call(
        functools.partial(_na_kernel, rows_per_step=rows_per_step, n_rows=n_rows),
        grid=(b, n_rows // rows_per_step),
        in_specs=[pl.BlockSpec((1, tq, GROUP_W), lambda bi, i: (bi, i, 0)),
                  seq(s), seq(s), seq(l), seq(l), _const_spec(table.shape)],
        out_specs=pl.BlockSpec((1, tq, GROUP_W), lambda bi, i: (bi, i, 0)),
        out_shape=jax.ShapeDtypeStruct((b, s, GROUP_W), BF16),
        scratch_shapes=([pltpu.VMEM((N_HEADS * GRID_W, n_keys), F32)] * 2
                        + [pltpu.VMEM((N_HEADS * GRID_W, n_keys), BF16)] * 2
                        + [pltpu.VMEM((N_HEADS * GRID_W, 1), F32)] * 2),
        compiler_params=_cparams(("parallel", "arbitrary")),
        name="na_attention",
    )(q, k, v, ky, vy, table)


def _swa_kernel(sink_ref, q_ref, kp_ref, kc_ref, kn_ref, vp_ref, vc_ref, vn_ref, ky_ref, vy_ref, o_ref,
                kw_ref, vw_ref, wb_ref, sa_ref, sb_ref, pa_ref, pb_ref, la_ref, lb_ref, *, blocks_per_step):
    step = pl.program_id(1)
    nb = pl.num_programs(1) * blocks_per_step
    bl = SWA_BLOCK
    g = N_HEADS // SWA_KV_HEADS
    tq = blocks_per_step * bl
    kw_ref[0:bl] = kp_ref[0]
    kw_ref[bl:bl + tq] = kc_ref[0]
    kw_ref[bl + tq:] = kn_ref[0]
    vw_ref[0:bl] = vp_ref[0]
    vw_ref[bl:bl + tq] = vc_ref[0]
    vw_ref[bl + tq:] = vn_ref[0]

    qi = lax.broadcasted_iota(jnp.int32, (g * bl, 3 * bl), 0) % bl
    jk = lax.broadcasted_iota(jnp.int32, (g * bl, 3 * bl), 1)
    wb_ref[...] = jnp.where(jnp.abs(jk - bl - qi) <= SWA_WINDOW, 0.0, NEG_INF)
    half = lax.broadcasted_iota(jnp.int32, (g * bl, 1), 0) // bl

    head_of_lane = lax.broadcasted_iota(jnp.int32, (bl, GROUP_W), 1) // HEAD_DIM
    in_head = [head_of_lane == h for h in range(N_HEADS)]
    head_mask = [jnp.where(mk, 1.0, 0.0).astype(BF16) for mk in in_head]

    def blk_slice(j):
        return pl.ds(pl.multiple_of(j * bl, bl), bl)

    def scores(j, s_ref):
        qoff = pl.multiple_of(j * bl, bl)
        q_all = q_ref[0, blk_slice(j), :]
        kw = kw_ref[pl.ds(qoff, 3 * bl), :]
        n = step * blocks_per_step + j
        lo_edge = jnp.where(n == 0, NEG_INF, 0.0)
        hi_edge = jnp.where(n == nb - 1, NEG_INF, 0.0)
        for kh in range(SWA_KV_HEADS):
            q = jnp.concatenate([q_all * head_mask[kh * g + gi] for gi in range(g)], axis=0)
            s = _dot_nt(q, kw) + wb_ref[...]
            s_ref[kh, :, :bl] = s[:, :bl] + lo_edge
            s_ref[kh, :, bl:2 * bl] = s[:, bl:2 * bl]
            s_ref[kh, :, 2 * bl:3 * bl] = s[:, 2 * bl:] + hi_edge
            s_ref[kh, :, 3 * bl:] = _dot_nt(q, ky_ref[0])

    def softmax(s_ref, p_ref, l_ref):
        for kh in range(SWA_KV_HEADS):
            s = s_ref[kh]
            sink = jnp.full((g * bl, 1), sink_ref[kh * g], F32)
            for gi in range(1, g):
                sink = jnp.where(half == gi, sink_ref[kh * g + gi], sink)
            m = jnp.maximum(_row_max(s), sink)
            p = jnp.exp(s - m)
            l_ref[kh] = 1.0 / (_row_sum(p) + jnp.exp(sink - m))
            p_ref[kh] = p.astype(BF16)

    def values(j, p_ref, l_ref):
        vw = vw_ref[pl.ds(pl.multiple_of(j * bl, bl), 3 * bl), :]
        out = jnp.zeros((bl, GROUP_W), F32)
        for kh in range(SWA_KV_HEADS):
            o = (_dot(p_ref[kh, :, :3 * bl], vw) + _dot(p_ref[kh, :, 3 * bl:], vy_ref[0])) * l_ref[kh]
            for gi in range(g):
                out = jnp.where(in_head[kh * g + gi], o[gi * bl:(gi + 1) * bl], out)
        o_ref[0, blk_slice(j), :] = out.astype(o_ref.dtype)

    pb_ref[...] = jnp.zeros(pb_ref.shape, BF16)
    lb_ref[...] = jnp.zeros(lb_ref.shape, F32)
    scores(0, sa_ref)

    def pair_body(tt, carry):
        t = 2 * tt
        values(jnp.maximum(t - 1, 0), pb_ref, lb_ref)
        scores(t + 1, sb_ref)
        softmax(sa_ref, pa_ref, la_ref)
        values(t, pa_ref, la_ref)
        scores(jnp.minimum(t + 2, blocks_per_step - 1), sa_ref)
        softmax(sb_ref, pb_ref, lb_ref)
        return carry

    lax.fori_loop(0, blocks_per_step // 2, pair_body, 0)
    values(blocks_per_step - 1, pb_ref, lb_ref)


def _swa_attention(sink, q, k, v, ky, vy, blocks_per_step):
    b, s, _ = q.shape
    l = ky.shape[1]
    tq = blocks_per_step * SWA_BLOCK
    n_steps = s // tq
    nb = s // SWA_BLOCK
    kvw = GROUP_W
    q_rows = N_HEADS // SWA_KV_HEADS * SWA_BLOCK
    n_keys = 3 * SWA_BLOCK + l
    prev = pl.BlockSpec((1, SWA_BLOCK, kvw), lambda bi, i: (bi, jnp.maximum(i * blocks_per_step - 1, 0), 0))
    cur = pl.BlockSpec((1, tq, kvw), lambda bi, i: (bi, i, 0))
    nxt = pl.BlockSpec((1, SWA_BLOCK, kvw), lambda bi, i: (bi, jnp.minimum((i + 1) * blocks_per_step, nb - 1), 0))
    ctx = pl.BlockSpec((1, l, kvw), lambda bi, i: (bi, 0, 0))
    return pl.pallas_call(
        functools.partial(_swa_kernel, blocks_per_step=blocks_per_step),
        grid=(b, n_steps),
        in_specs=[pl.BlockSpec(memory_space=pltpu.SMEM),
                  pl.BlockSpec((1, tq, GROUP_W), lambda bi, i: (bi, i, 0)),
                  prev, cur, nxt, prev, cur, nxt, ctx, ctx],
        out_specs=pl.BlockSpec((1, tq, GROUP_W), lambda bi, i: (bi, i, 0)),
        out_shape=jax.ShapeDtypeStruct((b, s, GROUP_W), BF16),
        scratch_shapes=[pltpu.VMEM((tq + 2 * SWA_BLOCK, kvw), BF16),
                        pltpu.VMEM((tq + 2 * SWA_BLOCK, kvw), BF16),
                        pltpu.VMEM((q_rows, 3 * SWA_BLOCK), F32)]
                       + [pltpu.VMEM((SWA_KV_HEADS, q_rows, n_keys), F32)] * 2
                       + [pltpu.VMEM((SWA_KV_HEADS, q_rows, n_keys), BF16)] * 2
                       + [pltpu.VMEM((SWA_KV_HEADS, q_rows, 1), F32)] * 2,
        compiler_params=_cparams(("parallel", "arbitrary")),
        name="swa_attention",
    )(sink, q, k, k, k, v, v, v, ky, vy)


def _outffn_kernel(*refs, hidden_chunk, final):
    if final:
        (x_ref, m0_ref, m1_ref, m2_ref, m3_ref, wo_ref, g1_ref, n2_ref, sc_ref, sh_ref, g2_ref,
         w1_ref, w3_ref, w2_ref, fg_ref, o_ref) = refs
    else:
        (x_ref, m0_ref, m1_ref, m2_ref, m3_ref, wo_ref, g1_ref, n2_ref, sc_ref, sh_ref, g2_ref,
         w1_ref, w3_ref, w2_ref, o_ref) = refs
    mix = None
    for gi, m_ref in enumerate((m0_ref, m1_ref, m2_ref, m3_ref)):
        part = _dot(m_ref[0], wo_ref[gi * GROUP_W:(gi + 1) * GROUP_W, :])
        mix = part if mix is None else mix + part
    x1 = x_ref[0] + g1_ref[0] * mix
    hb = (_rms(x1) * (n2_ref[...] * (1.0 + sc_ref[0])) + sh_ref[0]).astype(BF16)
    hidden = w1_ref.shape[1]
    acc = None
    for c0 in range(0, hidden, hidden_chunk):
        a = _dot(hb, w1_ref[:, c0:c0 + hidden_chunk])
        bgate = _dot(hb, w3_ref[:, c0:c0 + hidden_chunk])
        u = (_silu(a) * bgate).astype(BF16)
        part = _dot(u, w2_ref[c0:c0 + hidden_chunk, :])
        acc = part if acc is None else acc + part
    x2 = x1 + g2_ref[0] * acc
    if final:
        x2 = _rms(x2) * fg_ref[...]
    o_ref[0] = x2


def _outffn(x, mixes, wo, g1, n2, sc2, sh2, g2, w1, w3, w2, final_g, tm):
    b, t, d = x.shape
    tok = lambda wd: pl.BlockSpec((1, tm, wd), lambda bi, i: (bi, i, 0))
    vec = pl.BlockSpec((1, 1, d), lambda bi, i: (bi, 0, 0))
    in_specs = ([tok(d)] + [tok(GROUP_W)] * 4
                + [_const_spec(wo.shape), vec, _const_spec((1, d)), vec, vec, vec,
                   _const_spec(w1.shape), _const_spec(w3.shape), _const_spec(w2.shape)])
    args = [x, *mixes, wo, g1, n2, sc2, sh2, g2, w1, w3, w2]
    final = final_g is not None
    if final:
        in_specs.append(_const_spec((1, d)))
        args.append(final_g)
    return pl.pallas_call(
        functools.partial(_outffn_kernel, hidden_chunk=256, final=final),
        grid=(b, t // tm),
        in_specs=in_specs,
        out_specs=tok(d),
        out_shape=jax.ShapeDtypeStruct((b, t, d), F32),
        compiler_params=_cparams(("parallel", "parallel")),
        name="out_proj_ffn_final" if final else "out_proj_ffn",
    )(*args)


def _prep_weights(w_in, mla_w_uq, mla_w_ukv):
    depth, d, _ = w_in.shape
    offs = [0]
    for sz in IN_SIZES:
        offs.append(offs[-1] + sz)
    w_bf = w_in.astype(BF16)
    cols = [w_bf[:, :, offs[i]:offs[i + 1]] for i in range(len(IN_SIZES))]
    cq, ckv, kr, rq, rk, rv, rgf, rgb, nq, nk, nv, sq, sk, sv = cols
    scale = HEAD_DIM ** -0.5
    kr_slot = jnp.concatenate([kr, jnp.zeros((depth, d, LANES - MLA_ROPE), BF16)], axis=-1)
    src = jnp.arange(LANES)[:, None]
    dst = jnp.arange(N_HEADS * MLA_HEAD_PAD)[None, :]
    place = ((src < MLA_ROPE) & (dst % MLA_HEAD_PAD == src + MLA_NOPE)).astype(BF16)

    def per_query_head(t):
        g = N_HEADS // SWA_KV_HEADS
        t = t.reshape(depth, d, SWA_KV_HEADS, 1, HEAD_DIM)
        return jnp.broadcast_to(t, (depth, d, SWA_KV_HEADS, g, HEAD_DIM)).reshape(depth, d, GROUP_W)

    w = jnp.concatenate([cq, ckv, kr_slot, rq, rk * scale, rv, rgf, rgb, nq * scale, nk, nv,
                         sq * scale, per_query_head(sk), per_query_head(sv)], axis=-1).astype(BF16)

    qr = mla_w_uq.shape[1]
    uq = mla_w_uq.reshape(depth, qr, N_HEADS, MLA_NOPE + MLA_ROPE)
    wuq = jnp.concatenate([uq, jnp.zeros((depth, qr, N_HEADS, MLA_HEAD_PAD - MLA_NOPE - MLA_ROPE), F32)],
                          axis=-1).reshape(depth, qr, N_HEADS * MLA_HEAD_PAD).astype(BF16)
    kvr = mla_w_ukv.shape[1]
    ukv = mla_w_ukv.reshape(depth, kvr, N_HEADS, MLA_NOPE + MLA_V)
    zk = jnp.zeros((depth, kvr, N_HEADS, MLA_HEAD_PAD - MLA_NOPE), F32)
    wuk = jnp.concatenate([ukv[..., :MLA_NOPE], zk], axis=-1).reshape(depth, kvr, -1).astype(BF16)
    zv = jnp.zeros((depth, kvr, N_HEADS, MLA_HEAD_PAD - MLA_V), F32)
    wuv = jnp.concatenate([ukv[..., MLA_NOPE:], zv], axis=-1).reshape(depth, kvr, -1).astype(BF16)
    return w, wuq, wuk, wuv, place


def _ctx_head_specs():
    mla = tuple((h * MLA_HEAD_PAD, (h + 1) * MLA_HEAD_PAD, h * MLA_HEAD_PAD, (h + 1) * MLA_HEAD_PAD,
                 h * MLA_HEAD_PAD, h * MLA_HEAD_PAD + MLA_V, None) for h in range(N_HEADS))
    na = tuple((h * HEAD_DIM, (h + 1) * HEAD_DIM) * 3 + (None,) for h in range(N_HEADS))
    swa = tuple((h * HEAD_DIM, (h + 1) * HEAD_DIM) * 3 + (h,) for h in range(N_HEADS))
    return mla, na, swa


def kernel(x, c, ctx, c_ctx, ada_w, ada_b, norm1_g, w_in, mla_q_norm, mla_w_uq, mla_kv_norm, mla_w_ukv,
           ret_decay, na_rpb, swa_sink, w_out, norm2_g, ffn_w1, ffn_w3, ffn_w2, final_norm_g):
    b, s, d = x.shape
    l_ctx = ctx.shape[1]
    depth = ada_w.shape[0]
    assert b + 1 <= 8 and s % 2048 == 0 and l_ctx % 128 == 0

    cond = jnp.concatenate([c, c_ctx[None, :], jnp.zeros((8 - b - 1, d), F32)], axis=0)
    mod = _modulation(cond, ada_w, ada_b)
    tables = _rope_tables(s)
    w_all, wuq_all, wuk_all, wuv_all, place = _prep_weights(w_in, mla_w_uq, mla_w_ukv)
    wo_all = w_out.astype(BF16)
    w1_all, w3_all, w2_all = ffn_w1.astype(BF16), ffn_w3.astype(BF16), ffn_w2.astype(BF16)
    mla_heads, na_heads, swa_heads = _ctx_head_specs()
    no_sink = jnp.zeros((N_HEADS,), F32)

    tm_x = 512
    tm_y = min(256, l_ctx)
    y = ctx
    for l in range(depth):
        mx = [mod[l, :b, j * d:(j + 1) * d][:, None, :] for j in range(6)]
        my = [jnp.broadcast_to(mod[l, b, j * d:(j + 1) * d][None, None, :], (b, 1, d)) for j in range(6)]
        n1 = norm1_g[l][None, :]
        n2 = norm2_g[l][None, :]
        qn = mla_q_norm[l][None, :]
        kvn = mla_kv_norm[l][None, :]
        lw = (w_all[l], qn, kvn, wuq_all[l], wuk_all[l], wuv_all[l], place)

        px = _inproj(x, n1, mx[1], mx[0], *lw, tables, tm_x)
        py = _inproj(y, n1, my[1], my[0], *lw, None, tm_y)
        (xmq, xmk, xmv, xrq, xrk, xrv, xrg, xnq, xnk, xnv, xsq, xsk, xsv) = px
        (ymq, ymk, ymv, yrq, yrk, yrv, yrg, ynq, ynk, ynv, ysq, ysk, ysv) = py

        mla_x = _mla_attention(xmq, xmk, xmv, ymk, ymv, tq=1024, tk=512, n_tiles=2, n_sub=2, unroll=8)
        dec = ret_decay[l].reshape(-1)
        ret_x, ret_y = _retention(dec, xrq, xrk, xrv, xrg, yrq, yrk, yrv, yrg, chunk=128, chunks_per_step=8)
        table = _na_bias_table(na_rpb[l])
        na_x = _na_attention(xnq, xnk, xnv, ynk, ynv, table, rows_per_step=32)
        swa_x = _swa_attention(swa_sink[l], xsq, xsk, xsv, ysk, ysv, blocks_per_step=16)

        last = l == depth - 1
        x = _outffn(x, (mla_x, ret_x, na_x, swa_x), wo_all[l], mx[2], n2, mx[4], mx[3], mx[5],
                    w1_all[l], w3_all[l], w2_all[l], final_norm_g[None, :] if last else None, tm_x)
        if not last:
            mla_y = _ctx_attention(ymq, ymk, ymv, mla_heads, no_sink, "mla_ctx_attention", base2=True)
            na_y = _ctx_attention(ynq, ynk, ynv, na_heads, no_sink, "na_ctx_attention")
            swa_y = _ctx_attention(ysq, ysk, ysv, swa_heads, swa_sink[l], "swa_ctx_attention")
            y = _outffn(y, (mla_y, ret_y, na_y, swa_y), wo_all[l], my[2], n2, my[4], my[3], my[5],
                        w1_all[l], w3_all[l], w2_all[l], None, tm_y)
    return x
```

```python
import functools

import jax
import jax.numpy as jnp
from jax import lax
from jax.experimental import pallas as pl
from jax.experimental.pallas import tpu as pltpu

F32 = jnp.float32
BF16 = jnp.bfloat16

GRID_W = 64
HEAD_DIM = 64
N_HEADS = 4
GROUP_W = N_HEADS * HEAD_DIM
MLA_Q_RANK = 256
MLA_KV_RANK = 128
MLA_NOPE = 64
MLA_ROPE = 32
MLA_V = 64
MLA_HEAD_PAD = 128
MLA_V_ROWS = MLA_V + 16
NA_KR = 8
NA_KC = 16
SWA_KV_HEADS = 2
SWA_WINDOW = 128
SWA_BLOCK = 128
ROPE_THETA = 10000.0
EPS = 1e-6
NEG_INF = -1e30
LOG2_E = 1.4426950408889634
LANES = 128
VMEM_LIMIT = 56 * 1024 * 1024

IN_SIZES = (MLA_Q_RANK, MLA_KV_RANK, MLA_ROPE,
            GROUP_W, GROUP_W, GROUP_W, GROUP_W, GROUP_W,
            GROUP_W, GROUP_W, GROUP_W,
            GROUP_W, SWA_KV_HEADS * HEAD_DIM, SWA_KV_HEADS * HEAD_DIM)

_O_CQ = 0
_O_CKV = _O_CQ + MLA_Q_RANK
_O_KR = _O_CKV + MLA_KV_RANK
_O_RQK = _O_KR + LANES
_O_RV = _O_RQK + 2 * GROUP_W
_O_RG = _O_RV + GROUP_W
_O_NA = _O_RG + 2 * GROUP_W
_O_SQK = _O_NA + 3 * GROUP_W
_O_SV = _O_SQK + 2 * GROUP_W
_IN_COLS = _O_SV + GROUP_W


def _cparams(sem):
    return pltpu.CompilerParams(dimension_semantics=sem, vmem_limit_bytes=VMEM_LIMIT)


def _dot(a, b):
    return jnp.dot(a, b, preferred_element_type=F32)


def _dot_nt(a, b):
    return lax.dot_general(a, b, (((1,), (1,)), ((), ())), preferred_element_type=F32)


def _rms(x):
    return x * lax.rsqrt(jnp.mean(x * x, axis=-1, keepdims=True) + EPS)


def _silu(x):
    return x * jax.nn.sigmoid(x)


def _lane_chunks(arrays):
    return [a[:, j * LANES:(j + 1) * LANES] for a in arrays for j in range(a.shape[-1] // LANES)]


def _row_max(*arrays):
    return jnp.max(functools.reduce(jnp.maximum, _lane_chunks(arrays)), axis=-1, keepdims=True)


def _row_sum(*arrays):
    return jnp.sum(functools.reduce(jnp.add, _lane_chunks(arrays)), axis=-1, keepdims=True)


def _mod_kernel(c_ref, w_ref, b_ref, o_ref):
    o_ref[0] = _dot(_silu(c_ref[...]), w_ref[0]) + b_ref[0]


def _modulation(cond, ada_w, ada_b):
    depth, d, d6 = ada_w.shape
    n = d6 // d
    return pl.pallas_call(
        _mod_kernel,
        grid=(depth, n),
        in_specs=[pl.BlockSpec((8, d), lambda l, j: (0, 0)),
                  pl.BlockSpec((1, d, d), lambda l, j: (l, 0, j)),
                  pl.BlockSpec((1, 1, d), lambda l, j: (l, 0, j))],
        out_specs=pl.BlockSpec((1, 8, d), lambda l, j: (l, 0, j)),
        out_shape=jax.ShapeDtypeStruct((depth, 8, d6), F32),
        compiler_params=_cparams(("parallel", "parallel")),
        name="ada_modulation",
    )(cond, ada_w, ada_b.reshape(depth, 1, d6))


def _rope_tables(seq):
    n_rows = seq // GRID_W

    def parts(pos, d):
        inv = ROPE_THETA ** (-jnp.arange(0, d, 2, dtype=F32) / d)
        ang = pos.astype(F32)[:, None] * inv[None, :]
        z = jnp.zeros_like(ang)
        return (jnp.concatenate([jnp.cos(ang), jnp.cos(ang)], axis=-1),
                jnp.concatenate([z, jnp.sin(ang)], axis=-1),
                jnp.concatenate([-jnp.sin(ang), z], axis=-1))

    def expand(by_row, by_col):
        w = by_row.shape[-1]
        r = jnp.broadcast_to(by_row[:, None, :], (n_rows, GRID_W, w))
        c = jnp.broadcast_to(by_col[None, :, :], (n_rows, GRID_W, w))
        return jnp.concatenate([r, c], axis=-1).reshape(seq, 2 * w)

    def tables(d, fill):
        per_head = [expand(a, b) for a, b in zip(parts(jnp.arange(n_rows), d), parts(jnp.arange(GRID_W), d))]
        return [fill(t, i) for i, t in enumerate(per_head)]

    def two_heads(t, _):
        return jnp.concatenate([t, t], axis=-1)

    def mla_slot(t, i):
        lead = (jnp.ones if i == 0 else jnp.zeros)((seq, MLA_NOPE), F32)
        tail = (jnp.ones if i == 0 else jnp.zeros)((seq, MLA_HEAD_PAD - MLA_NOPE - MLA_ROPE), F32)
        return jnp.concatenate([lead, t, tail], axis=-1)

    def kr_slot(t, i):
        tail = (jnp.ones if i == 0 else jnp.zeros)((seq, LANES - MLA_ROPE), F32)
        return jnp.concatenate([t, tail], axis=-1)

    return tuple(tables(HEAD_DIM // 2, two_heads) + tables(MLA_ROPE // 2, mla_slot)
                 + tables(MLA_ROPE // 2, kr_slot))


def _rope(x, cos, s_prev, s_next, d):
    out = []
    for j in range(x.shape[-1] // LANES):
        xc = x[:, j * LANES:(j + 1) * LANES]
        out.append(xc * cos + pltpu.roll(xc, d, 1) * s_prev + pltpu.roll(xc, LANES - d, 1) * s_next)
    return out[0] if len(out) == 1 else jnp.concatenate(out, axis=-1)


def _inproj_kernel(*refs, rotate, mla_scale):
    if rotate:
        (x_ref, g_ref, sc_ref, sh_ref, w_ref, qn_ref, kvn_ref, wuq_ref, wuk_ref, wuv_ref, place_ref,
         c64_ref, p64_ref, n64_ref, cm_ref, pm_ref, nm_ref, ckr_ref, pkr_ref, nkr_ref,
         mq_ref, mk_ref, mv_ref, rq_ref, rk_ref, rv_ref, rg_ref,
         nq_ref, nk_ref, nv_ref, sq_ref, sk_ref, sv_ref) = refs
    else:
        (x_ref, g_ref, sc_ref, sh_ref, w_ref, qn_ref, kvn_ref, wuq_ref, wuk_ref, wuv_ref, place_ref,
         mq_ref, mk_ref, mv_ref, rq_ref, rk_ref, rv_ref, rg_ref,
         nq_ref, nk_ref, nv_ref, sq_ref, sk_ref, sv_ref) = refs

    x = x_ref[0]
    h = _rms(x) * (g_ref[...] * (1.0 + sc_ref[0])) + sh_ref[0]
    hb = h.astype(BF16)

    def proj(lo, hi):
        return _dot(hb, w_ref[:, lo:hi])

    def rope64(v):
        if not rotate:
            return v
        return _rope(v, c64_ref[...], p64_ref[...], n64_ref[...], HEAD_DIM // 4)

    def rope_mla(v):
        if not rotate:
            return v
        return _rope(v, cm_ref[...], pm_ref[...], nm_ref[...], MLA_ROPE // 4)

    cq = (_rms(proj(_O_CQ, _O_CKV)) * qn_ref[...]).astype(BF16)
    ckv_kr = proj(_O_CKV, _O_RQK)
    ckv = (_rms(ckv_kr[:, :MLA_KV_RANK]) * kvn_ref[...]).astype(BF16)
    kr = ckv_kr[:, MLA_KV_RANK:]
    if rotate:
        kr = _rope(kr, ckr_ref[...], pkr_ref[...], nkr_ref[...], MLA_ROPE // 4)
    kr = kr.astype(BF16)

    rqk = proj(_O_RQK, _O_RV)
    rq_ref[0] = rope64(rqk[:, :GROUP_W]).astype(BF16)
    rk_ref[0] = rope64(rqk[:, GROUP_W:]).astype(BF16)
    rv_ref[0] = proj(_O_RV, _O_RG).astype(BF16)
    rg_ref[0] = proj(_O_RG, _O_NA)

    na = proj(_O_NA, _O_SQK)
    nq_ref[0] = (na[:, :GROUP_W] * LOG2_E).astype(BF16)
    nk_ref[0] = na[:, GROUP_W:2 * GROUP_W].astype(BF16)
    nv_ref[0] = na[:, 2 * GROUP_W:].astype(BF16)

    sqk = proj(_O_SQK, _O_SV)
    sq_ref[0] = (rope64(sqk[:, :GROUP_W]) * LOG2_E).astype(BF16)
    sk_ref[0] = rope64(sqk[:, GROUP_W:]).astype(BF16)
    sv_ref[0] = proj(_O_SV, _IN_COLS).astype(BF16)

    q = rope_mla(_dot(cq, wuq_ref[...])) * mla_scale
    mq_ref[0] = q.astype(BF16)
    k = _dot(ckv, wuk_ref[...]) + _dot(kr, place_ref[...])
    mk_ref[0] = k.astype(BF16)
    v = _dot(ckv, wuv_ref[...])
    lane = lax.broadcasted_iota(jnp.int32, v.shape, 1)
    mv_ref[0] = jnp.where(lane % MLA_HEAD_PAD >= MLA_V, 1.0, v).astype(BF16)


def _const_spec(shape):
    nd = len(shape)
    return pl.BlockSpec(shape, lambda *_: (0,) * nd, pipeline_mode=pl.Buffered(1))


def _inproj(x, gain, scale, shift, w, qn, kvn, wuq, wuk, wuv, place, tables, tm):
    b, t, d = x.shape
    rotate = tables is not None
    kv_w = GROUP_W
    mla_w = N_HEADS * MLA_HEAD_PAD
    tok = lambda wd: pl.BlockSpec((1, tm, wd), lambda bi, i: (bi, i, 0))
    vec = pl.BlockSpec((1, 1, d), lambda bi, i: (bi, 0, 0))
    in_specs = [tok(d), _const_spec((1, d)), vec, vec, _const_spec(w.shape),
                _const_spec(qn.shape), _const_spec(kvn.shape), _const_spec(wuq.shape),
                _const_spec(wuk.shape), _const_spec(wuv.shape), _const_spec(place.shape)]
    args = [x, gain, scale, shift, w, qn, kvn, wuq, wuk, wuv, place]
    if rotate:
        in_specs += [pl.BlockSpec((tm, LANES), lambda bi, i: (i, 0))] * len(tables)
        args += list(tables)
    widths = [mla_w, mla_w, mla_w, GROUP_W, GROUP_W, GROUP_W, 2 * GROUP_W,
              GROUP_W, GROUP_W, GROUP_W, GROUP_W, kv_w, kv_w]
    dtypes = [BF16] * 6 + [F32] + [BF16] * 6
    return pl.pallas_call(
        functools.partial(_inproj_kernel, rotate=rotate, mla_scale=(MLA_NOPE + MLA_ROPE) ** -0.5 * LOG2_E),
        grid=(b, t // tm),
        in_specs=in_specs,
        out_specs=[tok(wd) for wd in widths],
        out_shape=[jax.ShapeDtypeStruct((b, t, wd), dt) for wd, dt in zip(widths, dtypes)],
        compiler_params=_cparams(("parallel", "parallel")),
        name="in_proj_rot" if rotate else "in_proj_ctx",
    )(*args)


def _mla_kernel(q_ref, kx_ref, vx_ref, ky_ref, vy_ref, o_ref, m_ref, acc_ref, sa_ref, sb_ref, sc_ref,
                ma_ref, mb_ref, mc_ref, *, tk, heads, n_tiles, n_sub, unroll):
    n_chunks = kx_ref.shape[1] // tk
    assert n_chunks % 2 == 0
    tq = q_ref.shape[1] // n_tiles
    ts = tq // n_sub
    chains = [(h, u) for h in range(heads) for u in range(n_sub)]

    def reset():
        for c in range(len(chains)):
            m_ref[c] = jnp.full(m_ref.shape[1:], NEG_INF, F32)
            acc_ref[c] = jnp.zeros(acc_ref.shape[1:], F32)

    def rows(h):
        return slice(h * MLA_HEAD_PAD, (h + 1) * MLA_HEAD_PAD)

    def vrows(h):
        return slice(h * MLA_HEAD_PAD, h * MLA_HEAD_PAD + MLA_V_ROWS)

    buf_a, buf_b, buf_c = (sa_ref, ma_ref), (sb_ref, mb_ref), (sc_ref, mc_ref)

    def scores(t, c, k, dst):
        h, u = chains[c]
        q0 = t * tq + u * ts
        st = _dot_nt(k, q_ref[0, q0:q0 + ts, rows(h)])
        dst[0][c] = st
        dst[1][c] = jnp.max(st, axis=0, keepdims=True)

    def absorb(c, src, vt):
        m_old = m_ref[c]
        m_new = jnp.maximum(m_old, src[1][c])
        pt = jnp.exp2(src[0][c] - m_new).astype(BF16)
        acc_ref[c, :MLA_V_ROWS] = acc_ref[c, :MLA_V_ROWS] * jnp.exp2(m_old - m_new) + _dot(vt, pt)
        m_ref[c] = m_new

    def kx(j, h):
        return kx_ref[0, pl.ds(pl.multiple_of(j * tk, tk), tk), rows(h)]

    def stage_at(t, j, parity):
        cur, nxt = (buf_a, buf_b) if parity == 0 else (buf_b, buf_a)
        for c, (h, _) in enumerate(chains):
            scores(t, c, kx(j + 1, h), nxt)
            absorb(c, cur, vx_ref[0, j, vrows(h), :])

    trips = (n_chunks - 1) // unroll
    reset()
    for c, (h, _) in enumerate(chains):
        scores(0, c, kx(0, h), buf_a)
    for t in range(n_tiles):
        def body(jj, carry):
            for i in range(unroll):
                stage_at(t, unroll * jj + i, i % 2)
            return carry

        lax.fori_loop(0, trips, body, 0)
        for j in range(unroll * trips, n_chunks - 1):
            stage_at(t, j, j % 2)
        for c, (h, _) in enumerate(chains):
            scores(t, c, ky_ref[0, :, rows(h)], buf_c)
            absorb(c, buf_b, vx_ref[0, n_chunks - 1, vrows(h), :])
        outs = [[None] * n_sub for _ in range(heads)]
        for c, (h, u) in enumerate(chains):
            if t + 1 < n_tiles:
                scores(t + 1, c, kx(0, h), buf_a)
            absorb(c, buf_c, vy_ref[0, 0, vrows(h), :])
            acc = acc_ref[c].T
            outs[h][u] = acc[:, :MLA_V] / acc[:, MLA_V:MLA_V + 1]
        o_ref[0, t * tq:(t + 1) * tq, :] = jnp.concatenate(
            [jnp.concatenate(outs[h], axis=0) for h in range(heads)], axis=-1).astype(o_ref.dtype)
        if t + 1 < n_tiles:
            reset()


def _mla_attention(q, kx, vx, ky, vy, tq, tk, n_tiles, n_sub, unroll):
    b, s, _ = q.shape
    l = ky.shape[1]
    hp = 2
    wd = hp * MLA_HEAD_PAD
    vxt = jnp.swapaxes(vx.reshape(b, s // tk, tk, -1), 2, 3)
    vyt = jnp.swapaxes(vy.reshape(b, 1, l, -1), 2, 3)
    ts = tq // n_tiles // n_sub
    return pl.pallas_call(
        functools.partial(_mla_kernel, tk=tk, heads=hp, n_tiles=n_tiles, n_sub=n_sub, unroll=unroll),
        grid=(b, N_HEADS // hp, s // tq),
        in_specs=[pl.BlockSpec((1, tq, wd), lambda bi, hi, i: (bi, i, hi)),
                  pl.BlockSpec((1, s, wd), lambda bi, hi, i: (bi, 0, hi)),
                  pl.BlockSpec((1, s // tk, wd, tk), lambda bi, hi, i: (bi, 0, hi, 0)),
                  pl.BlockSpec((1, l, wd), lambda bi, hi, i: (bi, 0, hi)),
                  pl.BlockSpec((1, 1, wd, l), lambda bi, hi, i: (bi, 0, hi, 0))],
        out_specs=pl.BlockSpec((1, tq, hp * MLA_V), lambda bi, hi, i: (bi, i, hi)),
        out_shape=jax.ShapeDtypeStruct((b, s, N_HEADS * MLA_V), BF16),
        scratch_shapes=[pltpu.VMEM((hp * n_sub, 1, ts), F32),
                        pltpu.VMEM((hp * n_sub, MLA_HEAD_PAD, ts), F32),
                        pltpu.VMEM((hp * n_sub, tk, ts), F32),
                        pltpu.VMEM((hp * n_sub, tk, ts), F32),
                        pltpu.VMEM((hp * n_sub, l, ts), F32)]
                       + [pltpu.VMEM((hp * n_sub, 1, ts), F32)] * 3,
        compiler_params=_cparams(("parallel", "parallel", "arbitrary")),
        name="mla_attention",
    )(q, kx, vxt, ky, vyt)


def _ctx_attn_kernel(sink_ref, q_ref, k_ref, v_ref, o_ref, *, heads):
    outs = []
    for (q0, q1, k0, k1, v0, v1, sink_idx) in heads:
        q = q_ref[0, :, q0:q1]
        s = _dot_nt(q, k_ref[0, :, k0:k1])
        m = jnp.max(s, axis=-1, keepdims=True)
        if sink_idx is not None:
            sink = jnp.full((1, 1), sink_ref[sink_idx] * LOG2_E, F32)
            m = jnp.maximum(m, sink)
        p = jnp.exp2(s - m)
        l = jnp.sum(p, axis=-1, keepdims=True)
        if sink_idx is not None:
            l = l + jnp.exp2(sink - m)
        outs.append(_dot(p.astype(BF16), v_ref[0, :, v0:v1]) / l)
    o_ref[0] = jnp.concatenate(outs, axis=-1).astype(o_ref.dtype)


def _ctx_attention(q, k, v, heads, sink, name):
    b, l, _ = q.shape
    full = lambda a: pl.BlockSpec((1, l, a.shape[-1]), lambda bi: (bi, 0, 0))
    return pl.pallas_call(
        functools.partial(_ctx_attn_kernel, heads=heads),
        grid=(b,),
        in_specs=[pl.BlockSpec(memory_space=pltpu.SMEM), full(q), full(k), full(v)],
        out_specs=pl.BlockSpec((1, l, GROUP_W), lambda bi: (bi, 0, 0)),
        out_shape=jax.ShapeDtypeStruct((b, l, GROUP_W), BF16),
        compiler_params=_cparams(("parallel",)),
        name=name,
    )(sink, q, k, v)


def _ret_kernel(*refs, chunk, n_chunks, direction, has_prev):
    if has_prev:
        (dec_ref, q_ref, k_ref, kt_ref, v_ref, g_ref, s0_ref, prev_ref, o_ref, sn_ref,
         st_ref, dm_ref, qd_ref, kdt_ref, cd_ref, ob_ref, qk_ref) = refs
    else:
        (dec_ref, q_ref, k_ref, kt_ref, v_ref, g_ref, s0_ref, o_ref, sn_ref,
         st_ref, dm_ref, qd_ref, kdt_ref, cd_ref, ob_ref, qk_ref) = refs
        prev_ref = None
    i = pl.program_id(0)
    c = chunk
    fwd = direction == 0
    batch = range(q_ref.shape[0])

    def head_of(shape, axis):
        return lax.broadcasted_iota(jnp.int32, shape, axis) // HEAD_DIM

    @pl.when(i == 0)
    def _init():
        st_ref[...] = s0_ref[...]
        ii = lax.broadcasted_iota(jnp.int32, (c, c), 0).astype(F32)
        jj = lax.broadcasted_iota(jnp.int32, (c, c), 1).astype(F32)
        diff = (ii - jj) if fwd else (jj - ii)
        pos = lax.broadcasted_iota(jnp.int32, (c, GROUP_W), 0).astype(F32)
        pos_t = lax.broadcasted_iota(jnp.int32, (GROUP_W, c), 1).astype(F32)
        q_steps = (pos + 1.0) if fwd else (c - pos)
        k_steps = (c - 1.0 - pos_t) if fwd else pos_t
        qd = jnp.zeros((c, GROUP_W), F32)
        kdt = jnp.zeros((GROUP_W, c), F32)
        cd = jnp.zeros((GROUP_W, GROUP_W), F32)
        for h in range(N_HEADS):
            dec = dec_ref[direction * N_HEADS + h]
            lg = jax.nn.log_sigmoid(jnp.full((c, c), dec, F32))
            dm_ref[h * c:(h + 1) * c] = jnp.where(diff >= 0, jnp.exp(lg * jnp.maximum(diff, 0.0)), 0.0)
            lgq = jax.nn.log_sigmoid(jnp.full((c, GROUP_W), dec, F32))
            qd = jnp.where(head_of((c, GROUP_W), 1) == h, jnp.exp(lgq * q_steps), qd)
            lgk = jax.nn.log_sigmoid(jnp.full((GROUP_W, c), dec, F32))
            kdt = jnp.where(head_of((GROUP_W, c), 0) == h, jnp.exp(lgk * k_steps), kdt)
            lgc = jax.nn.log_sigmoid(jnp.full((GROUP_W, GROUP_W), dec, F32))
            cd = jnp.where(head_of((GROUP_W, GROUP_W), 0) == h, jnp.exp(lgc * c), cd)
        qd_ref[...] = qd
        kdt_ref[...] = kdt
        cd_ref[...] = cd

    in_head = [head_of((c, GROUP_W), 1) == h for h in range(N_HEADS)]
    head_mask = [jnp.where(mk, 1.0, 0.0).astype(BF16) for mk in in_head]
    same_head = head_of((GROUP_W, GROUP_W), 0) == head_of((GROUP_W, GROUP_W), 1)

    def chunk_off(n):
        idx = n if fwd else n_chunks - 1 - n
        return pl.multiple_of(idx * c, c)

    def qk(bb, n):
        off = chunk_off(n)
        q_all = q_ref[bb, pl.ds(off, c), :]
        q4 = jnp.concatenate([q_all * head_mask[h] for h in range(N_HEADS)], axis=0)
        return _dot_nt(q4, k_ref[bb, pl.ds(off, c), :])

    def gate_and_store(bb, n):
        off = chunk_off(n)
        o = ob_ref[bb]
        oo = o * o
        ms = jnp.zeros((c, GROUP_W), F32)
        for h in range(N_HEADS):
            ms_h = jnp.sum(jnp.where(in_head[h], oo, 0.0), axis=-1, keepdims=True) * (1.0 / HEAD_DIM)
            ms = jnp.where(in_head[h], ms_h, ms)
        res = o * lax.rsqrt(ms + EPS) * _silu(g_ref[bb, pl.ds(off, c), :])
        if has_prev:
            res = res + prev_ref[bb, pl.ds(off, c), :]
        o_ref[bb, pl.ds(off, c), :] = res.astype(o_ref.dtype)

    ob_ref[...] = jnp.zeros(ob_ref.shape, F32)

    for bb in batch:
        qk_ref[bb] = qk(bb, 0)

    def body(n, carry):
        for bb in batch:
            gate_and_store(bb, jnp.maximum(n - 1, 0))
        off = chunk_off(n)
        atts = [(qk_ref[bb] * dm_ref[...]).astype(BF16) for bb in batch]
        for bb in batch:
            qk_ref[bb] = qk(bb, jnp.minimum(n + 1, n_chunks - 1))
        for bb in batch:
            v = v_ref[bb, pl.ds(off, c), :]
            att = atts[bb]
            intra4 = _dot(att, v)
            intra = intra4[:c]
            for h in range(1, N_HEADS):
                intra = jnp.where(in_head[h], intra4[h * c:(h + 1) * c], intra)
            state = st_ref[bb]
            ob_ref[bb] = intra + _dot(q_ref[bb, pl.ds(off, c), :], state.astype(BF16)) * qd_ref[...]
            kk = (kt_ref[bb, :, pl.ds(off, c)].astype(F32) * kdt_ref[...]).astype(BF16)
            st_ref[bb] = state * cd_ref[...] + jnp.where(same_head, _dot(kk, v), 0.0)
        return carry

    lax.fori_loop(0, n_chunks, body, 0)
    for bb in batch:
        gate_and_store(bb, n_chunks - 1)

    @pl.when(i == pl.num_programs(0) - 1)
    def _fin():
        sn_ref[...] = st_ref[...]


def _retention_pass(dec, q, k, v, gates, state0, prev, direction, chunk, n_chunks, out_dtype):
    b, t, _ = q.shape
    tb = chunk * n_chunks
    n = t // tb
    blk = (lambda i: i) if direction == 0 else (lambda i: n - 1 - i)
    tok = pl.BlockSpec((b, tb, GROUP_W), lambda i: (0, blk(i), 0))
    tok_t = pl.BlockSpec((b, GROUP_W, tb), lambda i: (0, 0, blk(i)))
    gate = pl.BlockSpec((b, tb, GROUP_W), lambda i: (0, blk(i), direction))
    st_spec = pl.BlockSpec((b, GROUP_W, GROUP_W), lambda i: (0, 0, 0))
    in_specs = [pl.BlockSpec(memory_space=pltpu.SMEM), tok, tok, tok_t, tok, gate, st_spec]
    args = [dec, q, k, jnp.swapaxes(k, 1, 2), v, gates, state0]
    if prev is not None:
        in_specs.append(tok)
        args.append(prev)
    return pl.pallas_call(
        functools.partial(_ret_kernel, chunk=chunk, n_chunks=n_chunks, direction=direction,
                          has_prev=prev is not None),
        grid=(n,),
        in_specs=in_specs,
        out_specs=[tok, st_spec],
        out_shape=[jax.ShapeDtypeStruct((b, t, GROUP_W), out_dtype),
                   jax.ShapeDtypeStruct((b, GROUP_W, GROUP_W), F32)],
        scratch_shapes=[pltpu.VMEM((b, GROUP_W, GROUP_W), F32),
                        pltpu.VMEM((N_HEADS * chunk, chunk), F32),
                        pltpu.VMEM((chunk, GROUP_W), F32),
                        pltpu.VMEM((GROUP_W, chunk), F32),
                        pltpu.VMEM((GROUP_W, GROUP_W), F32),
                        pltpu.VMEM((b, chunk, GROUP_W), F32),
                        pltpu.VMEM((b, N_HEADS * chunk, chunk), F32)],
        compiler_params=_cparams(("arbitrary",)),
        name="retention_fwd" if direction == 0 else "retention_bwd",
    )(*args)


def _retention(dec, xq, xk, xv, xg, yq, yk, yv, yg, chunk, chunks_per_step):
    b = xq.shape[0]
    zero = jnp.zeros((b, GROUP_W, GROUP_W), F32)
    ny = yq.shape[1] // chunk
    yb, sb = _retention_pass(dec, yq, yk, yv, yg, zero, None, 1, chunk, ny, F32)
    y, sf = _retention_pass(dec, yq, yk, yv, yg, zero, yb, 0, chunk, ny, BF16)
    xb, _ = _retention_pass(dec, xq, xk, xv, xg, sb, None, 1, chunk, chunks_per_step, F32)
    x, _ = _retention_pass(dec, xq, xk, xv, xg, sf, xb, 0, chunk, chunks_per_step, BF16)
    return x, y


def _na_bias_kernel(rpb_ref, o_ref):
    h = pl.program_id(0)
    dr0 = pl.program_id(1)
    c = lax.broadcasted_iota(jnp.int32, (GRID_W, GRID_W), 0)
    kc = lax.broadcasted_iota(jnp.int32, (GRID_W, GRID_W), 1)
    c0 = jnp.clip(c - NA_KC // 2, 0, GRID_W - NA_KC)
    col_in = (kc >= c0) & (kc < c0 + NA_KC)
    dc = jnp.clip(kc - c, -(NA_KC - 1), NA_KC - 1) + NA_KC - 1
    n_dc = 2 * NA_KC - 1
    accs = [jnp.zeros((GRID_W, GRID_W), F32) for _ in range(NA_KR)]
    for d in range(n_dc):
        at_d = dc == d
        for j in range(NA_KR):
            accs[j] = jnp.where(at_d, rpb_ref[(h * (2 * NA_KR - 1) + dr0 + j) * n_dc + d], accs[j])
    for j in range(NA_KR):
        o_ref[0, :, j * GRID_W:(j + 1) * GRID_W] = jnp.where(col_in, accs[j] * LOG2_E, NEG_INF)


def _na_bias_table(rpb):
    return pl.pallas_call(
        _na_bias_kernel,
        grid=(N_HEADS, NA_KR),
        in_specs=[pl.BlockSpec(memory_space=pltpu.SMEM)],
        out_specs=pl.BlockSpec((1, GRID_W, NA_KR * GRID_W), lambda h, r: (r, h, 0)),
        out_shape=jax.ShapeDtypeStruct((NA_KR, N_HEADS * GRID_W, NA_KR * GRID_W), F32),
        compiler_params=_cparams(("parallel", "parallel")),
        name="na_bias_table",
    )(rpb.reshape(-1))


def _na_kernel(q_ref, k_ref, v_ref, ky_ref, vy_ref, tb_ref, o_ref,
               sa_ref, sb_ref, pa_ref, pb_ref, la_ref, lb_ref, *, rows_per_step, n_rows):
    r_base = pl.program_id(1) * rows_per_step
    win = NA_KR * GRID_W

    head_of_lane = lax.broadcasted_iota(jnp.int32, (GRID_W, GROUP_W), 1) // HEAD_DIM
    in_head = [head_of_lane == h for h in range(N_HEADS)]
    head_mask = [jnp.where(mk, 1.0, 0.0).astype(BF16) for mk in in_head]

    def geometry(i):
        r = r_base + i
        r0 = jnp.clip(r - NA_KR // 2, 0, n_rows - NA_KR)
        return pl.multiple_of(r0 * GRID_W, GRID_W), r0 - r + NA_KR - 1

    def row_slice(i):
        return pl.ds(pl.multiple_of(i * GRID_W, GRID_W), GRID_W)

    def scores(i, s_ref):
        koff, dr0 = geometry(i)
        q_all = q_ref[0, row_slice(i), :]
        q4 = jnp.concatenate([q_all * head_mask[h] for h in range(N_HEADS)], axis=0)
        s_ref[:, :win] = _dot_nt(q4, k_ref[0, pl.ds(koff, win), :]) + tb_ref[dr0]
        s_ref[:, win:] = _dot_nt(q4, ky_ref[0])

    def softmax(s_ref, p_ref, l_ref):
        s = s_ref[...]
        p = jnp.exp2(s - _row_max(s))
        l_ref[...] = 1.0 / _row_sum(p)
        p_ref[...] = p.astype(BF16)

    def values(i, p_ref, l_ref):
        koff, _ = geometry(i)
        o4 = (_dot(p_ref[:, :win], v_ref[0, pl.ds(koff, win), :])
              + _dot(p_ref[:, win:], vy_ref[0])) * l_ref[...]
        out = o4[:GRID_W]
        for h in range(1, N_HEADS):
            out = jnp.where(in_head[h], o4[h * GRID_W:(h + 1) * GRID_W], out)
        o_ref[0, row_slice(i), :] = out.astype(o_ref.dtype)

    pb_ref[...] = jnp.zeros(pb_ref.shape, BF16)
    lb_ref[...] = jnp.zeros(lb_ref.shape, F32)
    scores(0, sa_ref)

    def pair_body(tt, carry):
        t = 2 * tt
        values(jnp.maximum(t - 1, 0), pb_ref, lb_ref)
        scores(t + 1, sb_ref)
        softmax(sa_ref, pa_ref, la_ref)
        values(t, pa_ref, la_ref)
        scores(jnp.minimum(t + 2, rows_per_step - 1), sa_ref)
        softmax(sb_ref, pb_ref, lb_ref)
        return carry

    lax.fori_loop(0, rows_per_step // 2, pair_body, 0)
    values(rows_per_step - 1, pb_ref, lb_ref)


def _na_attention(q, k, v, ky, vy, table, rows_per_step):
    b, s, _ = q.shape
    l = ky.shape[1]
    n_rows = s // GRID_W
    tq = rows_per_step * GRID_W
    n_keys = NA_KR * GRID_W + l
    seq = lambda n: pl.BlockSpec((1, n, GROUP_W), lambda bi, i: (bi, 0, 0))
    return pl.pallas_call(
        functools.partial(_na_kernel, rows_per_step=rows_per_step, n_rows=n_rows),
        grid=(b, n_rows // rows_per_step),
        in_specs=[pl.BlockSpec((1, tq, GROUP_W), lambda bi, i: (bi, i, 0)),
                  seq(s), seq(s), seq(l), seq(l), _const_spec(table.shape)],
        out_specs=pl.BlockSpec((1, tq, GROUP_W), lambda bi, i: (bi, i, 0)),
        out_shape=jax.ShapeDtypeStruct((b, s, GROUP_W), BF16),
        scratch_shapes=([pltpu.VMEM((N_HEADS * GRID_W, n_keys), F32)] * 2
                        + [pltpu.VMEM((N_HEADS * GRID_W, n_keys), BF16)] * 2
                        + [pltpu.VMEM((N_HEADS * GRID_W, 1), F32)] * 2),
        compiler_params=_cparams(("parallel", "arbitrary")),
        name="na_attention",
    )(q, k, v, ky, vy, table)


def _swa_kernel(sink_ref, q_ref, kp_ref, kc_ref, kn_ref, vp_ref, vc_ref, vn_ref, ky_ref, vy_ref, o_ref,
                kw_ref, vw_ref, wb_ref, sa_ref, sb_ref, pa_ref, pb_ref, la_ref, lb_ref, *, blocks_per_step):
    step = pl.program_id(1)
    nb = pl.num_programs(1) * blocks_per_step
    bl = SWA_BLOCK
    g = N_HEADS // SWA_KV_HEADS
    tq = blocks_per_step * bl
    kw_ref[0:bl] = kp_ref[0]
    kw_ref[bl:bl + tq] = kc_ref[0]
    kw_ref[bl + tq:] = kn_ref[0]
    vw_ref[0:bl] = vp_ref[0]
    vw_ref[bl:bl + tq] = vc_ref[0]
    vw_ref[bl + tq:] = vn_ref[0]

    qi = lax.broadcasted_iota(jnp.int32, (g * bl, 3 * bl), 0) % bl
    jk = lax.broadcasted_iota(jnp.int32, (g * bl, 3 * bl), 1)
    wb_ref[...] = jnp.where(jnp.abs(jk - bl - qi) <= SWA_WINDOW, 0.0, NEG_INF)
    half = lax.broadcasted_iota(jnp.int32, (g * bl, 1), 0) // bl

    head_of_lane = lax.broadcasted_iota(jnp.int32, (bl, GROUP_W), 1) // HEAD_DIM
    in_head = [head_of_lane == h for h in range(N_HEADS)]
    head_mask = [jnp.where(mk, 1.0, 0.0).astype(BF16) for mk in in_head]

    def blk_slice(j):
        return pl.ds(pl.multiple_of(j * bl, bl), bl)

    def scores(j, s_ref):
        qoff = pl.multiple_of(j * bl, bl)
        q_all = q_ref[0, blk_slice(j), :]
        kw = kw_ref[pl.ds(qoff, 3 * bl), :]
        n = step * blocks_per_step + j
        lo_edge = jnp.where(n == 0, NEG_INF, 0.0)
        hi_edge = jnp.where(n == nb - 1, NEG_INF, 0.0)
        for kh in range(SWA_KV_HEADS):
            q = jnp.concatenate([q_all * head_mask[kh * g + gi] for gi in range(g)], axis=0)
            s = _dot_nt(q, kw) + wb_ref[...]
            s_ref[kh, :, :bl] = s[:, :bl] + lo_edge
            s_ref[kh, :, bl:2 * bl] = s[:, bl:2 * bl]
            s_ref[kh, :, 2 * bl:3 * bl] = s[:, 2 * bl:] + hi_edge
            s_ref[kh, :, 3 * bl:] = _dot_nt(q, ky_ref[0])

    def softmax(s_ref, p_ref, l_ref):
        for kh in range(SWA_KV_HEADS):
            s = s_ref[kh]
            sink = jnp.full((g * bl, 1), sink_ref[kh * g] * LOG2_E, F32)
            for gi in range(1, g):
                sink = jnp.where(half == gi, sink_ref[kh * g + gi] * LOG2_E, sink)
            m = jnp.maximum(_row_max(s), sink)
            p = jnp.exp2(s - m)
            l_ref[kh] = 1.0 / (_row_sum(p) + jnp.exp2(sink - m))
            p_ref[kh] = p.astype(BF16)

    def values(j, p_ref, l_ref):
        vw = vw_ref[pl.ds(pl.multiple_of(j * bl, bl), 3 * bl), :]
        out = jnp.zeros((bl, GROUP_W), F32)
        for kh in range(SWA_KV_HEADS):
            o = (_dot(p_ref[kh, :, :3 * bl], vw) + _dot(p_ref[kh, :, 3 * bl:], vy_ref[0])) * l_ref[kh]
            for gi in range(g):
                out = jnp.where(in_head[kh * g + gi], o[gi * bl:(gi + 1) * bl], out)
        o_ref[0, blk_slice(j), :] = out.astype(o_ref.dtype)

    pb_ref[...] = jnp.zeros(pb_ref.shape, BF16)
    lb_ref[...] = jnp.zeros(lb_ref.shape, F32)
    scores(0, sa_ref)

    def pair_body(tt, carry):
        t = 2 * tt
        values(jnp.maximum(t - 1, 0), pb_ref, lb_ref)
        scores(t + 1, sb_ref)
        softmax(sa_ref, pa_ref, la_ref)
        values(t, pa_ref, la_ref)
        scores(jnp.minimum(t + 2, blocks_per_step - 1), sa_ref)
        softmax(sb_ref, pb_ref, lb_ref)
        return carry

    lax.fori_loop(0, blocks_per_step // 2, pair_body, 0)
    values(blocks_per_step - 1, pb_ref, lb_ref)


def _swa_attention(sink, q, k, v, ky, vy, blocks_per_step):
    b, s, _ = q.shape
    l = ky.shape[1]
    tq = blocks_per_step * SWA_BLOCK
    n_steps = s // tq
    nb = s // SWA_BLOCK
    kvw = GROUP_W
    q_rows = N_HEADS // SWA_KV_HEADS * SWA_BLOCK
    n_keys = 3 * SWA_BLOCK + l
    prev = pl.BlockSpec((1, SWA_BLOCK, kvw), lambda bi, i: (bi, jnp.maximum(i * blocks_per_step - 1, 0), 0))
    cur = pl.BlockSpec((1, tq, kvw), lambda bi, i: (bi, i, 0))
    nxt = pl.BlockSpec((1, SWA_BLOCK, kvw), lambda bi, i: (bi, jnp.minimum((i + 1) * blocks_per_step, nb - 1), 0))
    ctx = pl.BlockSpec((1, l, kvw), lambda bi, i: (bi, 0, 0))
    return pl.pallas_call(
        functools.partial(_swa_kernel, blocks_per_step=blocks_per_step),
        grid=(b, n_steps),
        in_specs=[pl.BlockSpec(memory_space=pltpu.SMEM),
                  pl.BlockSpec((1, tq, GROUP_W), lambda bi, i: (bi, i, 0)),
                  prev, cur, nxt, prev, cur, nxt, ctx, ctx],
        out_specs=pl.BlockSpec((1, tq, GROUP_W), lambda bi, i: (bi, i, 0)),
        out_shape=jax.ShapeDtypeStruct((b, s, GROUP_W), BF16),
        scratch_shapes=[pltpu.VMEM((tq + 2 * SWA_BLOCK, kvw), BF16),
                        pltpu.VMEM((tq + 2 * SWA_BLOCK, kvw), BF16),
                        pltpu.VMEM((q_rows, 3 * SWA_BLOCK), F32)]
                       + [pltpu.VMEM((SWA_KV_HEADS, q_rows, n_keys), F32)] * 2
                       + [pltpu.VMEM((SWA_KV_HEADS, q_rows, n_keys), BF16)] * 2
                       + [pltpu.VMEM((SWA_KV_HEADS, q_rows, 1), F32)] * 2,
        compiler_params=_cparams(("parallel", "arbitrary")),
        name="swa_attention",
    )(sink, q, k, k, k, v, v, v, ky, vy)


def _outffn_kernel(*refs, hidden_chunk, final):
    if final:
        (x_ref, m0_ref, m1_ref, m2_ref, m3_ref, wo_ref, g1_ref, n2_ref, sc_ref, sh_ref, g2_ref,
         w1_ref, w3_ref, w2_ref, fg_ref, o_ref) = refs
    else:
        (x_ref, m0_ref, m1_ref, m2_ref, m3_ref, wo_ref, g1_ref, n2_ref, sc_ref, sh_ref, g2_ref,
         w1_ref, w3_ref, w2_ref, o_ref) = refs
    mix = None
    for gi, m_ref in enumerate((m0_ref, m1_ref, m2_ref, m3_ref)):
        part = _dot(m_ref[0], wo_ref[gi * GROUP_W:(gi + 1) * GROUP_W, :])
        mix = part if mix is None else mix + part
    x1 = x_ref[0] + g1_ref[0] * mix
    hb = (_rms(x1) * (n2_ref[...] * (1.0 + sc_ref[0])) + sh_ref[0]).astype(BF16)
    hidden = w1_ref.shape[1]
    acc = None
    for c0 in range(0, hidden, hidden_chunk):
        a = _dot(hb, w1_ref[:, c0:c0 + hidden_chunk])
        bgate = _dot(hb, w3_ref[:, c0:c0 + hidden_chunk])
        u = (_silu(a) * bgate).astype(BF16)
        part = _dot(u, w2_ref[c0:c0 + hidden_chunk, :])
        acc = part if acc is None else acc + part
    x2 = x1 + g2_ref[0] * acc
    if final:
        x2 = _rms(x2) * fg_ref[...]
    o_ref[0] = x2


def _outffn(x, mixes, wo, g1, n2, sc2, sh2, g2, w1, w3, w2, final_g, tm):
    b, t, d = x.shape
    tok = lambda wd: pl.BlockSpec((1, tm, wd), lambda bi, i: (bi, i, 0))
    vec = pl.BlockSpec((1, 1, d), lambda bi, i: (bi, 0, 0))
    in_specs = ([tok(d)] + [tok(GROUP_W)] * 4
                + [_const_spec(wo.shape), vec, _const_spec((1, d)), vec, vec, vec,
                   _const_spec(w1.shape), _const_spec(w3.shape), _const_spec(w2.shape)])
    args = [x, *mixes, wo, g1, n2, sc2, sh2, g2, w1, w3, w2]
    final = final_g is not None
    if final:
        in_specs.append(_const_spec((1, d)))
        args.append(final_g)
    return pl.pallas_call(
        functools.partial(_outffn_kernel, hidden_chunk=256, final=final),
        grid=(b, t // tm),
        in_specs=in_specs,
        out_specs=tok(d),
        out_shape=jax.ShapeDtypeStruct((b, t, d), F32),
        compiler_params=_cparams(("parallel", "parallel")),
        name="out_proj_ffn_final" if final else "out_proj_ffn",
    )(*args)


def _prep_weights(w_in, mla_w_uq, mla_w_ukv):
    depth, d, _ = w_in.shape
    offs = [0]
    for sz in IN_SIZES:
        offs.append(offs[-1] + sz)
    w_bf = w_in.astype(BF16)
    cols = [w_bf[:, :, offs[i]:offs[i + 1]] for i in range(len(IN_SIZES))]
    cq, ckv, kr, rq, rk, rv, rgf, rgb, nq, nk, nv, sq, sk, sv = cols
    scale = HEAD_DIM ** -0.5
    kr_slot = jnp.concatenate([kr, jnp.zeros((depth, d, LANES - MLA_ROPE), BF16)], axis=-1)
    src = jnp.arange(LANES)[:, None]
    dst = jnp.arange(N_HEADS * MLA_HEAD_PAD)[None, :]
    place = ((src < MLA_ROPE) & (dst % MLA_HEAD_PAD == src + MLA_NOPE)).astype(BF16)

    def per_query_head(t):
        g = N_HEADS // SWA_KV_HEADS
        t = t.reshape(depth, d, SWA_KV_HEADS, 1, HEAD_DIM)
        return jnp.broadcast_to(t, (depth, d, SWA_KV_HEADS, g, HEAD_DIM)).reshape(depth, d, GROUP_W)

    w = jnp.concatenate([cq, ckv, kr_slot, rq, rk * scale, rv, rgf, rgb, nq * scale, nk, nv,
                         sq * scale, per_query_head(sk), per_query_head(sv)], axis=-1).astype(BF16)

    qr = mla_w_uq.shape[1]
    uq = mla_w_uq.reshape(depth, qr, N_HEADS, MLA_NOPE + MLA_ROPE)
    wuq = jnp.concatenate([uq, jnp.zeros((depth, qr, N_HEADS, MLA_HEAD_PAD - MLA_NOPE - MLA_ROPE), F32)],
                          axis=-1).reshape(depth, qr, N_HEADS * MLA_HEAD_PAD).astype(BF16)
    kvr = mla_w_ukv.shape[1]
    ukv = mla_w_ukv.reshape(depth, kvr, N_HEADS, MLA_NOPE + MLA_V)
    zk = jnp.zeros((depth, kvr, N_HEADS, MLA_HEAD_PAD - MLA_NOPE), F32)
    wuk = jnp.concatenate([ukv[..., :MLA_NOPE], zk], axis=-1).reshape(depth, kvr, -1).astype(BF16)
    zv = jnp.zeros((depth, kvr, N_HEADS, MLA_HEAD_PAD - MLA_V), F32)
    wuv = jnp.concatenate([ukv[..., MLA_NOPE:], zv], axis=-1).reshape(depth, kvr, -1).astype(BF16)
    return w, wuq, wuk, wuv, place


def _ctx_head_specs():
    mla = tuple((h * MLA_HEAD_PAD, (h + 1) * MLA_HEAD_PAD, h * MLA_HEAD_PAD, (h + 1) * MLA_HEAD_PAD,
                 h * MLA_HEAD_PAD, h * MLA_HEAD_PAD + MLA_V, None) for h in range(N_HEADS))
    na = tuple((h * HEAD_DIM, (h + 1) * HEAD_DIM) * 3 + (None,) for h in range(N_HEADS))
    swa = tuple((h * HEAD_DIM, (h + 1) * HEAD_DIM) * 3 + (h,) for h in range(N_HEADS))
    return mla, na, swa


def kernel(x, c, ctx, c_ctx, ada_w, ada_b, norm1_g, w_in, mla_q_norm, mla_w_uq, mla_kv_norm, mla_w_ukv,
           ret_decay, na_rpb, swa_sink, w_out, norm2_g, ffn_w1, ffn_w3, ffn_w2, final_norm_g):
    b, s, d = x.shape
    l_ctx = ctx.shape[1]
    depth = ada_w.shape[0]
    assert b + 1 <= 8 and s % 2048 == 0 and l_ctx % 128 == 0

    cond = jnp.concatenate([c, c_ctx[None, :], jnp.zeros((8 - b - 1, d), F32)], axis=0)
    mod = _modulation(cond, ada_w, ada_b)
    tables = _rope_tables(s)
    w_all, wuq_all, wuk_all, wuv_all, place = _prep_weights(w_in, mla_w_uq, mla_w_ukv)
    wo_all = w_out.astype(BF16)
    w1_all, w3_all, w2_all = ffn_w1.astype(BF16), ffn_w3.astype(BF16), ffn_w2.astype(BF16)
    mla_heads, na_heads, swa_heads = _ctx_head_specs()
    no_sink = jnp.zeros((N_HEADS,), F32)

    tm_x = 512
    tm_y = min(256, l_ctx)
    y = ctx
    for l in range(depth):
        mx = [mod[l, :b, j * d:(j + 1) * d][:, None, :] for j in range(6)]
        my = [jnp.broadcast_to(mod[l, b, j * d:(j + 1) * d][None, None, :], (b, 1, d)) for j in range(6)]
        n1 = norm1_g[l][None, :]
        n2 = norm2_g[l][None, :]
        qn = mla_q_norm[l][None, :]
        kvn = mla_kv_norm[l][None, :]
        lw = (w_all[l], qn, kvn, wuq_all[l], wuk_all[l], wuv_all[l], place)

        px = _inproj(x, n1, mx[1], mx[0], *lw, tables, tm_x)
        py = _inproj(y, n1, my[1], my[0], *lw, None, tm_y)
        (xmq, xmk, xmv, xrq, xrk, xrv, xrg, xnq, xnk, xnv, xsq, xsk, xsv) = px
        (ymq, ymk, ymv, yrq, yrk, yrv, yrg, ynq, ynk, ynv, ysq, ysk, ysv) = py

        mla_x = _mla_attention(xmq, xmk, xmv, ymk, ymv, tq=1024, tk=512, n_tiles=2, n_sub=2, unroll=8)
        dec = ret_decay[l].reshape(-1)
        ret_x, ret_y = _retention(dec, xrq, xrk, xrv, xrg, yrq, yrk, yrv, yrg, chunk=128, chunks_per_step=16)
        table = _na_bias_table(na_rpb[l])
        na_x = _na_attention(xnq, xnk, xnv, ynk, ynv, table, rows_per_step=32)
        swa_x = _swa_attention(swa_sink[l], xsq, xsk, xsv, ysk, ysv, blocks_per_step=16)

        last = l == depth - 1
        x = _outffn(x, (mla_x, ret_x, na_x, swa_x), wo_all[l], mx[2], n2, mx[4], mx[3], mx[5],
                    w1_all[l], w3_all[l], w2_all[l], final_norm_g[None, :] if last else None, tm_x)
        if not last:
            mla_y = _ctx_attention(ymq, ymk, ymv, mla_heads, no_sink, "mla_ctx_attention")
            na_y = _ctx_attention(ynq, ynk, ynv, na_heads, no_sink, "na_ctx_attention")
            swa_y = _ctx_attention(ysq, ysk, ysv, swa_heads, swa_sink[l], "swa_ctx_attention")
            y = _outffn(y, (mla_y, ret_y, na_y, swa_y), wo_all[l], my[2], n2, my[4], my[3], my[5],
                        w1_all[l], w3_all[l], w2_all[l], None, tm_y)
    return x
```

```python
import functools

import jax
import jax.numpy as jnp
from jax import lax
from jax.experimental import pallas as pl
from jax.experimental.pallas import tpu as pltpu

F32 = jnp.float32
BF16 = jnp.bfloat16

GRID_W = 64
HEAD_DIM = 64
N_HEADS = 4
GROUP_W = N_HEADS * HEAD_DIM
MLA_Q_RANK = 256
MLA_KV_RANK = 128
MLA_NOPE = 64
MLA_ROPE = 32
MLA_V = 64
MLA_HEAD_PAD = 128
MLA_V_ROWS = MLA_V + 16
NA_KR = 8
NA_KC = 16
SWA_KV_HEADS = 2
SWA_WINDOW = 128
SWA_BLOCK = 128
ROPE_THETA = 10000.0
EPS = 1e-6
NEG_INF = -1e30
LOG2_E = 1.4426950408889634
LANES = 128
VMEM_LIMIT = 56 * 1024 * 1024

IN_SIZES = (MLA_Q_RANK, MLA_KV_RANK, MLA_ROPE,
            GROUP_W, GROUP_W, GROUP_W, GROUP_W, GROUP_W,
            GROUP_W, GROUP_W, GROUP_W,
            GROUP_W, SWA_KV_HEADS * HEAD_DIM, SWA_KV_HEADS * HEAD_DIM)

_O_CQ = 0
_O_CKV = _O_CQ + MLA_Q_RANK
_O_KR = _O_CKV + MLA_KV_RANK
_O_RQK = _O_KR + LANES
_O_RV = _O_RQK + 2 * GROUP_W
_O_RG = _O_RV + GROUP_W
_O_NA = _O_RG + 2 * GROUP_W
_O_SQK = _O_NA + 3 * GROUP_W
_O_SV = _O_SQK + 2 * GROUP_W
_IN_COLS = _O_SV + GROUP_W


def _cparams(sem):
    return pltpu.CompilerParams(dimension_semantics=sem, vmem_limit_bytes=VMEM_LIMIT)


def _dot(a, b):
    return jnp.dot(a, b, preferred_element_type=F32)


def _dot_nt(a, b):
    return lax.dot_general(a, b, (((1,), (1,)), ((), ())), preferred_element_type=F32)


def _rms(x):
    return x * lax.rsqrt(jnp.mean(x * x, axis=-1, keepdims=True) + EPS)


def _silu(x):
    return x * jax.nn.sigmoid(x)


def _lane_chunks(arrays):
    return [a[:, j * LANES:(j + 1) * LANES] for a in arrays for j in range(a.shape[-1] // LANES)]


def _row_max(*arrays):
    return jnp.max(functools.reduce(jnp.maximum, _lane_chunks(arrays)), axis=-1, keepdims=True)


def _row_sum(*arrays):
    return jnp.sum(functools.reduce(jnp.add, _lane_chunks(arrays)), axis=-1, keepdims=True)


def _mod_kernel(c_ref, w_ref, b_ref, o_ref):
    o_ref[0] = _dot(_silu(c_ref[...]), w_ref[0]) + b_ref[0]


def _modulation(cond, ada_w, ada_b):
    depth, d, d6 = ada_w.shape
    n = d6 // d
    return pl.pallas_call(
        _mod_kernel,
        grid=(depth, n),
        in_specs=[pl.BlockSpec((8, d), lambda l, j: (0, 0)),
                  pl.BlockSpec((1, d, d), lambda l, j: (l, 0, j)),
                  pl.BlockSpec((1, 1, d), lambda l, j: (l, 0, j))],
        out_specs=pl.BlockSpec((1, 8, d), lambda l, j: (l, 0, j)),
        out_shape=jax.ShapeDtypeStruct((depth, 8, d6), F32),
        compiler_params=_cparams(("parallel", "parallel")),
        name="ada_modulation",
    )(cond, ada_w, ada_b.reshape(depth, 1, d6))


def _rope_tables(seq):
    n_rows = seq // GRID_W

    def parts(pos, d):
        inv = ROPE_THETA ** (-jnp.arange(0, d, 2, dtype=F32) / d)
        ang = pos.astype(F32)[:, None] * inv[None, :]
        z = jnp.zeros_like(ang)
        return (jnp.concatenate([jnp.cos(ang), jnp.cos(ang)], axis=-1),
                jnp.concatenate([z, jnp.sin(ang)], axis=-1),
                jnp.concatenate([-jnp.sin(ang), z], axis=-1))

    def expand(by_row, by_col):
        w = by_row.shape[-1]
        r = jnp.broadcast_to(by_row[:, None, :], (n_rows, GRID_W, w))
        c = jnp.broadcast_to(by_col[None, :, :], (n_rows, GRID_W, w))
        return jnp.concatenate([r, c], axis=-1).reshape(seq, 2 * w)

    def tables(d, fill):
        per_head = [expand(a, b) for a, b in zip(parts(jnp.arange(n_rows), d), parts(jnp.arange(GRID_W), d))]
        return [fill(t, i) for i, t in enumerate(per_head)]

    def two_heads(t, _):
        return jnp.concatenate([t, t], axis=-1)

    def mla_slot(t, i):
        lead = (jnp.ones if i == 0 else jnp.zeros)((seq, MLA_NOPE), F32)
        tail = (jnp.ones if i == 0 else jnp.zeros)((seq, MLA_HEAD_PAD - MLA_NOPE - MLA_ROPE), F32)
        return jnp.concatenate([lead, t, tail], axis=-1)

    def kr_slot(t, i):
        tail = (jnp.ones if i == 0 else jnp.zeros)((seq, LANES - MLA_ROPE), F32)
        return jnp.concatenate([t, tail], axis=-1)

    return tuple(tables(HEAD_DIM // 2, two_heads) + tables(MLA_ROPE // 2, mla_slot)
                 + tables(MLA_ROPE // 2, kr_slot))


def _rope(x, cos, s_prev, s_next, d):
    out = []
    for j in range(x.shape[-1] // LANES):
        xc = x[:, j * LANES:(j + 1) * LANES]
        out.append(xc * cos + pltpu.roll(xc, d, 1) * s_prev + pltpu.roll(xc, LANES - d, 1) * s_next)
    return out[0] if len(out) == 1 else jnp.concatenate(out, axis=-1)


def _inproj_kernel(*refs, rotate, mla_scale):
    if rotate:
        (x_ref, g_ref, sc_ref, sh_ref, w_ref, qn_ref, kvn_ref, wuq_ref, wuk_ref, wuv_ref, place_ref,
         c64_ref, p64_ref, n64_ref, cm_ref, pm_ref, nm_ref, ckr_ref, pkr_ref, nkr_ref,
         mq_ref, mk_ref, mv_ref, rq_ref, rk_ref, rv_ref, rg_ref,
         nq_ref, nk_ref, nv_ref, sq_ref, sk_ref, sv_ref) = refs
    else:
        (x_ref, g_ref, sc_ref, sh_ref, w_ref, qn_ref, kvn_ref, wuq_ref, wuk_ref, wuv_ref, place_ref,
         mq_ref, mk_ref, mv_ref, rq_ref, rk_ref, rv_ref, rg_ref,
         nq_ref, nk_ref, nv_ref, sq_ref, sk_ref, sv_ref) = refs

    x = x_ref[0]
    h = _rms(x) * (g_ref[...] * (1.0 + sc_ref[0])) + sh_ref[0]
    hb = h.astype(BF16)

    def proj(lo, hi):
        return _dot(hb, w_ref[:, lo:hi])

    def rope64(v):
        if not rotate:
            return v
        return _rope(v, c64_ref[...], p64_ref[...], n64_ref[...], HEAD_DIM // 4)

    def rope_mla(v):
        if not rotate:
            return v
        return _rope(v, cm_ref[...], pm_ref[...], nm_ref[...], MLA_ROPE // 4)

    cq = (_rms(proj(_O_CQ, _O_CKV)) * qn_ref[...]).astype(BF16)
    ckv_kr = proj(_O_CKV, _O_RQK)
    ckv = (_rms(ckv_kr[:, :MLA_KV_RANK]) * kvn_ref[...]).astype(BF16)
    kr = ckv_kr[:, MLA_KV_RANK:]
    if rotate:
        kr = _rope(kr, ckr_ref[...], pkr_ref[...], nkr_ref[...], MLA_ROPE // 4)
    kr = kr.astype(BF16)

    rqk = proj(_O_RQK, _O_RV)
    rq_ref[0] = rope64(rqk[:, :GROUP_W]).astype(BF16)
    rk_ref[0] = rope64(rqk[:, GROUP_W:]).astype(BF16)
    rv_ref[0] = proj(_O_RV, _O_RG).astype(BF16)

    na = proj(_O_NA, _O_SQK)
    nq_ref[0] = (na[:, :GROUP_W] * LOG2_E).astype(BF16)
    nk_ref[0] = na[:, GROUP_W:2 * GROUP_W].astype(BF16)
    nv_ref[0] = na[:, 2 * GROUP_W:].astype(BF16)

    sqk = proj(_O_SQK, _O_SV)
    sq_ref[0] = (rope64(sqk[:, :GROUP_W]) * LOG2_E).astype(BF16)
    sk_ref[0] = rope64(sqk[:, GROUP_W:]).astype(BF16)
    sv_ref[0] = proj(_O_SV, _IN_COLS).astype(BF16)

    q = rope_mla(_dot(cq, wuq_ref[...])) * mla_scale
    mq_ref[0] = q.astype(BF16)
    k = _dot(ckv, wuk_ref[...]) + _dot(kr, place_ref[...])
    mk_ref[0] = k.astype(BF16)
    v = _dot(ckv, wuv_ref[...])
    lane = lax.broadcasted_iota(jnp.int32, v.shape, 1)
    mv_ref[0] = jnp.where(lane % MLA_HEAD_PAD >= MLA_V, 1.0, v).astype(BF16)

    rg_ref[0] = proj(_O_RG, _O_NA)


def _const_spec(shape):
    nd = len(shape)
    return pl.BlockSpec(shape, lambda *_: (0,) * nd, pipeline_mode=pl.Buffered(1))


def _inproj(x, gain, scale, shift, w, qn, kvn, wuq, wuk, wuv, place, tables, tm):
    b, t, d = x.shape
    rotate = tables is not None
    kv_w = GROUP_W
    mla_w = N_HEADS * MLA_HEAD_PAD
    tok = lambda wd: pl.BlockSpec((1, tm, wd), lambda bi, i: (bi, i, 0))
    vec = pl.BlockSpec((1, 1, d), lambda bi, i: (bi, 0, 0))
    in_specs = [tok(d), _const_spec((1, d)), vec, vec, _const_spec(w.shape),
                _const_spec(qn.shape), _const_spec(kvn.shape), _const_spec(wuq.shape),
                _const_spec(wuk.shape), _const_spec(wuv.shape), _const_spec(place.shape)]
    args = [x, gain, scale, shift, w, qn, kvn, wuq, wuk, wuv, place]
    if rotate:
        in_specs += [pl.BlockSpec((tm, LANES), lambda bi, i: (i, 0))] * len(tables)
        args += list(tables)
    widths = [mla_w, mla_w, mla_w, GROUP_W, GROUP_W, GROUP_W, 2 * GROUP_W,
              GROUP_W, GROUP_W, GROUP_W, GROUP_W, kv_w, kv_w]
    dtypes = [BF16] * 6 + [F32] + [BF16] * 6
    return pl.pallas_call(
        functools.partial(_inproj_kernel, rotate=rotate, mla_scale=(MLA_NOPE + MLA_ROPE) ** -0.5 * LOG2_E),
        grid=(b, t // tm),
        in_specs=in_specs,
        out_specs=[tok(wd) for wd in widths],
        out_shape=[jax.ShapeDtypeStruct((b, t, wd), dt) for wd, dt in zip(widths, dtypes)],
        compiler_params=_cparams(("parallel", "parallel")),
        name="in_proj_rot" if rotate else "in_proj_ctx",
    )(*args)


def _mla_kernel(q_ref, kx_ref, vx_ref, ky_ref, vy_ref, o_ref, m_ref, acc_ref, sa_ref, sb_ref, sc_ref,
                ma_ref, mb_ref, mc_ref, *, tk, heads, n_tiles, n_sub, unroll):
    n_chunks = kx_ref.shape[1] // tk
    assert n_chunks % 2 == 0
    tq = q_ref.shape[1] // n_tiles
    ts = tq // n_sub
    chains = [(h, u) for h in range(heads) for u in range(n_sub)]

    def reset():
        for c in range(len(chains)):
            m_ref[c] = jnp.full(m_ref.shape[1:], NEG_INF, F32)
            acc_ref[c] = jnp.zeros(acc_ref.shape[1:], F32)

    def rows(h):
        return slice(h * MLA_HEAD_PAD, (h + 1) * MLA_HEAD_PAD)

    def vrows(h):
        return slice(h * MLA_HEAD_PAD, h * MLA_HEAD_PAD + MLA_V_ROWS)

    buf_a, buf_b, buf_c = (sa_ref, ma_ref), (sb_ref, mb_ref), (sc_ref, mc_ref)

    def scores(t, c, k, dst):
        h, u = chains[c]
        q0 = t * tq + u * ts
        st = _dot_nt(k, q_ref[0, q0:q0 + ts, rows(h)])
        dst[0][c] = st
        dst[1][c] = jnp.max(st, axis=0, keepdims=True)

    def absorb(c, src, vt):
        m_old = m_ref[c]
        m_new = jnp.maximum(m_old, src[1][c])
        pt = jnp.exp2(src[0][c] - m_new).astype(BF16)
        acc_ref[c, :MLA_V_ROWS] = acc_ref[c, :MLA_V_ROWS] * jnp.exp2(m_old - m_new) + _dot(vt, pt)
        m_ref[c] = m_new

    def kx(j, h):
        return kx_ref[0, pl.ds(pl.multiple_of(j * tk, tk), tk), rows(h)]

    def stage_at(t, j, parity):
        cur, nxt = (buf_a, buf_b) if parity == 0 else (buf_b, buf_a)
        for c, (h, _) in enumerate(chains):
            scores(t, c, kx(j + 1, h), nxt)
            absorb(c, cur, vx_ref[0, j, vrows(h), :])

    trips = (n_chunks - 1) // unroll
    reset()
    for c, (h, _) in enumerate(chains):
        scores(0, c, kx(0, h), buf_a)
    for t in range(n_tiles):
        def body(jj, carry):
            for i in range(unroll):
                stage_at(t, unroll * jj + i, i % 2)
            return carry

        lax.fori_loop(0, trips, body, 0)
        for j in range(unroll * trips, n_chunks - 1):
            stage_at(t, j, j % 2)
        for c, (h, _) in enumerate(chains):
            scores(t, c, ky_ref[0, :, rows(h)], buf_c)
            absorb(c, buf_b, vx_ref[0, n_chunks - 1, vrows(h), :])
        outs = [[None] * n_sub for _ in range(heads)]
        for c, (h, u) in enumerate(chains):
            if t + 1 < n_tiles:
                scores(t + 1, c, kx(0, h), buf_a)
            absorb(c, buf_c, vy_ref[0, 0, vrows(h), :])
            acc = acc_ref[c].T
            outs[h][u] = acc[:, :MLA_V] / acc[:, MLA_V:MLA_V + 1]
        o_ref[0, t * tq:(t + 1) * tq, :] = jnp.concatenate(
            [jnp.concatenate(outs[h], axis=0) for h in range(heads)], axis=-1).astype(o_ref.dtype)
        if t + 1 < n_tiles:
            reset()


def _mla_attention(q, kx, vx, ky, vy, tq, tk, n_tiles, n_sub, unroll):
    b, s, _ = q.shape
    l = ky.shape[1]
    hp = 2
    wd = hp * MLA_HEAD_PAD
    vxt = jnp.swapaxes(vx.reshape(b, s // tk, tk, -1), 2, 3)
    vyt = jnp.swapaxes(vy.reshape(b, 1, l, -1), 2, 3)
    ts = tq // n_tiles // n_sub
    return pl.pallas_call(
        functools.partial(_mla_kernel, tk=tk, heads=hp, n_tiles=n_tiles, n_sub=n_sub, unroll=unroll),
        grid=(b, N_HEADS // hp, s // tq),
        in_specs=[pl.BlockSpec((1, tq, wd), lambda bi, hi, i: (bi, i, hi)),
                  pl.BlockSpec((1, s, wd), lambda bi, hi, i: (bi, 0, hi)),
                  pl.BlockSpec((1, s // tk, wd, tk), lambda bi, hi, i: (bi, 0, hi, 0)),
                  pl.BlockSpec((1, l, wd), lambda bi, hi, i: (bi, 0, hi)),
                  pl.BlockSpec((1, 1, wd, l), lambda bi, hi, i: (bi, 0, hi, 0))],
        out_specs=pl.BlockSpec((1, tq, hp * MLA_V), lambda bi, hi, i: (bi, i, hi)),
        out_shape=jax.ShapeDtypeStruct((b, s, N_HEADS * MLA_V), BF16),
        scratch_shapes=[pltpu.VMEM((hp * n_sub, 1, ts), F32),
                        pltpu.VMEM((hp * n_sub, MLA_HEAD_PAD, ts), F32),
                        pltpu.VMEM((hp * n_sub, tk, ts), F32),
                        pltpu.VMEM((hp * n_sub, tk, ts), F32),
                        pltpu.VMEM((hp * n_sub, l, ts), F32)]
                       + [pltpu.VMEM((hp * n_sub, 1, ts), F32)] * 3,
        compiler_params=_cparams(("parallel", "parallel", "arbitrary")),
        name="mla_attention",
    )(q, kx, vxt, ky, vyt)


def _ctx_attn_kernel(sink_ref, q_ref, k_ref, v_ref, o_ref, *, heads):
    outs = []
    for (q0, q1, k0, k1, v0, v1, sink_idx) in heads:
        q = q_ref[0, :, q0:q1]
        s = _dot_nt(q, k_ref[0, :, k0:k1])
        m = jnp.max(s, axis=-1, keepdims=True)
        if sink_idx is not None:
            sink = jnp.full((1, 1), sink_ref[sink_idx] * LOG2_E, F32)
            m = jnp.maximum(m, sink)
        p = jnp.exp2(s - m)
        l = jnp.sum(p, axis=-1, keepdims=True)
        if sink_idx is not None:
            l = l + jnp.exp2(sink - m)
        outs.append(_dot(p.astype(BF16), v_ref[0, :, v0:v1]) / l)
    o_ref[0] = jnp.concatenate(outs, axis=-1).astype(o_ref.dtype)


def _ctx_attention(q, k, v, heads, sink, name):
    b, l, _ = q.shape
    full = lambda a: pl.BlockSpec((1, l, a.shape[-1]), lambda bi: (bi, 0, 0))
    return pl.pallas_call(
        functools.partial(_ctx_attn_kernel, heads=heads),
        grid=(b,),
        in_specs=[pl.BlockSpec(memory_space=pltpu.SMEM), full(q), full(k), full(v)],
        out_specs=pl.BlockSpec((1, l, GROUP_W), lambda bi: (bi, 0, 0)),
        out_shape=jax.ShapeDtypeStruct((b, l, GROUP_W), BF16),
        compiler_params=_cparams(("parallel",)),
        name=name,
    )(sink, q, k, v)


def _ret_kernel(*refs, chunk, n_chunks, direction, has_prev):
    if has_prev:
        (dec_ref, q_ref, k_ref, kt_ref, v_ref, g_ref, s0_ref, prev_ref, o_ref, sn_ref,
         st_ref, dm_ref, qd_ref, kdt_ref, cd_ref, ob_ref, qk_ref) = refs
    else:
        (dec_ref, q_ref, k_ref, kt_ref, v_ref, g_ref, s0_ref, o_ref, sn_ref,
         st_ref, dm_ref, qd_ref, kdt_ref, cd_ref, ob_ref, qk_ref) = refs
        prev_ref = None
    i = pl.program_id(0)
    c = chunk
    fwd = direction == 0
    batch = range(q_ref.shape[0])

    def head_of(shape, axis):
        return lax.broadcasted_iota(jnp.int32, shape, axis) // HEAD_DIM

    @pl.when(i == 0)
    def _init():
        st_ref[...] = s0_ref[...]
        ii = lax.broadcasted_iota(jnp.int32, (c, c), 0).astype(F32)
        jj = lax.broadcasted_iota(jnp.int32, (c, c), 1).astype(F32)
        diff = (ii - jj) if fwd else (jj - ii)
        pos = lax.broadcasted_iota(jnp.int32, (c, GROUP_W), 0).astype(F32)
        pos_t = lax.broadcasted_iota(jnp.int32, (GROUP_W, c), 1).astype(F32)
        q_steps = (pos + 1.0) if fwd else (c - pos)
        k_steps = (c - 1.0 - pos_t) if fwd else pos_t
        qd = jnp.zeros((c, GROUP_W), F32)
        kdt = jnp.zeros((GROUP_W, c), F32)
        cd = jnp.zeros((GROUP_W, GROUP_W), F32)
        for h in range(N_HEADS):
            dec = dec_ref[direction * N_HEADS + h]
            lg = jax.nn.log_sigmoid(jnp.full((c, c), dec, F32))
            dm_ref[h * c:(h + 1) * c] = jnp.where(diff >= 0, jnp.exp(lg * jnp.maximum(diff, 0.0)), 0.0)
            lgq = jax.nn.log_sigmoid(jnp.full((c, GROUP_W), dec, F32))
            qd = jnp.where(head_of((c, GROUP_W), 1) == h, jnp.exp(lgq * q_steps), qd)
            lgk = jax.nn.log_sigmoid(jnp.full((GROUP_W, c), dec, F32))
            kdt = jnp.where(head_of((GROUP_W, c), 0) == h, jnp.exp(lgk * k_steps), kdt)
            lgc = jax.nn.log_sigmoid(jnp.full((GROUP_W, GROUP_W), dec, F32))
            cd = jnp.where(head_of((GROUP_W, GROUP_W), 0) == h, jnp.exp(lgc * c), cd)
        qd_ref[...] = qd
        kdt_ref[...] = kdt
        cd_ref[...] = cd

    in_head = [head_of((c, GROUP_W), 1) == h for h in range(N_HEADS)]
    head_mask = [jnp.where(mk, 1.0, 0.0).astype(BF16) for mk in in_head]
    same_head = head_of((GROUP_W, GROUP_W), 0) == head_of((GROUP_W, GROUP_W), 1)

    def chunk_off(n):
        idx = n if fwd else n_chunks - 1 - n
        return pl.multiple_of(idx * c, c)

    def qk(bb, n):
        off = chunk_off(n)
        q_all = q_ref[bb, pl.ds(off, c), :]
        q4 = jnp.concatenate([q_all * head_mask[h] for h in range(N_HEADS)], axis=0)
        return _dot_nt(q4, k_ref[bb, pl.ds(off, c), :])

    def gate_and_store(bb, n):
        off = chunk_off(n)
        o = ob_ref[bb]
        oo = o * o
        ms = jnp.zeros((c, GROUP_W), F32)
        for h in range(N_HEADS):
            ms_h = jnp.sum(jnp.where(in_head[h], oo, 0.0), axis=-1, keepdims=True) * (1.0 / HEAD_DIM)
            ms = jnp.where(in_head[h], ms_h, ms)
        res = o * lax.rsqrt(ms + EPS) * _silu(g_ref[bb, pl.ds(off, c), :])
        if has_prev:
            res = res + prev_ref[bb, pl.ds(off, c), :]
        o_ref[bb, pl.ds(off, c), :] = res.astype(o_ref.dtype)

    ob_ref[...] = jnp.zeros(ob_ref.shape, F32)

    for bb in batch:
        qk_ref[bb] = qk(bb, 0)

    def body(n, carry):
        for bb in batch:
            gate_and_store(bb, jnp.maximum(n - 1, 0))
        off = chunk_off(n)
        atts = [(qk_ref[bb] * dm_ref[...]).astype(BF16) for bb in batch]
        for bb in batch:
            qk_ref[bb] = qk(bb, jnp.minimum(n + 1, n_chunks - 1))
        for bb in batch:
            v = v_ref[bb, pl.ds(off, c), :]
            att = atts[bb]
            intra4 = _dot(att, v)
            intra = intra4[:c]
            for h in range(1, N_HEADS):
                intra = jnp.where(in_head[h], intra4[h * c:(h + 1) * c], intra)
            state = st_ref[bb]
            ob_ref[bb] = intra + _dot(q_ref[bb, pl.ds(off, c), :], state.astype(BF16)) * qd_ref[...]
            kk = (kt_ref[bb, :, pl.ds(off, c)].astype(F32) * kdt_ref[...]).astype(BF16)
            st_ref[bb] = state * cd_ref[...] + jnp.where(same_head, _dot(kk, v), 0.0)
        return carry

    lax.fori_loop(0, n_chunks, body, 0)
    for bb in batch:
        gate_and_store(bb, n_chunks - 1)

    @pl.when(i == pl.num_programs(0) - 1)
    def _fin():
        sn_ref[...] = st_ref[...]


def _retention_pass(dec, q, k, v, gates, state0, prev, direction, chunk, n_chunks, out_dtype):
    b, t, _ = q.shape
    tb = chunk * n_chunks
    n = t // tb
    blk = (lambda i: i) if direction == 0 else (lambda i: n - 1 - i)
    tok = pl.BlockSpec((b, tb, GROUP_W), lambda i: (0, blk(i), 0))
    tok_t = pl.BlockSpec((b, GROUP_W, tb), lambda i: (0, 0, blk(i)))
    gate = pl.BlockSpec((b, tb, GROUP_W), lambda i: (0, blk(i), direction))
    st_spec = pl.BlockSpec((b, GROUP_W, GROUP_W), lambda i: (0, 0, 0))
    in_specs = [pl.BlockSpec(memory_space=pltpu.SMEM), tok, tok, tok_t, tok, gate, st_spec]
    args = [dec, q, k, jnp.swapaxes(k, 1, 2), v, gates, state0]
    if prev is not None:
        in_specs.append(tok)
        args.append(prev)
    return pl.pallas_call(
        functools.partial(_ret_kernel, chunk=chunk, n_chunks=n_chunks, direction=direction,
                          has_prev=prev is not None),
        grid=(n,),
        in_specs=in_specs,
        out_specs=[tok, st_spec],
        out_shape=[jax.ShapeDtypeStruct((b, t, GROUP_W), out_dtype),
                   jax.ShapeDtypeStruct((b, GROUP_W, GROUP_W), F32)],
        scratch_shapes=[pltpu.VMEM((b, GROUP_W, GROUP_W), F32),
                        pltpu.VMEM((N_HEADS * chunk, chunk), F32),
                        pltpu.VMEM((chunk, GROUP_W), F32),
                        pltpu.VMEM((GROUP_W, chunk), F32),
                        pltpu.VMEM((GROUP_W, GROUP_W), F32),
                        pltpu.VMEM((b, chunk, GROUP_W), F32),
                        pltpu.VMEM((b, N_HEADS * chunk, chunk), F32)],
        compiler_params=_cparams(("arbitrary",)),
        name="retention_fwd" if direction == 0 else "retention_bwd",
    )(*args)


def _retention(dec, xq, xk, xv, xg, yq, yk, yv, yg, chunk, chunks_per_step):
    b = xq.shape[0]
    zero = jnp.zeros((b, GROUP_W, GROUP_W), F32)
    ny = yq.shape[1] // chunk
    yb, sb = _retention_pass(dec, yq, yk, yv, yg, zero, None, 1, chunk, ny, F32)
    y, sf = _retention_pass(dec, yq, yk, yv, yg, zero, yb, 0, chunk, ny, BF16)
    xb, _ = _retention_pass(dec, xq, xk, xv, xg, sb, None, 1, chunk, chunks_per_step, F32)
    x, _ = _retention_pass(dec, xq, xk, xv, xg, sf, xb, 0, chunk, chunks_per_step, BF16)
    return x, y


def _na_bias_kernel(rpb_ref, o_ref):
    h = pl.program_id(0)
    dr0 = pl.program_id(1)
    c = lax.broadcasted_iota(jnp.int32, (GRID_W, GRID_W), 0)
    kc = lax.broadcasted_iota(jnp.int32, (GRID_W, GRID_W), 1)
    c0 = jnp.clip(c - NA_KC // 2, 0, GRID_W - NA_KC)
    col_in = (kc >= c0) & (kc < c0 + NA_KC)
    dc = jnp.clip(kc - c, -(NA_KC - 1), NA_KC - 1) + NA_KC - 1
    n_dc = 2 * NA_KC - 1
    accs = [jnp.zeros((GRID_W, GRID_W), F32) for _ in range(NA_KR)]
    for d in range(n_dc):
        at_d = dc == d
        for j in range(NA_KR):
            accs[j] = jnp.where(at_d, rpb_ref[(h * (2 * NA_KR - 1) + dr0 + j) * n_dc + d], accs[j])
    for j in range(NA_KR):
        o_ref[0, :, j * GRID_W:(j + 1) * GRID_W] = jnp.where(col_in, accs[j] * LOG2_E, NEG_INF)


def _na_bias_table(rpb):
    return pl.pallas_call(
        _na_bias_kernel,
        grid=(N_HEADS, NA_KR),
        in_specs=[pl.BlockSpec(memory_space=pltpu.SMEM)],
        out_specs=pl.BlockSpec((1, GRID_W, NA_KR * GRID_W), lambda h, r: (r, h, 0)),
        out_shape=jax.ShapeDtypeStruct((NA_KR, N_HEADS * GRID_W, NA_KR * GRID_W), F32),
        compiler_params=_cparams(("parallel", "parallel")),
        name="na_bias_table",
    )(rpb.reshape(-1))


def _na_kernel(q_ref, k_ref, v_ref, ky_ref, vy_ref, tb_ref, o_ref,
               sa_ref, sb_ref, pa_ref, pb_ref, la_ref, lb_ref, *, rows_per_step, n_rows):
    r_base = pl.program_id(1) * rows_per_step
    win = NA_KR * GRID_W

    head_of_lane = lax.broadcasted_iota(jnp.int32, (GRID_W, GROUP_W), 1) // HEAD_DIM
    in_head = [head_of_lane == h for h in range(N_HEADS)]
    head_mask = [jnp.where(mk, 1.0, 0.0).astype(BF16) for mk in in_head]

    def geometry(i):
        r = r_base + i
        r0 = jnp.clip(r - NA_KR // 2, 0, n_rows - NA_KR)
        return pl.multiple_of(r0 * GRID_W, GRID_W), r0 - r + NA_KR - 1

    def row_slice(i):
        return pl.ds(pl.multiple_of(i * GRID_W, GRID_W), GRID_W)

    def scores(i, s_ref):
        koff, dr0 = geometry(i)
        q_all = q_ref[0, row_slice(i), :]
        q4 = jnp.concatenate([q_all * head_mask[h] for h in range(N_HEADS)], axis=0)
        s_ref[:, :win] = _dot_nt(q4, k_ref[0, pl.ds(koff, win), :]) + tb_ref[dr0]
        s_ref[:, win:] = _dot_nt(q4, ky_ref[0])

    def softmax(s_ref, p_ref, l_ref):
        s = s_ref[...]
        p = jnp.exp2(s - _row_max(s))
        l_ref[...] = 1.0 / _row_sum(p)
        p_ref[...] = p.astype(BF16)

    def values(i, p_ref, l_ref):
        koff, _ = geometry(i)
        o4 = (_dot(p_ref[:, :win], v_ref[0, pl.ds(koff, win), :])
              + _dot(p_ref[:, win:], vy_ref[0])) * l_ref[...]
        out = o4[:GRID_W]
        for h in range(1, N_HEADS):
            out = jnp.where(in_head[h], o4[h * GRID_W:(h + 1) * GRID_W], out)
        o_ref[0, row_slice(i), :] = out.astype(o_ref.dtype)

    pb_ref[...] = jnp.zeros(pb_ref.shape, BF16)
    lb_ref[...] = jnp.zeros(lb_ref.shape, F32)
    scores(0, sa_ref)

    def pair_body(tt, carry):
        t = 2 * tt
        values(jnp.maximum(t - 1, 0), pb_ref, lb_ref)
        scores(t + 1, sb_ref)
        softmax(sa_ref, pa_ref, la_ref)
        values(t, pa_ref, la_ref)
        scores(jnp.minimum(t + 2, rows_per_step - 1), sa_ref)
        softmax(sb_ref, pb_ref, lb_ref)
        return carry

    lax.fori_loop(0, rows_per_step // 2, pair_body, 0)
    values(rows_per_step - 1, pb_ref, lb_ref)


def _na_attention(q, k, v, ky, vy, table, rows_per_step):
    b, s, _ = q.shape
    l = ky.shape[1]
    n_rows = s // GRID_W
    tq = rows_per_step * GRID_W
    n_keys = NA_KR * GRID_W + l
    seq = lambda n: pl.BlockSpec((1, n, GROUP_W), lambda bi, i: (bi, 0, 0))
    return pl.pallas_call(
        functools.partial(_na_kernel, rows_per_step=rows_per_step, n_rows=n_rows),
        grid=(b, n_rows // rows_per_step),
        in_specs=[pl.BlockSpec((1, tq, GROUP_W), lambda bi, i: (bi, i, 0)),
                  seq(s), seq(s), seq(l), seq(l), _const_spec(table.shape)],
        out_specs=pl.BlockSpec((1, tq, GROUP_W), lambda bi, i: (bi, i, 0)),
        out_shape=jax.ShapeDtypeStruct((b, s, GROUP_W), BF16),
        scratch_shapes=([pltpu.VMEM((N_HEADS * GRID_W, n_keys), F32)] * 2
                        + [pltpu.VMEM((N_HEADS * GRID_W, n_keys), BF16)] * 2
                        + [pltpu.VMEM((N_HEADS * GRID_W, 1), F32)] * 2),
        compiler_params=_cparams(("parallel", "arbitrary")),
        name="na_attention",
    )(q, k, v, ky, vy, table)


def _swa_kernel(sink_ref, q_ref, kp_ref, kc_ref, kn_ref, vp_ref, vc_ref, vn_ref, ky_ref, vy_ref, o_ref,
                kw_ref, vw_ref, wb_ref, sa_ref, sb_ref, pa_ref, pb_ref, la_ref, lb_ref, *, blocks_per_step):
    step = pl.program_id(1)
    nb = pl.num_programs(1) * blocks_per_step
    bl = SWA_BLOCK
    g = N_HEADS // SWA_KV_HEADS
    tq = blocks_per_step * bl
    kw_ref[0:bl] = kp_ref[0]
    kw_ref[bl:bl + tq] = kc_ref[0]
    kw_ref[bl + tq:] = kn_ref[0]
    vw_ref[0:bl] = vp_ref[0]
    vw_ref[bl:bl + tq] = vc_ref[0]
    vw_ref[bl + tq:] = vn_ref[0]

    qi = lax.broadcasted_iota(jnp.int32, (g * bl, 3 * bl), 0) % bl
    jk = lax.broadcasted_iota(jnp.int32, (g * bl, 3 * bl), 1)
    wb_ref[...] = jnp.where(jnp.abs(jk - bl - qi) <= SWA_WINDOW, 0.0, NEG_INF)
    half = lax.broadcasted_iota(jnp.int32, (g * bl, 1), 0) // bl

    head_of_lane = lax.broadcasted_iota(jnp.int32, (bl, GROUP_W), 1) // HEAD_DIM
    in_head = [head_of_lane == h for h in range(N_HEADS)]
    head_mask = [jnp.where(mk, 1.0, 0.0).astype(BF16) for mk in in_head]

    def blk_slice(j):
        return pl.ds(pl.multiple_of(j * bl, bl), bl)

    def scores(j, s_ref):
        qoff = pl.multiple_of(j * bl, bl)
        q_all = q_ref[0, blk_slice(j), :]
        kw = kw_ref[pl.ds(qoff, 3 * bl), :]
        n = step * blocks_per_step + j
        lo_edge = jnp.where(n == 0, NEG_INF, 0.0)
        hi_edge = jnp.where(n == nb - 1, NEG_INF, 0.0)
        for kh in range(SWA_KV_HEADS):
            q = jnp.concatenate([q_all * head_mask[kh * g + gi] for gi in range(g)], axis=0)
            s = _dot_nt(q, kw) + wb_ref[...]
            s_ref[kh, :, :bl] = s[:, :bl] + lo_edge
            s_ref[kh, :, bl:2 * bl] = s[:, bl:2 * bl]
            s_ref[kh, :, 2 * bl:3 * bl] = s[:, 2 * bl:] + hi_edge
            s_ref[kh, :, 3 * bl:] = _dot_nt(q, ky_ref[0])

    def softmax(s_ref, p_ref, l_ref):
        for kh in range(SWA_KV_HEADS):
            s = s_ref[kh]
            sink = jnp.full((g * bl, 1), sink_ref[kh * g] * LOG2_E, F32)
            for gi in range(1, g):
                sink = jnp.where(half == gi, sink_ref[kh * g + gi] * LOG2_E, sink)
            m = jnp.maximum(_row_max(s), sink)
            p = jnp.exp2(s - m)
            l_ref[kh] = 1.0 / (_row_sum(p) + jnp.exp2(sink - m))
            p_ref[kh] = p.astype(BF16)

    def values(j, p_ref, l_ref):
        vw = vw_ref[pl.ds(pl.multiple_of(j * bl, bl), 3 * bl), :]
        out = jnp.zeros((bl, GROUP_W), F32)
        for kh in range(SWA_KV_HEADS):
            o = (_dot(p_ref[kh, :, :3 * bl], vw) + _dot(p_ref[kh, :, 3 * bl:], vy_ref[0])) * l_ref[kh]
            for gi in range(g):
                out = jnp.where(in_head[kh * g + gi], o[gi * bl:(gi + 1) * bl], out)
        o_ref[0, blk_slice(j), :] = out.astype(o_ref.dtype)

    pb_ref[...] = jnp.zeros(pb_ref.shape, BF16)
    lb_ref[...] = jnp.zeros(lb_ref.shape, F32)
    scores(0, sa_ref)

    def pair_body(tt, carry):
        t = 2 * tt
        values(jnp.maximum(t - 1, 0), pb_ref, lb_ref)
        scores(t + 1, sb_ref)
        softmax(sa_ref, pa_ref, la_ref)
        values(t, pa_ref, la_ref)
        scores(jnp.minimum(t + 2, blocks_per_step - 1), sa_ref)
        softmax(sb_ref, pb_ref, lb_ref)
        return carry

    lax.fori_loop(0, blocks_per_step // 2, pair_body, 0)
    values(blocks_per_step - 1, pb_ref, lb_ref)


def _swa_attention(sink, q, k, v, ky, vy, blocks_per_step):
    b, s, _ = q.shape
    l = ky.shape[1]
    tq = blocks_per_step * SWA_BLOCK
    n_steps = s // tq
    nb = s // SWA_BLOCK
    kvw = GROUP_W
    q_rows = N_HEADS // SWA_KV_HEADS * SWA_BLOCK
    n_keys = 3 * SWA_BLOCK + l
    prev = pl.BlockSpec((1, SWA_BLOCK, kvw), lambda bi, i: (bi, jnp.maximum(i * blocks_per_step - 1, 0), 0))
    cur = pl.BlockSpec((1, tq, kvw), lambda bi, i: (bi, i, 0))
    nxt = pl.BlockSpec((1, SWA_BLOCK, kvw), lambda bi, i: (bi, jnp.minimum((i + 1) * blocks_per_step, nb - 1), 0))
    ctx = pl.BlockSpec((1, l, kvw), lambda bi, i: (bi, 0, 0))
    return pl.pallas_call(
        functools.partial(_swa_kernel, blocks_per_step=blocks_per_step),
        grid=(b, n_steps),
        in_specs=[pl.BlockSpec(memory_space=pltpu.SMEM),
                  pl.BlockSpec((1, tq, GROUP_W), lambda bi, i: (bi, i, 0)),
                  prev, cur, nxt, prev, cur, nxt, ctx, ctx],
        out_specs=pl.BlockSpec((1, tq, GROUP_W), lambda bi, i: (bi, i, 0)),
        out_shape=jax.ShapeDtypeStruct((b, s, GROUP_W), BF16),
        scratch_shapes=[pltpu.VMEM((tq + 2 * SWA_BLOCK, kvw), BF16),
                        pltpu.VMEM((tq + 2 * SWA_BLOCK, kvw), BF16),
                        pltpu.VMEM((q_rows, 3 * SWA_BLOCK), F32)]
                       + [pltpu.VMEM((SWA_KV_HEADS, q_rows, n_keys), F32)] * 2
                       + [pltpu.VMEM((SWA_KV_HEADS, q_rows, n_keys), BF16)] * 2
                       + [pltpu.VMEM((SWA_KV_HEADS, q_rows, 1), F32)] * 2,
        compiler_params=_cparams(("parallel", "arbitrary")),
        name="swa_attention",
    )(sink, q, k, k, k, v, v, v, ky, vy)


def _outffn_kernel(*refs, hidden_chunk, final):
    if final:
        (x_ref, m0_ref, m1_ref, m2_ref, m3_ref, wo_ref, g1_ref, n2_ref, sc_ref, sh_ref, g2_ref,
         w1_ref, w3_ref, w2_ref, fg_ref, o_ref) = refs
    else:
        (x_ref, m0_ref, m1_ref, m2_ref, m3_ref, wo_ref, g1_ref, n2_ref, sc_ref, sh_ref, g2_ref,
         w1_ref, w3_ref, w2_ref, o_ref) = refs
    mix = None
    for gi, m_ref in enumerate((m0_ref, m1_ref, m2_ref, m3_ref)):
        part = _dot(m_ref[0], wo_ref[gi * GROUP_W:(gi + 1) * GROUP_W, :])
        mix = part if mix is None else mix + part
    x1 = x_ref[0] + g1_ref[0] * mix
    hb = (_rms(x1) * (n2_ref[...] * (1.0 + sc_ref[0])) + sh_ref[0]).astype(BF16)
    hidden = w1_ref.shape[1]
    acc = None
    for c0 in range(0, hidden, hidden_chunk):
        a = _dot(hb, w1_ref[:, c0:c0 + hidden_chunk])
        bgate = _dot(hb, w3_ref[:, c0:c0 + hidden_chunk])
        u = (_silu(a) * bgate).astype(BF16)
        part = _dot(u, w2_ref[c0:c0 + hidden_chunk, :])
        acc = part if acc is None else acc + part
    x2 = x1 + g2_ref[0] * acc
    if final:
        x2 = _rms(x2) * fg_ref[...]
    o_ref[0] = x2


def _outffn(x, mixes, wo, g1, n2, sc2, sh2, g2, w1, w3, w2, final_g, tm):
    b, t, d = x.shape
    tok = lambda wd: pl.BlockSpec((1, tm, wd), lambda bi, i: (bi, i, 0))
    vec = pl.BlockSpec((1, 1, d), lambda bi, i: (bi, 0, 0))
    in_specs = ([tok(d)] + [tok(GROUP_W)] * 4
                + [_const_spec(wo.shape), vec, _const_spec((1, d)), vec, vec, vec,
                   _const_spec(w1.shape), _const_spec(w3.shape), _const_spec(w2.shape)])
    args = [x, *mixes, wo, g1, n2, sc2, sh2, g2, w1, w3, w2]
    final = final_g is not None
    if final:
        in_specs.append(_const_spec((1, d)))
        args.append(final_g)
    return pl.pallas_call(
        functools.partial(_outffn_kernel, hidden_chunk=256, final=final),
        grid=(b, t // tm),
        in_specs=in_specs,
        out_specs=tok(d),
        out_shape=jax.ShapeDtypeStruct((b, t, d), F32),
        compiler_params=_cparams(("parallel", "parallel")),
        name="out_proj_ffn_final" if final else "out_proj_ffn",
    )(*args)


def _prep_weights(w_in, mla_w_uq, mla_w_ukv):
    depth, d, _ = w_in.shape
    offs = [0]
    for sz in IN_SIZES:
        offs.append(offs[-1] + sz)
    w_bf = w_in.astype(BF16)
    cols = [w_bf[:, :, offs[i]:offs[i + 1]] for i in range(len(IN_SIZES))]
    cq, ckv, kr, rq, rk, rv, rgf, rgb, nq, nk, nv, sq, sk, sv = cols
    scale = HEAD_DIM ** -0.5
    kr_slot = jnp.concatenate([kr, jnp.zeros((depth, d, LANES - MLA_ROPE), BF16)], axis=-1)
    src = jnp.arange(LANES)[:, None]
    dst = jnp.arange(N_HEADS * MLA_HEAD_PAD)[None, :]
    place = ((src < MLA_ROPE) & (dst % MLA_HEAD_PAD == src + MLA_NOPE)).astype(BF16)

    def per_query_head(t):
        g = N_HEADS // SWA_KV_HEADS
        t = t.reshape(depth, d, SWA_KV_HEADS, 1, HEAD_DIM)
        return jnp.broadcast_to(t, (depth, d, SWA_KV_HEADS, g, HEAD_DIM)).reshape(depth, d, GROUP_W)

    w = jnp.concatenate([cq, ckv, kr_slot, rq, rk * scale, rv, rgf, rgb, nq * scale, nk, nv,
                         sq * scale, per_query_head(sk), per_query_head(sv)], axis=-1).astype(BF16)

    qr = mla_w_uq.shape[1]
    uq = mla_w_uq.reshape(depth, qr, N_HEADS, MLA_NOPE + MLA_ROPE)
    wuq = jnp.concatenate([uq, jnp.zeros((depth, qr, N_HEADS, MLA_HEAD_PAD - MLA_NOPE - MLA_ROPE), F32)],
                          axis=-1).reshape(depth, qr, N_HEADS * MLA_HEAD_PAD).astype(BF16)
    kvr = mla_w_ukv.shape[1]
    ukv = mla_w_ukv.reshape(depth, kvr, N_HEADS, MLA_NOPE + MLA_V)
    zk = jnp.zeros((depth, kvr, N_HEADS, MLA_HEAD_PAD - MLA_NOPE), F32)
    wuk = jnp.concatenate([ukv[..., :MLA_NOPE], zk], axis=-1).reshape(depth, kvr, -1).astype(BF16)
    zv = jnp.zeros((depth, kvr, N_HEADS, MLA_HEAD_PAD - MLA_V), F32)
    wuv = jnp.concatenate([ukv[..., MLA_NOPE:], zv], axis=-1).reshape(depth, kvr, -1).astype(BF16)
    return w, wuq, wuk, wuv, place


def _ctx_head_specs():
    mla = tuple((h * MLA_HEAD_PAD, (h + 1) * MLA_HEAD_PAD, h * MLA_HEAD_PAD, (h + 1) * MLA_HEAD_PAD,
                 h * MLA_HEAD_PAD, h * MLA_HEAD_PAD + MLA_V, None) for h in range(N_HEADS))
    na = tuple((h * HEAD_DIM, (h + 1) * HEAD_DIM) * 3 + (None,) for h in range(N_HEADS))
    swa = tuple((h * HEAD_DIM, (h + 1) * HEAD_DIM) * 3 + (h,) for h in range(N_HEADS))
    return mla, na, swa


def kernel(x, c, ctx, c_ctx, ada_w, ada_b, norm1_g, w_in, mla_q_norm, mla_w_uq, mla_kv_norm, mla_w_ukv,
           ret_decay, na_rpb, swa_sink, w_out, norm2_g, ffn_w1, ffn_w3, ffn_w2, final_norm_g):
    b, s, d = x.shape
    l_ctx = ctx.shape[1]
    depth = ada_w.shape[0]
    assert b + 1 <= 8 and s % 2048 == 0 and l_ctx % 128 == 0

    cond = jnp.concatenate([c, c_ctx[None, :], jnp.zeros((8 - b - 1, d), F32)], axis=0)
    mod = _modulation(cond, ada_w, ada_b)
    tables = _rope_tables(s)
    w_all, wuq_all, wuk_all, wuv_all, place = _prep_weights(w_in, mla_w_uq, mla_w_ukv)
    wo_all = w_out.astype(BF16)
    w1_all, w3_all, w2_all = ffn_w1.astype(BF16), ffn_w3.astype(BF16), ffn_w2.astype(BF16)
    mla_heads, na_heads, swa_heads = _ctx_head_specs()
    no_sink = jnp.zeros((N_HEADS,), F32)

    tm_x = 512
    tm_y = min(256, l_ctx)
    y = ctx
    for l in range(depth):
        mx = [mod[l, :b, j * d:(j + 1) * d][:, None, :] for j in range(6)]
        my = [jnp.broadcast_to(mod[l, b, j * d:(j + 1) * d][None, None, :], (b, 1, d)) for j in range(6)]
        n1 = norm1_g[l][None, :]
        n2 = norm2_g[l][None, :]
        qn = mla_q_norm[l][None, :]
        kvn = mla_kv_norm[l][None, :]
        lw = (w_all[l], qn, kvn, wuq_all[l], wuk_all[l], wuv_all[l], place)

        px = _inproj(x, n1, mx[1], mx[0], *lw, tables, tm_x)
        py = _inproj(y, n1, my[1], my[0], *lw, None, tm_y)
        (xmq, xmk, xmv, xrq, xrk, xrv, xrg, xnq, xnk, xnv, xsq, xsk, xsv) = px
        (ymq, ymk, ymv, yrq, yrk, yrv, yrg, ynq, ynk, ynv, ysq, ysk, ysv) = py

        mla_x = _mla_attention(xmq, xmk, xmv, ymk, ymv, tq=1024, tk=512, n_tiles=2, n_sub=2, unroll=8)
        dec = ret_decay[l].reshape(-1)
        ret_x, ret_y = _retention(dec, xrq, xrk, xrv, xrg, yrq, yrk, yrv, yrg, chunk=128, chunks_per_step=16)
        table = _na_bias_table(na_rpb[l])
        na_x = _na_attention(xnq, xnk, xnv, ynk, ynv, table, rows_per_step=32)
        swa_x = _swa_attention(swa_sink[l], xsq, xsk, xsv, ysk, ysv, blocks_per_step=16)

        last = l == depth - 1
        x = _outffn(x, (mla_x, ret_x, na_x, swa_x), wo_all[l], mx[2], n2, mx[4], mx[3], mx[5],
                    w1_all[l], w3_all[l], w2_all[l], final_norm_g[None, :] if last else None, tm_x)
        if not last:
            mla_y = _ctx_attention(ymq, ymk, ymv, mla_heads, no_sink, "mla_ctx_attention")
            na_y = _ctx_attention(ynq, ynk, ynv, na_heads, no_sink, "na_ctx_attention")
            swa_y = _ctx_attention(ysq, ysk, ysv, swa_heads, swa_sink[l], "swa_ctx_attention")
            y = _outffn(y, (mla_y, ret_y, na_y, swa_y), wo_all[l], my[2], n2, my[4], my[3], my[5],
                        w1_all[l], w3_all[l], w2_all[l], None, tm_y)
    return x
```

```python
import functools

import jax
import jax.numpy as jnp
from jax import lax
from jax.experimental import pallas as pl
from jax.experimental.pallas import tpu as pltpu

F32 = jnp.float32
BF16 = jnp.bfloat16

GRID_W = 64
HEAD_DIM = 64
N_HEADS = 4
GROUP_W = N_HEADS * HEAD_DIM
MLA_Q_RANK = 256
MLA_KV_RANK = 128
MLA_NOPE = 64
MLA_ROPE = 32
MLA_V = 64
MLA_HEAD_PAD = 128
MLA_V_ROWS = MLA_V + 16
NA_KR = 8
NA_KC = 16
SWA_KV_HEADS = 2
SWA_WINDOW = 128
SWA_BLOCK = 128
ROPE_THETA = 10000.0
EPS = 1e-6
NEG_INF = -1e30
LOG2_E = 1.4426950408889634
LANES = 128
VMEM_LIMIT = 56 * 1024 * 1024

IN_SIZES = (MLA_Q_RANK, MLA_KV_RANK, MLA_ROPE,
            GROUP_W, GROUP_W, GROUP_W, GROUP_W, GROUP_W,
            GROUP_W, GROUP_W, GROUP_W,
            GROUP_W, SWA_KV_HEADS * HEAD_DIM, SWA_KV_HEADS * HEAD_DIM)

_O_CQ = 0
_O_CKV = _O_CQ + MLA_Q_RANK
_O_KR = _O_CKV + MLA_KV_RANK
_O_RQK = _O_KR + LANES
_O_RV = _O_RQK + 2 * GROUP_W
_O_RG = _O_RV + GROUP_W
_O_NA = _O_RG + 2 * GROUP_W
_O_SQK = _O_NA + 3 * GROUP_W
_O_SV = _O_SQK + 2 * GROUP_W
_IN_COLS = _O_SV + GROUP_W


def _cparams(sem):
    return pltpu.CompilerParams(dimension_semantics=sem, vmem_limit_bytes=VMEM_LIMIT)


def _dot(a, b):
    return jnp.dot(a, b, preferred_element_type=F32)


def _dot_nt(a, b):
    return lax.dot_general(a, b, (((1,), (1,)), ((), ())), preferred_element_type=F32)


def _rms(x):
    return x * lax.rsqrt(jnp.mean(x * x, axis=-1, keepdims=True) + EPS)


def _silu(x):
    return x * jax.nn.sigmoid(x)


def _lane_chunks(arrays):
    return [a[:, j * LANES:(j + 1) * LANES] for a in arrays for j in range(a.shape[-1] // LANES)]


def _row_max(*arrays):
    return jnp.max(functools.reduce(jnp.maximum, _lane_chunks(arrays)), axis=-1, keepdims=True)


def _row_sum(*arrays):
    return jnp.sum(functools.reduce(jnp.add, _lane_chunks(arrays)), axis=-1, keepdims=True)


def _mod_kernel(c_ref, w_ref, b_ref, o_ref):
    o_ref[0] = _dot(_silu(c_ref[...]), w_ref[0]) + b_ref[0]


def _modulation(cond, ada_w, ada_b):
    depth, d, d6 = ada_w.shape
    n = d6 // d
    return pl.pallas_call(
        _mod_kernel,
        grid=(depth, n),
        in_specs=[pl.BlockSpec((8, d), lambda l, j: (0, 0)),
                  pl.BlockSpec((1, d, d), lambda l, j: (l, 0, j)),
                  pl.BlockSpec((1, 1, d), lambda l, j: (l, 0, j))],
        out_specs=pl.BlockSpec((1, 8, d), lambda l, j: (l, 0, j)),
        out_shape=jax.ShapeDtypeStruct((depth, 8, d6), F32),
        compiler_params=_cparams(("parallel", "parallel")),
        name="ada_modulation",
    )(cond, ada_w, ada_b.reshape(depth, 1, d6))


def _rope_tables(seq):
    n_rows = seq // GRID_W

    def parts(pos, d):
        inv = ROPE_THETA ** (-jnp.arange(0, d, 2, dtype=F32) / d)
        ang = pos.astype(F32)[:, None] * inv[None, :]
        z = jnp.zeros_like(ang)
        return (jnp.concatenate([jnp.cos(ang), jnp.cos(ang)], axis=-1),
                jnp.concatenate([z, jnp.sin(ang)], axis=-1),
                jnp.concatenate([-jnp.sin(ang), z], axis=-1))

    def expand(by_row, by_col):
        w = by_row.shape[-1]
        r = jnp.broadcast_to(by_row[:, None, :], (n_rows, GRID_W, w))
        c = jnp.broadcast_to(by_col[None, :, :], (n_rows, GRID_W, w))
        return jnp.concatenate([r, c], axis=-1).reshape(seq, 2 * w)

    def tables(d, fill):
        per_head = [expand(a, b) for a, b in zip(parts(jnp.arange(n_rows), d), parts(jnp.arange(GRID_W), d))]
        return [fill(t, i) for i, t in enumerate(per_head)]

    def two_heads(t, _):
        return jnp.concatenate([t, t], axis=-1)

    def mla_slot(t, i):
        lead = (jnp.ones if i == 0 else jnp.zeros)((seq, MLA_NOPE), F32)
        tail = (jnp.ones if i == 0 else jnp.zeros)((seq, MLA_HEAD_PAD - MLA_NOPE - MLA_ROPE), F32)
        return jnp.concatenate([lead, t, tail], axis=-1)

    def kr_slot(t, i):
        tail = (jnp.ones if i == 0 else jnp.zeros)((seq, LANES - MLA_ROPE), F32)
        return jnp.concatenate([t, tail], axis=-1)

    return tuple(tables(HEAD_DIM // 2, two_heads) + tables(MLA_ROPE // 2, mla_slot)
                 + tables(MLA_ROPE // 2, kr_slot))


def _rope(x, cos, s_prev, s_next, d):
    out = []
    for j in range(x.shape[-1] // LANES):
        xc = x[:, j * LANES:(j + 1) * LANES]
        out.append(xc * cos + pltpu.roll(xc, d, 1) * s_prev + pltpu.roll(xc, LANES - d, 1) * s_next)
    return out[0] if len(out) == 1 else jnp.concatenate(out, axis=-1)


def _inproj_kernel(*refs, rotate, mla_scale):
    if rotate:
        (x_ref, g_ref, sc_ref, sh_ref, w_ref, qn_ref, kvn_ref, wuq_ref, wuk_ref, wuv_ref, place_ref,
         c64_ref, p64_ref, n64_ref, cm_ref, pm_ref, nm_ref, ckr_ref, pkr_ref, nkr_ref,
         mq_ref, mk_ref, mv_ref, rq_ref, rk_ref, rv_ref, rg_ref,
         nq_ref, nk_ref, nv_ref, sq_ref, sk_ref, sv_ref) = refs
    else:
        (x_ref, g_ref, sc_ref, sh_ref, w_ref, qn_ref, kvn_ref, wuq_ref, wuk_ref, wuv_ref, place_ref,
         mq_ref, mk_ref, mv_ref, rq_ref, rk_ref, rv_ref, rg_ref,
         nq_ref, nk_ref, nv_ref, sq_ref, sk_ref, sv_ref) = refs

    x = x_ref[0]
    h = _rms(x) * (g_ref[...] * (1.0 + sc_ref[0])) + sh_ref[0]
    hb = h.astype(BF16)

    def proj(lo, hi):
        return _dot(hb, w_ref[:, lo:hi])

    def rope64(v):
        if not rotate:
            return v
        return _rope(v, c64_ref[...], p64_ref[...], n64_ref[...], HEAD_DIM // 4)

    def rope_mla(v):
        if not rotate:
            return v
        return _rope(v, cm_ref[...], pm_ref[...], nm_ref[...], MLA_ROPE // 4)

    cq = (_rms(proj(_O_CQ, _O_CKV)) * qn_ref[...]).astype(BF16)
    ckv_kr = proj(_O_CKV, _O_RQK)
    ckv = (_rms(ckv_kr[:, :MLA_KV_RANK]) * kvn_ref[...]).astype(BF16)
    kr = ckv_kr[:, MLA_KV_RANK:]
    if rotate:
        kr = _rope(kr, ckr_ref[...], pkr_ref[...], nkr_ref[...], MLA_ROPE // 4)
    kr = kr.astype(BF16)

    rqk = proj(_O_RQK, _O_RV)
    rq_ref[0] = rope64(rqk[:, :GROUP_W]).astype(BF16)
    rk_ref[0] = rope64(rqk[:, GROUP_W:]).astype(BF16)
    rv_ref[0] = proj(_O_RV, _O_RG).astype(BF16)

    na = proj(_O_NA, _O_SQK)
    nq_ref[0] = (na[:, :GROUP_W] * LOG2_E).astype(BF16)
    nk_ref[0] = na[:, GROUP_W:2 * GROUP_W].astype(BF16)
    nv_ref[0] = na[:, 2 * GROUP_W:].astype(BF16)

    sqk = proj(_O_SQK, _O_SV)
    sq_ref[0] = (rope64(sqk[:, :GROUP_W]) * LOG2_E).astype(BF16)
    sk_ref[0] = rope64(sqk[:, GROUP_W:]).astype(BF16)
    sv_ref[0] = proj(_O_SV, _IN_COLS).astype(BF16)

    q = rope_mla(_dot(cq, wuq_ref[...])) * mla_scale
    mq_ref[0] = q.astype(BF16)
    k = _dot(ckv, wuk_ref[...]) + _dot(kr, place_ref[...])
    mk_ref[0] = k.astype(BF16)
    v = _dot(ckv, wuv_ref[...])
    lane = lax.broadcasted_iota(jnp.int32, v.shape, 1)
    mv_ref[0] = jnp.where(lane % MLA_HEAD_PAD >= MLA_V, 1.0, v).astype(BF16)

    rg_ref[0] = proj(_O_RG, _O_NA)


def _const_spec(shape):
    nd = len(shape)
    return pl.BlockSpec(shape, lambda *_: (0,) * nd, pipeline_mode=pl.Buffered(1))


def _inproj(x, gain, scale, shift, w, qn, kvn, wuq, wuk, wuv, place, tables, tm):
    b, t, d = x.shape
    rotate = tables is not None
    kv_w = GROUP_W
    mla_w = N_HEADS * MLA_HEAD_PAD
    tok = lambda wd: pl.BlockSpec((1, tm, wd), lambda bi, i: (bi, i, 0))
    vec = pl.BlockSpec((1, 1, d), lambda bi, i: (bi, 0, 0))
    in_specs = [tok(d), _const_spec((1, d)), vec, vec, _const_spec(w.shape),
                _const_spec(qn.shape), _const_spec(kvn.shape), _const_spec(wuq.shape),
                _const_spec(wuk.shape), _const_spec(wuv.shape), _const_spec(place.shape)]
    args = [x, gain, scale, shift, w, qn, kvn, wuq, wuk, wuv, place]
    if rotate:
        in_specs += [pl.BlockSpec((tm, LANES), lambda bi, i: (i, 0))] * len(tables)
        args += list(tables)
    widths = [mla_w, mla_w, mla_w, GROUP_W, GROUP_W, GROUP_W, 2 * GROUP_W,
              GROUP_W, GROUP_W, GROUP_W, GROUP_W, kv_w, kv_w]
    dtypes = [BF16] * 6 + [F32] + [BF16] * 6
    return pl.pallas_call(
        functools.partial(_inproj_kernel, rotate=rotate, mla_scale=(MLA_NOPE + MLA_ROPE) ** -0.5 * LOG2_E),
        grid=(b, t // tm),
        in_specs=in_specs,
        out_specs=[tok(wd) for wd in widths],
        out_shape=[jax.ShapeDtypeStruct((b, t, wd), dt) for wd, dt in zip(widths, dtypes)],
        compiler_params=_cparams(("parallel", "parallel")),
        name="in_proj_rot" if rotate else "in_proj_ctx",
    )(*args)


def _mla_kernel(q_ref, kx_ref, vx_ref, ky_ref, vy_ref, o_ref, m_ref, acc_ref, sa_ref, sb_ref, sc_ref,
                ma_ref, mb_ref, mc_ref, *, tk, heads, n_tiles, n_sub, unroll):
    n_chunks = kx_ref.shape[1] // tk
    assert n_chunks % 2 == 0
    tq = q_ref.shape[1] // n_tiles
    ts = tq // n_sub
    chains = [(h, u) for h in range(heads) for u in range(n_sub)]

    def reset():
        for c in range(len(chains)):
            m_ref[c] = jnp.full(m_ref.shape[1:], NEG_INF, F32)
            acc_ref[c] = jnp.zeros(acc_ref.shape[1:], F32)

    def rows(h):
        return slice(h * MLA_HEAD_PAD, (h + 1) * MLA_HEAD_PAD)

    def vrows(h):
        return slice(h * MLA_HEAD_PAD, h * MLA_HEAD_PAD + MLA_V_ROWS)

    buf_a, buf_b, buf_c = (sa_ref, ma_ref), (sb_ref, mb_ref), (sc_ref, mc_ref)

    def scores(t, c, k, dst):
        h, u = chains[c]
        q0 = t * tq + u * ts
        st = _dot_nt(k, q_ref[0, q0:q0 + ts, rows(h)])
        dst[0][c] = st
        dst[1][c] = jnp.max(st, axis=0, keepdims=True)

    def absorb(c, src, vt):
        m_old = m_ref[c]
        m_new = jnp.maximum(m_old, src[1][c])
        pt = jnp.exp2(src[0][c] - m_new).astype(BF16)
        acc_ref[c, :MLA_V_ROWS] = acc_ref[c, :MLA_V_ROWS] * jnp.exp2(m_old - m_new) + _dot(vt, pt)
        m_ref[c] = m_new

    def kx(j, h):
        return kx_ref[0, pl.ds(pl.multiple_of(j * tk, tk), tk), rows(h)]

    def stage_at(t, j, parity):
        cur, nxt = (buf_a, buf_b) if parity == 0 else (buf_b, buf_a)
        for c, (h, _) in enumerate(chains):
            scores(t, c, kx(j + 1, h), nxt)
            absorb(c, cur, vx_ref[0, j, vrows(h), :])

    trips = (n_chunks - 1) // unroll
    reset()
    for c, (h, _) in enumerate(chains):
        scores(0, c, kx(0, h), buf_a)
    for t in range(n_tiles):
        def body(jj, carry):
            for i in range(unroll):
                stage_at(t, unroll * jj + i, i % 2)
            return carry

        lax.fori_loop(0, trips, body, 0)
        for j in range(unroll * trips, n_chunks - 1):
            stage_at(t, j, j % 2)
        for c, (h, _) in enumerate(chains):
            scores(t, c, ky_ref[0, :, rows(h)], buf_c)
            absorb(c, buf_b, vx_ref[0, n_chunks - 1, vrows(h), :])
        outs = [[None] * n_sub for _ in range(heads)]
        for c, (h, u) in enumerate(chains):
            if t + 1 < n_tiles:
                scores(t + 1, c, kx(0, h), buf_a)
            absorb(c, buf_c, vy_ref[0, 0, vrows(h), :])
            acc = acc_ref[c].T
            outs[h][u] = acc[:, :MLA_V] / acc[:, MLA_V:MLA_V + 1]
        o_ref[0, t * tq:(t + 1) * tq, :] = jnp.concatenate(
            [jnp.concatenate(outs[h], axis=0) for h in range(heads)], axis=-1).astype(o_ref.dtype)
        if t + 1 < n_tiles:
            reset()


def _mla_attention(q, kx, vx, ky, vy, tq, tk, n_tiles, n_sub, unroll):
    b, s, _ = q.shape
    l = ky.shape[1]
    hp = 2
    wd = hp * MLA_HEAD_PAD
    vxt = jnp.swapaxes(vx.reshape(b, s // tk, tk, -1), 2, 3)
    vyt = jnp.swapaxes(vy.reshape(b, 1, l, -1), 2, 3)
    ts = tq // n_tiles // n_sub
    return pl.pallas_call(
        functools.partial(_mla_kernel, tk=tk, heads=hp, n_tiles=n_tiles, n_sub=n_sub, unroll=unroll),
        grid=(b, N_HEADS // hp, s // tq),
        in_specs=[pl.BlockSpec((1, tq, wd), lambda bi, hi, i: (bi, i, hi)),
                  pl.BlockSpec((1, s, wd), lambda bi, hi, i: (bi, 0, hi)),
                  pl.BlockSpec((1, s // tk, wd, tk), lambda bi, hi, i: (bi, 0, hi, 0)),
                  pl.BlockSpec((1, l, wd), lambda bi, hi, i: (bi, 0, hi)),
                  pl.BlockSpec((1, 1, wd, l), lambda bi, hi, i: (bi, 0, hi, 0))],
        out_specs=pl.BlockSpec((1, tq, hp * MLA_V), lambda bi, hi, i: (bi, i, hi)),
        out_shape=jax.ShapeDtypeStruct((b, s, N_HEADS * MLA_V), BF16),
        scratch_shapes=[pltpu.VMEM((hp * n_sub, 1, ts), F32),
                        pltpu.VMEM((hp * n_sub, MLA_HEAD_PAD, ts), F32),
                        pltpu.VMEM((hp * n_sub, tk, ts), F32),
                        pltpu.VMEM((hp * n_sub, tk, ts), F32),
                        pltpu.VMEM((hp * n_sub, l, ts), F32)]
                       + [pltpu.VMEM((hp * n_sub, 1, ts), F32)] * 3,
        compiler_params=_cparams(("parallel", "parallel", "arbitrary")),
        name="mla_attention",
    )(q, kx, vxt, ky, vyt)


def _ctx_attn_kernel(sink_ref, q_ref, k_ref, v_ref, o_ref, *, heads):
    outs = []
    for (q0, q1, k0, k1, v0, v1, sink_idx) in heads:
        q = q_ref[0, :, q0:q1]
        s = _dot_nt(q, k_ref[0, :, k0:k1])
        m = jnp.max(s, axis=-1, keepdims=True)
        if sink_idx is not None:
            sink = jnp.full((1, 1), sink_ref[sink_idx] * LOG2_E, F32)
            m = jnp.maximum(m, sink)
        p = jnp.exp2(s - m)
        l = jnp.sum(p, axis=-1, keepdims=True)
        if sink_idx is not None:
            l = l + jnp.exp2(sink - m)
        outs.append(_dot(p.astype(BF16), v_ref[0, :, v0:v1]) / l)
    o_ref[0] = jnp.concatenate(outs, axis=-1).astype(o_ref.dtype)


def _ctx_attention(q, k, v, heads, sink, name):
    b, l, _ = q.shape
    full = lambda a: pl.BlockSpec((1, l, a.shape[-1]), lambda bi: (bi, 0, 0))
    return pl.pallas_call(
        functools.partial(_ctx_attn_kernel, heads=heads),
        grid=(b,),
        in_specs=[pl.BlockSpec(memory_space=pltpu.SMEM), full(q), full(k), full(v)],
        out_specs=pl.BlockSpec((1, l, GROUP_W), lambda bi: (bi, 0, 0)),
        out_shape=jax.ShapeDtypeStruct((b, l, GROUP_W), BF16),
        compiler_params=_cparams(("parallel",)),
        name=name,
    )(sink, q, k, v)


def _ret_kernel(*refs, chunk, n_chunks, direction, has_prev):
    if has_prev:
        (dec_ref, q_ref, k_ref, kt_ref, v_ref, g_ref, s0_ref, prev_ref, o_ref, sn_ref,
         st_ref, dm_ref, qd_ref, kdt_ref, cd_ref, ob_ref, qk_ref) = refs
    else:
        (dec_ref, q_ref, k_ref, kt_ref, v_ref, g_ref, s0_ref, o_ref, sn_ref,
         st_ref, dm_ref, qd_ref, kdt_ref, cd_ref, ob_ref, qk_ref) = refs
        prev_ref = None
    i = pl.program_id(0)
    c = chunk
    fwd = direction == 0
    batch = range(q_ref.shape[0])

    def head_of(shape, axis):
        return lax.broadcasted_iota(jnp.int32, shape, axis) // HEAD_DIM

    @pl.when(i == 0)
    def _init():
        st_ref[...] = s0_ref[...]
        ii = lax.broadcasted_iota(jnp.int32, (c, c), 0).astype(F32)
        jj = lax.broadcasted_iota(jnp.int32, (c, c), 1).astype(F32)
        diff = (ii - jj) if fwd else (jj - ii)
        pos = lax.broadcasted_iota(jnp.int32, (c, GROUP_W), 0).astype(F32)
        pos_t = lax.broadcasted_iota(jnp.int32, (GROUP_W, c), 1).astype(F32)
        q_steps = (pos + 1.0) if fwd else (c - pos)
        k_steps = (c - 1.0 - pos_t) if fwd else pos_t
        qd = jnp.zeros((c, GROUP_W), F32)
        kdt = jnp.zeros((GROUP_W, c), F32)
        cd = jnp.zeros((GROUP_W, GROUP_W), F32)
        for h in range(N_HEADS):
            dec = dec_ref[direction * N_HEADS + h]
            lg = jax.nn.log_sigmoid(jnp.full((c, c), dec, F32))
            dm_ref[h * c:(h + 1) * c] = jnp.where(diff >= 0, jnp.exp(lg * jnp.maximum(diff, 0.0)), 0.0)
            lgq = jax.nn.log_sigmoid(jnp.full((c, GROUP_W), dec, F32))
            qd = jnp.where(head_of((c, GROUP_W), 1) == h, jnp.exp(lgq * q_steps), qd)
            lgk = jax.nn.log_sigmoid(jnp.full((GROUP_W, c), dec, F32))
            kdt = jnp.where(head_of((GROUP_W, c), 0) == h, jnp.exp(lgk * k_steps), kdt)
            lgc = jax.nn.log_sigmoid(jnp.full((GROUP_W, GROUP_W), dec, F32))
            cd = jnp.where(head_of((GROUP_W, GROUP_W), 0) == h, jnp.exp(lgc * c), cd)
        qd_ref[...] = qd
        kdt_ref[...] = kdt
        cd_ref[...] = cd

    in_head = [head_of((c, GROUP_W), 1) == h for h in range(N_HEADS)]
    head_mask = [jnp.where(mk, 1.0, 0.0).astype(BF16) for mk in in_head]
    same_head = head_of((GROUP_W, GROUP_W), 0) == head_of((GROUP_W, GROUP_W), 1)

    def chunk_off(n):
        idx = n if fwd else n_chunks - 1 - n
        return pl.multiple_of(idx * c, c)

    def qk(bb, n):
        off = chunk_off(n)
        q_all = q_ref[bb, pl.ds(off, c), :]
        q4 = jnp.concatenate([q_all * head_mask[h] for h in range(N_HEADS)], axis=0)
        return _dot_nt(q4, k_ref[bb, pl.ds(off, c), :])

    def gate_and_store(bb, n):
        off = chunk_off(n)
        o = ob_ref[bb]
        oo = o * o
        ms = jnp.zeros((c, GROUP_W), F32)
        for h in range(N_HEADS):
            ms_h = jnp.sum(jnp.where(in_head[h], oo, 0.0), axis=-1, keepdims=True) * (1.0 / HEAD_DIM)
            ms = jnp.where(in_head[h], ms_h, ms)
        res = o * lax.rsqrt(ms + EPS) * _silu(g_ref[bb, pl.ds(off, c), :])
        if has_prev:
            res = res + prev_ref[bb, pl.ds(off, c), :]
        o_ref[bb, pl.ds(off, c), :] = res.astype(o_ref.dtype)

    ob_ref[...] = jnp.zeros(ob_ref.shape, F32)

    for bb in batch:
        qk_ref[bb] = qk(bb, 0)

    def body(n, carry):
        for bb in batch:
            gate_and_store(bb, jnp.maximum(n - 1, 0))
        off = chunk_off(n)
        atts = [(qk_ref[bb] * dm_ref[...]).astype(BF16) for bb in batch]
        for bb in batch:
            qk_ref[bb] = qk(bb, jnp.minimum(n + 1, n_chunks - 1))
        for bb in batch:
            v = v_ref[bb, pl.ds(off, c), :]
            att = atts[bb]
            intra4 = _dot(att, v)
            intra = intra4[:c]
            for h in range(1, N_HEADS):
                intra = jnp.where(in_head[h], intra4[h * c:(h + 1) * c], intra)
            state = st_ref[bb]
            ob_ref[bb] = intra + _dot(q_ref[bb, pl.ds(off, c), :], state.astype(BF16)) * qd_ref[...]
            kk = (kt_ref[bb, :, pl.ds(off, c)].astype(F32) * kdt_ref[...]).astype(BF16)
            st_ref[bb] = state * cd_ref[...] + jnp.where(same_head, _dot(kk, v), 0.0)
        return carry

    lax.fori_loop(0, n_chunks, body, 0)
    for bb in batch:
        gate_and_store(bb, n_chunks - 1)

    @pl.when(i == pl.num_programs(0) - 1)
    def _fin():
        sn_ref[...] = st_ref[...]


def _retention_pass(dec, q, k, v, gates, state0, prev, direction, chunk, n_chunks, out_dtype):
    b, t, _ = q.shape
    tb = chunk * n_chunks
    n = t // tb
    blk = (lambda i: i) if direction == 0 else (lambda i: n - 1 - i)
    tok = pl.BlockSpec((b, tb, GROUP_W), lambda i: (0, blk(i), 0))
    tok_t = pl.BlockSpec((b, GROUP_W, tb), lambda i: (0, 0, blk(i)))
    gate = pl.BlockSpec((b, tb, GROUP_W), lambda i: (0, blk(i), direction))
    st_spec = pl.BlockSpec((b, GROUP_W, GROUP_W), lambda i: (0, 0, 0))
    in_specs = [pl.BlockSpec(memory_space=pltpu.SMEM), tok, tok, tok_t, tok, gate, st_spec]
    args = [dec, q, k, jnp.swapaxes(k, 1, 2), v, gates, state0]
    if prev is not None:
        in_specs.append(tok)
        args.append(prev)
    return pl.pallas_call(
        functools.partial(_ret_kernel, chunk=chunk, n_chunks=n_chunks, direction=direction,
                          has_prev=prev is not None),
        grid=(n,),
        in_specs=in_specs,
        out_specs=[tok, st_spec],
        out_shape=[jax.ShapeDtypeStruct((b, t, GROUP_W), out_dtype),
                   jax.ShapeDtypeStruct((b, GROUP_W, GROUP_W), F32)],
        scratch_shapes=[pltpu.VMEM((b, GROUP_W, GROUP_W), F32),
                        pltpu.VMEM((N_HEADS * chunk, chunk), F32),
                        pltpu.VMEM((chunk, GROUP_W), F32),
                        pltpu.VMEM((GROUP_W, chunk), F32),
                        pltpu.VMEM((GROUP_W, GROUP_W), F32),
                        pltpu.VMEM((b, chunk, GROUP_W), F32),
                        pltpu.VMEM((b, N_HEADS * chunk, chunk), F32)],
        compiler_params=_cparams(("arbitrary",)),
        name="retention_fwd" if direction == 0 else "retention_bwd",
    )(*args)


def _retention(dec, xq, xk, xv, xg, yq, yk, yv, yg, chunk, chunks_per_step):
    b = xq.shape[0]
    zero = jnp.zeros((b, GROUP_W, GROUP_W), F32)
    ny = yq.shape[1] // chunk
    yb, sb = _retention_pass(dec, yq, yk, yv, yg, zero, None, 1, chunk, ny, F32)
    y, sf = _retention_pass(dec, yq, yk, yv, yg, zero, yb, 0, chunk, ny, BF16)
    xb, _ = _retention_pass(dec, xq, xk, xv, xg, sb, None, 1, chunk, chunks_per_step, F32)
    x, _ = _retention_pass(dec, xq, xk, xv, xg, sf, xb, 0, chunk, chunks_per_step, BF16)
    return x, y


def _na_bias_kernel(rpb_ref, o_ref):
    h = pl.program_id(0)
    dr0 = pl.program_id(1)
    c = lax.broadcasted_iota(jnp.int32, (GRID_W, GRID_W), 0)
    kc = lax.broadcasted_iota(jnp.int32, (GRID_W, GRID_W), 1)
    c0 = jnp.clip(c - NA_KC // 2, 0, GRID_W - NA_KC)
    col_in = (kc >= c0) & (kc < c0 + NA_KC)
    dc = jnp.clip(kc - c, -(NA_KC - 1), NA_KC - 1) + NA_KC - 1
    n_dc = 2 * NA_KC - 1
    accs = [jnp.zeros((GRID_W, GRID_W), F32) for _ in range(NA_KR)]
    for d in range(n_dc):
        at_d = dc == d
        for j in range(NA_KR):
            accs[j] = jnp.where(at_d, rpb_ref[(h * (2 * NA_KR - 1) + dr0 + j) * n_dc + d], accs[j])
    for j in range(NA_KR):
        o_ref[0, :, j * GRID_W:(j + 1) * GRID_W] = jnp.where(col_in, accs[j] * LOG2_E, NEG_INF)


def _na_bias_table(rpb):
    return pl.pallas_call(
        _na_bias_kernel,
        grid=(N_HEADS, NA_KR),
        in_specs=[pl.BlockSpec(memory_space=pltpu.SMEM)],
        out_specs=pl.BlockSpec((1, GRID_W, NA_KR * GRID_W), lambda h, r: (r, h, 0)),
        out_shape=jax.ShapeDtypeStruct((NA_KR, N_HEADS * GRID_W, NA_KR * GRID_W), F32),
        compiler_params=_cparams(("parallel", "parallel")),
        name="na_bias_table",
    )(rpb.reshape(-1))


def _na_kernel(q_ref, k_ref, v_ref, ky_ref, vy_ref, tb_ref, o_ref,
               sa_ref, sb_ref, pa_ref, pb_ref, la_ref, lb_ref, *, rows_per_step, n_rows):
    r_base = pl.program_id(1) * rows_per_step
    win = NA_KR * GRID_W

    head_of_lane = lax.broadcasted_iota(jnp.int32, (GRID_W, GROUP_W), 1) // HEAD_DIM
    in_head = [head_of_lane == h for h in range(N_HEADS)]
    head_mask = [jnp.where(mk, 1.0, 0.0).astype(BF16) for mk in in_head]

    def geometry(i):
        r = r_base + i
        r0 = jnp.clip(r - NA_KR // 2, 0, n_rows - NA_KR)
        return pl.multiple_of(r0 * GRID_W, GRID_W), r0 - r + NA_KR - 1

    def row_slice(i):
        return pl.ds(pl.multiple_of(i * GRID_W, GRID_W), GRID_W)

    def scores(i, s_ref):
        koff, dr0 = geometry(i)
        q_all = q_ref[0, row_slice(i), :]
        q4 = jnp.concatenate([q_all * head_mask[h] for h in range(N_HEADS)], axis=0)
        s_ref[:, :win] = _dot_nt(q4, k_ref[0, pl.ds(koff, win), :]) + tb_ref[dr0]
        s_ref[:, win:] = _dot_nt(q4, ky_ref[0])

    def softmax(s_ref, p_ref, l_ref):
        s = s_ref[...]
        p = jnp.exp2(s - _row_max(s))
        l_ref[...] = 1.0 / _row_sum(p)
        p_ref[...] = p.astype(BF16)

    def values(i, p_ref, l_ref):
        koff, _ = geometry(i)
        o4 = (_dot(p_ref[:, :win], v_ref[0, pl.ds(koff, win), :])
              + _dot(p_ref[:, win:], vy_ref[0])) * l_ref[...]
        out = o4[:GRID_W]
        for h in range(1, N_HEADS):
            out = jnp.where(in_head[h], o4[h * GRID_W:(h + 1) * GRID_W], out)
        o_ref[0, row_slice(i), :] = out.astype(o_ref.dtype)

    pb_ref[...] = jnp.zeros(pb_ref.shape, BF16)
    lb_ref[...] = jnp.zeros(lb_ref.shape, F32)
    scores(0, sa_ref)

    def pair_body(tt, carry):
        t = 2 * tt
        values(jnp.maximum(t - 1, 0), pb_ref, lb_ref)
        scores(t + 1, sb_ref)
        softmax(sa_ref, pa_ref, la_ref)
        values(t, pa_ref, la_ref)
        scores(jnp.minimum(t + 2, rows_per_step - 1), sa_ref)
        softmax(sb_ref, pb_ref, lb_ref)
        return carry

    lax.fori_loop(0, rows_per_step // 2, pair_body, 0)
    values(rows_per_step - 1, pb_ref, lb_ref)


def _na_attention(q, k, v, ky, vy, table, rows_per_step):
    b, s, _ = q.shape
    l = ky.shape[1]
    n_rows = s // GRID_W
    tq = rows_per_step * GRID_W
    n_keys = NA_KR * GRID_W + l
    seq = lambda n: pl.BlockSpec((1, n, GROUP_W), lambda bi, i: (bi, 0, 0))
    return pl.pallas_call(
        functools.partial(_na_kernel, rows_per_step=rows_per_step, n_rows=n_rows),
        grid=(b, n_rows // rows_per_step),
        in_specs=[pl.BlockSpec((1, tq, GROUP_W), lambda bi, i: (bi, i, 0)),
                  seq(s), seq(s), seq(l), seq(l), _const_spec(table.shape)],
        out_specs=pl.BlockSpec((1, tq, GROUP_W), lambda bi, i: (bi, i, 0)),
        out_shape=jax.ShapeDtypeStruct((b, s, GROUP_W), BF16),
        scratch_shapes=([pltpu.VMEM((N_HEADS * GRID_W, n_keys), F32)] * 2
                        + [pltpu.VMEM((N_HEADS * GRID_W, n_keys), BF16)] * 2
                        + [pltpu.VMEM((N_HEADS * GRID_W, 1), F32)] * 2),
        compiler_params=_cparams(("parallel", "arbitrary")),
        name="na_attention",
    )(q, k, v, ky, vy, table)


def _swa_kernel(sink_ref, q_ref, kp_ref, kc_ref, kn_ref, vp_ref, vc_ref, vn_ref, ky_ref, vy_ref, o_ref,
                kw_ref, vw_ref, wb_ref, sa_ref, sb_ref, pa_ref, pb_ref, la_ref, lb_ref, *, blocks_per_step):
    step = pl.program_id(1)
    nb = pl.num_programs(1) * blocks_per_step
    bl = SWA_BLOCK
    g = N_HEADS // SWA_KV_HEADS
    tq = blocks_per_step * bl
    kw_ref[0:bl] = kp_ref[0]
    kw_ref[bl:bl + tq] = kc_ref[0]
    kw_ref[bl + tq:] = kn_ref[0]
    vw_ref[0:bl] = vp_ref[0]
    vw_ref[bl:bl + tq] = vc_ref[0]
    vw_ref[bl + tq:] = vn_ref[0]

    qi = lax.broadcasted_iota(jnp.int32, (g * bl, 3 * bl), 0) % bl
    jk = lax.broadcasted_iota(jnp.int32, (g * bl, 3 * bl), 1)
    wb_ref[...] = jnp.where(jnp.abs(jk - bl - qi) <= SWA_WINDOW, 0.0, NEG_INF)
    half = lax.broadcasted_iota(jnp.int32, (g * bl, 1), 0) // bl

    head_of_lane = lax.broadcasted_iota(jnp.int32, (bl, GROUP_W), 1) // HEAD_DIM
    in_head = [head_of_lane == h for h in range(N_HEADS)]
    head_mask = [jnp.where(mk, 1.0, 0.0).astype(BF16) for mk in in_head]

    def blk_slice(j):
        return pl.ds(pl.multiple_of(j * bl, bl), bl)

    def scores(j, s_ref):
        qoff = pl.multiple_of(j * bl, bl)
        q_all = q_ref[0, blk_slice(j), :]
        kw = kw_ref[pl.ds(qoff, 3 * bl), :]
        n = step * blocks_per_step + j
        lo_edge = jnp.where(n == 0, NEG_INF, 0.0)
        hi_edge = jnp.where(n == nb - 1, NEG_INF, 0.0)
        for kh in range(SWA_KV_HEADS):
            q = jnp.concatenate([q_all * head_mask[kh * g + gi] for gi in range(g)], axis=0)
            s = _dot_nt(q, kw) + wb_ref[...]
            s_ref[kh, :, :bl] = s[:, :bl] + lo_edge
            s_ref[kh, :, bl:2 * bl] = s[:, bl:2 * bl]
            s_ref[kh, :, 2 * bl:3 * bl] = s[:, 2 * bl:] + hi_edge
            s_ref[kh, :, 3 * bl:] = _dot_nt(q, ky_ref[0])

    def softmax(s_ref, p_ref, l_ref):
        for kh in range(SWA_KV_HEADS):
            s = s_ref[kh]
            sink = jnp.full((g * bl, 1), sink_ref[kh * g] * LOG2_E, F32)
            for gi in range(1, g):
                sink = jnp.where(half == gi, sink_ref[kh * g + gi] * LOG2_E, sink)
            m = jnp.maximum(_row_max(s), sink)
            p = jnp.exp2(s - m)
            l_ref[kh] = 1.0 / (_row_sum(p) + jnp.exp2(sink - m))
            p_ref[kh] = p.astype(BF16)

    def values(j, p_ref, l_ref):
        vw = vw_ref[pl.ds(pl.multiple_of(j * bl, bl), 3 * bl), :]
        out = jnp.zeros((bl, GROUP_W), F32)
        for kh in range(SWA_KV_HEADS):
            o = (_dot(p_ref[kh, :, :3 * bl], vw) + _dot(p_ref[kh, :, 3 * bl:], vy_ref[0])) * l_ref[kh]
            for gi in range(g):
                out = jnp.where(in_head[kh * g + gi], o[gi * bl:(gi + 1) * bl], out)
        o_ref[0, blk_slice(j), :] = out.astype(o_ref.dtype)

    pb_ref[...] = jnp.zeros(pb_ref.shape, BF16)
    lb_ref[...] = jnp.zeros(lb_ref.shape, F32)
    scores(0, sa_ref)

    def pair_body(tt, carry):
        t = 2 * tt
        values(jnp.maximum(t - 1, 0), pb_ref, lb_ref)
        scores(t + 1, sb_ref)
        softmax(sa_ref, pa_ref, la_ref)
        values(t, pa_ref, la_ref)
        scores(jnp.minimum(t + 2, blocks_per_step - 1), sa_ref)
        softmax(sb_ref, pb_ref, lb_ref)
        return carry

    lax.fori_loop(0, blocks_per_step // 2, pair_body, 0)
    values(blocks_per_step - 1, pb_ref, lb_ref)


def _swa_attention(sink, q, k, v, ky, vy, blocks_per_step):
    b, s, _ = q.shape
    l = ky.shape[1]
    tq = blocks_per_step * SWA_BLOCK
    n_steps = s // tq
    nb = s // SWA_BLOCK
    kvw = GROUP_W
    q_rows = N_HEADS // SWA_KV_HEADS * SWA_BLOCK
    n_keys = 3 * SWA_BLOCK + l
    prev = pl.BlockSpec((1, SWA_BLOCK, kvw), lambda bi, i: (bi, jnp.maximum(i * blocks_per_step - 1, 0), 0))
    cur = pl.BlockSpec((1, tq, kvw), lambda bi, i: (bi, i, 0))
    nxt = pl.BlockSpec((1, SWA_BLOCK, kvw), lambda bi, i: (bi, jnp.minimum((i + 1) * blocks_per_step, nb - 1), 0))
    ctx = pl.BlockSpec((1, l, kvw), lambda bi, i: (bi, 0, 0))
    return pl.pallas_call(
        functools.partial(_swa_kernel, blocks_per_step=blocks_per_step),
        grid=(b, n_steps),
        in_specs=[pl.BlockSpec(memory_space=pltpu.SMEM),
                  pl.BlockSpec((1, tq, GROUP_W), lambda bi, i: (bi, i, 0)),
                  prev, cur, nxt, prev, cur, nxt, ctx, ctx],
        out_specs=pl.BlockSpec((1, tq, GROUP_W), lambda bi, i: (bi, i, 0)),
        out_shape=jax.ShapeDtypeStruct((b, s, GROUP_W), BF16),
        scratch_shapes=[pltpu.VMEM((tq + 2 * SWA_BLOCK, kvw), BF16),
                        pltpu.VMEM((tq + 2 * SWA_BLOCK, kvw), BF16),
                        pltpu.VMEM((q_rows, 3 * SWA_BLOCK), F32)]
                       + [pltpu.VMEM((SWA_KV_HEADS, q_rows, n_keys), F32)] * 2
                       + [pltpu.VMEM((SWA_KV_HEADS, q_rows, n_keys), BF16)] * 2
                       + [pltpu.VMEM((SWA_KV_HEADS, q_rows, 1), F32)] * 2,
        compiler_params=_cparams(("parallel", "arbitrary")),
        name="swa_attention",
    )(sink, q, k, k, k, v, v, v, ky, vy)


def _outffn_kernel(*refs, hidden_chunk, final):
    if final:
        (x_ref, m0_ref, m1_ref, m2_ref, m3_ref, wo_ref, g1_ref, n2_ref, sc_ref, sh_ref, g2_ref,
         w1_ref, w3_ref, w2_ref, fg_ref, o_ref) = refs
    else:
        (x_ref, m0_ref, m1_ref, m2_ref, m3_ref, wo_ref, g1_ref, n2_ref, sc_ref, sh_ref, g2_ref,
         w1_ref, w3_ref, w2_ref, o_ref) = refs
    half_rows = x_ref.shape[1] // 2
    mod = n2_ref[...] * (1.0 + sc_ref[0])
    x1_halves, hb_halves = [], []
    for r0 in (0, half_rows):
        rows = slice(r0, r0 + half_rows)
        mix = None
        for gi, m_ref in enumerate((m0_ref, m1_ref, m2_ref, m3_ref)):
            part = _dot(m_ref[0, rows, :], wo_ref[gi * GROUP_W:(gi + 1) * GROUP_W, :])
            mix = part if mix is None else mix + part
        x1_h = x_ref[0, rows, :] + g1_ref[0] * mix
        x1_halves.append(x1_h)
        hb_halves.append((_rms(x1_h) * mod + sh_ref[0]).astype(BF16))
    x1 = jnp.concatenate(x1_halves, axis=0)
    hb = jnp.concatenate(hb_halves, axis=0)
    hidden = w1_ref.shape[1]
    acc = None
    for c0 in range(0, hidden, hidden_chunk):
        a = _dot(hb, w1_ref[:, c0:c0 + hidden_chunk])
        bgate = _dot(hb, w3_ref[:, c0:c0 + hidden_chunk])
        u = (_silu(a) * bgate).astype(BF16)
        part = _dot(u, w2_ref[c0:c0 + hidden_chunk, :])
        acc = part if acc is None else acc + part
    x2 = x1 + g2_ref[0] * acc
    if final:
        x2 = _rms(x2) * fg_ref[...]
    o_ref[0] = x2


def _outffn(x, mixes, wo, g1, n2, sc2, sh2, g2, w1, w3, w2, final_g, tm):
    b, t, d = x.shape
    tok = lambda wd: pl.BlockSpec((1, tm, wd), lambda bi, i: (bi, i, 0))
    vec = pl.BlockSpec((1, 1, d), lambda bi, i: (bi, 0, 0))
    in_specs = ([tok(d)] + [tok(GROUP_W)] * 4
                + [_const_spec(wo.shape), vec, _const_spec((1, d)), vec, vec, vec,
                   _const_spec(w1.shape), _const_spec(w3.shape), _const_spec(w2.shape)])
    args = [x, *mixes, wo, g1, n2, sc2, sh2, g2, w1, w3, w2]
    final = final_g is not None
    if final:
        in_specs.append(_const_spec((1, d)))
        args.append(final_g)
    return pl.pallas_call(
        functools.partial(_outffn_kernel, hidden_chunk=256, final=final),
        grid=(b, t // tm),
        in_specs=in_specs,
        out_specs=tok(d),
        out_shape=jax.ShapeDtypeStruct((b, t, d), F32),
        compiler_params=_cparams(("parallel", "parallel")),
        name="out_proj_ffn_final" if final else "out_proj_ffn",
    )(*args)


def _prep_weights(w_in, mla_w_uq, mla_w_ukv):
    depth, d, _ = w_in.shape
    offs = [0]
    for sz in IN_SIZES:
        offs.append(offs[-1] + sz)
    w_bf = w_in.astype(BF16)
    cols = [w_bf[:, :, offs[i]:offs[i + 1]] for i in range(len(IN_SIZES))]
    cq, ckv, kr, rq, rk, rv, rgf, rgb, nq, nk, nv, sq, sk, sv = cols
    scale = HEAD_DIM ** -0.5
    kr_slot = jnp.concatenate([kr, jnp.zeros((depth, d, LANES - MLA_ROPE), BF16)], axis=-1)
    src = jnp.arange(LANES)[:, None]
    dst = jnp.arange(N_HEADS * MLA_HEAD_PAD)[None, :]
    place = ((src < MLA_ROPE) & (dst % MLA_HEAD_PAD == src + MLA_NOPE)).astype(BF16)

    def per_query_head(t):
        g = N_HEADS // SWA_KV_HEADS
        t = t.reshape(depth, d, SWA_KV_HEADS, 1, HEAD_DIM)
        return jnp.broadcast_to(t, (depth, d, SWA_KV_HEADS, g, HEAD_DIM)).reshape(depth, d, GROUP_W)

    w = jnp.concatenate([cq, ckv, kr_slot, rq, rk * scale, rv, rgf, rgb, nq * scale, nk, nv,
                         sq * scale, per_query_head(sk), per_query_head(sv)], axis=-1).astype(BF16)

    qr = mla_w_uq.shape[1]
    uq = mla_w_uq.reshape(depth, qr, N_HEADS, MLA_NOPE + MLA_ROPE)
    wuq = jnp.concatenate([uq, jnp.zeros((depth, qr, N_HEADS, MLA_HEAD_PAD - MLA_NOPE - MLA_ROPE), F32)],
                          axis=-1).reshape(depth, qr, N_HEADS * MLA_HEAD_PAD).astype(BF16)
    kvr = mla_w_ukv.shape[1]
    ukv = mla_w_ukv.reshape(depth, kvr, N_HEADS, MLA_NOPE + MLA_V)
    zk = jnp.zeros((depth, kvr, N_HEADS, MLA_HEAD_PAD - MLA_NOPE), F32)
    wuk = jnp.concatenate([ukv[..., :MLA_NOPE], zk], axis=-1).reshape(depth, kvr, -1).astype(BF16)
    zv = jnp.zeros((depth, kvr, N_HEADS, MLA_HEAD_PAD - MLA_V), F32)
    wuv = jnp.concatenate([ukv[..., MLA_NOPE:], zv], axis=-1).reshape(depth, kvr, -1).astype(BF16)
    return w, wuq, wuk, wuv, place


def _ctx_head_specs():
    mla = tuple((h * MLA_HEAD_PAD, (h + 1) * MLA_HEAD_PAD, h * MLA_HEAD_PAD, (h + 1) * MLA_HEAD_PAD,
                 h * MLA_HEAD_PAD, h * MLA_HEAD_PAD + MLA_V, None) for h in range(N_HEADS))
    na = tuple((h * HEAD_DIM, (h + 1) * HEAD_DIM) * 3 + (None,) for h in range(N_HEADS))
    swa = tuple((h * HEAD_DIM, (h + 1) * HEAD_DIM) * 3 + (h,) for h in range(N_HEADS))
    return mla, na, swa


def kernel(x, c, ctx, c_ctx, ada_w, ada_b, norm1_g, w_in, mla_q_norm, mla_w_uq, mla_kv_norm, mla_w_ukv,
           ret_decay, na_rpb, swa_sink, w_out, norm2_g, ffn_w1, ffn_w3, ffn_w2, final_norm_g):
    b, s, d = x.shape
    l_ctx = ctx.shape[1]
    depth = ada_w.shape[0]
    assert b + 1 <= 8 and s % 2048 == 0 and l_ctx % 128 == 0

    cond = jnp.concatenate([c, c_ctx[None, :], jnp.zeros((8 - b - 1, d), F32)], axis=0)
    mod = _modulation(cond, ada_w, ada_b)
    tables = _rope_tables(s)
    w_all, wuq_all, wuk_all, wuv_all, place = _prep_weights(w_in, mla_w_uq, mla_w_ukv)
    wo_all = w_out.astype(BF16)
    w1_all, w3_all, w2_all = ffn_w1.astype(BF16), ffn_w3.astype(BF16), ffn_w2.astype(BF16)
    mla_heads, na_heads, swa_heads = _ctx_head_specs()
    no_sink = jnp.zeros((N_HEADS,), F32)

    tm_x = 512
    tm_y = min(256, l_ctx)
    y = ctx
    for l in range(depth):
        mx = [mod[l, :b, j * d:(j + 1) * d][:, None, :] for j in range(6)]
        my = [jnp.broadcast_to(mod[l, b, j * d:(j + 1) * d][None, None, :], (b, 1, d)) for j in range(6)]
        n1 = norm1_g[l][None, :]
        n2 = norm2_g[l][None, :]
        qn = mla_q_norm[l][None, :]
        kvn = mla_kv_norm[l][None, :]
        lw = (w_all[l], qn, kvn, wuq_all[l], wuk_all[l], wuv_all[l], place)

        px = _inproj(x, n1, mx[1], mx[0], *lw, tables, tm_x)
        py = _inproj(y, n1, my[1], my[0], *lw, None, tm_y)
        (xmq, xmk, xmv, xrq, xrk, xrv, xrg, xnq, xnk, xnv, xsq, xsk, xsv) = px
        (ymq, ymk, ymv, yrq, yrk, yrv, yrg, ynq, ynk, ynv, ysq, ysk, ysv) = py

        mla_x = _mla_attention(xmq, xmk, xmv, ymk, ymv, tq=1024, tk=512, n_tiles=2, n_sub=2, unroll=8)
        dec = ret_decay[l].reshape(-1)
        ret_x, ret_y = _retention(dec, xrq, xrk, xrv, xrg, yrq, yrk, yrv, yrg, chunk=128, chunks_per_step=16)
        table = _na_bias_table(na_rpb[l])
        na_x = _na_attention(xnq, xnk, xnv, ynk, ynv, table, rows_per_step=32)
        swa_x = _swa_attention(swa_sink[l], xsq, xsk, xsv, ysk, ysv, blocks_per_step=16)

        last = l == depth - 1
        x = _outffn(x, (mla_x, ret_x, na_x, swa_x), wo_all[l], mx[2], n2, mx[4], mx[3], mx[5],
                    w1_all[l], w3_all[l], w2_all[l], final_norm_g[None, :] if last else None, tm_x)
        if not last:
            mla_y = _ctx_attention(ymq, ymk, ymv, mla_heads, no_sink, "mla_ctx_attention")
            na_y = _ctx_attention(ynq, ynk, ynv, na_heads, no_sink, "na_ctx_attention")
            swa_y = _ctx_attention(ysq, ysk, ysv, swa_heads, swa_sink[l], "swa_ctx_attention")
            y = _outffn(y, (mla_y, ret_y, na_y, swa_y), wo_all[l], my[2], n2, my[4], my[3], my[5],
                        w1_all[l], w3_all[l], w2_all[l], None, tm_y)
    return x
```

```python
import functools

import jax
import jax.numpy as jnp
from jax import lax
from jax.experimental import pallas as pl
from jax.experimental.pallas import tpu as pltpu

F32 = jnp.float32
BF16 = jnp.bfloat16

GRID_W = 64
HEAD_DIM = 64
N_HEADS = 4
GROUP_W = N_HEADS * HEAD_DIM
MLA_Q_RANK = 256
MLA_KV_RANK = 128
MLA_NOPE = 64
MLA_ROPE = 32
MLA_V = 64
MLA_HEAD_PAD = 128
MLA_V_ROWS = MLA_V + 16
NA_KR = 8
NA_KC = 16
SWA_KV_HEADS = 2
SWA_WINDOW = 128
SWA_BLOCK = 128
ROPE_THETA = 10000.0
EPS = 1e-6
NEG_INF = -1e30
LOG2_E = 1.4426950408889634
LANES = 128
VMEM_LIMIT = 56 * 1024 * 1024

IN_SIZES = (MLA_Q_RANK, MLA_KV_RANK, MLA_ROPE,
            GROUP_W, GROUP_W, GROUP_W, GROUP_W, GROUP_W,
            GROUP_W, GROUP_W, GROUP_W,
            GROUP_W, SWA_KV_HEADS * HEAD_DIM, SWA_KV_HEADS * HEAD_DIM)

_O_CQ = 0
_O_CKV = _O_CQ + MLA_Q_RANK
_O_KR = _O_CKV + MLA_KV_RANK
_O_RQK = _O_KR + LANES
_O_RV = _O_RQK + 2 * GROUP_W
_O_RG = _O_RV + GROUP_W
_O_NA = _O_RG + 2 * GROUP_W
_O_SQK = _O_NA + 3 * GROUP_W
_O_SV = _O_SQK + 2 * GROUP_W
_IN_COLS = _O_SV + GROUP_W


def _cparams(sem):
    return pltpu.CompilerParams(dimension_semantics=sem, vmem_limit_bytes=VMEM_LIMIT)


def _dot(a, b):
    return jnp.dot(a, b, preferred_element_type=F32)


def _dot_nt(a, b):
    return lax.dot_general(a, b, (((1,), (1,)), ((), ())), preferred_element_type=F32)


def _rms(x):
    return x * lax.rsqrt(jnp.mean(x * x, axis=-1, keepdims=True) + EPS)


def _silu(x):
    return x * jax.nn.sigmoid(x)


def _lane_chunks(arrays):
    return [a[:, j * LANES:(j + 1) * LANES] for a in arrays for j in range(a.shape[-1] // LANES)]


def _row_max(*arrays):
    return jnp.max(functools.reduce(jnp.maximum, _lane_chunks(arrays)), axis=-1, keepdims=True)


def _row_sum(*arrays):
    return jnp.sum(functools.reduce(jnp.add, _lane_chunks(arrays)), axis=-1, keepdims=True)


def _mod_kernel(c_ref, w_ref, b_ref, o_ref):
    o_ref[0] = _dot(_silu(c_ref[...]), w_ref[0]) + b_ref[0]


def _modulation(cond, ada_w, ada_b):
    depth, d, d6 = ada_w.shape
    n = d6 // d
    return pl.pallas_call(
        _mod_kernel,
        grid=(depth, n),
        in_specs=[pl.BlockSpec((8, d), lambda l, j: (0, 0)),
                  pl.BlockSpec((1, d, d), lambda l, j: (l, 0, j)),
                  pl.BlockSpec((1, 1, d), lambda l, j: (l, 0, j))],
        out_specs=pl.BlockSpec((1, 8, d), lambda l, j: (l, 0, j)),
        out_shape=jax.ShapeDtypeStruct((depth, 8, d6), F32),
        compiler_params=_cparams(("parallel", "parallel")),
        name="ada_modulation",
    )(cond, ada_w, ada_b.reshape(depth, 1, d6))


def _rope_tables(seq):
    n_rows = seq // GRID_W

    def parts(pos, d):
        inv = ROPE_THETA ** (-jnp.arange(0, d, 2, dtype=F32) / d)
        ang = pos.astype(F32)[:, None] * inv[None, :]
        z = jnp.zeros_like(ang)
        return (jnp.concatenate([jnp.cos(ang), jnp.cos(ang)], axis=-1),
                jnp.concatenate([z, jnp.sin(ang)], axis=-1),
                jnp.concatenate([-jnp.sin(ang), z], axis=-1))

    def expand(by_row, by_col):
        w = by_row.shape[-1]
        r = jnp.broadcast_to(by_row[:, None, :], (n_rows, GRID_W, w))
        c = jnp.broadcast_to(by_col[None, :, :], (n_rows, GRID_W, w))
        return jnp.concatenate([r, c], axis=-1).reshape(seq, 2 * w)

    def tables(d, fill):
        per_head = [expand(a, b) for a, b in zip(parts(jnp.arange(n_rows), d), parts(jnp.arange(GRID_W), d))]
        return [fill(t, i) for i, t in enumerate(per_head)]

    def two_heads(t, _):
        return jnp.concatenate([t, t], axis=-1)

    def mla_slot(t, i):
        lead = (jnp.ones if i == 0 else jnp.zeros)((seq, MLA_NOPE), F32)
        tail = (jnp.ones if i == 0 else jnp.zeros)((seq, MLA_HEAD_PAD - MLA_NOPE - MLA_ROPE), F32)
        return jnp.concatenate([lead, t, tail], axis=-1)

    def kr_slot(t, i):
        tail = (jnp.ones if i == 0 else jnp.zeros)((seq, LANES - MLA_ROPE), F32)
        return jnp.concatenate([t, tail], axis=-1)

    return tuple(tables(HEAD_DIM // 2, two_heads) + tables(MLA_ROPE // 2, mla_slot)
                 + tables(MLA_ROPE // 2, kr_slot))


def _rope(x, cos, s_prev, s_next, d):
    out = []
    for j in range(x.shape[-1] // LANES):
        xc = x[:, j * LANES:(j + 1) * LANES]
        out.append(xc * cos + pltpu.roll(xc, d, 1) * s_prev + pltpu.roll(xc, LANES - d, 1) * s_next)
    return out[0] if len(out) == 1 else jnp.concatenate(out, axis=-1)


def _inproj_kernel(*refs, rotate, mla_scale):
    if rotate:
        (x_ref, g_ref, sc_ref, sh_ref, w_ref, qn_ref, kvn_ref, wuq_ref, wuk_ref, wuv_ref, place_ref,
         c64_ref, p64_ref, n64_ref, cm_ref, pm_ref, nm_ref, ckr_ref, pkr_ref, nkr_ref,
         mq_ref, mk_ref, mv_ref, rq_ref, rk_ref, rv_ref, rg_ref,
         nq_ref, nk_ref, nv_ref, sq_ref, sk_ref, sv_ref) = refs
    else:
        (x_ref, g_ref, sc_ref, sh_ref, w_ref, qn_ref, kvn_ref, wuq_ref, wuk_ref, wuv_ref, place_ref,
         mq_ref, mk_ref, mv_ref, rq_ref, rk_ref, rv_ref, rg_ref,
         nq_ref, nk_ref, nv_ref, sq_ref, sk_ref, sv_ref) = refs

    x = x_ref[0]
    h = _rms(x) * (g_ref[...] * (1.0 + sc_ref[0])) + sh_ref[0]
    hb = h.astype(BF16)

    def proj(lo, hi):
        return _dot(hb, w_ref[:, lo:hi])

    def rope64(v):
        if not rotate:
            return v
        return _rope(v, c64_ref[...], p64_ref[...], n64_ref[...], HEAD_DIM // 4)

    def rope_mla(v):
        if not rotate:
            return v
        return _rope(v, cm_ref[...], pm_ref[...], nm_ref[...], MLA_ROPE // 4)

    cq = (_rms(proj(_O_CQ, _O_CKV)) * qn_ref[...]).astype(BF16)
    ckv_kr = proj(_O_CKV, _O_RQK)
    ckv = (_rms(ckv_kr[:, :MLA_KV_RANK]) * kvn_ref[...]).astype(BF16)
    kr = ckv_kr[:, MLA_KV_RANK:]
    if rotate:
        kr = _rope(kr, ckr_ref[...], pkr_ref[...], nkr_ref[...], MLA_ROPE // 4)
    kr = kr.astype(BF16)

    rqk = proj(_O_RQK, _O_RV)
    rq_ref[0] = rope64(rqk[:, :GROUP_W]).astype(BF16)
    rk_ref[0] = rope64(rqk[:, GROUP_W:]).astype(BF16)
    rv_ref[0] = proj(_O_RV, _O_RG).astype(BF16)

    na = proj(_O_NA, _O_SQK)
    nq_ref[0] = (na[:, :GROUP_W] * LOG2_E).astype(BF16)
    nk_ref[0] = na[:, GROUP_W:2 * GROUP_W].astype(BF16)
    nv_ref[0] = na[:, 2 * GROUP_W:].astype(BF16)

    sqk = proj(_O_SQK, _O_SV)
    sq_ref[0] = (rope64(sqk[:, :GROUP_W]) * LOG2_E).astype(BF16)
    sk_ref[0] = rope64(sqk[:, GROUP_W:]).astype(BF16)
    sv_ref[0] = proj(_O_SV, _IN_COLS).astype(BF16)

    q = rope_mla(_dot(cq, wuq_ref[...])) * mla_scale
    mq_ref[0] = q.astype(BF16)
    k = _dot(ckv, wuk_ref[...]) + _dot(kr, place_ref[...])
    mk_ref[0] = k.astype(BF16)
    v = _dot(ckv, wuv_ref[...])
    lane = lax.broadcasted_iota(jnp.int32, v.shape, 1)
    mv_ref[0] = jnp.where(lane % MLA_HEAD_PAD >= MLA_V, 1.0, v).astype(BF16)

    rg_ref[0] = proj(_O_RG, _O_NA)


def _const_spec(shape):
    nd = len(shape)
    return pl.BlockSpec(shape, lambda *_: (0,) * nd, pipeline_mode=pl.Buffered(1))


def _inproj(x, gain, scale, shift, w, qn, kvn, wuq, wuk, wuv, place, tables, tm):
    b, t, d = x.shape
    rotate = tables is not None
    kv_w = GROUP_W
    mla_w = N_HEADS * MLA_HEAD_PAD
    tok = lambda wd: pl.BlockSpec((1, tm, wd), lambda bi, i: (bi, i, 0))
    vec = pl.BlockSpec((1, 1, d), lambda bi, i: (bi, 0, 0))
    in_specs = [tok(d), _const_spec((1, d)), vec, vec, _const_spec(w.shape),
                _const_spec(qn.shape), _const_spec(kvn.shape), _const_spec(wuq.shape),
                _const_spec(wuk.shape), _const_spec(wuv.shape), _const_spec(place.shape)]
    args = [x, gain, scale, shift, w, qn, kvn, wuq, wuk, wuv, place]
    if rotate:
        in_specs += [pl.BlockSpec((tm, LANES), lambda bi, i: (i, 0))] * len(tables)
        args += list(tables)
    widths = [mla_w, mla_w, mla_w, GROUP_W, GROUP_W, GROUP_W, 2 * GROUP_W,
              GROUP_W, GROUP_W, GROUP_W, GROUP_W, kv_w, kv_w]
    dtypes = [BF16] * 6 + [F32] + [BF16] * 6
    return pl.pallas_call(
        functools.partial(_inproj_kernel, rotate=rotate, mla_scale=(MLA_NOPE + MLA_ROPE) ** -0.5 * LOG2_E),
        grid=(b, t // tm),
        in_specs=in_specs,
        out_specs=[tok(wd) for wd in widths],
        out_shape=[jax.ShapeDtypeStruct((b, t, wd), dt) for wd, dt in zip(widths, dtypes)],
        compiler_params=_cparams(("parallel", "parallel")),
        name="in_proj_rot" if rotate else "in_proj_ctx",
    )(*args)


def _mla_kernel(q_ref, kx_ref, vx_ref, ky_ref, vy_ref, o_ref, m_ref, acc_ref, sa_ref, sb_ref, sc_ref,
                ma_ref, mb_ref, mc_ref, *, tk, heads, n_tiles, n_sub, unroll):
    n_chunks = kx_ref.shape[1] // tk
    assert n_chunks % 2 == 0
    tq = q_ref.shape[1] // n_tiles
    ts = tq // n_sub
    chains = [(h, u) for h in range(heads) for u in range(n_sub)]

    def reset():
        for c in range(len(chains)):
            m_ref[c] = jnp.full(m_ref.shape[1:], NEG_INF, F32)
            acc_ref[c] = jnp.zeros(acc_ref.shape[1:], F32)

    def rows(h):
        return slice(h * MLA_HEAD_PAD, (h + 1) * MLA_HEAD_PAD)

    def vrows(h):
        return slice(h * MLA_HEAD_PAD, h * MLA_HEAD_PAD + MLA_V_ROWS)

    buf_a, buf_b, buf_c = (sa_ref, ma_ref), (sb_ref, mb_ref), (sc_ref, mc_ref)

    def scores(t, c, k, dst):
        h, u = chains[c]
        q0 = t * tq + u * ts
        st = _dot_nt(k, q_ref[0, q0:q0 + ts, rows(h)])
        dst[0][c] = st
        dst[1][c] = jnp.max(st, axis=0, keepdims=True)

    def absorb(c, src, vt):
        m_old = m_ref[c]
        m_new = jnp.maximum(m_old, src[1][c])
        pt = jnp.exp2(src[0][c] - m_new).astype(BF16)
        acc_ref[c, :MLA_V_ROWS] = acc_ref[c, :MLA_V_ROWS] * jnp.exp2(m_old - m_new) + _dot(vt, pt)
        m_ref[c] = m_new

    def kx(j, h):
        return kx_ref[0, pl.ds(pl.multiple_of(j * tk, tk), tk), rows(h)]

    def stage_at(t, j, parity):
        cur, nxt = (buf_a, buf_b) if parity == 0 else (buf_b, buf_a)
        for c, (h, _) in enumerate(chains):
            scores(t, c, kx(j + 1, h), nxt)
            absorb(c, cur, vx_ref[0, j, vrows(h), :])

    trips = (n_chunks - 1) // unroll
    reset()
    for c, (h, _) in enumerate(chains):
        scores(0, c, kx(0, h), buf_a)
    for t in range(n_tiles):
        def body(jj, carry):
            for i in range(unroll):
                stage_at(t, unroll * jj + i, i % 2)
            return carry

        lax.fori_loop(0, trips, body, 0)
        for j in range(unroll * trips, n_chunks - 1):
            stage_at(t, j, j % 2)
        for c, (h, _) in enumerate(chains):
            scores(t, c, ky_ref[0, :, rows(h)], buf_c)
            absorb(c, buf_b, vx_ref[0, n_chunks - 1, vrows(h), :])
        outs = [[None] * n_sub for _ in range(heads)]
        for c, (h, u) in enumerate(chains):
            if t + 1 < n_tiles:
                scores(t + 1, c, kx(0, h), buf_a)
            absorb(c, buf_c, vy_ref[0, 0, vrows(h), :])
            acc = acc_ref[c].T
            outs[h][u] = acc[:, :MLA_V] / acc[:, MLA_V:MLA_V + 1]
        o_ref[0, t * tq:(t + 1) * tq, :] = jnp.concatenate(
            [jnp.concatenate(outs[h], axis=0) for h in range(heads)], axis=-1).astype(o_ref.dtype)
        if t + 1 < n_tiles:
            reset()


def _mla_attention(q, kx, vx, ky, vy, tq, tk, n_tiles, n_sub, unroll):
    b, s, _ = q.shape
    l = ky.shape[1]
    hp = 2
    wd = hp * MLA_HEAD_PAD
    vxt = jnp.swapaxes(vx.reshape(b, s // tk, tk, -1), 2, 3)
    vyt = jnp.swapaxes(vy.reshape(b, 1, l, -1), 2, 3)
    ts = tq // n_tiles // n_sub
    return pl.pallas_call(
        functools.partial(_mla_kernel, tk=tk, heads=hp, n_tiles=n_tiles, n_sub=n_sub, unroll=unroll),
        grid=(b, N_HEADS // hp, s // tq),
        in_specs=[pl.BlockSpec((1, tq, wd), lambda bi, hi, i: (bi, i, hi)),
                  pl.BlockSpec((1, s, wd), lambda bi, hi, i: (bi, 0, hi)),
                  pl.BlockSpec((1, s // tk, wd, tk), lambda bi, hi, i: (bi, 0, hi, 0)),
                  pl.BlockSpec((1, l, wd), lambda bi, hi, i: (bi, 0, hi)),
                  pl.BlockSpec((1, 1, wd, l), lambda bi, hi, i: (bi, 0, hi, 0))],
        out_specs=pl.BlockSpec((1, tq, hp * MLA_V), lambda bi, hi, i: (bi, i, hi)),
        out_shape=jax.ShapeDtypeStruct((b, s, N_HEADS * MLA_V), BF16),
        scratch_shapes=[pltpu.VMEM((hp * n_sub, 1, ts), F32),
                        pltpu.VMEM((hp * n_sub, MLA_HEAD_PAD, ts), F32),
                        pltpu.VMEM((hp * n_sub, tk, ts), F32),
                        pltpu.VMEM((hp * n_sub, tk, ts), F32),
                        pltpu.VMEM((hp * n_sub, l, ts), F32)]
                       + [pltpu.VMEM((hp * n_sub, 1, ts), F32)] * 3,
        compiler_params=_cparams(("parallel", "parallel", "arbitrary")),
        name="mla_attention",
    )(q, kx, vxt, ky, vyt)


def _ctx_attn_kernel(sink_ref, q_ref, k_ref, v_ref, o_ref, *, heads):
    outs = []
    for (q0, q1, k0, k1, v0, v1, sink_idx) in heads:
        q = q_ref[0, :, q0:q1]
        s = _dot_nt(q, k_ref[0, :, k0:k1])
        m = jnp.max(s, axis=-1, keepdims=True)
        if sink_idx is not None:
            sink = jnp.full((1, 1), sink_ref[sink_idx] * LOG2_E, F32)
            m = jnp.maximum(m, sink)
        p = jnp.exp2(s - m)
        l = jnp.sum(p, axis=-1, keepdims=True)
        if sink_idx is not None:
            l = l + jnp.exp2(sink - m)
        outs.append(_dot(p.astype(BF16), v_ref[0, :, v0:v1]) / l)
    o_ref[0] = jnp.concatenate(outs, axis=-1).astype(o_ref.dtype)


def _ctx_attention(q, k, v, heads, sink, name):
    b, l, _ = q.shape
    full = lambda a: pl.BlockSpec((1, l, a.shape[-1]), lambda bi: (bi, 0, 0))
    return pl.pallas_call(
        functools.partial(_ctx_attn_kernel, heads=heads),
        grid=(b,),
        in_specs=[pl.BlockSpec(memory_space=pltpu.SMEM), full(q), full(k), full(v)],
        out_specs=pl.BlockSpec((1, l, GROUP_W), lambda bi: (bi, 0, 0)),
        out_shape=jax.ShapeDtypeStruct((b, l, GROUP_W), BF16),
        compiler_params=_cparams(("parallel",)),
        name=name,
    )(sink, q, k, v)


def _ret_kernel(*refs, chunk, n_chunks, direction, has_prev):
    if has_prev:
        (dec_ref, q_ref, k_ref, kt_ref, v_ref, g_ref, s0_ref, prev_ref, o_ref, sn_ref,
         st_ref, dm_ref, qd_ref, kdt_ref, cd_ref, ob_ref, qk_ref) = refs
    else:
        (dec_ref, q_ref, k_ref, kt_ref, v_ref, g_ref, s0_ref, o_ref, sn_ref,
         st_ref, dm_ref, qd_ref, kdt_ref, cd_ref, ob_ref, qk_ref) = refs
        prev_ref = None
    i = pl.program_id(0)
    c = chunk
    fwd = direction == 0
    batch = range(q_ref.shape[0])

    def head_of(shape, axis):
        return lax.broadcasted_iota(jnp.int32, shape, axis) // HEAD_DIM

    @pl.when(i == 0)
    def _init():
        st_ref[...] = s0_ref[...]
        ii = lax.broadcasted_iota(jnp.int32, (c, c), 0).astype(F32)
        jj = lax.broadcasted_iota(jnp.int32, (c, c), 1).astype(F32)
        diff = (ii - jj) if fwd else (jj - ii)
        pos = lax.broadcasted_iota(jnp.int32, (c, GROUP_W), 0).astype(F32)
        pos_t = lax.broadcasted_iota(jnp.int32, (GROUP_W, c), 1).astype(F32)
        q_steps = (pos + 1.0) if fwd else (c - pos)
        k_steps = (c - 1.0 - pos_t) if fwd else pos_t
        qd = jnp.zeros((c, GROUP_W), F32)
        kdt = jnp.zeros((GROUP_W, c), F32)
        cd = jnp.zeros((GROUP_W, GROUP_W), F32)
        for h in range(N_HEADS):
            dec = dec_ref[direction * N_HEADS + h]
            lg = jax.nn.log_sigmoid(jnp.full((c, c), dec, F32))
            dm_ref[h * c:(h + 1) * c] = jnp.where(diff >= 0, jnp.exp(lg * jnp.maximum(diff, 0.0)), 0.0)
            lgq = jax.nn.log_sigmoid(jnp.full((c, GROUP_W), dec, F32))
            qd = jnp.where(head_of((c, GROUP_W), 1) == h, jnp.exp(lgq * q_steps), qd)
            lgk = jax.nn.log_sigmoid(jnp.full((GROUP_W, c), dec, F32))
            kdt = jnp.where(head_of((GROUP_W, c), 0) == h, jnp.exp(lgk * k_steps), kdt)
            lgc = jax.nn.log_sigmoid(jnp.full((GROUP_W, GROUP_W), dec, F32))
            cd = jnp.where(head_of((GROUP_W, GROUP_W), 0) == h, jnp.exp(lgc * c), cd)
        qd_ref[...] = qd
        kdt_ref[...] = kdt
        cd_ref[...] = cd

    in_head = [head_of((c, GROUP_W), 1) == h for h in range(N_HEADS)]
    head_mask = [jnp.where(mk, 1.0, 0.0).astype(BF16) for mk in in_head]
    same_head = head_of((GROUP_W, GROUP_W), 0) == head_of((GROUP_W, GROUP_W), 1)

    def chunk_off(n):
        idx = n if fwd else n_chunks - 1 - n
        return pl.multiple_of(idx * c, c)

    def qk(bb, n):
        off = chunk_off(n)
        q_all = q_ref[bb, pl.ds(off, c), :]
        q4 = jnp.concatenate([q_all * head_mask[h] for h in range(N_HEADS)], axis=0)
        return _dot_nt(q4, k_ref[bb, pl.ds(off, c), :])

    def gate_and_store(bb, n):
        off = chunk_off(n)
        o = ob_ref[bb]
        oo = o * o
        ms = jnp.zeros((c, GROUP_W), F32)
        for h in range(N_HEADS):
            ms_h = jnp.sum(jnp.where(in_head[h], oo, 0.0), axis=-1, keepdims=True) * (1.0 / HEAD_DIM)
            ms = jnp.where(in_head[h], ms_h, ms)
        res = o * lax.rsqrt(ms + EPS) * _silu(g_ref[bb, pl.ds(off, c), :])
        if has_prev:
            res = res + prev_ref[bb, pl.ds(off, c), :]
        o_ref[bb, pl.ds(off, c), :] = res.astype(o_ref.dtype)

    ob_ref[...] = jnp.zeros(ob_ref.shape, F32)

    for bb in batch:
        qk_ref[bb] = qk(bb, 0)

    def body(n, carry):
        for bb in batch:
            gate_and_store(bb, jnp.maximum(n - 1, 0))
        off = chunk_off(n)
        atts = [(qk_ref[bb] * dm_ref[...]).astype(BF16) for bb in batch]
        for bb in batch:
            qk_ref[bb] = qk(bb, jnp.minimum(n + 1, n_chunks - 1))
        for bb in batch:
            v = v_ref[bb, pl.ds(off, c), :]
            att = atts[bb]
            intra4 = _dot(att, v)
            intra = intra4[:c]
            for h in range(1, N_HEADS):
                intra = jnp.where(in_head[h], intra4[h * c:(h + 1) * c], intra)
            state = st_ref[bb]
            ob_ref[bb] = intra + _dot(q_ref[bb, pl.ds(off, c), :], state.astype(BF16)) * qd_ref[...]
            kk = (kt_ref[bb, :, pl.ds(off, c)].astype(F32) * kdt_ref[...]).astype(BF16)
            st_ref[bb] = state * cd_ref[...] + jnp.where(same_head, _dot(kk, v), 0.0)
        return carry

    lax.fori_loop(0, n_chunks, body, 0)
    for bb in batch:
        gate_and_store(bb, n_chunks - 1)

    @pl.when(i == pl.num_programs(0) - 1)
    def _fin():
        sn_ref[...] = st_ref[...]


def _retention_pass(dec, q, k, v, gates, state0, prev, direction, chunk, n_chunks, out_dtype):
    b, t, _ = q.shape
    tb = chunk * n_chunks
    n = t // tb
    blk = (lambda i: i) if direction == 0 else (lambda i: n - 1 - i)
    tok = pl.BlockSpec((b, tb, GROUP_W), lambda i: (0, blk(i), 0))
    tok_t = pl.BlockSpec((b, GROUP_W, tb), lambda i: (0, 0, blk(i)))
    gate = pl.BlockSpec((b, tb, GROUP_W), lambda i: (0, blk(i), direction))
    st_spec = pl.BlockSpec((b, GROUP_W, GROUP_W), lambda i: (0, 0, 0))
    in_specs = [pl.BlockSpec(memory_space=pltpu.SMEM), tok, tok, tok_t, tok, gate, st_spec]
    args = [dec, q, k, jnp.swapaxes(k, 1, 2), v, gates, state0]
    if prev is not None:
        in_specs.append(tok)
        args.append(prev)
    return pl.pallas_call(
        functools.partial(_ret_kernel, chunk=chunk, n_chunks=n_chunks, direction=direction,
                          has_prev=prev is not None),
        grid=(n,),
        in_specs=in_specs,
        out_specs=[tok, st_spec],
        out_shape=[jax.ShapeDtypeStruct((b, t, GROUP_W), out_dtype),
                   jax.ShapeDtypeStruct((b, GROUP_W, GROUP_W), F32)],
        scratch_shapes=[pltpu.VMEM((b, GROUP_W, GROUP_W), F32),
                        pltpu.VMEM((N_HEADS * chunk, chunk), F32),
                        pltpu.VMEM((chunk, GROUP_W), F32),
                        pltpu.VMEM((GROUP_W, chunk), F32),
                        pltpu.VMEM((GROUP_W, GROUP_W), F32),
                        pltpu.VMEM((b, chunk, GROUP_W), F32),
                        pltpu.VMEM((b, N_HEADS * chunk, chunk), F32)],
        compiler_params=_cparams(("arbitrary",)),
        name="retention_fwd" if direction == 0 else "retention_bwd",
    )(*args)


def _retention(dec, xq, xk, xv, xg, yq, yk, yv, yg, chunk, chunks_per_step):
    b = xq.shape[0]
    zero = jnp.zeros((b, GROUP_W, GROUP_W), F32)
    ny = yq.shape[1] // chunk
    yb, sb = _retention_pass(dec, yq, yk, yv, yg, zero, None, 1, chunk, ny, F32)
    y, sf = _retention_pass(dec, yq, yk, yv, yg, zero, yb, 0, chunk, ny, BF16)
    xb, _ = _retention_pass(dec, xq, xk, xv, xg, sb, None, 1, chunk, chunks_per_step, F32)
    x, _ = _retention_pass(dec, xq, xk, xv, xg, sf, xb, 0, chunk, chunks_per_step, BF16)
    return x, y


def _na_bias_kernel(rpb_ref, o_ref):
    h = pl.program_id(0)
    dr0 = pl.program_id(1)
    c = lax.broadcasted_iota(jnp.int32, (GRID_W, GRID_W), 0)
    kc = lax.broadcasted_iota(jnp.int32, (GRID_W, GRID_W), 1)
    c0 = jnp.clip(c - NA_KC // 2, 0, GRID_W - NA_KC)
    col_in = (kc >= c0) & (kc < c0 + NA_KC)
    dc = jnp.clip(kc - c, -(NA_KC - 1), NA_KC - 1) + NA_KC - 1
    n_dc = 2 * NA_KC - 1
    accs = [jnp.zeros((GRID_W, GRID_W), F32) for _ in range(NA_KR)]
    for d in range(n_dc):
        at_d = dc == d
        for j in range(NA_KR):
            accs[j] = jnp.where(at_d, rpb_ref[(h * (2 * NA_KR - 1) + dr0 + j) * n_dc + d], accs[j])
    for j in range(NA_KR):
        o_ref[0, :, j * GRID_W:(j + 1) * GRID_W] = jnp.where(col_in, accs[j] * LOG2_E, NEG_INF)


def _na_bias_table(rpb):
    return pl.pallas_call(
        _na_bias_kernel,
        grid=(N_HEADS, NA_KR),
        in_specs=[pl.BlockSpec(memory_space=pltpu.SMEM)],
        out_specs=pl.BlockSpec((1, GRID_W, NA_KR * GRID_W), lambda h, r: (r, h, 0)),
        out_shape=jax.ShapeDtypeStruct((NA_KR, N_HEADS * GRID_W, NA_KR * GRID_W), F32),
        compiler_params=_cparams(("parallel", "parallel")),
        name="na_bias_table",
    )(rpb.reshape(-1))


def _na_kernel(q_ref, k_ref, v_ref, ky_ref, vy_ref, tb_ref, o_ref,
               sa_ref, sb_ref, pa_ref, pb_ref, la_ref, lb_ref, *, rows_per_step, n_rows):
    r_base = pl.program_id(1) * rows_per_step
    win = NA_KR * GRID_W

    head_of_lane = lax.broadcasted_iota(jnp.int32, (GRID_W, GROUP_W), 1) // HEAD_DIM
    in_head = [head_of_lane == h for h in range(N_HEADS)]
    head_mask = [jnp.where(mk, 1.0, 0.0).astype(BF16) for mk in in_head]

    def geometry(i):
        r = r_base + i
        r0 = jnp.clip(r - NA_KR // 2, 0, n_rows - NA_KR)
        return pl.multiple_of(r0 * GRID_W, GRID_W), r0 - r + NA_KR - 1

    def row_slice(i):
        return pl.ds(pl.multiple_of(i * GRID_W, GRID_W), GRID_W)

    def scores(i, s_ref):
        koff, dr0 = geometry(i)
        q_all = q_ref[0, row_slice(i), :]
        q4 = jnp.concatenate([q_all * head_mask[h] for h in range(N_HEADS)], axis=0)
        s_ref[:, :win] = _dot_nt(q4, k_ref[0, pl.ds(koff, win), :]) + tb_ref[dr0]
        s_ref[:, win:] = _dot_nt(q4, ky_ref[0])

    def softmax(s_ref, p_ref, l_ref):
        s = s_ref[...]
        p = jnp.exp2(s - _row_max(s))
        l_ref[...] = 1.0 / _row_sum(p)
        p_ref[...] = p.astype(BF16)

    def values(i, p_ref, l_ref):
        koff, _ = geometry(i)
        o4 = (_dot(p_ref[:, :win], v_ref[0, pl.ds(koff, win), :])
              + _dot(p_ref[:, win:], vy_ref[0])) * l_ref[...]
        out = o4[:GRID_W]
        for h in range(1, N_HEADS):
            out = jnp.where(in_head[h], o4[h * GRID_W:(h + 1) * GRID_W], out)
        o_ref[0, row_slice(i), :] = out.astype(o_ref.dtype)

    pb_ref[...] = jnp.zeros(pb_ref.shape, BF16)
    lb_ref[...] = jnp.zeros(lb_ref.shape, F32)
    scores(0, sa_ref)

    def pair_body(tt, carry):
        t = 2 * tt
        values(jnp.maximum(t - 1, 0), pb_ref, lb_ref)
        scores(t + 1, sb_ref)
        softmax(sa_ref, pa_ref, la_ref)
        values(t, pa_ref, la_ref)
        scores(jnp.minimum(t + 2, rows_per_step - 1), sa_ref)
        softmax(sb_ref, pb_ref, lb_ref)
        return carry

    lax.fori_loop(0, rows_per_step // 2, pair_body, 0)
    values(rows_per_step - 1, pb_ref, lb_ref)


def _na_attention(q, k, v, ky, vy, table, rows_per_step):
    b, s, _ = q.shape
    l = ky.shape[1]
    n_rows = s // GRID_W
    tq = rows_per_step * GRID_W
    n_keys = NA_KR * GRID_W + l
    seq = lambda n: pl.BlockSpec((1, n, GROUP_W), lambda bi, i: (bi, 0, 0))
    return pl.pallas_call(
        functools.partial(_na_kernel, rows_per_step=rows_per_step, n_rows=n_rows),
        grid=(b, n_rows // rows_per_step),
        in_specs=[pl.BlockSpec((1, tq, GROUP_W), lambda bi, i: (bi, i, 0)),
                  seq(s), seq(s), seq(l), seq(l), _const_spec(table.shape)],
        out_specs=pl.BlockSpec((1, tq, GROUP_W), lambda bi, i: (bi, i, 0)),
        out_shape=jax.ShapeDtypeStruct((b, s, GROUP_W), BF16),
        scratch_shapes=([pltpu.VMEM((N_HEADS * GRID_W, n_keys), F32)] * 2
                        + [pltpu.VMEM((N_HEADS * GRID_W, n_keys), BF16)] * 2
                        + [pltpu.VMEM((N_HEADS * GRID_W, 1), F32)] * 2),
        compiler_params=_cparams(("parallel", "arbitrary")),
        name="na_attention",
    )(q, k, v, ky, vy, table)


def _swa_kernel(sink_ref, q_ref, kp_ref, kc_ref, kn_ref, vp_ref, vc_ref, vn_ref, ky_ref, vy_ref, o_ref,
                kw_ref, vw_ref, wb_ref, sa_ref, sb_ref, pa_ref, pb_ref, la_ref, lb_ref, *, blocks_per_step):
    step = pl.program_id(1)
    nb = pl.num_programs(1) * blocks_per_step
    bl = SWA_BLOCK
    g = N_HEADS // SWA_KV_HEADS
    tq = blocks_per_step * bl
    kw_ref[0:bl] = kp_ref[0]
    kw_ref[bl:bl + tq] = kc_ref[0]
    kw_ref[bl + tq:] = kn_ref[0]
    vw_ref[0:bl] = vp_ref[0]
    vw_ref[bl:bl + tq] = vc_ref[0]
    vw_ref[bl + tq:] = vn_ref[0]

    qi = lax.broadcasted_iota(jnp.int32, (g * bl, 3 * bl), 0) % bl
    jk = lax.broadcasted_iota(jnp.int32, (g * bl, 3 * bl), 1)
    wb_ref[...] = jnp.where(jnp.abs(jk - bl - qi) <= SWA_WINDOW, 0.0, NEG_INF)
    half = lax.broadcasted_iota(jnp.int32, (g * bl, 1), 0) // bl

    head_of_lane = lax.broadcasted_iota(jnp.int32, (bl, GROUP_W), 1) // HEAD_DIM
    in_head = [head_of_lane == h for h in range(N_HEADS)]
    head_mask = [jnp.where(mk, 1.0, 0.0).astype(BF16) for mk in in_head]

    def blk_slice(j):
        return pl.ds(pl.multiple_of(j * bl, bl), bl)

    def scores(j, s_ref):
        qoff = pl.multiple_of(j * bl, bl)
        q_all = q_ref[0, blk_slice(j), :]
        kw = kw_ref[pl.ds(qoff, 3 * bl), :]
        n = step * blocks_per_step + j
        lo_edge = jnp.where(n == 0, NEG_INF, 0.0)
        hi_edge = jnp.where(n == nb - 1, NEG_INF, 0.0)
        for kh in range(SWA_KV_HEADS):
            q = jnp.concatenate([q_all * head_mask[kh * g + gi] for gi in range(g)], axis=0)
            s = _dot_nt(q, kw) + wb_ref[...]
            s_ref[kh, :, :bl] = s[:, :bl] + lo_edge
            s_ref[kh, :, bl:2 * bl] = s[:, bl:2 * bl]
            s_ref[kh, :, 2 * bl:3 * bl] = s[:, 2 * bl:] + hi_edge
            s_ref[kh, :, 3 * bl:] = _dot_nt(q, ky_ref[0])

    def softmax(s_ref, p_ref, l_ref):
        for kh in range(SWA_KV_HEADS):
            s = s_ref[kh]
            sink = jnp.full((g * bl, 1), sink_ref[kh * g] * LOG2_E, F32)
            for gi in range(1, g):
                sink = jnp.where(half == gi, sink_ref[kh * g + gi] * LOG2_E, sink)
            m = jnp.maximum(_row_max(s), sink)
            p = jnp.exp2(s - m)
            l_ref[kh] = 1.0 / (_row_sum(p) + jnp.exp2(sink - m))
            p_ref[kh] = p.astype(BF16)

    def values(j, p_ref, l_ref):
        vw = vw_ref[pl.ds(pl.multiple_of(j * bl, bl), 3 * bl), :]
        out = jnp.zeros((bl, GROUP_W), F32)
        for kh in range(SWA_KV_HEADS):
            o = (_dot(p_ref[kh, :, :3 * bl], vw) + _dot(p_ref[kh, :, 3 * bl:], vy_ref[0])) * l_ref[kh]
            for gi in range(g):
                out = jnp.where(in_head[kh * g + gi], o[gi * bl:(gi + 1) * bl], out)
        o_ref[0, blk_slice(j), :] = out.astype(o_ref.dtype)

    pb_ref[...] = jnp.zeros(pb_ref.shape, BF16)
    lb_ref[...] = jnp.zeros(lb_ref.shape, F32)
    scores(0, sa_ref)

    def pair_body(tt, carry):
        t = 2 * tt
        values(jnp.maximum(t - 1, 0), pb_ref, lb_ref)
        scores(t + 1, sb_ref)
        softmax(sa_ref, pa_ref, la_ref)
        values(t, pa_ref, la_ref)
        scores(jnp.minimum(t + 2, blocks_per_step - 1), sa_ref)
        softmax(sb_ref, pb_ref, lb_ref)
        return carry

    lax.fori_loop(0, blocks_per_step // 2, pair_body, 0)
    values(blocks_per_step - 1, pb_ref, lb_ref)


def _swa_attention(sink, q, k, v, ky, vy, blocks_per_step):
    b, s, _ = q.shape
    l = ky.shape[1]
    tq = blocks_per_step * SWA_BLOCK
    n_steps = s // tq
    nb = s // SWA_BLOCK
    kvw = GROUP_W
    q_rows = N_HEADS // SWA_KV_HEADS * SWA_BLOCK
    n_keys = 3 * SWA_BLOCK + l
    prev = pl.BlockSpec((1, SWA_BLOCK, kvw), lambda bi, i: (bi, jnp.maximum(i * blocks_per_step - 1, 0), 0))
    cur = pl.BlockSpec((1, tq, kvw), lambda bi, i: (bi, i, 0))
    nxt = pl.BlockSpec((1, SWA_BLOCK, kvw), lambda bi, i: (bi, jnp.minimum((i + 1) * blocks_per_step, nb - 1), 0))
    ctx = pl.BlockSpec((1, l, kvw), lambda bi, i: (bi, 0, 0))
    return pl.pallas_call(
        functools.partial(_swa_kernel, blocks_per_step=blocks_per_step),
        grid=(b, n_steps),
        in_specs=[pl.BlockSpec(memory_space=pltpu.SMEM),
                  pl.BlockSpec((1, tq, GROUP_W), lambda bi, i: (bi, i, 0)),
                  prev, cur, nxt, prev, cur, nxt, ctx, ctx],
        out_specs=pl.BlockSpec((1, tq, GROUP_W), lambda bi, i: (bi, i, 0)),
        out_shape=jax.ShapeDtypeStruct((b, s, GROUP_W), BF16),
        scratch_shapes=[pltpu.VMEM((tq + 2 * SWA_BLOCK, kvw), BF16),
                        pltpu.VMEM((tq + 2 * SWA_BLOCK, kvw), BF16),
                        pltpu.VMEM((q_rows, 3 * SWA_BLOCK), F32)]
                       + [pltpu.VMEM((SWA_KV_HEADS, q_rows, n_keys), F32)] * 2
                       + [pltpu.VMEM((SWA_KV_HEADS, q_rows, n_keys), BF16)] * 2
                       + [pltpu.VMEM((SWA_KV_HEADS, q_rows, 1), F32)] * 2,
        compiler_params=_cparams(("parallel", "arbitrary")),
        name="swa_attention",
    )(sink, q, k, k, k, v, v, v, ky, vy)


def _outffn_kernel(*refs, hidden_chunk, final):
    if final:
        (x_ref, m0_ref, m1_ref, m2_ref, m3_ref, wo_ref, g1_ref, n2_ref, sc_ref, sh_ref, g2_ref,
         w1_ref, w3_ref, w2_ref, fg_ref, o_ref) = refs
    else:
        (x_ref, m0_ref, m1_ref, m2_ref, m3_ref, wo_ref, g1_ref, n2_ref, sc_ref, sh_ref, g2_ref,
         w1_ref, w3_ref, w2_ref, o_ref) = refs
    half_rows = x_ref.shape[1] // 2
    mod = n2_ref[...] * (1.0 + sc_ref[0])
    x1_halves, hb_halves = [], []
    for r0 in (0, half_rows):
        rows = slice(r0, r0 + half_rows)
        mix = None
        for gi, m_ref in enumerate((m0_ref, m1_ref, m2_ref, m3_ref)):
            part = _dot(m_ref[0, rows, :], wo_ref[gi * GROUP_W:(gi + 1) * GROUP_W, :])
            mix = part if mix is None else mix + part
        x1_h = x_ref[0, rows, :] + g1_ref[0] * mix
        x1_halves.append(x1_h)
        hb_halves.append((_rms(x1_h) * mod + sh_ref[0]).astype(BF16))
    x1 = jnp.concatenate(x1_halves, axis=0)
    hb = jnp.concatenate(hb_halves, axis=0)
    hidden = w1_ref.shape[1]
    acc = None
    for c0 in range(0, hidden, hidden_chunk):
        a = _dot(hb, w1_ref[:, c0:c0 + hidden_chunk])
        bgate = _dot(hb, w3_ref[:, c0:c0 + hidden_chunk])
        u = (_silu(a) * bgate).astype(BF16)
        part = _dot(u, w2_ref[c0:c0 + hidden_chunk, :])
        acc = part if acc is None else acc + part
    x2 = x1 + g2_ref[0] * acc
    if final:
        x2 = _rms(x2) * fg_ref[...]
    o_ref[0] = x2


def _outffn(x, mixes, wo, g1, n2, sc2, sh2, g2, w1, w3, w2, final_g, tm):
    b, t, d = x.shape
    tok = lambda wd: pl.BlockSpec((1, tm, wd), lambda bi, i: (bi, i, 0))
    vec = pl.BlockSpec((1, 1, d), lambda bi, i: (bi, 0, 0))
    in_specs = ([tok(d)] + [tok(GROUP_W)] * 4
                + [_const_spec(wo.shape), vec, _const_spec((1, d)), vec, vec, vec,
                   _const_spec(w1.shape), _const_spec(w3.shape), _const_spec(w2.shape)])
    args = [x, *mixes, wo, g1, n2, sc2, sh2, g2, w1, w3, w2]
    final = final_g is not None
    if final:
        in_specs.append(_const_spec((1, d)))
        args.append(final_g)
    return pl.pallas_call(
        functools.partial(_outffn_kernel, hidden_chunk=256, final=final),
        grid=(b, t // tm),
        in_specs=in_specs,
        out_specs=tok(d),
        out_shape=jax.ShapeDtypeStruct((b, t, d), F32),
        compiler_params=_cparams(("parallel", "parallel")),
        name="out_proj_ffn_final" if final else "out_proj_ffn",
    )(*args)


def _prep_weights(w_in, mla_w_uq, mla_w_ukv):
    depth, d, _ = w_in.shape
    offs = [0]
    for sz in IN_SIZES:
        offs.append(offs[-1] + sz)
    w_bf = w_in.astype(BF16)
    cols = [w_bf[:, :, offs[i]:offs[i + 1]] for i in range(len(IN_SIZES))]
    cq, ckv, kr, rq, rk, rv, rgf, rgb, nq, nk, nv, sq, sk, sv = cols
    scale = HEAD_DIM ** -0.5
    kr_slot = jnp.concatenate([kr, jnp.zeros((depth, d, LANES - MLA_ROPE), BF16)], axis=-1)
    src = jnp.arange(LANES)[:, None]
    dst = jnp.arange(N_HEADS * MLA_HEAD_PAD)[None, :]
    place = ((src < MLA_ROPE) & (dst % MLA_HEAD_PAD == src + MLA_NOPE)).astype(BF16)

    def per_query_head(t):
        g = N_HEADS // SWA_KV_HEADS
        t = t.reshape(depth, d, SWA_KV_HEADS, 1, HEAD_DIM)
        return jnp.broadcast_to(t, (depth, d, SWA_KV_HEADS, g, HEAD_DIM)).reshape(depth, d, GROUP_W)

    w = jnp.concatenate([cq, ckv, kr_slot, rq, rk * scale, rv, rgf, rgb, nq * scale, nk, nv,
                         sq * scale, per_query_head(sk), per_query_head(sv)], axis=-1).astype(BF16)

    qr = mla_w_uq.shape[1]
    uq = mla_w_uq.reshape(depth, qr, N_HEADS, MLA_NOPE + MLA_ROPE)
    wuq = jnp.concatenate([uq, jnp.zeros((depth, qr, N_HEADS, MLA_HEAD_PAD - MLA_NOPE - MLA_ROPE), F32)],
                          axis=-1).reshape(depth, qr, N_HEADS * MLA_HEAD_PAD).astype(BF16)
    kvr = mla_w_ukv.shape[1]
    ukv = mla_w_ukv.reshape(depth, kvr, N_HEADS, MLA_NOPE + MLA_V)
    zk = jnp.zeros((depth, kvr, N_HEADS, MLA_HEAD_PAD - MLA_NOPE), F32)
    wuk = jnp.concatenate([ukv[..., :MLA_NOPE], zk], axis=-1).reshape(depth, kvr, -1).astype(BF16)
    zv = jnp.zeros((depth, kvr, N_HEADS, MLA_HEAD_PAD - MLA_V), F32)
    wuv = jnp.concatenate([ukv[..., MLA_NOPE:], zv], axis=-1).reshape(depth, kvr, -1).astype(BF16)
    return w, wuq, wuk, wuv, place


def _ctx_head_specs():
    mla = tuple((h * MLA_HEAD_PAD, (h + 1) * MLA_HEAD_PAD, h * MLA_HEAD_PAD, (h + 1) * MLA_HEAD_PAD,
                 h * MLA_HEAD_PAD, h * MLA_HEAD_PAD + MLA_V, None) for h in range(N_HEADS))
    na = tuple((h * HEAD_DIM, (h + 1) * HEAD_DIM) * 3 + (None,) for h in range(N_HEADS))
    swa = tuple((h * HEAD_DIM, (h + 1) * HEAD_DIM) * 3 + (h,) for h in range(N_HEADS))
    return mla, na, swa


def kernel(x, c, ctx, c_ctx, ada_w, ada_b, norm1_g, w_in, mla_q_norm, mla_w_uq, mla_kv_norm, mla_w_ukv,
           ret_decay, na_rpb, swa_sink, w_out, norm2_g, ffn_w1, ffn_w3, ffn_w2, final_norm_g):
    b, s, d = x.shape
    l_ctx = ctx.shape[1]
    depth = ada_w.shape[0]
    assert b + 1 <= 8 and s % 2048 == 0 and l_ctx % 128 == 0

    cond = jnp.concatenate([c, c_ctx[None, :], jnp.zeros((8 - b - 1, d), F32)], axis=0)
    mod = _modulation(cond, ada_w, ada_b)
    tables = _rope_tables(s)
    w_all, wuq_all, wuk_all, wuv_all, place = _prep_weights(w_in, mla_w_uq, mla_w_ukv)
    wo_all = w_out.astype(BF16)
    w1_all, w3_all, w2_all = ffn_w1.astype(BF16), ffn_w3.astype(BF16), ffn_w2.astype(BF16)
    mla_heads, na_heads, swa_heads = _ctx_head_specs()
    no_sink = jnp.zeros((N_HEADS,), F32)

    tm_x = 512
    tm_y = min(256, l_ctx)
    y = ctx
    for l in range(depth):
        mx = [mod[l, :b, j * d:(j + 1) * d][:, None, :] for j in range(6)]
        my = [jnp.broadcast_to(mod[l, b, j * d:(j + 1) * d][None, None, :], (b, 1, d)) for j in range(6)]
        n1 = norm1_g[l][None, :]
        n2 = norm2_g[l][None, :]
        qn = mla_q_norm[l][None, :]
        kvn = mla_kv_norm[l][None, :]
        lw = (w_all[l], qn, kvn, wuq_all[l], wuk_all[l], wuv_all[l], place)

        px = _inproj(x, n1, mx[1], mx[0], *lw, tables, tm_x)
        py = _inproj(y, n1, my[1], my[0], *lw, None, tm_y)
        (xmq, xmk, xmv, xrq, xrk, xrv, xrg, xnq, xnk, xnv, xsq, xsk, xsv) = px
        (ymq, ymk, ymv, yrq, yrk, yrv, yrg, ynq, ynk, ynv, ysq, ysk, ysv) = py

        mla_x = _mla_attention(xmq, xmk, xmv, ymk, ymv, tq=1024, tk=512, n_tiles=2, n_sub=2, unroll=15)
        dec = ret_decay[l].reshape(-1)
        ret_x, ret_y = _retention(dec, xrq, xrk, xrv, xrg, yrq, yrk, yrv, yrg, chunk=128, chunks_per_step=16)
        table = _na_bias_table(na_rpb[l])
        na_x = _na_attention(xnq, xnk, xnv, ynk, ynv, table, rows_per_step=32)
        swa_x = _swa_attention(swa_sink[l], xsq, xsk, xsv, ysk, ysv, blocks_per_step=16)

        last = l == depth - 1
        x = _outffn(x, (mla_x, ret_x, na_x, swa_x), wo_all[l], mx[2], n2, mx[4], mx[3], mx[5],
                    w1_all[l], w3_all[l], w2_all[l], final_norm_g[None, :] if last else None, tm_x)
        if not last:
            mla_y = _ctx_attention(ymq, ymk, ymv, mla_heads, no_sink, "mla_ctx_attention")
            na_y = _ctx_attention(ynq, ynk, ynv, na_heads, no_sink, "na_ctx_attention")
            swa_y = _ctx_attention(ysq, ysk, ysv, swa_heads, swa_sink[l], "swa_ctx_attention")
            y = _outffn(y, (mla_y, ret_y, na_y, swa_y), wo_all[l], my[2], n2, my[4], my[3], my[5],
                        w1_all[l], w3_all[l], w2_all[l], None, tm_y)
    return x
```

```python
import functools

import jax
import jax.numpy as jnp
from jax import lax
from jax.experimental import pallas as pl
from jax.experimental.pallas import tpu as pltpu

F32 = jnp.float32
BF16 = jnp.bfloat16

GRID_W = 64
HEAD_DIM = 64
N_HEADS = 4
GROUP_W = N_HEADS * HEAD_DIM
MLA_Q_RANK = 256
MLA_KV_RANK = 128
MLA_NOPE = 64
MLA_ROPE = 32
MLA_V = 64
MLA_HEAD_PAD = 128
MLA_V_ROWS = MLA_V + 16
NA_KR = 8
NA_KC = 16
SWA_KV_HEADS = 2
SWA_WINDOW = 128
SWA_BLOCK = 128
ROPE_THETA = 10000.0
EPS = 1e-6
NEG_INF = -1e30
LOG2_E = 1.4426950408889634
LANES = 128
VMEM_LIMIT = 56 * 1024 * 1024

IN_SIZES = (MLA_Q_RANK, MLA_KV_RANK, MLA_ROPE,
            GROUP_W, GROUP_W, GROUP_W, GROUP_W, GROUP_W,
            GROUP_W, GROUP_W, GROUP_W,
            GROUP_W, SWA_KV_HEADS * HEAD_DIM, SWA_KV_HEADS * HEAD_DIM)

_O_CQ = 0
_O_CKV = _O_CQ + MLA_Q_RANK
_O_KR = _O_CKV + MLA_KV_RANK
_O_RQK = _O_KR + LANES
_O_RV = _O_RQK + 2 * GROUP_W
_O_RG = _O_RV + GROUP_W
_O_NA = _O_RG + 2 * GROUP_W
_O_SQK = _O_NA + 3 * GROUP_W
_O_SV = _O_SQK + 2 * GROUP_W
_IN_COLS = _O_SV + GROUP_W


def _cparams(sem):
    return pltpu.CompilerParams(dimension_semantics=sem, vmem_limit_bytes=VMEM_LIMIT)


def _dot(a, b):
    return jnp.dot(a, b, preferred_element_type=F32)


def _dot_nt(a, b):
    return lax.dot_general(a, b, (((1,), (1,)), ((), ())), preferred_element_type=F32)


def _rms(x):
    return x * lax.rsqrt(jnp.mean(x * x, axis=-1, keepdims=True) + EPS)


def _silu(x):
    return x * jax.nn.sigmoid(x)


def _lane_chunks(arrays):
    return [a[:, j * LANES:(j + 1) * LANES] for a in arrays for j in range(a.shape[-1] // LANES)]


def _row_max(*arrays):
    return jnp.max(functools.reduce(jnp.maximum, _lane_chunks(arrays)), axis=-1, keepdims=True)


def _row_sum(*arrays):
    return jnp.sum(functools.reduce(jnp.add, _lane_chunks(arrays)), axis=-1, keepdims=True)


def _mod_kernel(c_ref, w_ref, b_ref, o_ref):
    o_ref[0] = _dot(_silu(c_ref[...]), w_ref[0]) + b_ref[0]


def _modulation(cond, ada_w, ada_b):
    depth, d, d6 = ada_w.shape
    n = d6 // d
    return pl.pallas_call(
        _mod_kernel,
        grid=(depth, n),
        in_specs=[pl.BlockSpec((8, d), lambda l, j: (0, 0)),
                  pl.BlockSpec((1, d, d), lambda l, j: (l, 0, j)),
                  pl.BlockSpec((1, 1, d), lambda l, j: (l, 0, j))],
        out_specs=pl.BlockSpec((1, 8, d), lambda l, j: (l, 0, j)),
        out_shape=jax.ShapeDtypeStruct((depth, 8, d6), F32),
        compiler_params=_cparams(("parallel", "parallel")),
        name="ada_modulation",
    )(cond, ada_w, ada_b.reshape(depth, 1, d6))


def _rope_tables(seq):
    n_rows = seq // GRID_W

    def parts(pos, d):
        inv = ROPE_THETA ** (-jnp.arange(0, d, 2, dtype=F32) / d)
        ang = pos.astype(F32)[:, None] * inv[None, :]
        z = jnp.zeros_like(ang)
        return (jnp.concatenate([jnp.cos(ang), jnp.cos(ang)], axis=-1),
                jnp.concatenate([z, jnp.sin(ang)], axis=-1),
                jnp.concatenate([-jnp.sin(ang), z], axis=-1))

    def expand(by_row, by_col):
        w = by_row.shape[-1]
        r = jnp.broadcast_to(by_row[:, None, :], (n_rows, GRID_W, w))
        c = jnp.broadcast_to(by_col[None, :, :], (n_rows, GRID_W, w))
        return jnp.concatenate([r, c], axis=-1).reshape(seq, 2 * w)

    def tables(d, fill):
        per_head = [expand(a, b) for a, b in zip(parts(jnp.arange(n_rows), d), parts(jnp.arange(GRID_W), d))]
        return [fill(t, i) for i, t in enumerate(per_head)]

    def two_heads(t, _):
        return jnp.concatenate([t, t], axis=-1)

    def mla_slot(t, i):
        lead = (jnp.ones if i == 0 else jnp.zeros)((seq, MLA_NOPE), F32)
        tail = (jnp.ones if i == 0 else jnp.zeros)((seq, MLA_HEAD_PAD - MLA_NOPE - MLA_ROPE), F32)
        return jnp.concatenate([lead, t, tail], axis=-1)

    def kr_slot(t, i):
        tail = (jnp.ones if i == 0 else jnp.zeros)((seq, LANES - MLA_ROPE), F32)
        return jnp.concatenate([t, tail], axis=-1)

    return tuple(tables(HEAD_DIM // 2, two_heads) + tables(MLA_ROPE // 2, mla_slot)
                 + tables(MLA_ROPE // 2, kr_slot))


def _rope(x, cos, s_prev, s_next, d):
    out = []
    for j in range(x.shape[-1] // LANES):
        xc = x[:, j * LANES:(j + 1) * LANES]
        out.append(xc * cos + pltpu.roll(xc, d, 1) * s_prev + pltpu.roll(xc, LANES - d, 1) * s_next)
    return out[0] if len(out) == 1 else jnp.concatenate(out, axis=-1)


def _inproj_kernel(*refs, rotate, mla_scale):
    if rotate:
        (x_ref, g_ref, sc_ref, sh_ref, w_ref, qn_ref, kvn_ref, wuq_ref, wuk_ref, wuv_ref, place_ref,
         c64_ref, p64_ref, n64_ref, cm_ref, pm_ref, nm_ref, ckr_ref, pkr_ref, nkr_ref,
         mq_ref, mk_ref, mv_ref, rq_ref, rk_ref, rv_ref, rg_ref,
         nq_ref, nk_ref, nv_ref, sq_ref, sk_ref, sv_ref) = refs
    else:
        (x_ref, g_ref, sc_ref, sh_ref, w_ref, qn_ref, kvn_ref, wuq_ref, wuk_ref, wuv_ref, place_ref,
         mq_ref, mk_ref, mv_ref, rq_ref, rk_ref, rv_ref, rg_ref,
         nq_ref, nk_ref, nv_ref, sq_ref, sk_ref, sv_ref) = refs

    x = x_ref[0]
    h = _rms(x) * (g_ref[...] * (1.0 + sc_ref[0])) + sh_ref[0]
    hb = h.astype(BF16)

    def proj(lo, hi):
        return _dot(hb, w_ref[:, lo:hi])

    def rope64(v):
        if not rotate:
            return v
        return _rope(v, c64_ref[...], p64_ref[...], n64_ref[...], HEAD_DIM // 4)

    def rope_mla(v):
        if not rotate:
            return v
        return _rope(v, cm_ref[...], pm_ref[...], nm_ref[...], MLA_ROPE // 4)

    cq = (_rms(proj(_O_CQ, _O_CKV)) * qn_ref[...]).astype(BF16)
    ckv_kr = proj(_O_CKV, _O_RQK)
    ckv = (_rms(ckv_kr[:, :MLA_KV_RANK]) * kvn_ref[...]).astype(BF16)
    kr = ckv_kr[:, MLA_KV_RANK:]
    if rotate:
        kr = _rope(kr, ckr_ref[...], pkr_ref[...], nkr_ref[...], MLA_ROPE // 4)
    kr = kr.astype(BF16)

    rqk = proj(_O_RQK, _O_RV)
    rq_ref[0] = rope64(rqk[:, :GROUP_W]).astype(BF16)
    rk_ref[0] = rope64(rqk[:, GROUP_W:]).astype(BF16)
    rv_ref[0] = proj(_O_RV, _O_RG).astype(BF16)

    na = proj(_O_NA, _O_SQK)
    nq_ref[0] = (na[:, :GROUP_W] * LOG2_E).astype(BF16)
    nk_ref[0] = na[:, GROUP_W:2 * GROUP_W].astype(BF16)
    nv_ref[0] = na[:, 2 * GROUP_W:].astype(BF16)

    sqk = proj(_O_SQK, _O_SV)
    sq_ref[0] = (rope64(sqk[:, :GROUP_W]) * LOG2_E).astype(BF16)
    sk_ref[0] = rope64(sqk[:, GROUP_W:]).astype(BF16)
    sv_ref[0] = proj(_O_SV, _IN_COLS).astype(BF16)

    q = rope_mla(_dot(cq, wuq_ref[...])) * mla_scale
    mq_ref[0] = q.astype(BF16)
    k = _dot(ckv, wuk_ref[...]) + _dot(kr, place_ref[...])
    mk_ref[0] = k.astype(BF16)
    v = _dot(ckv, wuv_ref[...])
    lane = lax.broadcasted_iota(jnp.int32, v.shape, 1)
    mv_ref[0] = jnp.where(lane % MLA_HEAD_PAD >= MLA_V, 1.0, v).astype(BF16)

    rg_ref[0] = proj(_O_RG, _O_NA)


def _const_spec(shape):
    nd = len(shape)
    return pl.BlockSpec(shape, lambda *_: (0,) * nd, pipeline_mode=pl.Buffered(1))


def _inproj(x, gain, scale, shift, w, qn, kvn, wuq, wuk, wuv, place, tables, tm):
    b, t, d = x.shape
    rotate = tables is not None
    kv_w = GROUP_W
    mla_w = N_HEADS * MLA_HEAD_PAD
    tok = lambda wd: pl.BlockSpec((1, tm, wd), lambda bi, i: (bi, i, 0))
    vec = pl.BlockSpec((1, 1, d), lambda bi, i: (bi, 0, 0))
    in_specs = [tok(d), _const_spec((1, d)), vec, vec, _const_spec(w.shape),
                _const_spec(qn.shape), _const_spec(kvn.shape), _const_spec(wuq.shape),
                _const_spec(wuk.shape), _const_spec(wuv.shape), _const_spec(place.shape)]
    args = [x, gain, scale, shift, w, qn, kvn, wuq, wuk, wuv, place]
    if rotate:
        in_specs += [pl.BlockSpec((tm, LANES), lambda bi, i: (i, 0))] * len(tables)
        args += list(tables)
    widths = [mla_w, mla_w, mla_w, GROUP_W, GROUP_W, GROUP_W, 2 * GROUP_W,
              GROUP_W, GROUP_W, GROUP_W, GROUP_W, kv_w, kv_w]
    dtypes = [BF16] * 6 + [F32] + [BF16] * 6
    return pl.pallas_call(
        functools.partial(_inproj_kernel, rotate=rotate, mla_scale=(MLA_NOPE + MLA_ROPE) ** -0.5 * LOG2_E),
        grid=(b, t // tm),
        in_specs=in_specs,
        out_specs=[tok(wd) for wd in widths],
        out_shape=[jax.ShapeDtypeStruct((b, t, wd), dt) for wd, dt in zip(widths, dtypes)],
        compiler_params=_cparams(("parallel", "parallel")),
        name="in_proj_rot" if rotate else "in_proj_ctx",
    )(*args)


def _mla_kernel(q_ref, kx_ref, vx_ref, ky_ref, vy_ref, o_ref, m_ref, acc_ref, sa_ref, sb_ref, sc_ref,
                ma_ref, mb_ref, mc_ref, *, tk, heads, n_tiles, n_sub, unroll):
    n_chunks = kx_ref.shape[1] // tk
    assert n_chunks % 2 == 0
    tq = q_ref.shape[1] // n_tiles
    ts = tq // n_sub
    chains = [(h, u) for h in range(heads) for u in range(n_sub)]

    def reset():
        for c in range(len(chains)):
            m_ref[c] = jnp.full(m_ref.shape[1:], NEG_INF, F32)
            acc_ref[c] = jnp.zeros(acc_ref.shape[1:], F32)

    def rows(h):
        return slice(h * MLA_HEAD_PAD, (h + 1) * MLA_HEAD_PAD)

    def vrows(h):
        return slice(h * MLA_HEAD_PAD, h * MLA_HEAD_PAD + MLA_V_ROWS)

    buf_a, buf_b, buf_c = (sa_ref, ma_ref), (sb_ref, mb_ref), (sc_ref, mc_ref)

    def scores(t, c, k, dst):
        h, u = chains[c]
        q0 = t * tq + u * ts
        st = _dot_nt(k, q_ref[0, q0:q0 + ts, rows(h)])
        dst[0][c] = st
        dst[1][c] = jnp.max(st, axis=0, keepdims=True)

    def absorb(c, src, vt):
        m_old = m_ref[c]
        m_new = jnp.maximum(m_old, src[1][c])
        pt = jnp.exp2(src[0][c] - m_new).astype(BF16)
        acc_ref[c, :MLA_V_ROWS] = acc_ref[c, :MLA_V_ROWS] * jnp.exp2(m_old - m_new) + _dot(vt, pt)
        m_ref[c] = m_new

    def kx(j, h):
        return kx_ref[0, pl.ds(pl.multiple_of(j * tk, tk), tk), rows(h)]

    def stage_at(t, j, parity):
        cur, nxt = (buf_a, buf_b) if parity == 0 else (buf_b, buf_a)
        for c, (h, _) in enumerate(chains):
            scores(t, c, kx(j + 1, h), nxt)
            absorb(c, cur, vx_ref[0, j, vrows(h), :])

    trips = (n_chunks - 1) // unroll
    reset()
    for c, (h, _) in enumerate(chains):
        scores(0, c, kx(0, h), buf_a)
    for t in range(n_tiles):
        def body(jj, carry):
            for i in range(unroll):
                stage_at(t, unroll * jj + i, i % 2)
            return carry

        lax.fori_loop(0, trips, body, 0)
        for j in range(unroll * trips, n_chunks - 1):
            stage_at(t, j, j % 2)
        for c, (h, _) in enumerate(chains):
            scores(t, c, ky_ref[0, :, rows(h)], buf_c)
            absorb(c, buf_b, vx_ref[0, n_chunks - 1, vrows(h), :])
        outs = [[None] * n_sub for _ in range(heads)]
        for c, (h, u) in enumerate(chains):
            if t + 1 < n_tiles:
                scores(t + 1, c, kx(0, h), buf_a)
            absorb(c, buf_c, vy_ref[0, 0, vrows(h), :])
            acc = acc_ref[c].T
            outs[h][u] = acc[:, :MLA_V] / acc[:, MLA_V:MLA_V + 1]
        o_ref[0, t * tq:(t + 1) * tq, :] = jnp.concatenate(
            [jnp.concatenate(outs[h], axis=0) for h in range(heads)], axis=-1).astype(o_ref.dtype)
        if t + 1 < n_tiles:
            reset()


def _mla_attention(q, kx, vx, ky, vy, tq, tk, n_tiles, n_sub, unroll):
    b, s, _ = q.shape
    l = ky.shape[1]
    hp = 2
    wd = hp * MLA_HEAD_PAD
    vxt = jnp.swapaxes(vx.reshape(b, s // tk, tk, -1), 2, 3)
    vyt = jnp.swapaxes(vy.reshape(b, 1, l, -1), 2, 3)
    ts = tq // n_tiles // n_sub
    return pl.pallas_call(
        functools.partial(_mla_kernel, tk=tk, heads=hp, n_tiles=n_tiles, n_sub=n_sub, unroll=unroll),
        grid=(b, N_HEADS // hp, s // tq),
        in_specs=[pl.BlockSpec((1, tq, wd), lambda bi, hi, i: (bi, i, hi)),
                  pl.BlockSpec((1, s, wd), lambda bi, hi, i: (bi, 0, hi)),
                  pl.BlockSpec((1, s // tk, wd, tk), lambda bi, hi, i: (bi, 0, hi, 0)),
                  pl.BlockSpec((1, l, wd), lambda bi, hi, i: (bi, 0, hi)),
                  pl.BlockSpec((1, 1, wd, l), lambda bi, hi, i: (bi, 0, hi, 0))],
        out_specs=pl.BlockSpec((1, tq, hp * MLA_V), lambda bi, hi, i: (bi, i, hi)),
        out_shape=jax.ShapeDtypeStruct((b, s, N_HEADS * MLA_V), BF16),
        scratch_shapes=[pltpu.VMEM((hp * n_sub, 1, ts), F32),
                        pltpu.VMEM((hp * n_sub, MLA_HEAD_PAD, ts), F32),
                        pltpu.VMEM((hp * n_sub, tk, ts), F32),
                        pltpu.VMEM((hp * n_sub, tk, ts), F32),
                        pltpu.VMEM((hp * n_sub, l, ts), F32)]
                       + [pltpu.VMEM((hp * n_sub, 1, ts), F32)] * 3,
        compiler_params=_cparams(("parallel", "parallel", "arbitrary")),
        name="mla_attention",
    )(q, kx, vxt, ky, vyt)


def _ctx_attn_kernel(sink_ref, q_ref, k_ref, v_ref, o_ref, *, heads):
    outs = []
    for (q0, q1, k0, k1, v0, v1, sink_idx) in heads:
        q = q_ref[0, :, q0:q1]
        s = _dot_nt(q, k_ref[0, :, k0:k1])
        m = jnp.max(s, axis=-1, keepdims=True)
        if sink_idx is not None:
            sink = jnp.full((1, 1), sink_ref[sink_idx] * LOG2_E, F32)
            m = jnp.maximum(m, sink)
        p = jnp.exp2(s - m)
        l = jnp.sum(p, axis=-1, keepdims=True)
        if sink_idx is not None:
            l = l + jnp.exp2(sink - m)
        outs.append(_dot(p.astype(BF16), v_ref[0, :, v0:v1]) / l)
    o_ref[0] = jnp.concatenate(outs, axis=-1).astype(o_ref.dtype)


def _ctx_attention(q, k, v, heads, sink, name):
    b, l, _ = q.shape
    full = lambda a: pl.BlockSpec((1, l, a.shape[-1]), lambda bi: (bi, 0, 0))
    return pl.pallas_call(
        functools.partial(_ctx_attn_kernel, heads=heads),
        grid=(b,),
        in_specs=[pl.BlockSpec(memory_space=pltpu.SMEM), full(q), full(k), full(v)],
        out_specs=pl.BlockSpec((1, l, GROUP_W), lambda bi: (bi, 0, 0)),
        out_shape=jax.ShapeDtypeStruct((b, l, GROUP_W), BF16),
        compiler_params=_cparams(("parallel",)),
        name=name,
    )(sink, q, k, v)


def _ret_kernel(*refs, chunk, n_chunks, direction, has_prev):
    if has_prev:
        (dec_ref, q_ref, k_ref, kt_ref, v_ref, g_ref, s0_ref, prev_ref, o_ref, sn_ref,
         st_ref, dm_ref, qd_ref, kdt_ref, cd_ref, ob_ref, qk_ref) = refs
    else:
        (dec_ref, q_ref, k_ref, kt_ref, v_ref, g_ref, s0_ref, o_ref, sn_ref,
         st_ref, dm_ref, qd_ref, kdt_ref, cd_ref, ob_ref, qk_ref) = refs
        prev_ref = None
    i = pl.program_id(0)
    c = chunk
    fwd = direction == 0
    batch = range(q_ref.shape[0])

    def head_of(shape, axis):
        return lax.broadcasted_iota(jnp.int32, shape, axis) // HEAD_DIM

    @pl.when(i == 0)
    def _init():
        st_ref[...] = s0_ref[...]
        ii = lax.broadcasted_iota(jnp.int32, (c, c), 0).astype(F32)
        jj = lax.broadcasted_iota(jnp.int32, (c, c), 1).astype(F32)
        diff = (ii - jj) if fwd else (jj - ii)
        pos = lax.broadcasted_iota(jnp.int32, (c, GROUP_W), 0).astype(F32)
        pos_t = lax.broadcasted_iota(jnp.int32, (GROUP_W, c), 1).astype(F32)
        q_steps = (pos + 1.0) if fwd else (c - pos)
        k_steps = (c - 1.0 - pos_t) if fwd else pos_t
        qd = jnp.zeros((c, GROUP_W), F32)
        kdt = jnp.zeros((GROUP_W, c), F32)
        cd = jnp.zeros((GROUP_W, GROUP_W), F32)
        for h in range(N_HEADS):
            dec = dec_ref[direction * N_HEADS + h]
            lg = jax.nn.log_sigmoid(jnp.full((c, c), dec, F32))
            dm_ref[h * c:(h + 1) * c] = jnp.where(diff >= 0, jnp.exp(lg * jnp.maximum(diff, 0.0)), 0.0)
            lgq = jax.nn.log_sigmoid(jnp.full((c, GROUP_W), dec, F32))
            qd = jnp.where(head_of((c, GROUP_W), 1) == h, jnp.exp(lgq * q_steps), qd)
            lgk = jax.nn.log_sigmoid(jnp.full((GROUP_W, c), dec, F32))
            kdt = jnp.where(head_of((GROUP_W, c), 0) == h, jnp.exp(lgk * k_steps), kdt)
            lgc = jax.nn.log_sigmoid(jnp.full((GROUP_W, GROUP_W), dec, F32))
            cd = jnp.where(head_of((GROUP_W, GROUP_W), 0) == h, jnp.exp(lgc * c), cd)
        qd_ref[...] = qd
        kdt_ref[...] = kdt
        cd_ref[...] = cd

    in_head = [head_of((c, GROUP_W), 1) == h for h in range(N_HEADS)]
    head_mask = [jnp.where(mk, 1.0, 0.0).astype(BF16) for mk in in_head]
    same_head = head_of((GROUP_W, GROUP_W), 0) == head_of((GROUP_W, GROUP_W), 1)

    def chunk_off(n):
        idx = n if fwd else n_chunks - 1 - n
        return pl.multiple_of(idx * c, c)

    def qk(bb, n):
        off = chunk_off(n)
        q_all = q_ref[bb, pl.ds(off, c), :]
        q4 = jnp.concatenate([q_all * head_mask[h] for h in range(N_HEADS)], axis=0)
        return _dot_nt(q4, k_ref[bb, pl.ds(off, c), :])

    def gate_and_store(bb, n):
        off = chunk_off(n)
        o = ob_ref[bb]
        oo = o * o
        ms = jnp.zeros((c, GROUP_W), F32)
        for h in range(N_HEADS):
            ms_h = jnp.sum(jnp.where(in_head[h], oo, 0.0), axis=-1, keepdims=True) * (1.0 / HEAD_DIM)
            ms = jnp.where(in_head[h], ms_h, ms)
        res = o * lax.rsqrt(ms + EPS) * _silu(g_ref[bb, pl.ds(off, c), :])
        if has_prev:
            res = res + prev_ref[bb, pl.ds(off, c), :]
        o_ref[bb, pl.ds(off, c), :] = res.astype(o_ref.dtype)

    ob_ref[...] = jnp.zeros(ob_ref.shape, F32)

    for bb in batch:
        qk_ref[bb] = qk(bb, 0)

    def body(n, carry):
        for bb in batch:
            gate_and_store(bb, jnp.maximum(n - 1, 0))
        off = chunk_off(n)
        atts = [(qk_ref[bb] * dm_ref[...]).astype(BF16) for bb in batch]
        for bb in batch:
            qk_ref[bb] = qk(bb, jnp.minimum(n + 1, n_chunks - 1))
        for bb in batch:
            v = v_ref[bb, pl.ds(off, c), :]
            att = atts[bb]
            intra4 = _dot(att, v)
            intra = intra4[:c]
            for h in range(1, N_HEADS):
                intra = jnp.where(in_head[h], intra4[h * c:(h + 1) * c], intra)
            state = st_ref[bb]
            ob_ref[bb] = intra + _dot(q_ref[bb, pl.ds(off, c), :], state.astype(BF16)) * qd_ref[...]
            kk = (kt_ref[bb, :, pl.ds(off, c)].astype(F32) * kdt_ref[...]).astype(BF16)
            st_ref[bb] = state * cd_ref[...] + jnp.where(same_head, _dot(kk, v), 0.0)
        return carry

    lax.fori_loop(0, n_chunks, body, 0)
    for bb in batch:
        gate_and_store(bb, n_chunks - 1)

    @pl.when(i == pl.num_programs(0) - 1)
    def _fin():
        sn_ref[...] = st_ref[...]


def _retention_pass(dec, q, k, v, gates, state0, prev, direction, chunk, n_chunks, out_dtype):
    b, t, _ = q.shape
    tb = chunk * n_chunks
    n = t // tb
    blk = (lambda i: i) if direction == 0 else (lambda i: n - 1 - i)
    tok = pl.BlockSpec((b, tb, GROUP_W), lambda i: (0, blk(i), 0))
    tok_t = pl.BlockSpec((b, GROUP_W, tb), lambda i: (0, 0, blk(i)))
    gate = pl.BlockSpec((b, tb, GROUP_W), lambda i: (0, blk(i), direction))
    st_spec = pl.BlockSpec((b, GROUP_W, GROUP_W), lambda i: (0, 0, 0))
    in_specs = [pl.BlockSpec(memory_space=pltpu.SMEM), tok, tok, tok_t, tok, gate, st_spec]
    args = [dec, q, k, jnp.swapaxes(k, 1, 2), v, gates, state0]
    if prev is not None:
        in_specs.append(tok)
        args.append(prev)
    return pl.pallas_call(
        functools.partial(_ret_kernel, chunk=chunk, n_chunks=n_chunks, direction=direction,
                          has_prev=prev is not None),
        grid=(n,),
        in_specs=in_specs,
        out_specs=[tok, st_spec],
        out_shape=[jax.ShapeDtypeStruct((b, t, GROUP_W), out_dtype),
                   jax.ShapeDtypeStruct((b, GROUP_W, GROUP_W), F32)],
        scratch_shapes=[pltpu.VMEM((b, GROUP_W, GROUP_W), F32),
                        pltpu.VMEM((N_HEADS * chunk, chunk), F32),
                        pltpu.VMEM((chunk, GROUP_W), F32),
                        pltpu.VMEM((GROUP_W, chunk), F32),
                        pltpu.VMEM((GROUP_W, GROUP_W), F32),
                        pltpu.VMEM((b, chunk, GROUP_W), F32),
                        pltpu.VMEM((b, N_HEADS * chunk, chunk), F32)],
        compiler_params=_cparams(("arbitrary",)),
        name="retention_fwd" if direction == 0 else "retention_bwd",
    )(*args)


def _retention(dec, xq, xk, xv, xg, yq, yk, yv, yg, chunk, chunks_per_step):
    b = xq.shape[0]
    zero = jnp.zeros((b, GROUP_W, GROUP_W), F32)
    ny = yq.shape[1] // chunk
    yb, sb = _retention_pass(dec, yq, yk, yv, yg, zero, None, 1, chunk, ny, F32)
    y, sf = _retention_pass(dec, yq, yk, yv, yg, zero, yb, 0, chunk, ny, BF16)
    xb, _ = _retention_pass(dec, xq, xk, xv, xg, sb, None, 1, chunk, chunks_per_step, F32)
    x, _ = _retention_pass(dec, xq, xk, xv, xg, sf, xb, 0, chunk, chunks_per_step, BF16)
    return x, y


def _na_bias_kernel(rpb_ref, o_ref):
    h = pl.program_id(0)
    dr0 = pl.program_id(1)
    c = lax.broadcasted_iota(jnp.int32, (GRID_W, GRID_W), 0)
    kc = lax.broadcasted_iota(jnp.int32, (GRID_W, GRID_W), 1)
    c0 = jnp.clip(c - NA_KC // 2, 0, GRID_W - NA_KC)
    col_in = (kc >= c0) & (kc < c0 + NA_KC)
    dc = jnp.clip(kc - c, -(NA_KC - 1), NA_KC - 1) + NA_KC - 1
    n_dc = 2 * NA_KC - 1
    accs = [jnp.zeros((GRID_W, GRID_W), F32) for _ in range(NA_KR)]
    for d in range(n_dc):
        at_d = dc == d
        for j in range(NA_KR):
            accs[j] = jnp.where(at_d, rpb_ref[(h * (2 * NA_KR - 1) + dr0 + j) * n_dc + d], accs[j])
    for j in range(NA_KR):
        o_ref[0, :, j * GRID_W:(j + 1) * GRID_W] = jnp.where(col_in, accs[j] * LOG2_E, NEG_INF)


def _na_bias_table(rpb):
    return pl.pallas_call(
        _na_bias_kernel,
        grid=(N_HEADS, NA_KR),
        in_specs=[pl.BlockSpec(memory_space=pltpu.SMEM)],
        out_specs=pl.BlockSpec((1, GRID_W, NA_KR * GRID_W), lambda h, r: (r, h, 0)),
        out_shape=jax.ShapeDtypeStruct((NA_KR, N_HEADS * GRID_W, NA_KR * GRID_W), F32),
        compiler_params=_cparams(("parallel", "parallel")),
        name="na_bias_table",
    )(rpb.reshape(-1))


def _na_kernel(q_ref, k_ref, v_ref, ky_ref, vy_ref, tb_ref, o_ref,
               sa_ref, sb_ref, pa_ref, pb_ref, la_ref, lb_ref, *, rows_per_step, n_rows):
    r_base = pl.program_id(1) * rows_per_step
    win = NA_KR * GRID_W

    head_of_lane = lax.broadcasted_iota(jnp.int32, (GRID_W, GROUP_W), 1) // HEAD_DIM
    in_head = [head_of_lane == h for h in range(N_HEADS)]
    head_mask = [jnp.where(mk, 1.0, 0.0).astype(BF16) for mk in in_head]

    def geometry(i):
        r = r_base + i
        r0 = jnp.clip(r - NA_KR // 2, 0, n_rows - NA_KR)
        return pl.multiple_of(r0 * GRID_W, GRID_W), r0 - r + NA_KR - 1

    def row_slice(i):
        return pl.ds(pl.multiple_of(i * GRID_W, GRID_W), GRID_W)

    def scores(i, s_ref):
        koff, dr0 = geometry(i)
        q_all = q_ref[0, row_slice(i), :]
        q4 = jnp.concatenate([q_all * head_mask[h] for h in range(N_HEADS)], axis=0)
        s_ref[:, :win] = _dot_nt(q4, k_ref[0, pl.ds(koff, win), :]) + tb_ref[dr0]
        s_ref[:, win:] = _dot_nt(q4, ky_ref[0])

    def softmax(s_ref, p_ref, l_ref):
        s = s_ref[...]
        p = jnp.exp2(s - _row_max(s))
        l_ref[...] = 1.0 / _row_sum(p)
        p_ref[...] = p.astype(BF16)

    def values(i, p_ref, l_ref):
        koff, _ = geometry(i)
        o4 = (_dot(p_ref[:, :win], v_ref[0, pl.ds(koff, win), :])
              + _dot(p_ref[:, win:], vy_ref[0])) * l_ref[...]
        out = o4[:GRID_W]
        for h in range(1, N_HEADS):
            out = jnp.where(in_head[h], o4[h * GRID_W:(h + 1) * GRID_W], out)
        o_ref[0, row_slice(i), :] = out.astype(o_ref.dtype)

    pb_ref[...] = jnp.zeros(pb_ref.shape, BF16)
    lb_ref[...] = jnp.zeros(lb_ref.shape, F32)
    scores(0, sa_ref)

    def pair_body(tt, carry):
        t = 2 * tt
        values(jnp.maximum(t - 1, 0), pb_ref, lb_ref)
        scores(t + 1, sb_ref)
        softmax(sa_ref, pa_ref, la_ref)
        values(t, pa_ref, la_ref)
        scores(jnp.minimum(t + 2, rows_per_step - 1), sa_ref)
        softmax(sb_ref, pb_ref, lb_ref)
        return carry

    lax.fori_loop(0, rows_per_step // 2, pair_body, 0)
    values(rows_per_step - 1, pb_ref, lb_ref)


def _na_attention(q, k, v, ky, vy, table, rows_per_step):
    b, s, _ = q.shape
    l = ky.shape[1]
    n_rows = s // GRID_W
    tq = rows_per_step * GRID_W
    n_keys = NA_KR * GRID_W + l
    seq = lambda n: pl.BlockSpec((1, n, GROUP_W), lambda bi, i: (bi, 0, 0))
    return pl.pallas_call(
        functools.partial(_na_kernel, rows_per_step=rows_per_step, n_rows=n_rows),
        grid=(b, n_rows // rows_per_step),
        in_specs=[pl.BlockSpec((1, tq, GROUP_W), lambda bi, i: (bi, i, 0)),
                  seq(s), seq(s), seq(l), seq(l), _const_spec(table.shape)],
        out_specs=pl.BlockSpec((1, tq, GROUP_W), lambda bi, i: (bi, i, 0)),
        out_shape=jax.ShapeDtypeStruct((b, s, GROUP_W), BF16),
        scratch_shapes=([pltpu.VMEM((N_HEADS * GRID_W, n_keys), F32)] * 2
                        + [pltpu.VMEM((N_HEADS * GRID_W, n_keys), BF16)] * 2
                        + [pltpu.VMEM((N_HEADS * GRID_W, 1), F32)] * 2),
        compiler_params=_cparams(("parallel", "arbitrary")),
        name="na_attention",
    )(q, k, v, ky, vy, table)


def _swa_kernel(sink_ref, q_ref, kp_ref, kc_ref, kn_ref, vp_ref, vc_ref, vn_ref, ky_ref, vy_ref, o_ref,
                kw_ref, vw_ref, wb_ref, sa_ref, sb_ref, pa_ref, pb_ref, la_ref, lb_ref, *, blocks_per_step):
    step = pl.program_id(1)
    nb = pl.num_programs(1) * blocks_per_step
    bl = SWA_BLOCK
    g = N_HEADS // SWA_KV_HEADS
    tq = blocks_per_step * bl
    kw_ref[0:bl] = kp_ref[0]
    kw_ref[bl:bl + tq] = kc_ref[0]
    kw_ref[bl + tq:] = kn_ref[0]
    vw_ref[0:bl] = vp_ref[0]
    vw_ref[bl:bl + tq] = vc_ref[0]
    vw_ref[bl + tq:] = vn_ref[0]

    qi = lax.broadcasted_iota(jnp.int32, (g * bl, 3 * bl), 0) % bl
    jk = lax.broadcasted_iota(jnp.int32, (g * bl, 3 * bl), 1)
    wb_ref[...] = jnp.where(jnp.abs(jk - bl - qi) <= SWA_WINDOW, 0.0, NEG_INF)
    half = lax.broadcasted_iota(jnp.int32, (g * bl, 1), 0) // bl

    head_of_lane = lax.broadcasted_iota(jnp.int32, (bl, GROUP_W), 1) // HEAD_DIM
    in_head = [head_of_lane == h for h in range(N_HEADS)]
    head_mask = [jnp.where(mk, 1.0, 0.0).astype(BF16) for mk in in_head]

    def blk_slice(j):
        return pl.ds(pl.multiple_of(j * bl, bl), bl)

    def scores(j, s_ref):
        qoff = pl.multiple_of(j * bl, bl)
        q_all = q_ref[0, blk_slice(j), :]
        kw = kw_ref[pl.ds(qoff, 3 * bl), :]
        n = step * blocks_per_step + j
        lo_edge = jnp.where(n == 0, NEG_INF, 0.0)
        hi_edge = jnp.where(n == nb - 1, NEG_INF, 0.0)
        for kh in range(SWA_KV_HEADS):
            q = jnp.concatenate([q_all * head_mask[kh * g + gi] for gi in range(g)], axis=0)
            s = _dot_nt(q, kw) + wb_ref[...]
            s_ref[kh, :, :bl] = s[:, :bl] + lo_edge
            s_ref[kh, :, bl:2 * bl] = s[:, bl:2 * bl]
            s_ref[kh, :, 2 * bl:3 * bl] = s[:, 2 * bl:] + hi_edge
            s_ref[kh, :, 3 * bl:] = _dot_nt(q, ky_ref[0])

    def softmax(s_ref, p_ref, l_ref):
        for kh in range(SWA_KV_HEADS):
            s = s_ref[kh]
            sink = jnp.full((g * bl, 1), sink_ref[kh * g] * LOG2_E, F32)
            for gi in range(1, g):
                sink = jnp.where(half == gi, sink_ref[kh * g + gi] * LOG2_E, sink)
            m = jnp.maximum(_row_max(s), sink)
            p = jnp.exp2(s - m)
            l_ref[kh] = 1.0 / (_row_sum(p) + jnp.exp2(sink - m))
            p_ref[kh] = p.astype(BF16)

    def values(j, p_ref, l_ref):
        vw = vw_ref[pl.ds(pl.multiple_of(j * bl, bl), 3 * bl), :]
        out = jnp.zeros((bl, GROUP_W), F32)
        for kh in range(SWA_KV_HEADS):
            o = (_dot(p_ref[kh, :, :3 * bl], vw) + _dot(p_ref[kh, :, 3 * bl:], vy_ref[0])) * l_ref[kh]
            for gi in range(g):
                out = jnp.where(in_head[kh * g + gi], o[gi * bl:(gi + 1) * bl], out)
        o_ref[0, blk_slice(j), :] = out.astype(o_ref.dtype)

    pb_ref[...] = jnp.zeros(pb_ref.shape, BF16)
    lb_ref[...] = jnp.zeros(lb_ref.shape, F32)
    scores(0, sa_ref)

    def pair_body(tt, carry):
        t = 2 * tt
        values(jnp.maximum(t - 1, 0), pb_ref, lb_ref)
        scores(t + 1, sb_ref)
        softmax(sa_ref, pa_ref, la_ref)
        values(t, pa_ref, la_ref)
        scores(jnp.minimum(t + 2, blocks_per_step - 1), sa_ref)
        softmax(sb_ref, pb_ref, lb_ref)
        return carry

    lax.fori_loop(0, blocks_per_step // 2, pair_body, 0)
    values(blocks_per_step - 1, pb_ref, lb_ref)


def _swa_attention(sink, q, k, v, ky, vy, blocks_per_step):
    b, s, _ = q.shape
    l = ky.shape[1]
    tq = blocks_per_step * SWA_BLOCK
    n_steps = s // tq
    nb = s // SWA_BLOCK
    kvw = GROUP_W
    q_rows = N_HEADS // SWA_KV_HEADS * SWA_BLOCK
    n_keys = 3 * SWA_BLOCK + l
    prev = pl.BlockSpec((1, SWA_BLOCK, kvw), lambda bi, i: (bi, jnp.maximum(i * blocks_per_step - 1, 0), 0))
    cur = pl.BlockSpec((1, tq, kvw), lambda bi, i: (bi, i, 0))
    nxt = pl.BlockSpec((1, SWA_BLOCK, kvw), lambda bi, i: (bi, jnp.minimum((i + 1) * blocks_per_step, nb - 1), 0))
    ctx = pl.BlockSpec((1, l, kvw), lambda bi, i: (bi, 0, 0))
    return pl.pallas_call(
        functools.partial(_swa_kernel, blocks_per_step=blocks_per_step),
        grid=(b, n_steps),
        in_specs=[pl.BlockSpec(memory_space=pltpu.SMEM),
                  pl.BlockSpec((1, tq, GROUP_W), lambda bi, i: (bi, i, 0)),
                  prev, cur, nxt, prev, cur, nxt, ctx, ctx],
        out_specs=pl.BlockSpec((1, tq, GROUP_W), lambda bi, i: (bi, i, 0)),
        out_shape=jax.ShapeDtypeStruct((b, s, GROUP_W), BF16),
        scratch_shapes=[pltpu.VMEM((tq + 2 * SWA_BLOCK, kvw), BF16),
                        pltpu.VMEM((tq + 2 * SWA_BLOCK, kvw), BF16),
                        pltpu.VMEM((q_rows, 3 * SWA_BLOCK), F32)]
                       + [pltpu.VMEM((SWA_KV_HEADS, q_rows, n_keys), F32)] * 2
                       + [pltpu.VMEM((SWA_KV_HEADS, q_rows, n_keys), BF16)] * 2
                       + [pltpu.VMEM((SWA_KV_HEADS, q_rows, 1), F32)] * 2,
        compiler_params=_cparams(("parallel", "arbitrary")),
        name="swa_attention",
    )(sink, q, k, k, k, v, v, v, ky, vy)


def _outffn_kernel(*refs, hidden_chunk, final):
    if final:
        (x_ref, m0_ref, m1_ref, m2_ref, m3_ref, wo_ref, g1_ref, n2_ref, sc_ref, sh_ref, g2_ref,
         w1_ref, w3_ref, w2_ref, fg_ref, o_ref) = refs
    else:
        (x_ref, m0_ref, m1_ref, m2_ref, m3_ref, wo_ref, g1_ref, n2_ref, sc_ref, sh_ref, g2_ref,
         w1_ref, w3_ref, w2_ref, o_ref) = refs
    half_rows = x_ref.shape[1] // 2
    mod = n2_ref[...] * (1.0 + sc_ref[0])
    x1_halves, hb_halves = [], []
    for r0 in (0, half_rows):
        rows = slice(r0, r0 + half_rows)
        mix = None
        for gi, m_ref in enumerate((m0_ref, m1_ref, m2_ref, m3_ref)):
            part = _dot(m_ref[0, rows, :], wo_ref[gi * GROUP_W:(gi + 1) * GROUP_W, :])
            mix = part if mix is None else mix + part
        x1_h = x_ref[0, rows, :] + g1_ref[0] * mix
        x1_halves.append(x1_h)
        hb_halves.append((_rms(x1_h) * mod + sh_ref[0]).astype(BF16))
    x1 = jnp.concatenate(x1_halves, axis=0)
    hb = jnp.concatenate(hb_halves, axis=0)
    hidden = w1_ref.shape[1]
    acc = None
    for c0 in range(0, hidden, hidden_chunk):
        a = _dot(hb, w1_ref[:, c0:c0 + hidden_chunk])
        bgate = _dot(hb, w3_ref[:, c0:c0 + hidden_chunk])
        u = (_silu(a) * bgate).astype(BF16)
        part = _dot(u, w2_ref[c0:c0 + hidden_chunk, :])
        acc = part if acc is None else acc + part
    x2 = x1 + g2_ref[0] * acc
    if final:
        x2 = _rms(x2) * fg_ref[...]
    o_ref[0] = x2


def _outffn(x, mixes, wo, g1, n2, sc2, sh2, g2, w1, w3, w2, final_g, tm):
    b, t, d = x.shape
    tok = lambda wd: pl.BlockSpec((1, tm, wd), lambda bi, i: (bi, i, 0))
    vec = pl.BlockSpec((1, 1, d), lambda bi, i: (bi, 0, 0))
    in_specs = ([tok(d)] + [tok(GROUP_W)] * 4
                + [_const_spec(wo.shape), vec, _const_spec((1, d)), vec, vec, vec,
                   _const_spec(w1.shape), _const_spec(w3.shape), _const_spec(w2.shape)])
    args = [x, *mixes, wo, g1, n2, sc2, sh2, g2, w1, w3, w2]
    final = final_g is not None
    if final:
        in_specs.append(_const_spec((1, d)))
        args.append(final_g)
    return pl.pallas_call(
        functools.partial(_outffn_kernel, hidden_chunk=256, final=final),
        grid=(b, t // tm),
        in_specs=in_specs,
        out_specs=tok(d),
        out_shape=jax.ShapeDtypeStruct((b, t, d), F32),
        compiler_params=_cparams(("parallel", "parallel")),
        name="out_proj_ffn_final" if final else "out_proj_ffn",
    )(*args)


def _prep_weights(w_in, mla_w_uq, mla_w_ukv):
    depth, d, _ = w_in.shape
    offs = [0]
    for sz in IN_SIZES:
        offs.append(offs[-1] + sz)
    w_bf = w_in.astype(BF16)
    cols = [w_bf[:, :, offs[i]:offs[i + 1]] for i in range(len(IN_SIZES))]
    cq, ckv, kr, rq, rk, rv, rgf, rgb, nq, nk, nv, sq, sk, sv = cols
    scale = HEAD_DIM ** -0.5
    kr_slot = jnp.concatenate([kr, jnp.zeros((depth, d, LANES - MLA_ROPE), BF16)], axis=-1)
    src = jnp.arange(LANES)[:, None]
    dst = jnp.arange(N_HEADS * MLA_HEAD_PAD)[None, :]
    place = ((src < MLA_ROPE) & (dst % MLA_HEAD_PAD == src + MLA_NOPE)).astype(BF16)

    def per_query_head(t):
        g = N_HEADS // SWA_KV_HEADS
        t = t.reshape(depth, d, SWA_KV_HEADS, 1, HEAD_DIM)
        return jnp.broadcast_to(t, (depth, d, SWA_KV_HEADS, g, HEAD_DIM)).reshape(depth, d, GROUP_W)

    w = jnp.concatenate([cq, ckv, kr_slot, rq, rk * scale, rv, rgf, rgb, nq * scale, nk, nv,
                         sq * scale, per_query_head(sk), per_query_head(sv)], axis=-1).astype(BF16)

    qr = mla_w_uq.shape[1]
    uq = mla_w_uq.reshape(depth, qr, N_HEADS, MLA_NOPE + MLA_ROPE)
    wuq = jnp.concatenate([uq, jnp.zeros((depth, qr, N_HEADS, MLA_HEAD_PAD - MLA_NOPE - MLA_ROPE), F32)],
                          axis=-1).reshape(depth, qr, N_HEADS * MLA_HEAD_PAD).astype(BF16)
    kvr = mla_w_ukv.shape[1]
    ukv = mla_w_ukv.reshape(depth, kvr, N_HEADS, MLA_NOPE + MLA_V)
    zk = jnp.zeros((depth, kvr, N_HEADS, MLA_HEAD_PAD - MLA_NOPE), F32)
    wuk = jnp.concatenate([ukv[..., :MLA_NOPE], zk], axis=-1).reshape(depth, kvr, -1).astype(BF16)
    zv = jnp.zeros((depth, kvr, N_HEADS, MLA_HEAD_PAD - MLA_V), F32)
    wuv = jnp.concatenate([ukv[..., MLA_NOPE:], zv], axis=-1).reshape(depth, kvr, -1).astype(BF16)
    return w, wuq, wuk, wuv, place


def _ctx_head_specs():
    mla = tuple((h * MLA_HEAD_PAD, (h + 1) * MLA_HEAD_PAD, h * MLA_HEAD_PAD, (h + 1) * MLA_HEAD_PAD,
                 h * MLA_HEAD_PAD, h * MLA_HEAD_PAD + MLA_V, None) for h in range(N_HEADS))
    na = tuple((h * HEAD_DIM, (h + 1) * HEAD_DIM) * 3 + (None,) for h in range(N_HEADS))
    swa = tuple((h * HEAD_DIM, (h + 1) * HEAD_DIM) * 3 + (h,) for h in range(N_HEADS))
    return mla, na, swa


def kernel(x, c, ctx, c_ctx, ada_w, ada_b, norm1_g, w_in, mla_q_norm, mla_w_uq, mla_kv_norm, mla_w_ukv,
           ret_decay, na_rpb, swa_sink, w_out, norm2_g, ffn_w1, ffn_w3, ffn_w2, final_norm_g):
    b, s, d = x.shape
    l_ctx = ctx.shape[1]
    depth = ada_w.shape[0]
    assert b + 1 <= 8 and s % 2048 == 0 and l_ctx % 128 == 0

    cond = jnp.concatenate([c, c_ctx[None, :], jnp.zeros((8 - b - 1, d), F32)], axis=0)
    mod = _modulation(cond, ada_w, ada_b)
    tables = _rope_tables(s)
    w_all, wuq_all, wuk_all, wuv_all, place = _prep_weights(w_in, mla_w_uq, mla_w_ukv)
    wo_all = w_out.astype(BF16)
    w1_all, w3_all, w2_all = ffn_w1.astype(BF16), ffn_w3.astype(BF16), ffn_w2.astype(BF16)
    mla_heads, na_heads, swa_heads = _ctx_head_specs()
    no_sink = jnp.zeros((N_HEADS,), F32)

    tm_x = 512
    tm_y = min(256, l_ctx)
    y = ctx
    for l in range(depth):
        mx = [mod[l, :b, j * d:(j + 1) * d][:, None, :] for j in range(6)]
        my = [jnp.broadcast_to(mod[l, b, j * d:(j + 1) * d][None, None, :], (b, 1, d)) for j in range(6)]
        n1 = norm1_g[l][None, :]
        n2 = norm2_g[l][None, :]
        qn = mla_q_norm[l][None, :]
        kvn = mla_kv_norm[l][None, :]
        lw = (w_all[l], qn, kvn, wuq_all[l], wuk_all[l], wuv_all[l], place)

        px = _inproj(x, n1, mx[1], mx[0], *lw, tables, tm_x)
        py = _inproj(y, n1, my[1], my[0], *lw, None, tm_y)
        (xmq, xmk, xmv, xrq, xrk, xrv, xrg, xnq, xnk, xnv, xsq, xsk, xsv) = px
        (ymq, ymk, ymv, yrq, yrk, yrv, yrg, ynq, ynk, ynv, ysq, ysk, ysv) = py

        mla_x = _mla_attention(xmq, xmk, xmv, ymk, ymv, tq=1024, tk=512, n_tiles=2, n_sub=2, unroll=31)
        dec = ret_decay[l].reshape(-1)
        ret_x, ret_y = _retention(dec, xrq, xrk, xrv, xrg, yrq, yrk, yrv, yrg, chunk=128, chunks_per_step=16)
        table = _na_bias_table(na_rpb[l])
        na_x = _na_attention(xnq, xnk, xnv, ynk, ynv, table, rows_per_step=32)
        swa_x = _swa_attention(swa_sink[l], xsq, xsk, xsv, ysk, ysv, blocks_per_step=16)

        last = l == depth - 1
        x = _outffn(x, (mla_x, ret_x, na_x, swa_x), wo_all[l], mx[2], n2, mx[4], mx[3], mx[5],
                    w1_all[l], w3_all[l], w2_all[l], final_norm_g[None, :] if last else None, tm_x)
        if not last:
            mla_y = _ctx_attention(ymq, ymk, ymv, mla_heads, no_sink, "mla_ctx_attention")
            na_y = _ctx_attention(ynq, ynk, ynv, na_heads, no_sink, "na_ctx_attention")
            swa_y = _ctx_attention(ysq, ysk, ysv, swa_heads, swa_sink[l], "swa_ctx_attention")
            y = _outffn(y, (mla_y, ret_y, na_y, swa_y), wo_all[l], my[2], n2, my[4], my[3], my[5],
                        w1_all[l], w3_all[l], w2_all[l], None, tm_y)
    return x
```

```python
import functools

import jax
import jax.numpy as jnp
from jax import lax
from jax.experimental import pallas as pl
from jax.experimental.pallas import tpu as pltpu

F32 = jnp.float32
BF16 = jnp.bfloat16

GRID_W = 64
HEAD_DIM = 64
N_HEADS = 4
GROUP_W = N_HEADS * HEAD_DIM
MLA_Q_RANK = 256
MLA_KV_RANK = 128
MLA_NOPE = 64
MLA_ROPE = 32
MLA_V = 64
MLA_HEAD_PAD = 128
MLA_V_ROWS = MLA_V + 16
NA_KR = 8
NA_KC = 16
SWA_KV_HEADS = 2
SWA_WINDOW = 128
SWA_BLOCK = 128
ROPE_THETA = 10000.0
EPS = 1e-6
NEG_INF = -1e30
LOG2_E = 1.4426950408889634
LANES = 128
VMEM_LIMIT = 56 * 1024 * 1024

IN_SIZES = (MLA_Q_RANK, MLA_KV_RANK, MLA_ROPE,
            GROUP_W, GROUP_W, GROUP_W, GROUP_W, GROUP_W,
            GROUP_W, GROUP_W, GROUP_W,
            GROUP_W, SWA_KV_HEADS * HEAD_DIM, SWA_KV_HEADS * HEAD_DIM)

_O_CQ = 0
_O_CKV = _O_CQ + MLA_Q_RANK
_O_KR = _O_CKV + MLA_KV_RANK
_O_RQK = _O_KR + LANES
_O_RV = _O_RQK + 2 * GROUP_W
_O_RG = _O_RV + GROUP_W
_O_NA = _O_RG + 2 * GROUP_W
_O_SQK = _O_NA + 3 * GROUP_W
_O_SV = _O_SQK + 2 * GROUP_W
_IN_COLS = _O_SV + GROUP_W


def _cparams(sem):
    return pltpu.CompilerParams(dimension_semantics=sem, vmem_limit_bytes=VMEM_LIMIT)


def _dot(a, b):
    return jnp.dot(a, b, preferred_element_type=F32)


def _dot_nt(a, b):
    return lax.dot_general(a, b, (((1,), (1,)), ((), ())), preferred_element_type=F32)


def _rms(x):
    return x * lax.rsqrt(jnp.mean(x * x, axis=-1, keepdims=True) + EPS)


def _silu(x):
    return x * jax.nn.sigmoid(x)


def _lane_chunks(arrays):
    return [a[:, j * LANES:(j + 1) * LANES] for a in arrays for j in range(a.shape[-1] // LANES)]


def _row_max(*arrays):
    return jnp.max(functools.reduce(jnp.maximum, _lane_chunks(arrays)), axis=-1, keepdims=True)


def _row_sum(*arrays):
    return jnp.sum(functools.reduce(jnp.add, _lane_chunks(arrays)), axis=-1, keepdims=True)


def _mod_kernel(c_ref, w_ref, b_ref, o_ref):
    o_ref[0] = _dot(_silu(c_ref[...]), w_ref[0]) + b_ref[0]


def _modulation(cond, ada_w, ada_b):
    depth, d, d6 = ada_w.shape
    n = d6 // d
    return pl.pallas_call(
        _mod_kernel,
        grid=(depth, n),
        in_specs=[pl.BlockSpec((8, d), lambda l, j: (0, 0)),
                  pl.BlockSpec((1, d, d), lambda l, j: (l, 0, j)),
                  pl.BlockSpec((1, 1, d), lambda l, j: (l, 0, j))],
        out_specs=pl.BlockSpec((1, 8, d), lambda l, j: (l, 0, j)),
        out_shape=jax.ShapeDtypeStruct((depth, 8, d6), F32),
        compiler_params=_cparams(("parallel", "parallel")),
        name="ada_modulation",
    )(cond, ada_w, ada_b.reshape(depth, 1, d6))


def _rope_tables(seq):
    n_rows = seq // GRID_W

    def parts(pos, d):
        inv = ROPE_THETA ** (-jnp.arange(0, d, 2, dtype=F32) / d)
        ang = pos.astype(F32)[:, None] * inv[None, :]
        z = jnp.zeros_like(ang)
        return (jnp.concatenate([jnp.cos(ang), jnp.cos(ang)], axis=-1),
                jnp.concatenate([z, jnp.sin(ang)], axis=-1),
                jnp.concatenate([-jnp.sin(ang), z], axis=-1))

    def expand(by_row, by_col):
        w = by_row.shape[-1]
        r = jnp.broadcast_to(by_row[:, None, :], (n_rows, GRID_W, w))
        c = jnp.broadcast_to(by_col[None, :, :], (n_rows, GRID_W, w))
        return jnp.concatenate([r, c], axis=-1).reshape(seq, 2 * w)

    def tables(d, fill):
        per_head = [expand(a, b) for a, b in zip(parts(jnp.arange(n_rows), d), parts(jnp.arange(GRID_W), d))]
        return [fill(t, i) for i, t in enumerate(per_head)]

    def two_heads(t, _):
        return jnp.concatenate([t, t], axis=-1)

    def mla_slot(t, i):
        lead = (jnp.ones if i == 0 else jnp.zeros)((seq, MLA_NOPE), F32)
        tail = (jnp.ones if i == 0 else jnp.zeros)((seq, MLA_HEAD_PAD - MLA_NOPE - MLA_ROPE), F32)
        return jnp.concatenate([lead, t, tail], axis=-1)

    def kr_slot(t, i):
        tail = (jnp.ones if i == 0 else jnp.zeros)((seq, LANES - MLA_ROPE), F32)
        return jnp.concatenate([t, tail], axis=-1)

    return tuple(tables(HEAD_DIM // 2, two_heads) + tables(MLA_ROPE // 2, mla_slot)
                 + tables(MLA_ROPE // 2, kr_slot))


def _rope(x, cos, s_prev, s_next, d):
    out = []
    for j in range(x.shape[-1] // LANES):
        xc = x[:, j * LANES:(j + 1) * LANES]
        out.append(xc * cos + pltpu.roll(xc, d, 1) * s_prev + pltpu.roll(xc, LANES - d, 1) * s_next)
    return out[0] if len(out) == 1 else jnp.concatenate(out, axis=-1)


def _inproj_kernel(*refs, rotate, mla_scale):
    if rotate:
        (x_ref, g_ref, sc_ref, sh_ref, w_ref, qn_ref, kvn_ref, wuq_ref, wuk_ref, wuv_ref, place_ref,
         c64_ref, p64_ref, n64_ref, cm_ref, pm_ref, nm_ref, ckr_ref, pkr_ref, nkr_ref,
         mq_ref, mk_ref, mv_ref, rq_ref, rk_ref, rv_ref, rg_ref,
         nq_ref, nk_ref, nv_ref, sq_ref, sk_ref, sv_ref) = refs
    else:
        (x_ref, g_ref, sc_ref, sh_ref, w_ref, qn_ref, kvn_ref, wuq_ref, wuk_ref, wuv_ref, place_ref,
         mq_ref, mk_ref, mv_ref, rq_ref, rk_ref, rv_ref, rg_ref,
         nq_ref, nk_ref, nv_ref, sq_ref, sk_ref, sv_ref) = refs

    x = x_ref[0]
    h = _rms(x) * (g_ref[...] * (1.0 + sc_ref[0])) + sh_ref[0]
    hb = h.astype(BF16)

    def proj(lo, hi):
        return _dot(hb, w_ref[:, lo:hi])

    def rope64(v):
        if not rotate:
            return v
        return _rope(v, c64_ref[...], p64_ref[...], n64_ref[...], HEAD_DIM // 4)

    def rope_mla(v):
        if not rotate:
            return v
        return _rope(v, cm_ref[...], pm_ref[...], nm_ref[...], MLA_ROPE // 4)

    cq = (_rms(proj(_O_CQ, _O_CKV)) * qn_ref[...]).astype(BF16)
    ckv_kr = proj(_O_CKV, _O_RQK)
    ckv = (_rms(ckv_kr[:, :MLA_KV_RANK]) * kvn_ref[...]).astype(BF16)
    kr = ckv_kr[:, MLA_KV_RANK:]
    if rotate:
        kr = _rope(kr, ckr_ref[...], pkr_ref[...], nkr_ref[...], MLA_ROPE // 4)
    kr = kr.astype(BF16)

    rqk = proj(_O_RQK, _O_RV)
    rq_ref[0] = rope64(rqk[:, :GROUP_W]).astype(BF16)
    rk_ref[0] = rope64(rqk[:, GROUP_W:]).astype(BF16)
    rv_ref[0] = proj(_O_RV, _O_RG).astype(BF16)

    na = proj(_O_NA, _O_SQK)
    nq_ref[0] = (na[:, :GROUP_W] * LOG2_E).astype(BF16)
    nk_ref[0] = na[:, GROUP_W:2 * GROUP_W].astype(BF16)
    nv_ref[0] = na[:, 2 * GROUP_W:].astype(BF16)

    sqk = proj(_O_SQK, _O_SV)
    sq_ref[0] = (rope64(sqk[:, :GROUP_W]) * LOG2_E).astype(BF16)
    sk_ref[0] = rope64(sqk[:, GROUP_W:]).astype(BF16)
    sv_ref[0] = proj(_O_SV, _IN_COLS).astype(BF16)

    q = rope_mla(_dot(cq, wuq_ref[...])) * mla_scale
    mq_ref[0] = q.astype(BF16)
    k = _dot(ckv, wuk_ref[...]) + _dot(kr, place_ref[...])
    mk_ref[0] = k.astype(BF16)
    v = _dot(ckv, wuv_ref[...])
    lane = lax.broadcasted_iota(jnp.int32, v.shape, 1)
    mv_ref[0] = jnp.where(lane % MLA_HEAD_PAD >= MLA_V, 1.0, v).astype(BF16)

    rg_ref[0] = proj(_O_RG, _O_NA)


def _const_spec(shape):
    nd = len(shape)
    return pl.BlockSpec(shape, lambda *_: (0,) * nd, pipeline_mode=pl.Buffered(1))


def _inproj(x, gain, scale, shift, w, qn, kvn, wuq, wuk, wuv, place, tables, tm):
    b, t, d = x.shape
    rotate = tables is not None
    kv_w = GROUP_W
    mla_w = N_HEADS * MLA_HEAD_PAD
    tok = lambda wd: pl.BlockSpec((1, tm, wd), lambda bi, i: (bi, i, 0))
    vec = pl.BlockSpec((1, 1, d), lambda bi, i: (bi, 0, 0))
    in_specs = [tok(d), _const_spec((1, d)), vec, vec, _const_spec(w.shape),
                _const_spec(qn.shape), _const_spec(kvn.shape), _const_spec(wuq.shape),
                _const_spec(wuk.shape), _const_spec(wuv.shape), _const_spec(place.shape)]
    args = [x, gain, scale, shift, w, qn, kvn, wuq, wuk, wuv, place]
    if rotate:
        in_specs += [pl.BlockSpec((tm, LANES), lambda bi, i: (i, 0))] * len(tables)
        args += list(tables)
    widths = [mla_w, mla_w, mla_w, GROUP_W, GROUP_W, GROUP_W, 2 * GROUP_W,
              GROUP_W, GROUP_W, GROUP_W, GROUP_W, kv_w, kv_w]
    dtypes = [BF16] * 6 + [F32] + [BF16] * 6
    return pl.pallas_call(
        functools.partial(_inproj_kernel, rotate=rotate, mla_scale=(MLA_NOPE + MLA_ROPE) ** -0.5 * LOG2_E),
        grid=(b, t // tm),
        in_specs=in_specs,
        out_specs=[tok(wd) for wd in widths],
        out_shape=[jax.ShapeDtypeStruct((b, t, wd), dt) for wd, dt in zip(widths, dtypes)],
        compiler_params=_cparams(("parallel", "parallel")),
        name="in_proj_rot" if rotate else "in_proj_ctx",
    )(*args)


def _mla_kernel(q_ref, kx_ref, vx_ref, ky_ref, vy_ref, o_ref, m_ref, acc_ref, sa_ref, sb_ref, sc_ref,
                ma_ref, mb_ref, mc_ref, *, tk, heads, n_tiles, n_sub, unroll):
    n_chunks = kx_ref.shape[1] // tk
    assert n_chunks % 2 == 0
    tq = q_ref.shape[1] // n_tiles
    ts = tq // n_sub
    chains = [(h, u) for h in range(heads) for u in range(n_sub)]

    def reset():
        for c in range(len(chains)):
            m_ref[c] = jnp.full(m_ref.shape[1:], NEG_INF, F32)
            acc_ref[c] = jnp.zeros(acc_ref.shape[1:], F32)

    def rows(h):
        return slice(h * MLA_HEAD_PAD, (h + 1) * MLA_HEAD_PAD)

    def vrows(h):
        return slice(h * MLA_HEAD_PAD, h * MLA_HEAD_PAD + MLA_V_ROWS)

    buf_a, buf_b, buf_c = (sa_ref, ma_ref), (sb_ref, mb_ref), (sc_ref, mc_ref)

    def scores(t, c, k, dst):
        h, u = chains[c]
        q0 = t * tq + u * ts
        st = _dot_nt(k, q_ref[0, q0:q0 + ts, rows(h)])
        dst[0][c] = st
        dst[1][c] = jnp.max(st, axis=0, keepdims=True)

    def absorb(c, src, vt):
        m_old = m_ref[c]
        m_new = jnp.maximum(m_old, src[1][c])
        pt = jnp.exp2(src[0][c] - m_new).astype(BF16)
        acc_ref[c, :MLA_V_ROWS] = acc_ref[c, :MLA_V_ROWS] * jnp.exp2(m_old - m_new) + _dot(vt, pt)
        m_ref[c] = m_new

    def kx(j, h):
        return kx_ref[0, pl.ds(pl.multiple_of(j * tk, tk), tk), rows(h)]

    def stage_at(t, j, parity):
        cur, nxt = (buf_a, buf_b) if parity == 0 else (buf_b, buf_a)
        for c, (h, _) in enumerate(chains):
            scores(t, c, kx(j + 1, h), nxt)
            absorb(c, cur, vx_ref[0, j, vrows(h), :])

    trips = (n_chunks - 1) // unroll
    reset()
    for c, (h, _) in enumerate(chains):
        scores(0, c, kx(0, h), buf_a)
    for t in range(n_tiles):
        def body(jj, carry):
            for i in range(unroll):
                stage_at(t, unroll * jj + i, i % 2)
            return carry

        lax.fori_loop(0, trips, body, 0)
        for j in range(unroll * trips, n_chunks - 1):
            stage_at(t, j, j % 2)
        for c, (h, _) in enumerate(chains):
            scores(t, c, ky_ref[0, :, rows(h)], buf_c)
            absorb(c, buf_b, vx_ref[0, n_chunks - 1, vrows(h), :])
        outs = [[None] * n_sub for _ in range(heads)]
        for c, (h, u) in enumerate(chains):
            if t + 1 < n_tiles:
                scores(t + 1, c, kx(0, h), buf_a)
            absorb(c, buf_c, vy_ref[0, 0, vrows(h), :])
            acc = acc_ref[c].T
            outs[h][u] = acc[:, :MLA_V] / acc[:, MLA_V:MLA_V + 1]
        o_ref[0, t * tq:(t + 1) * tq, :] = jnp.concatenate(
            [jnp.concatenate(outs[h], axis=0) for h in range(heads)], axis=-1).astype(o_ref.dtype)
        if t + 1 < n_tiles:
            reset()


def _mla_attention(q, kx, vx, ky, vy, tq, tk, n_tiles, n_sub, unroll):
    b, s, _ = q.shape
    l = ky.shape[1]
    hp = 2
    wd = hp * MLA_HEAD_PAD
    vxt = jnp.swapaxes(vx.reshape(b, s // tk, tk, -1), 2, 3)
    vyt = jnp.swapaxes(vy.reshape(b, 1, l, -1), 2, 3)
    ts = tq // n_tiles // n_sub
    return pl.pallas_call(
        functools.partial(_mla_kernel, tk=tk, heads=hp, n_tiles=n_tiles, n_sub=n_sub, unroll=unroll),
        grid=(b, N_HEADS // hp, s // tq),
        in_specs=[pl.BlockSpec((1, tq, wd), lambda bi, hi, i: (bi, i, hi)),
                  pl.BlockSpec((1, s, wd), lambda bi, hi, i: (bi, 0, hi)),
                  pl.BlockSpec((1, s // tk, wd, tk), lambda bi, hi, i: (bi, 0, hi, 0)),
                  pl.BlockSpec((1, l, wd), lambda bi, hi, i: (bi, 0, hi)),
                  pl.BlockSpec((1, 1, wd, l), lambda bi, hi, i: (bi, 0, hi, 0))],
        out_specs=pl.BlockSpec((1, tq, hp * MLA_V), lambda bi, hi, i: (bi, i, hi)),
        out_shape=jax.ShapeDtypeStruct((b, s, N_HEADS * MLA_V), BF16),
        scratch_shapes=[pltpu.VMEM((hp * n_sub, 1, ts), F32),
                        pltpu.VMEM((hp * n_sub, MLA_HEAD_PAD, ts), F32),
                        pltpu.VMEM((hp * n_sub, tk, ts), F32),
                        pltpu.VMEM((hp * n_sub, tk, ts), F32),
                        pltpu.VMEM((hp * n_sub, l, ts), F32)]
                       + [pltpu.VMEM((hp * n_sub, 1, ts), F32)] * 3,
        compiler_params=_cparams(("parallel", "parallel", "arbitrary")),
        name="mla_attention",
    )(q, kx, vxt, ky, vyt)


def _ctx_attn_kernel(sink_ref, q_ref, k_ref, v_ref, o_ref, *, heads):
    outs = []
    for (q0, q1, k0, k1, v0, v1, sink_idx) in heads:
        q = q_ref[0, :, q0:q1]
        s = _dot_nt(q, k_ref[0, :, k0:k1])
        m = jnp.max(s, axis=-1, keepdims=True)
        if sink_idx is not None:
            sink = jnp.full((1, 1), sink_ref[sink_idx] * LOG2_E, F32)
            m = jnp.maximum(m, sink)
        p = jnp.exp2(s - m)
        l = jnp.sum(p, axis=-1, keepdims=True)
        if sink_idx is not None:
            l = l + jnp.exp2(sink - m)
        outs.append(_dot(p.astype(BF16), v_ref[0, :, v0:v1]) / l)
    o_ref[0] = jnp.concatenate(outs, axis=-1).astype(o_ref.dtype)


def _ctx_attention(q, k, v, heads, sink, name):
    b, l, _ = q.shape
    full = lambda a: pl.BlockSpec((1, l, a.shape[-1]), lambda bi: (bi, 0, 0))
    return pl.pallas_call(
        functools.partial(_ctx_attn_kernel, heads=heads),
        grid=(b,),
        in_specs=[pl.BlockSpec(memory_space=pltpu.SMEM), full(q), full(k), full(v)],
        out_specs=pl.BlockSpec((1, l, GROUP_W), lambda bi: (bi, 0, 0)),
        out_shape=jax.ShapeDtypeStruct((b, l, GROUP_W), BF16),
        compiler_params=_cparams(("parallel",)),
        name=name,
    )(sink, q, k, v)


def _ret_kernel(*refs, chunk, n_chunks, direction, has_prev):
    if has_prev:
        (dec_ref, q_ref, k_ref, kt_ref, v_ref, g_ref, s0_ref, prev_ref, o_ref, sn_ref,
         st_ref, dm_ref, qd_ref, kdt_ref, cd_ref, ob_ref, qk_ref) = refs
    else:
        (dec_ref, q_ref, k_ref, kt_ref, v_ref, g_ref, s0_ref, o_ref, sn_ref,
         st_ref, dm_ref, qd_ref, kdt_ref, cd_ref, ob_ref, qk_ref) = refs
        prev_ref = None
    i = pl.program_id(0)
    c = chunk
    fwd = direction == 0
    batch = range(q_ref.shape[0])

    def head_of(shape, axis):
        return lax.broadcasted_iota(jnp.int32, shape, axis) // HEAD_DIM

    @pl.when(i == 0)
    def _init():
        st_ref[...] = s0_ref[...]
        ii = lax.broadcasted_iota(jnp.int32, (c, c), 0).astype(F32)
        jj = lax.broadcasted_iota(jnp.int32, (c, c), 1).astype(F32)
        diff = (ii - jj) if fwd else (jj - ii)
        pos = lax.broadcasted_iota(jnp.int32, (c, GROUP_W), 0).astype(F32)
        pos_t = lax.broadcasted_iota(jnp.int32, (GROUP_W, c), 1).astype(F32)
        q_steps = (pos + 1.0) if fwd else (c - pos)
        k_steps = (c - 1.0 - pos_t) if fwd else pos_t
        qd = jnp.zeros((c, GROUP_W), F32)
        kdt = jnp.zeros((GROUP_W, c), F32)
        cd = jnp.zeros((GROUP_W, GROUP_W), F32)
        for h in range(N_HEADS):
            dec = dec_ref[direction * N_HEADS + h]
            lg = jax.nn.log_sigmoid(jnp.full((c, c), dec, F32))
            dm_ref[h * c:(h + 1) * c] = jnp.where(diff >= 0, jnp.exp(lg * jnp.maximum(diff, 0.0)), 0.0)
            lgq = jax.nn.log_sigmoid(jnp.full((c, GROUP_W), dec, F32))
            qd = jnp.where(head_of((c, GROUP_W), 1) == h, jnp.exp(lgq * q_steps), qd)
            lgk = jax.nn.log_sigmoid(jnp.full((GROUP_W, c), dec, F32))
            kdt = jnp.where(head_of((GROUP_W, c), 0) == h, jnp.exp(lgk * k_steps), kdt)
            lgc = jax.nn.log_sigmoid(jnp.full((GROUP_W, GROUP_W), dec, F32))
            cd = jnp.where(head_of((GROUP_W, GROUP_W), 0) == h, jnp.exp(lgc * c), cd)
        qd_ref[...] = qd
        kdt_ref[...] = kdt
        cd_ref[...] = cd

    in_head = [head_of((c, GROUP_W), 1) == h for h in range(N_HEADS)]
    head_mask = [jnp.where(mk, 1.0, 0.0).astype(BF16) for mk in in_head]
    same_head = head_of((GROUP_W, GROUP_W), 0) == head_of((GROUP_W, GROUP_W), 1)

    def chunk_off(n):
        idx = n if fwd else n_chunks - 1 - n
        return pl.multiple_of(idx * c, c)

    def qk(bb, n):
        off = chunk_off(n)
        q_all = q_ref[bb, pl.ds(off, c), :]
        q4 = jnp.concatenate([q_all * head_mask[h] for h in range(N_HEADS)], axis=0)
        return _dot_nt(q4, k_ref[bb, pl.ds(off, c), :])

    def gate_and_store(bb, n):
        off = chunk_off(n)
        o = ob_ref[bb]
        oo = o * o
        ms = jnp.zeros((c, GROUP_W), F32)
        for h in range(N_HEADS):
            ms_h = jnp.sum(jnp.where(in_head[h], oo, 0.0), axis=-1, keepdims=True) * (1.0 / HEAD_DIM)
            ms = jnp.where(in_head[h], ms_h, ms)
        res = o * lax.rsqrt(ms + EPS) * _silu(g_ref[bb, pl.ds(off, c), :])
        if has_prev:
            res = res + prev_ref[bb, pl.ds(off, c), :]
        o_ref[bb, pl.ds(off, c), :] = res.astype(o_ref.dtype)

    ob_ref[...] = jnp.zeros(ob_ref.shape, F32)

    for bb in batch:
        qk_ref[bb] = qk(bb, 0)

    def body(n, carry):
        for bb in batch:
            gate_and_store(bb, jnp.maximum(n - 1, 0))
        off = chunk_off(n)
        atts = [(qk_ref[bb] * dm_ref[...]).astype(BF16) for bb in batch]
        for bb in batch:
            qk_ref[bb] = qk(bb, jnp.minimum(n + 1, n_chunks - 1))
        for bb in batch:
            v = v_ref[bb, pl.ds(off, c), :]
            att = atts[bb]
            intra4 = _dot(att, v)
            intra = intra4[:c]
            for h in range(1, N_HEADS):
                intra = jnp.where(in_head[h], intra4[h * c:(h + 1) * c], intra)
            state = st_ref[bb]
            ob_ref[bb] = intra + _dot(q_ref[bb, pl.ds(off, c), :], state.astype(BF16)) * qd_ref[...]
            kk = (kt_ref[bb, :, pl.ds(off, c)].astype(F32) * kdt_ref[...]).astype(BF16)
            st_ref[bb] = state * cd_ref[...] + jnp.where(same_head, _dot(kk, v), 0.0)
        return carry

    lax.fori_loop(0, n_chunks, body, 0)
    for bb in batch:
        gate_and_store(bb, n_chunks - 1)

    @pl.when(i == pl.num_programs(0) - 1)
    def _fin():
        sn_ref[...] = st_ref[...]


def _retention_pass(dec, q, k, v, gates, state0, prev, direction, chunk, n_chunks, out_dtype):
    b, t, _ = q.shape
    tb = chunk * n_chunks
    n = t // tb
    blk = (lambda i: i) if direction == 0 else (lambda i: n - 1 - i)
    tok = pl.BlockSpec((b, tb, GROUP_W), lambda i: (0, blk(i), 0))
    tok_t = pl.BlockSpec((b, GROUP_W, tb), lambda i: (0, 0, blk(i)))
    gate = pl.BlockSpec((b, tb, GROUP_W), lambda i: (0, blk(i), direction))
    st_spec = pl.BlockSpec((b, GROUP_W, GROUP_W), lambda i: (0, 0, 0))
    in_specs = [pl.BlockSpec(memory_space=pltpu.SMEM), tok, tok, tok_t, tok, gate, st_spec]
    args = [dec, q, k, jnp.swapaxes(k, 1, 2), v, gates, state0]
    if prev is not None:
        in_specs.append(tok)
        args.append(prev)
    return pl.pallas_call(
        functools.partial(_ret_kernel, chunk=chunk, n_chunks=n_chunks, direction=direction,
                          has_prev=prev is not None),
        grid=(n,),
        in_specs=in_specs,
        out_specs=[tok, st_spec],
        out_shape=[jax.ShapeDtypeStruct((b, t, GROUP_W), out_dtype),
                   jax.ShapeDtypeStruct((b, GROUP_W, GROUP_W), F32)],
        scratch_shapes=[pltpu.VMEM((b, GROUP_W, GROUP_W), F32),
                        pltpu.VMEM((N_HEADS * chunk, chunk), F32),
                        pltpu.VMEM((chunk, GROUP_W), F32),
                        pltpu.VMEM((GROUP_W, chunk), F32),
                        pltpu.VMEM((GROUP_W, GROUP_W), F32),
                        pltpu.VMEM((b, chunk, GROUP_W), F32),
                        pltpu.VMEM((b, N_HEADS * chunk, chunk), F32)],
        compiler_params=_cparams(("arbitrary",)),
        name="retention_fwd" if direction == 0 else "retention_bwd",
    )(*args)


def _retention(dec, xq, xk, xv, xg, yq, yk, yv, yg, chunk, chunks_per_step):
    b = xq.shape[0]
    zero = jnp.zeros((b, GROUP_W, GROUP_W), F32)
    ny = yq.shape[1] // chunk
    yb, sb = _retention_pass(dec, yq, yk, yv, yg, zero, None, 1, chunk, ny, F32)
    y, sf = _retention_pass(dec, yq, yk, yv, yg, zero, yb, 0, chunk, ny, BF16)
    xb, _ = _retention_pass(dec, xq, xk, xv, xg, sb, None, 1, chunk, chunks_per_step, F32)
    x, _ = _retention_pass(dec, xq, xk, xv, xg, sf, xb, 0, chunk, chunks_per_step, BF16)
    return x, y


def _na_bias_kernel(rpb_ref, o_ref):
    h = pl.program_id(0)
    dr0 = pl.program_id(1)
    c = lax.broadcasted_iota(jnp.int32, (GRID_W, GRID_W), 0)
    kc = lax.broadcasted_iota(jnp.int32, (GRID_W, GRID_W), 1)
    c0 = jnp.clip(c - NA_KC // 2, 0, GRID_W - NA_KC)
    col_in = (kc >= c0) & (kc < c0 + NA_KC)
    dc = jnp.clip(kc - c, -(NA_KC - 1), NA_KC - 1) + NA_KC - 1
    n_dc = 2 * NA_KC - 1
    accs = [jnp.zeros((GRID_W, GRID_W), F32) for _ in range(NA_KR)]
    for d in range(n_dc):
        at_d = dc == d
        for j in range(NA_KR):
            accs[j] = jnp.where(at_d, rpb_ref[(h * (2 * NA_KR - 1) + dr0 + j) * n_dc + d], accs[j])
    for j in range(NA_KR):
        o_ref[0, :, j * GRID_W:(j + 1) * GRID_W] = jnp.where(col_in, accs[j] * LOG2_E, NEG_INF)


def _na_bias_table(rpb):
    return pl.pallas_call(
        _na_bias_kernel,
        grid=(N_HEADS, NA_KR),
        in_specs=[pl.BlockSpec(memory_space=pltpu.SMEM)],
        out_specs=pl.BlockSpec((1, GRID_W, NA_KR * GRID_W), lambda h, r: (r, h, 0)),
        out_shape=jax.ShapeDtypeStruct((NA_KR, N_HEADS * GRID_W, NA_KR * GRID_W), F32),
        compiler_params=_cparams(("parallel", "parallel")),
        name="na_bias_table",
    )(rpb.reshape(-1))


def _na_kernel(q_ref, k_ref, v_ref, ky_ref, vy_ref, tb_ref, o_ref,
               sa_ref, sb_ref, pa_ref, pb_ref, la_ref, lb_ref, *, rows_per_step, n_rows):
    r_base = pl.program_id(1) * rows_per_step
    win = NA_KR * GRID_W

    head_of_lane = lax.broadcasted_iota(jnp.int32, (GRID_W, GROUP_W), 1) // HEAD_DIM
    in_head = [head_of_lane == h for h in range(N_HEADS)]
    head_mask = [jnp.where(mk, 1.0, 0.0).astype(BF16) for mk in in_head]

    def geometry(i):
        r = r_base + i
        r0 = jnp.clip(r - NA_KR // 2, 0, n_rows - NA_KR)
        return pl.multiple_of(r0 * GRID_W, GRID_W), r0 - r + NA_KR - 1

    def row_slice(i):
        return pl.ds(pl.multiple_of(i * GRID_W, GRID_W), GRID_W)

    def scores(i, s_ref):
        koff, dr0 = geometry(i)
        q_all = q_ref[0, row_slice(i), :]
        q4 = jnp.concatenate([q_all * head_mask[h] for h in range(N_HEADS)], axis=0)
        s_ref[:, :win] = _dot_nt(q4, k_ref[0, pl.ds(koff, win), :]) + tb_ref[dr0]
        s_ref[:, win:] = _dot_nt(q4, ky_ref[0])

    def softmax(s_ref, p_ref, l_ref):
        s = s_ref[...]
        p = jnp.exp2(s - _row_max(s))
        l_ref[...] = 1.0 / _row_sum(p)
        p_ref[...] = p.astype(BF16)

    def values(i, p_ref, l_ref):
        koff, _ = geometry(i)
        o4 = (_dot(p_ref[:, :win], v_ref[0, pl.ds(koff, win), :])
              + _dot(p_ref[:, win:], vy_ref[0])) * l_ref[...]
        out = o4[:GRID_W]
        for h in range(1, N_HEADS):
            out = jnp.where(in_head[h], o4[h * GRID_W:(h + 1) * GRID_W], out)
        o_ref[0, row_slice(i), :] = out.astype(o_ref.dtype)

    pb_ref[...] = jnp.zeros(pb_ref.shape, BF16)
    lb_ref[...] = jnp.zeros(lb_ref.shape, F32)
    scores(0, sa_ref)

    def pair_body(tt, carry):
        t = 2 * tt
        values(jnp.maximum(t - 1, 0), pb_ref, lb_ref)
        scores(t + 1, sb_ref)
        softmax(sa_ref, pa_ref, la_ref)
        values(t, pa_ref, la_ref)
        scores(jnp.minimum(t + 2, rows_per_step - 1), sa_ref)
        softmax(sb_ref, pb_ref, lb_ref)
        return carry

    lax.fori_loop(0, rows_per_step // 2, pair_body, 0)
    values(rows_per_step - 1, pb_ref, lb_ref)


def _na_attention(q, k, v, ky, vy, table, rows_per_step):
    b, s, _ = q.shape
    l = ky.shape[1]
    n_rows = s // GRID_W
    tq = rows_per_step * GRID_W
    n_keys = NA_KR * GRID_W + l
    seq = lambda n: pl.BlockSpec((1, n, GROUP_W), lambda bi, i: (bi, 0, 0))
    return pl.pallas_call(
        functools.partial(_na_kernel, rows_per_step=rows_per_step, n_rows=n_rows),
        grid=(b, n_rows // rows_per_step),
        in_specs=[pl.BlockSpec((1, tq, GROUP_W), lambda bi, i: (bi, i, 0)),
                  seq(s), seq(s), seq(l), seq(l), _const_spec(table.shape)],
        out_specs=pl.BlockSpec((1, tq, GROUP_W), lambda bi, i: (bi, i, 0)),
        out_shape=jax.ShapeDtypeStruct((b, s, GROUP_W), BF16),
        scratch_shapes=([pltpu.VMEM((N_HEADS * GRID_W, n_keys), F32)] * 2
                        + [pltpu.VMEM((N_HEADS * GRID_W, n_keys), BF16)] * 2
                        + [pltpu.VMEM((N_HEADS * GRID_W, 1), F32)] * 2),
        compiler_params=_cparams(("parallel", "arbitrary")),
        name="na_attention",
    )(q, k, v, ky, vy, table)


def _swa_kernel(sink_ref, q_ref, kp_ref, kc_ref, kn_ref, vp_ref, vc_ref, vn_ref, ky_ref, vy_ref, o_ref,
                kw_ref, vw_ref, wb_ref, sa_ref, sb_ref, pa_ref, pb_ref, la_ref, lb_ref, *, blocks_per_step):
    step = pl.program_id(1)
    nb = pl.num_programs(1) * blocks_per_step
    bl = SWA_BLOCK
    g = N_HEADS // SWA_KV_HEADS
    tq = blocks_per_step * bl
    kw_ref[0:bl] = kp_ref[0]
    kw_ref[bl:bl + tq] = kc_ref[0]
    kw_ref[bl + tq:] = kn_ref[0]
    vw_ref[0:bl] = vp_ref[0]
    vw_ref[bl:bl + tq] = vc_ref[0]
    vw_ref[bl + tq:] = vn_ref[0]

    qi = lax.broadcasted_iota(jnp.int32, (g * bl, 3 * bl), 0) % bl
    jk = lax.broadcasted_iota(jnp.int32, (g * bl, 3 * bl), 1)
    wb_ref[...] = jnp.where(jnp.abs(jk - bl - qi) <= SWA_WINDOW, 0.0, NEG_INF)
    half = lax.broadcasted_iota(jnp.int32, (g * bl, 1), 0) // bl

    head_of_lane = lax.broadcasted_iota(jnp.int32, (bl, GROUP_W), 1) // HEAD_DIM
    in_head = [head_of_lane == h for h in range(N_HEADS)]
    head_mask = [jnp.where(mk, 1.0, 0.0).astype(BF16) for mk in in_head]

    def blk_slice(j):
        return pl.ds(pl.multiple_of(j * bl, bl), bl)

    def scores(j, s_ref):
        qoff = pl.multiple_of(j * bl, bl)
        q_all = q_ref[0, blk_slice(j), :]
        kw = kw_ref[pl.ds(qoff, 3 * bl), :]
        n = step * blocks_per_step + j
        lo_edge = jnp.where(n == 0, NEG_INF, 0.0)
        hi_edge = jnp.where(n == nb - 1, NEG_INF, 0.0)
        for kh in range(SWA_KV_HEADS):
            q = jnp.concatenate([q_all * head_mask[kh * g + gi] for gi in range(g)], axis=0)
            s = _dot_nt(q, kw) + wb_ref[...]
            s_ref[kh, :, :bl] = s[:, :bl] + lo_edge
            s_ref[kh, :, bl:2 * bl] = s[:, bl:2 * bl]
            s_ref[kh, :, 2 * bl:3 * bl] = s[:, 2 * bl:] + hi_edge
            s_ref[kh, :, 3 * bl:] = _dot_nt(q, ky_ref[0])

    def softmax(s_ref, p_ref, l_ref):
        for kh in range(SWA_KV_HEADS):
            s = s_ref[kh]
            sink = jnp.full((g * bl, 1), sink_ref[kh * g] * LOG2_E, F32)
            for gi in range(1, g):
                sink = jnp.where(half == gi, sink_ref[kh * g + gi] * LOG2_E, sink)
            m = jnp.maximum(_row_max(s), sink)
            p = jnp.exp2(s - m)
            l_ref[kh] = 1.0 / (_row_sum(p) + jnp.exp2(sink - m))
            p_ref[kh] = p.astype(BF16)

    def values(j, p_ref, l_ref):
        vw = vw_ref[pl.ds(pl.multiple_of(j * bl, bl), 3 * bl), :]
        out = jnp.zeros((bl, GROUP_W), F32)
        for kh in range(SWA_KV_HEADS):
            o = (_dot(p_ref[kh, :, :3 * bl], vw) + _dot(p_ref[kh, :, 3 * bl:], vy_ref[0])) * l_ref[kh]
            for gi in range(g):
                out = jnp.where(in_head[kh * g + gi], o[gi * bl:(gi + 1) * bl], out)
        o_ref[0, blk_slice(j), :] = out.astype(o_ref.dtype)

    pb_ref[...] = jnp.zeros(pb_ref.shape, BF16)
    lb_ref[...] = jnp.zeros(lb_ref.shape, F32)
    scores(0, sa_ref)

    def pair_body(tt, carry):
        t = 2 * tt
        values(jnp.maximum(t - 1, 0), pb_ref, lb_ref)
        scores(t + 1, sb_ref)
        softmax(sa_ref, pa_ref, la_ref)
        values(t, pa_ref, la_ref)
        scores(jnp.minimum(t + 2, blocks_per_step - 1), sa_ref)
        softmax(sb_ref, pb_ref, lb_ref)
        return carry

    lax.fori_loop(0, blocks_per_step // 2, pair_body, 0)
    values(blocks_per_step - 1, pb_ref, lb_ref)


def _swa_attention(sink, q, k, v, ky, vy, blocks_per_step):
    b, s, _ = q.shape
    l = ky.shape[1]
    tq = blocks_per_step * SWA_BLOCK
    n_steps = s // tq
    nb = s // SWA_BLOCK
    kvw = GROUP_W
    q_rows = N_HEADS // SWA_KV_HEADS * SWA_BLOCK
    n_keys = 3 * SWA_BLOCK + l
    prev = pl.BlockSpec((1, SWA_BLOCK, kvw), lambda bi, i: (bi, jnp.maximum(i * blocks_per_step - 1, 0), 0))
    cur = pl.BlockSpec((1, tq, kvw), lambda bi, i: (bi, i, 0))
    nxt = pl.BlockSpec((1, SWA_BLOCK, kvw), lambda bi, i: (bi, jnp.minimum((i + 1) * blocks_per_step, nb - 1), 0))
    ctx = pl.BlockSpec((1, l, kvw), lambda bi, i: (bi, 0, 0))
    return pl.pallas_call(
        functools.partial(_swa_kernel, blocks_per_step=blocks_per_step),
        grid=(b, n_steps),
        in_specs=[pl.BlockSpec(memory_space=pltpu.SMEM),
                  pl.BlockSpec((1, tq, GROUP_W), lambda bi, i: (bi, i, 0)),
                  prev, cur, nxt, prev, cur, nxt, ctx, ctx],
        out_specs=pl.BlockSpec((1, tq, GROUP_W), lambda bi, i: (bi, i, 0)),
        out_shape=jax.ShapeDtypeStruct((b, s, GROUP_W), BF16),
        scratch_shapes=[pltpu.VMEM((tq + 2 * SWA_BLOCK, kvw), BF16),
                        pltpu.VMEM((tq + 2 * SWA_BLOCK, kvw), BF16),
                        pltpu.VMEM((q_rows, 3 * SWA_BLOCK), F32)]
                       + [pltpu.VMEM((SWA_KV_HEADS, q_rows, n_keys), F32)] * 2
                       + [pltpu.VMEM((SWA_KV_HEADS, q_rows, n_keys), BF16)] * 2
                       + [pltpu.VMEM((SWA_KV_HEADS, q_rows, 1), F32)] * 2,
        compiler_params=_cparams(("parallel", "arbitrary")),
        name="swa_attention",
    )(sink, q, k, k, k, v, v, v, ky, vy)


def _outffn_kernel(*refs, hidden_chunk, final):
    if final:
        (x_ref, m0_ref, m1_ref, m2_ref, m3_ref, wo_ref, g1_ref, n2_ref, sc_ref, sh_ref, g2_ref,
         w1_ref, w3_ref, w2_ref, fg_ref, o_ref) = refs
    else:
        (x_ref, m0_ref, m1_ref, m2_ref, m3_ref, wo_ref, g1_ref, n2_ref, sc_ref, sh_ref, g2_ref,
         w1_ref, w3_ref, w2_ref, o_ref) = refs
    half_rows = x_ref.shape[1] // 2
    mod = n2_ref[...] * (1.0 + sc_ref[0])
    x1_halves, hb_halves = [], []
    for r0 in (0, half_rows):
        rows = slice(r0, r0 + half_rows)
        mix = None
        for gi, m_ref in enumerate((m0_ref, m1_ref, m2_ref, m3_ref)):
            part = _dot(m_ref[0, rows, :], wo_ref[gi * GROUP_W:(gi + 1) * GROUP_W, :])
            mix = part if mix is None else mix + part
        x1_h = x_ref[0, rows, :] + g1_ref[0] * mix
        x1_halves.append(x1_h)
        hb_halves.append((_rms(x1_h) * mod + sh_ref[0]).astype(BF16))
    x1 = jnp.concatenate(x1_halves, axis=0)
    hb = jnp.concatenate(hb_halves, axis=0)
    hidden = w1_ref.shape[1]
    acc = None
    for c0 in range(0, hidden, hidden_chunk):
        a = _dot(hb, w1_ref[:, c0:c0 + hidden_chunk])
        bgate = _dot(hb, w3_ref[:, c0:c0 + hidden_chunk])
        u = (_silu(a) * bgate).astype(BF16)
        part = _dot(u, w2_ref[c0:c0 + hidden_chunk, :])
        acc = part if acc is None else acc + part
    x2 = x1 + g2_ref[0] * acc
    if final:
        x2 = _rms(x2) * fg_ref[...]
    o_ref[0] = x2


def _outffn(x, mixes, wo, g1, n2, sc2, sh2, g2, w1, w3, w2, final_g, tm):
    b, t, d = x.shape
    tok = lambda wd: pl.BlockSpec((1, tm, wd), lambda bi, i: (bi, i, 0))
    vec = pl.BlockSpec((1, 1, d), lambda bi, i: (bi, 0, 0))
    in_specs = ([tok(d)] + [tok(GROUP_W)] * 4
                + [_const_spec(wo.shape), vec, _const_spec((1, d)), vec, vec, vec,
                   _const_spec(w1.shape), _const_spec(w3.shape), _const_spec(w2.shape)])
    args = [x, *mixes, wo, g1, n2, sc2, sh2, g2, w1, w3, w2]
    final = final_g is not None
    if final:
        in_specs.append(_const_spec((1, d)))
        args.append(final_g)
    return pl.pallas_call(
        functools.partial(_outffn_kernel, hidden_chunk=256, final=final),
        grid=(b, t // tm),
        in_specs=in_specs,
        out_specs=tok(d),
        out_shape=jax.ShapeDtypeStruct((b, t, d), F32),
        compiler_params=_cparams(("parallel", "parallel")),
        name="out_proj_ffn_final" if final else "out_proj_ffn",
    )(*args)


def _prep_weights(w_in, mla_w_uq, mla_w_ukv):
    depth, d, _ = w_in.shape
    offs = [0]
    for sz in IN_SIZES:
        offs.append(offs[-1] + sz)
    w_bf = w_in.astype(BF16)
    cols = [w_bf[:, :, offs[i]:offs[i + 1]] for i in range(len(IN_SIZES))]
    cq, ckv, kr, rq, rk, rv, rgf, rgb, nq, nk, nv, sq, sk, sv = cols
    scale = HEAD_DIM ** -0.5
    kr_slot = jnp.concatenate([kr, jnp.zeros((depth, d, LANES - MLA_ROPE), BF16)], axis=-1)
    src = jnp.arange(LANES)[:, None]
    dst = jnp.arange(N_HEADS * MLA_HEAD_PAD)[None, :]
    place = ((src < MLA_ROPE) & (dst % MLA_HEAD_PAD == src + MLA_NOPE)).astype(BF16)

    def per_query_head(t):
        g = N_HEADS // SWA_KV_HEADS
        t = t.reshape(depth, d, SWA_KV_HEADS, 1, HEAD_DIM)
        return jnp.broadcast_to(t, (depth, d, SWA_KV_HEADS, g, HEAD_DIM)).reshape(depth, d, GROUP_W)

    w = jnp.concatenate([cq, ckv, kr_slot, rq, rk * scale, rv, rgf, rgb, nq * scale, nk, nv,
                         sq * scale, per_query_head(sk), per_query_head(sv)], axis=-1).astype(BF16)

    qr = mla_w_uq.shape[1]
    uq = mla_w_uq.reshape(depth, qr, N_HEADS, MLA_NOPE + MLA_ROPE)
    wuq = jnp.concatenate([uq, jnp.zeros((depth, qr, N_HEADS, MLA_HEAD_PAD - MLA_NOPE - MLA_ROPE), F32)],
                          axis=-1).reshape(depth, qr, N_HEADS * MLA_HEAD_PAD).astype(BF16)
    kvr = mla_w_ukv.shape[1]
    ukv = mla_w_ukv.reshape(depth, kvr, N_HEADS, MLA_NOPE + MLA_V)
    zk = jnp.zeros((depth, kvr, N_HEADS, MLA_HEAD_PAD - MLA_NOPE), F32)
    wuk = jnp.concatenate([ukv[..., :MLA_NOPE], zk], axis=-1).reshape(depth, kvr, -1).astype(BF16)
    zv = jnp.zeros((depth, kvr, N_HEADS, MLA_HEAD_PAD - MLA_V), F32)
    wuv = jnp.concatenate([ukv[..., MLA_NOPE:], zv], axis=-1).reshape(depth, kvr, -1).astype(BF16)
    return w, wuq, wuk, wuv, place


def _ctx_head_specs():
    mla = tuple((h * MLA_HEAD_PAD, (h + 1) * MLA_HEAD_PAD, h * MLA_HEAD_PAD, (h + 1) * MLA_HEAD_PAD,
                 h * MLA_HEAD_PAD, h * MLA_HEAD_PAD + MLA_V, None) for h in range(N_HEADS))
    na = tuple((h * HEAD_DIM, (h + 1) * HEAD_DIM) * 3 + (None,) for h in range(N_HEADS))
    swa = tuple((h * HEAD_DIM, (h + 1) * HEAD_DIM) * 3 + (h,) for h in range(N_HEADS))
    return mla, na, swa


def kernel(x, c, ctx, c_ctx, ada_w, ada_b, norm1_g, w_in, mla_q_norm, mla_w_uq, mla_kv_norm, mla_w_ukv,
           ret_decay, na_rpb, swa_sink, w_out, norm2_g, ffn_w1, ffn_w3, ffn_w2, final_norm_g):
    b, s, d = x.shape
    l_ctx = ctx.shape[1]
    depth = ada_w.shape[0]
    assert b + 1 <= 8 and s % 2048 == 0 and l_ctx % 128 == 0

    cond = jnp.concatenate([c, c_ctx[None, :], jnp.zeros((8 - b - 1, d), F32)], axis=0)
    mod = _modulation(cond, ada_w, ada_b)
    tables = _rope_tables(s)
    w_all, wuq_all, wuk_all, wuv_all, place = _prep_weights(w_in, mla_w_uq, mla_w_ukv)
    wo_all = w_out.astype(BF16)
    w1_all, w3_all, w2_all = ffn_w1.astype(BF16), ffn_w3.astype(BF16), ffn_w2.astype(BF16)
    mla_heads, na_heads, swa_heads = _ctx_head_specs()
    no_sink = jnp.zeros((N_HEADS,), F32)

    tm_x = 512
    tm_y = min(256, l_ctx)
    y = ctx
    for l in range(depth):
        mx = [mod[l, :b, j * d:(j + 1) * d][:, None, :] for j in range(6)]
        my = [jnp.broadcast_to(mod[l, b, j * d:(j + 1) * d][None, None, :], (b, 1, d)) for j in range(6)]
        n1 = norm1_g[l][None, :]
        n2 = norm2_g[l][None, :]
        qn = mla_q_norm[l][None, :]
        kvn = mla_kv_norm[l][None, :]
        lw = (w_all[l], qn, kvn, wuq_all[l], wuk_all[l], wuv_all[l], place)

        px = _inproj(x, n1, mx[1], mx[0], *lw, tables, tm_x)
        py = _inproj(y, n1, my[1], my[0], *lw, None, tm_y)
        (xmq, xmk, xmv, xrq, xrk, xrv, xrg, xnq, xnk, xnv, xsq, xsk, xsv) = px
        (ymq, ymk, ymv, yrq, yrk, yrv, yrg, ynq, ynk, ynv, ysq, ysk, ysv) = py

        mla_x = _mla_attention(xmq, xmk, xmv, ymk, ymv, tq=512, tk=512, n_tiles=1, n_sub=2, unroll=15)
        dec = ret_decay[l].reshape(-1)
        ret_x, ret_y = _retention(dec, xrq, xrk, xrv, xrg, yrq, yrk, yrv, yrg, chunk=128, chunks_per_step=16)
        table = _na_bias_table(na_rpb[l])
        na_x = _na_attention(xnq, xnk, xnv, ynk, ynv, table, rows_per_step=32)
        swa_x = _swa_attention(swa_sink[l], xsq, xsk, xsv, ysk, ysv, blocks_per_step=16)

        last = l == depth - 1
        x = _outffn(x, (mla_x, ret_x, na_x, swa_x), wo_all[l], mx[2], n2, mx[4], mx[3], mx[5],
                    w1_all[l], w3_all[l], w2_all[l], final_norm_g[None, :] if last else None, tm_x)
        if not last:
            mla_y = _ctx_attention(ymq, ymk, ymv, mla_heads, no_sink, "mla_ctx_attention")
            na_y = _ctx_attention(ynq, ynk, ynv, na_heads, no_sink, "na_ctx_attention")
            swa_y = _ctx_attention(ysq, ysk, ysv, swa_heads, swa_sink[l], "swa_ctx_attention")
            y = _outffn(y, (mla_y, ret_y, na_y, swa_y), wo_all[l], my[2], n2, my[4], my[3], my[5],
                        w1_all[l], w3_all[l], w2_all[l], None, tm_y)
    return x
```
